```python
import math
import jax, jax.numpy as jnp
from jax import lax
import numpy as np

D_MODEL = 1024
BATCH = 16
SEQ = 4096
DEPTH = 1

PLE_DIM = 256
RMS_EPS = 1e-6
A_HEAD_DIM = 64
A_HEADS = D_MODEL // 128
A_WIDTH = A_HEADS * A_HEAD_DIM
MOBA_BLOCK = 256
MOBA_TOPK = 3
Q_CHUNK = 16
REL_BUCKETS = 32
REL_MAX_EXACT = REL_BUCKETS // 2
REL_MAX_DIST = 128
B_HEAD_DIM = 64
B_HEADS = D_MODEL // 128
B_WIDTH = B_HEADS * B_HEAD_DIM
DECAY_RANK = 64
ICLR_RANK = 64
GN_EPS = 64e-5
A_COLS = 4 * A_WIDTH
RW_COLS = 4 * B_WIDTH + DECAY_RANK + ICLR_RANK
IN_COLS = A_COLS + RW_COLS + 2 * D_MODEL

kernel_name = "hybrid_moba_rwkv7_gated_block"


def _rms_norm(x, g):
    xf = x.astype(jnp.float32)
    y = xf * lax.rsqrt(jnp.mean(xf * xf, axis=-1, keepdims=True) + RMS_EPS)
    return y * g.astype(jnp.float32)


def _t5_bucket(dist):
    n = jnp.maximum(dist, 0)
    nf = jnp.maximum(n, 1).astype(jnp.float32)
    large = REL_MAX_EXACT + (jnp.log(nf / REL_MAX_EXACT) / math.log(REL_MAX_DIST / REL_MAX_EXACT)
                             * (REL_BUCKETS - REL_MAX_EXACT)).astype(jnp.int32)
    large = jnp.minimum(large, REL_BUCKETS - 1)
    return jnp.where(n < REL_MAX_EXACT, n, large)


def _moba_attention(q, k, v, rel_bias):
    bsz, seq, nh, hd = q.shape
    n_blk = -(-seq // MOBA_BLOCK)
    pad = n_blk * MOBA_BLOCK - seq
    topk = min(MOBA_TOPK, n_blk)
    qh = jnp.swapaxes(q, 1, 2).astype(jnp.float32) * (hd ** -0.5)
    kh = jnp.pad(jnp.swapaxes(k, 1, 2).astype(jnp.float32), ((0, 0), (0, 0), (0, pad), (0, 0)))
    vh = jnp.pad(jnp.swapaxes(v, 1, 2).astype(jnp.float32), ((0, 0), (0, 0), (0, pad), (0, 0)))
    k_blk = kh.reshape(bsz, nh, n_blk, MOBA_BLOCK, hd)
    v_blk = vh.reshape(bsz, nh, n_blk, MOBA_BLOCK, hd)
    k_mean = jnp.mean(k_blk, axis=3)
    gate = jnp.einsum('bhsd,bhnd->bhsn', qh, k_mean)
    q_blk = jnp.arange(seq) // MOBA_BLOCK
    past = jnp.arange(n_blk)[None, :] < q_blk[:, None]
    gate = jnp.where(past[None, None], gate, -jnp.inf)
    _, sel = lax.top_k(gate, topk)
    bias_h = rel_bias.T.astype(jnp.float32)
    b_idx = jnp.arange(bsz)[:, None, None, None]
    h_idx = jnp.arange(nh)[None, :, None, None]
    offs = jnp.arange(MOBA_BLOCK)

    def chunk(c):
        t0 = c * Q_CHUNK
        pos = t0 + jnp.arange(Q_CHUNK)
        qb = t0 // MOBA_BLOCK
        qc = lax.dynamic_slice_in_dim(qh, t0, Q_CHUNK, axis=2)
        sc = lax.dynamic_slice_in_dim(sel, t0, Q_CHUNK, axis=2)
        ks = k_blk[b_idx, h_idx, sc]
        vs = v_blk[b_idx, h_idx, sc]
        key_pos = sc[..., None] * MOBA_BLOCK + offs
        dist = pos[None, None, :, None, None] - key_pos
        l_sel = jnp.einsum('bhqd,bhqkjd->bhqkj', qc, ks) + bias_h[h_idx[..., None], _t5_bucket(dist)]
        valid = (sc < qb)[..., None]
        l_sel = jnp.where(valid, l_sel, -jnp.inf).reshape(bsz, nh, Q_CHUNK, topk * MOBA_BLOCK)
        ko = lax.dynamic_index_in_dim(k_blk, qb, axis=2, keepdims=False)
        vo = lax.dynamic_index_in_dim(v_blk, qb, axis=2, keepdims=False)
        dist_o = pos[:, None] - (qb * MOBA_BLOCK + offs)[None, :]
        l_own = jnp.einsum('bhqd,bhjd->bhqj', qc, ko) + bias_h[:, _t5_bucket(dist_o)][None]
        l_own = jnp.where((dist_o >= 0)[None, None], l_own, -jnp.inf)
        probs = jax.nn.softmax(jnp.concatenate([l_sel, l_own], axis=-1), axis=-1)
        p_sel = probs[..., :topk * MOBA_BLOCK].reshape(bsz, nh, Q_CHUNK, topk, MOBA_BLOCK)
        p_own = probs[..., topk * MOBA_BLOCK:]
        return (jnp.einsum('bhqkj,bhqkjd->bhqd', p_sel, vs)
                + jnp.einsum('bhqj,bhjd->bhqd', p_own, vo))

    outs = lax.map(chunk, jnp.arange(seq // Q_CHUNK))
    return outs.transpose(1, 0, 3, 2, 4).reshape(bsz, seq, nh * hd)


def _rwkv7_time_mix(cols, mu, w0, w_up, a0, a_up, k_k, k_a, r_k, ln_w, ln_b):
    bsz, seq, _ = cols.shape
    cols = cols.astype(jnp.float32)
    prev = jnp.pad(cols, ((0, 0), (1, 0), (0, 0)))[:, :-1]
    cols = cols + (prev - cols) * mu.astype(jnp.float32)
    o = np.cumsum([0, B_WIDTH, B_WIDTH, B_WIDTH, B_WIDTH, DECAY_RANK])
    r, k, v, z = (cols[..., o[j]:o[j + 1]] for j in range(4))
    wd = cols[..., o[4]:o[5]]
    ad = cols[..., o[5]:]
    w_log = -jax.nn.softplus(-(w0 + jnp.tanh(wd) @ w_up)) - 0.5
    decay = jnp.exp(-jnp.exp(w_log))
    a = jax.nn.sigmoid(a0 + ad @ a_up)
    heads = lambda t: t.reshape(bsz, seq, B_HEADS, B_HEAD_DIM)
    kk = heads(k * k_k)
    kk = kk / jnp.maximum(jnp.linalg.norm(kk, axis=-1, keepdims=True), 1e-12)
    k = k * (1.0 + (a - 1.0) * k_a)
    rh, kh, vh, ah = heads(r), heads(k), heads(v), heads(a)
    xs = tuple(jnp.moveaxis(t, 1, 0) for t in (rh, heads(decay), kh, vh, -kk, kk * ah))

    def step(state, inp):
        r_t, w_t, k_t, v_t, a_t, b_t = inp
        sa = jnp.einsum('bhvk,bhk->bhv', state, a_t)
        state = (state * w_t[:, :, None, :] + sa[..., None] * b_t[:, :, None, :]
                 + v_t[..., None] * k_t[:, :, None, :])
        return state, jnp.einsum('bhvk,bhk->bhv', state, r_t)

    s0 = jnp.zeros((bsz, B_HEADS, B_HEAD_DIM, B_HEAD_DIM), jnp.float32)
    _, y = lax.scan(step, s0, xs)
    y = jnp.moveaxis(y, 0, 1)
    mean = jnp.mean(y, axis=-1, keepdims=True)
    var = jnp.mean(jnp.square(y - mean), axis=-1, keepdims=True)
    y = ((y - mean) * lax.rsqrt(var + GN_EPS)).reshape(bsz, seq, B_WIDTH) * ln_w + ln_b
    bonus = jnp.sum(rh * kh * r_k, axis=-1, keepdims=True) * vh
    y = y + bonus.reshape(bsz, seq, B_WIDTH)
    return y, z


def setup_inputs(seed: int = 0) -> dict:
    key = jax.random.key(seed)
    ks = jax.random.split(key, 22)
    n = lambda k, s, sc: jax.random.normal(k, s, jnp.float32) * sc
    L = DEPTH
    return {
        "x": n(ks[0], (BATCH, SEQ, D_MODEL), 1.0),
        "p": n(ks[1], (DEPTH, BATCH, SEQ, PLE_DIM), 1.0),
        "g_pre": 1.0 + n(ks[2], (L, D_MODEL), 0.05),
        "w_in": n(ks[3], (L, D_MODEL, IN_COLS), D_MODEL ** -0.5),
        "rel_bias": n(ks[4], (REL_BUCKETS, A_HEADS), 0.5),
        "mu_shift": jax.random.uniform(ks[5], (L, RW_COLS), jnp.float32, 0.0, 1.0),
        "w0": jax.random.uniform(ks[6], (L, B_WIDTH), jnp.float32, -4.0, 1.0),
        "w_up": n(ks[7], (L, DECAY_RANK, B_WIDTH), 0.5 * DECAY_RANK ** -0.5),
        "a0": n(ks[8], (L, B_WIDTH), 0.5),
        "a_up": n(ks[9], (L, ICLR_RANK, B_WIDTH), 0.5 * ICLR_RANK ** -0.5),
        "k_k": 0.85 + n(ks[10], (L, B_WIDTH), 0.05),
        "k_a": 1.0 + n(ks[11], (L, B_WIDTH), 0.05),
        "r_k": n(ks[12], (L, B_HEADS, B_HEAD_DIM), 0.1),
        "ln_x_w": 1.0 + n(ks[13], (L, B_WIDTH), 0.05),
        "ln_x_b": n(ks[14], (L, B_WIDTH), 0.02),
        "p_a": n(ks[15], (L, A_WIDTH, D_MODEL), A_WIDTH ** -0.5),
        "p_b": n(ks[16], (L, B_WIDTH, D_MODEL), B_WIDTH ** -0.5),
        "w_out": n(ks[17], (L, D_MODEL, D_MODEL), D_MODEL ** -0.5),
        "g_post": 1.0 + n(ks[18], (L, D_MODEL), 0.05),
        "w_ple_up": n(ks[19], (L, PLE_DIM, D_MODEL), PLE_DIM ** -0.5),
        "w_ple_gate": n(ks[20], (L, D_MODEL, D_MODEL), D_MODEL ** -0.5),
    }


def reference(x, p, g_pre, w_in, rel_bias, mu_shift, w0, w_up, a0, a_up, k_k, k_a, r_k,
              ln_x_w, ln_x_b, p_a, p_b, w_out, g_post, w_ple_up, w_ple_gate):
    bsz, seq, _ = x.shape
    h = x.astype(jnp.float32)
    for i in range(DEPTH):
        u = _rms_norm(h, g_pre[i]).astype(w_in.dtype)
        cols = u @ w_in[i]
        a_cols = cols[..., :A_COLS]
        b_cols = cols[..., A_COLS:A_COLS + RW_COLS]
        gate_a = cols[..., A_COLS + RW_COLS:A_COLS + RW_COLS + D_MODEL].astype(jnp.float32)
        gate_b = cols[..., A_COLS + RW_COLS + D_MODEL:].astype(jnp.float32)
        qa, ka, va, za = (a_cols[..., j * A_WIDTH:(j + 1) * A_WIDTH].reshape(bsz, seq, A_HEADS, A_HEAD_DIM)
                          for j in range(4))
        y_a = _moba_attention(qa, ka, va, rel_bias) * jax.nn.silu(za.reshape(bsz, seq, A_WIDTH).astype(jnp.float32))
        y_b, z_b = _rwkv7_time_mix(b_cols, mu_shift[i], w0[i], w_up[i], a0[i], a_up[i], k_k[i], k_a[i],
                                   r_k[i], ln_x_w[i], ln_x_b[i])
        y_b = y_b * jax.nn.silu(z_b)
        merged = (jax.nn.sigmoid(gate_a) * (y_a.astype(p_a.dtype) @ p_a[i])
                  + jax.nn.sigmoid(gate_b) * (y_b.astype(p_b.dtype) @ p_b[i]))
        y = merged.astype(w_out.dtype) @ w_out[i]
        h = h + _rms_norm(y, g_post[i])
        e = p[i] @ w_ple_up[i]
        h = h + jax.nn.sigmoid((h.astype(w_ple_gate.dtype) @ w_ple_gate[i]).astype(jnp.float32)) * e
    return h.astype(x.dtype)
```

```python
import functools
import math

import jax
import jax.numpy as jnp
import numpy as np
from jax import lax
from jax.experimental import pallas as pl
from jax.experimental.pallas import tpu as pltpu

F32 = jnp.float32
BF16 = jnp.bfloat16

D_MODEL = 1024
PLE_DIM = 256
RMS_EPS = 1e-6
HEAD_DIM = 64
N_HEADS = 8
WIDTH = N_HEADS * HEAD_DIM
MOBA_BLOCK = 256
MOBA_TOPK = 3
REL_BUCKETS = 32
REL_MAX_EXACT = REL_BUCKETS // 2
REL_MAX_DIST = 128
LORA_RANK = 64
GN_EPS = 64e-5
A_COLS = 4 * WIDTH
RW_COLS = 4 * WIDTH + 2 * LORA_RANK
G_COLS = 2 * D_MODEL
IN_COLS = A_COLS + RW_COLS + G_COLS

LANES = 128
N_PAIRS = N_HEADS // 2
CHUNK = 64
VMEM_LIMIT = 48 * 1024 * 1024


def _dot(a, b):
    return jnp.dot(a, b, preferred_element_type=F32)


def _dot_nt(a, b):
    return lax.dot_general(a, b, (((1,), (1,)), ((), ())), preferred_element_type=F32)


def _dot_tn(a, b):
    return lax.dot_general(a, b, (((0,), (0,)), ((), ())), preferred_element_type=F32)


def _split2(x):
    hi = x.astype(BF16)
    lo = (x - hi.astype(F32)).astype(BF16)
    return hi, lo


def _split3(x):
    h1 = x.astype(BF16)
    r1 = x - h1.astype(F32)
    h2 = r1.astype(BF16)
    h3 = (r1 - h2.astype(F32)).astype(BF16)
    return h1, h2, h3


def _sigmoid(x):
    return 1.0 / (1.0 + jnp.exp(-x))


def _proj_kernel(x_ref, g_ref, w_ref, qkv_ref, za_ref, rw_ref, gt_ref):
    x = x_ref[...]
    ms = jnp.mean(x * x, axis=-1, keepdims=True)
    u = (x * lax.rsqrt(ms + RMS_EPS) * g_ref[...]).astype(BF16)
    step = 512

    def emit(out_ref, col0, width):
        for c in range(0, width, step):
            w = min(step, width - c)
            out_ref[:, c:c + w] = _dot(u, w_ref[:, col0 + c:col0 + c + w]).astype(out_ref.dtype)

    emit(qkv_ref, 0, 3 * WIDTH)
    emit(za_ref, 3 * WIDTH, WIDTH)
    emit(rw_ref, A_COLS, RW_COLS)
    emit(gt_ref, A_COLS + RW_COLS, G_COLS)


def _project(x2, g_pre, w_in_bf, tm):
    n = x2.shape[0]
    return pl.pallas_call(
        _proj_kernel,
        grid=(n // tm,),
        in_specs=[
            pl.BlockSpec((tm, D_MODEL), lambda i: (i, 0)),
            pl.BlockSpec((1, D_MODEL), lambda i: (0, 0)),
            pl.BlockSpec((D_MODEL, IN_COLS), lambda i: (0, 0)),
        ],
        out_specs=[
            pl.BlockSpec((tm, 3 * WIDTH), lambda i: (i, 0)),
            pl.BlockSpec((tm, WIDTH), lambda i: (i, 0)),
            pl.BlockSpec((tm, RW_COLS), lambda i: (i, 0)),
            pl.BlockSpec((tm, G_COLS), lambda i: (i, 0)),
        ],
        out_shape=[
            jax.ShapeDtypeStruct((n, 3 * WIDTH), BF16),
            jax.ShapeDtypeStruct((n, WIDTH), F32),
            jax.ShapeDtypeStruct((n, RW_COLS), F32),
            jax.ShapeDtypeStruct((n, G_COLS), F32),
        ],
        compiler_params=pltpu.CompilerParams(
            dimension_semantics=("arbitrary",), vmem_limit_bytes=VMEM_LIMIT),
        name="proj",
    )(x2, g_pre, w_in_bf)


def _t5_bucket_np(dist):
    n = np.maximum(dist, 0)
    nf = np.maximum(n, 1).astype(np.float32)
    large = REL_MAX_EXACT + (np.log(nf / np.float32(REL_MAX_EXACT)) / np.float32(math.log(REL_MAX_DIST / REL_MAX_EXACT))
                             * np.float32(REL_BUCKETS - REL_MAX_EXACT)).astype(np.int32)
    large = np.minimum(large, REL_BUCKETS - 1)
    return np.where(n < REL_MAX_EXACT, n, large).astype(np.int32)


def _bucket_tables():
    s = np.arange(MOBA_BLOCK)[:, None]
    t = np.arange(MOBA_BLOCK)[None, :]
    own = np.where(t >= s, _t5_bucket_np(t - s), -1).astype(np.int32)
    prev = _t5_bucket_np(MOBA_BLOCK + t - s)
    return own, prev


def _moba_kernel(own_b_ref, prev_b_ref, relb_ref, q_ref, k_ref, v_ref, z_ref, o_ref,
                 tab_own, tab_prev, kmean, vt, sel_ref, m_ref, l_ref, acc_ref, *, n_blk):
    hp = pl.program_id(0)
    b = pl.program_id(1)
    qb = pl.program_id(2)
    blk = MOBA_BLOCK
    lane = lax.broadcasted_iota(jnp.int32, (1, LANES), 1)
    head_mask = [lane < HEAD_DIM, lane >= HEAD_DIM]
    neg_inf = jnp.float32(-jnp.inf)

    @pl.when((b == 0) & (qb == 0))
    def _build_bias_tables():
        ob = own_b_ref[...]
        pb = prev_b_ref[...]
        for hh in range(2):
            to = jnp.full((blk, blk), neg_inf, F32)
            tp = jnp.zeros((blk, blk), F32)
            for bkt in range(REL_BUCKETS):
                val = relb_ref[bkt, 2 * hp + hh]
                to = jnp.where(ob == bkt, val, to)
                tp = jnp.where(pb == bkt, val, tp)
            tab_own[hh] = to
            tab_prev[hh] = tp

    @pl.when(qb == 0)
    def _per_sequence_setup():
        for j in range(n_blk):
            kj = k_ref[0, j * blk:(j + 1) * blk, :].astype(F32)
            kmean[j:j + 1, :] = jnp.mean(kj, axis=0, keepdims=True)
            vj = v_ref[0, j * blk:(j + 1) * blk, :].astype(F32)
            vt[j] = vj.T.astype(BF16)

    q = q_ref[0].astype(F32) * (HEAD_DIM ** -0.5)
    qm = [jnp.where(head_mask[hh], q, 0.0) for hh in range(2)]
    qm_bf = [x.astype(BF16) for x in qm]

    blk_id = lax.broadcasted_iota(jnp.int32, (n_blk, blk), 0)
    km = kmean[...]
    for hh in range(2):
        km_hi, km_lo = _split2(km)
        q_hi, q_lo = _split2(qm[hh])
        g = _dot_nt(km_hi, q_hi) + _dot_nt(km_lo, q_hi) + _dot_nt(km_hi, q_lo)
        g = jnp.where(blk_id < qb, g, neg_inf)
        sel = jnp.zeros((n_blk, blk), F32)
        for _ in range(MOBA_TOPK):
            mx = jnp.max(g, axis=0, keepdims=True)
            first = jnp.min(jnp.where(g == mx, blk_id, n_blk), axis=0, keepdims=True)
            hit = (blk_id == first) & (mx > neg_inf)
            sel = jnp.where(hit, 1.0, sel)
            g = jnp.where(hit, neg_inf, g)
        sel_ref[hh] = sel

    def absorb(hh, logits, vt_j, first):
        bm = jnp.max(logits, axis=0, keepdims=True)
        if first:
            m_new = bm
            p = jnp.exp(logits - m_new)
            l_ref[hh] = jnp.sum(p, axis=0, keepdims=True)
            acc_ref[hh] = _dot(vt_j[hh * HEAD_DIM:(hh + 1) * HEAD_DIM, :], p.astype(BF16))
        else:
            m_old = m_ref[hh]
            m_new = jnp.maximum(m_old, bm)
            alpha = jnp.exp(m_old - m_new)
            p = jnp.exp(logits - m_new)
            l_ref[hh] = alpha * l_ref[hh] + jnp.sum(p, axis=0, keepdims=True)
            acc_ref[hh] = alpha * acc_ref[hh] + _dot(vt_j[hh * HEAD_DIM:(hh + 1) * HEAD_DIM, :], p.astype(BF16))
        m_ref[hh] = m_new

    k_own = k_ref[0, pl.ds(pl.multiple_of(qb * blk, blk), blk), :]
    vt_own = vt[qb]
    for hh in range(2):
        st = _dot_nt(k_own, qm_bf[hh])
        absorb(hh, st + tab_own[hh], vt_own, True)

    @pl.when(qb >= 1)
    def _prev_block():
        j = qb - 1
        k_j = k_ref[0, pl.ds(pl.multiple_of(j * blk, blk), blk), :]
        vt_j = vt[j]
        for hh in range(2):
            st = _dot_nt(k_j, qm_bf[hh])
            chosen = sel_ref[hh, pl.ds(j, 1), :] > 0.0
            absorb(hh, jnp.where(chosen, st + tab_prev[hh], neg_inf), vt_j, False)

    def far_block(j, carry):
        k_j = k_ref[0, pl.ds(pl.multiple_of(j * blk, blk), blk), :]
        vt_j = vt[j]
        for hh in range(2):
            st = _dot_nt(k_j, qm_bf[hh])
            chosen = sel_ref[hh, pl.ds(j, 1), :] > 0.0
            far_bias = relb_ref[REL_BUCKETS - 1, 2 * hp + hh]
            absorb(hh, jnp.where(chosen, st + far_bias, neg_inf), vt_j, False)
        return carry

    lax.fori_loop(0, jnp.maximum(qb - 1, 0), far_block, 0)

    out_t = jnp.concatenate([acc_ref[hh] / l_ref[hh] for hh in range(2)], axis=0)
    z = z_ref[0]
    o_ref[0] = out_t.T * (z * _sigmoid(z))


def _moba(qkv, za, rel_bias):
    bsz, seq, _ = qkv.shape
    n_blk = seq // MOBA_BLOCK
    own_b, prev_b = _bucket_tables()
    blk = MOBA_BLOCK
    kernel = functools.partial(_moba_kernel, n_blk=n_blk)
    return pl.pallas_call(
        kernel,
        grid=(N_PAIRS, bsz, n_blk),
        in_specs=[
            pl.BlockSpec((blk, blk), lambda hp, b, i: (0, 0)),
            pl.BlockSpec((blk, blk), lambda hp, b, i: (0, 0)),
            pl.BlockSpec(memory_space=pltpu.SMEM),
            pl.BlockSpec((1, blk, LANES), lambda hp, b, i: (b, i, hp)),
            pl.BlockSpec((1, seq, LANES), lambda hp, b, i: (b, 0, N_PAIRS + hp)),
            pl.BlockSpec((1, seq, LANES), lambda hp, b, i: (b, 0, 2 * N_PAIRS + hp)),
            pl.BlockSpec((1, blk, LANES), lambda hp, b, i: (b, i, hp)),
        ],
        out_specs=pl.BlockSpec((1, blk, LANES), lambda hp, b, i: (b, i, hp)),
        out_shape=jax.ShapeDtypeStruct((bsz, seq, WIDTH), F32),
        scratch_shapes=[
            pltpu.VMEM((2, blk, blk), F32),
            pltpu.VMEM((2, blk, blk), F32),
            pltpu.VMEM((n_blk, LANES), F32),
            pltpu.VMEM((n_blk, LANES, blk), BF16),
            pltpu.VMEM((2, n_blk, blk), F32),
            pltpu.VMEM((2, 1, blk), F32),
            pltpu.VMEM((2, 1, blk), F32),
            pltpu.VMEM((2, HEAD_DIM, blk), F32),
        ],
        compiler_params=pltpu.CompilerParams(
            dimension_semantics=("arbitrary", "arbitrary", "arbitrary"), vmem_limit_bytes=VMEM_LIMIT),
        name="moba",
    )(jnp.asarray(own_b), jnp.asarray(prev_b), rel_bias, qkv, qkv, qkv, za)


def _rwkv_kernel(rw_ref, mu_ref, w0_ref, lora_hi_ref, lora_lo_ref, a0_ref, kk_ref, ka_ref, rk_ref,
                 lnw_ref, lnb_ref, ltri_ref, ones_ref, o_ref, prev_row, state):
    c = pl.program_id(1)
    C = CHUNK

    @pl.when(c == 0)
    def _reset():
        prev_row[...] = jnp.zeros_like(prev_row)
        state[...] = jnp.zeros_like(state)

    cols = rw_ref[0]
    row = lax.broadcasted_iota(jnp.int32, (C, 1), 0)
    shifted = jnp.where(row == 0, prev_row[0:1, :], pltpu.roll(cols, 1, axis=0))
    prev_row[0:1, :] = cols[C - 1:C, :]
    xs = cols + (shifted - cols) * mu_ref[...]

    r = xs[:, 0:WIDTH]
    k = xs[:, WIDTH:2 * WIDTH]
    v = xs[:, 2 * WIDTH:3 * WIDTH]
    z = xs[:, 3 * WIDTH:4 * WIDTH]
    lo_in = xs[:, 4 * WIDTH:4 * WIDTH + LANES]
    lane = lax.broadcasted_iota(jnp.int32, (1, LANES), 1)
    first_half = lane < HEAD_DIM
    lo_in = jnp.where(first_half, jnp.tanh(lo_in), lo_in)
    x_hi, x_lo = _split2(lo_in)
    lora = (_dot(x_hi, lora_hi_ref[...]) + _dot(x_lo, lora_hi_ref[...]) + _dot(x_hi, lora_lo_ref[...]))
    dw = lora[:, 0:WIDTH]
    da = lora[:, WIDTH:2 * WIDTH]

    t = -(w0_ref[...] + dw)
    softplus = jnp.maximum(t, 0.0) + jnp.log(1.0 + jnp.exp(-jnp.abs(t)))
    w_log = -softplus - 0.5
    lw = -jnp.exp(w_log)
    a_ic = _sigmoid(a0_ref[...] + da)

    ones_bd = ones_ref[...]

    def seg_sum(x):
        outs = []
        for p in range(N_PAIRS):
            hi, lo = _split2(x[:, p * LANES:(p + 1) * LANES])
            outs.append(_dot(hi, ones_bd) + _dot(lo, ones_bd))
        return jnp.concatenate(outs, axis=1)

    kk = k * kk_ref[...]
    kk = kk / jnp.maximum(jnp.sqrt(seg_sum(kk * kk)), 1e-12)
    k2 = k * (1.0 + (a_ic - 1.0) * ka_ref[...])
    a_vec = -kk
    b_vec = kk * a_ic

    l1, l2, l3 = _split3(lw)
    ltri = ltri_ref[...]
    g_inc = _dot(ltri, l1) + _dot(ltri, l2) + _dot(ltri, l3)
    g_exc = g_inc - lw
    g_end = g_inc[C - 1:C, :]
    e_inc = jnp.exp(g_inc)
    e_neg = jnp.exp(-g_inc)
    e_end = jnp.exp(g_end - g_inc)
    rt = r * e_inc
    at = a_vec * jnp.exp(g_exc)
    bt = b_vec * e_neg
    kt = k2 * e_neg
    bh = b_vec * e_end
    kh = k2 * e_end
    decay_end = jnp.exp(g_end)

    second_half = jnp.logical_not(first_half)

    def stack(x):
        return jnp.concatenate([jnp.where(first_half, x, 0.0), jnp.where(second_half, x, 0.0)], axis=0)

    ri = lax.broadcasted_iota(jnp.int32, (2 * C, 2 * C), 0)
    ci = lax.broadcasted_iota(jnp.int32, (2 * C, 2 * C), 1)
    same_head = (ri < C) == (ci < C)
    incl = same_head & (ci <= ri)
    strict = same_head & (ci < ri)
    eye = (ri == ci).astype(F32)

    y_pairs = []
    for p in range(N_PAIRS):
        sl = slice(p * LANES, (p + 1) * LANES)
        ra = jnp.concatenate([stack(rt[:, sl]), stack(at[:, sl])], axis=0).astype(BF16)
        bk = jnp.concatenate([stack(bt[:, sl]), stack(kt[:, sl])], axis=0).astype(BF16)
        v_st = stack(v[:, sl])
        s_t = state[p]
        m_all = _dot_nt(ra, bk)
        a_rb = jnp.where(incl, m_all[0:2 * C, 0:2 * C], 0.0)
        a_rk = jnp.where(incl, m_all[0:2 * C, 2 * C:4 * C], 0.0)
        n_ab = jnp.where(strict, m_all[2 * C:4 * C, 0:2 * C], 0.0)
        a_ak = jnp.where(strict, m_all[2 * C:4 * C, 2 * C:4 * C], 0.0)
        rah = _dot_nt(ra, s_t.astype(BF16))
        tinv = eye + n_ab
        pw = n_ab
        for _ in range(int(math.log2(C)) - 1):
            pw_bf = pw.astype(BF16)
            pw = _dot(pw_bf, pw_bf)
            tinv = tinv + _dot(tinv.astype(BF16), pw.astype(BF16))
        v_bf = v_st.astype(BF16)
        w0m = rah[2 * C:4 * C, :] + _dot(a_ak.astype(BF16), v_bf)
        u_st = _dot(tinv.astype(BF16), w0m.astype(BF16))
        uv = jnp.concatenate([u_st.astype(BF16), v_bf], axis=0)
        y_st = rah[0:2 * C, :] + _dot(jnp.concatenate([a_rb, a_rk], axis=1).astype(BF16), uv)
        y_pairs.append(y_st[0:C, :] + y_st[C:2 * C, :])
        bkh = jnp.concatenate([stack(bh[:, sl]), stack(kh[:, sl])], axis=0).astype(BF16)
        state[p] = s_t * decay_end[:, sl] + _dot_tn(uv, bkh)
    y = jnp.concatenate(y_pairs, axis=1)

    inv_n = 1.0 / HEAD_DIM
    mean = seg_sum(y) * inv_n
    d = y - mean
    var = seg_sum(d * d) * inv_n
    yn = d * lax.rsqrt(var + GN_EPS) * lnw_ref[...] + lnb_ref[...]
    bonus = seg_sum(r * k2 * rk_ref[...]) * v
    o_ref[0] = (yn + bonus) * (z * _sigmoid(z))


def _rwkv(rw, mu, w0, w_up, a0, a_up, k_k, k_a, r_k, ln_w, ln_b):
    bsz, seq, _ = rw.shape
    C = CHUNK
    row = lambda t: t.reshape(1, -1).astype(F32)
    lora = jnp.zeros((LANES, 2 * WIDTH), F32)
    lora = lora.at[:LORA_RANK, :WIDTH].set(w_up).at[LORA_RANK:, WIDTH:].set(a_up)
    lora_hi = lora.astype(BF16)
    lora_lo = (lora - lora_hi.astype(F32)).astype(BF16)
    ltri = jnp.asarray(np.tril(np.ones((C, C), np.float32))).astype(BF16)
    hid = np.arange(LANES) // HEAD_DIM
    ones_bd = jnp.asarray((hid[:, None] == hid[None, :]).astype(np.float32)).astype(BF16)
    vec = lambda n: pl.BlockSpec((1, n), lambda b, c: (0, 0))
    return pl.pallas_call(
        _rwkv_kernel,
        grid=(bsz, seq // C),
        in_specs=[
            pl.BlockSpec((1, C, RW_COLS), lambda b, c: (b, c, 0)),
            vec(RW_COLS), vec(WIDTH),
            pl.BlockSpec((LANES, 2 * WIDTH), lambda b, c: (0, 0)),
            pl.BlockSpec((LANES, 2 * WIDTH), lambda b, c: (0, 0)),
            vec(WIDTH), vec(WIDTH), vec(WIDTH), vec(WIDTH), vec(WIDTH), vec(WIDTH),
            pl.BlockSpec((C, C), lambda b, c: (0, 0)),
            pl.BlockSpec((LANES, LANES), lambda b, c: (0, 0)),
        ],
        out_specs=pl.BlockSpec((1, C, WIDTH), lambda b, c: (b, c, 0)),
        out_shape=jax.ShapeDtypeStruct((bsz, seq, WIDTH), F32),
        scratch_shapes=[
            pltpu.VMEM((8, RW_COLS), F32),
            pltpu.VMEM((N_PAIRS, LANES, LANES), F32),
        ],
        compiler_params=pltpu.CompilerParams(
            dimension_semantics=("arbitrary", "arbitrary"), vmem_limit_bytes=VMEM_LIMIT),
        name="rwkv",
    )(rw, row(mu), row(w0), lora_hi, lora_lo, row(a0), row(k_k), row(k_a), row(r_k), row(ln_w), row(ln_b),
      ltri, ones_bd)


def _merge_kernel(ya_ref, yb_ref, ga_ref, gb_ref, x_ref, p_ref, pa_ref, pb_ref, wo_ref, gpost_ref,
                  wpu_ref, wpg_ref, o_ref):
    ma = _dot(ya_ref[...].astype(BF16), pa_ref[...])
    mb = _dot(yb_ref[...].astype(BF16), pb_ref[...])
    merged = _sigmoid(ga_ref[...]) * ma + _sigmoid(gb_ref[...]) * mb
    y = _dot(merged.astype(BF16), wo_ref[...])
    ms = jnp.mean(y * y, axis=-1, keepdims=True)
    h = x_ref[...] + y * lax.rsqrt(ms + RMS_EPS) * gpost_ref[...]
    e = _dot(p_ref[...].astype(BF16), wpu_ref[...])
    gate = _dot(h.astype(BF16), wpg_ref[...])
    o_ref[...] = h + _sigmoid(gate) * e


def _merge(ya, yb, gates, x2, p2, p_a, p_b, w_out, g_post, w_pu, w_pg, tm):
    n = x2.shape[0]
    full = lambda a: pl.BlockSpec(a.shape, lambda i: (0, 0))
    return pl.pallas_call(
        _merge_kernel,
        grid=(n // tm,),
        in_specs=[
            pl.BlockSpec((tm, WIDTH), lambda i: (i, 0)),
            pl.BlockSpec((tm, WIDTH), lambda i: (i, 0)),
            pl.BlockSpec((tm, D_MODEL), lambda i: (i, 0)),
            pl.BlockSpec((tm, D_MODEL), lambda i: (i, 1)),
            pl.BlockSpec((tm, D_MODEL), lambda i: (i, 0)),
            pl.BlockSpec((tm, PLE_DIM), lambda i: (i, 0)),
            full(p_a), full(p_b), full(w_out), full(g_post), full(w_pu), full(w_pg),
        ],
        out_specs=pl.BlockSpec((tm, D_MODEL), lambda i: (i, 0)),
        out_shape=jax.ShapeDtypeStruct((n, D_MODEL), F32),
        compiler_params=pltpu.CompilerParams(
            dimension_semantics=("arbitrary",), vmem_limit_bytes=VMEM_LIMIT),
        name="merge",
    )(ya, yb, gates, gates, x2, p2, p_a, p_b, w_out, g_post, w_pu, w_pg)


def kernel(x, p, g_pre, w_in, rel_bias, mu_shift, w0, w_up, a0, a_up, k_k, k_a, r_k, ln_x_w, ln_x_b,
           p_a, p_b, w_out, g_post, w_ple_up, w_ple_gate):
    bsz, seq, d = x.shape
    assert d == D_MODEL and seq % MOBA_BLOCK == 0 and seq // MOBA_BLOCK >= MOBA_TOPK
    assert g_pre.shape[0] == 1, "one layer"
    n = bsz * seq
    x2 = x.reshape(n, d).astype(F32)
    qkv, za, rw, gates = _project(x2, g_pre.astype(F32), w_in[0].astype(BF16), tm=256)
    ya = _moba(qkv.reshape(bsz, seq, 3 * WIDTH), za.reshape(bsz, seq, WIDTH), rel_bias.astype(F32))
    yb = _rwkv(rw.reshape(bsz, seq, RW_COLS), mu_shift[0], w0[0], w_up[0], a0[0], a_up[0], k_k[0], k_a[0],
               r_k[0], ln_x_w[0], ln_x_b[0])
    out = _merge(ya.reshape(n, WIDTH), yb.reshape(n, WIDTH), gates, x2, p[0].reshape(n, PLE_DIM),
                 p_a[0].astype(BF16), p_b[0].astype(BF16), w_out[0].astype(BF16), g_post.astype(F32),
                 w_ple_up[0].astype(BF16), w_ple_gate[0].astype(BF16), tm=512)
    return out.reshape(bsz, seq, d).astype(x.dtype)
```

```python
import functools
import math

import jax
import jax.numpy as jnp
import numpy as np
from jax import lax
from jax.experimental import pallas as pl
from jax.experimental.pallas import tpu as pltpu

F32 = jnp.float32
BF16 = jnp.bfloat16

D_MODEL = 1024
PLE_DIM = 256
RMS_EPS = 1e-6
HEAD_DIM = 64
N_HEADS = 8
WIDTH = N_HEADS * HEAD_DIM
MOBA_BLOCK = 256
MOBA_TOPK = 3
REL_BUCKETS = 32
REL_MAX_EXACT = REL_BUCKETS // 2
REL_MAX_DIST = 128
LORA_RANK = 64
GN_EPS = 64e-5
A_COLS = 4 * WIDTH
RW_COLS = 4 * WIDTH + 2 * LORA_RANK
G_COLS = 2 * D_MODEL
IN_COLS = A_COLS + RW_COLS + G_COLS

LANES = 128
N_PAIRS = N_HEADS // 2
CHUNK = 64
VMEM_LIMIT = 48 * 1024 * 1024


def _dot(a, b):
    return jnp.dot(a, b, preferred_element_type=F32)


def _dot_nt(a, b):
    return lax.dot_general(a, b, (((1,), (1,)), ((), ())), preferred_element_type=F32)


def _dot_tn(a, b):
    return lax.dot_general(a, b, (((0,), (0,)), ((), ())), preferred_element_type=F32)


def _split2(x):
    hi = x.astype(BF16)
    lo = (x - hi.astype(F32)).astype(BF16)
    return hi, lo


def _split3(x):
    h1 = x.astype(BF16)
    r1 = x - h1.astype(F32)
    h2 = r1.astype(BF16)
    h3 = (r1 - h2.astype(F32)).astype(BF16)
    return h1, h2, h3


def _sigmoid(x):
    return 1.0 / (1.0 + jnp.exp(-x))


def _proj_kernel(x_ref, g_ref, w_ref, qkv_ref, za_ref, rw_ref, gt_ref):
    x = x_ref[...]
    ms = jnp.mean(x * x, axis=-1, keepdims=True)
    u = (x * lax.rsqrt(ms + RMS_EPS) * g_ref[...]).astype(BF16)
    step = 512

    def emit(out_ref, col0, width):
        for c in range(0, width, step):
            w = min(step, width - c)
            out_ref[:, c:c + w] = _dot(u, w_ref[:, col0 + c:col0 + c + w]).astype(out_ref.dtype)

    emit(qkv_ref, 0, 3 * WIDTH)
    emit(za_ref, 3 * WIDTH, WIDTH)
    emit(rw_ref, A_COLS, RW_COLS)
    emit(gt_ref, A_COLS + RW_COLS, G_COLS)


def _project(x2, g_pre, w_in_bf, tm):
    n = x2.shape[0]
    return pl.pallas_call(
        _proj_kernel,
        grid=(n // tm,),
        in_specs=[
            pl.BlockSpec((tm, D_MODEL), lambda i: (i, 0)),
            pl.BlockSpec((1, D_MODEL), lambda i: (0, 0)),
            pl.BlockSpec((D_MODEL, IN_COLS), lambda i: (0, 0)),
        ],
        out_specs=[
            pl.BlockSpec((tm, 3 * WIDTH), lambda i: (i, 0)),
            pl.BlockSpec((tm, WIDTH), lambda i: (i, 0)),
            pl.BlockSpec((tm, RW_COLS), lambda i: (i, 0)),
            pl.BlockSpec((tm, G_COLS), lambda i: (i, 0)),
        ],
        out_shape=[
            jax.ShapeDtypeStruct((n, 3 * WIDTH), BF16),
            jax.ShapeDtypeStruct((n, WIDTH), F32),
            jax.ShapeDtypeStruct((n, RW_COLS), F32),
            jax.ShapeDtypeStruct((n, G_COLS), F32),
        ],
        compiler_params=pltpu.CompilerParams(
            dimension_semantics=("arbitrary",), vmem_limit_bytes=VMEM_LIMIT),
        name="proj",
    )(x2, g_pre, w_in_bf)


def _t5_bucket_np(dist):
    n = np.maximum(dist, 0)
    nf = np.maximum(n, 1).astype(np.float32)
    large = REL_MAX_EXACT + (np.log(nf / np.float32(REL_MAX_EXACT)) / np.float32(math.log(REL_MAX_DIST / REL_MAX_EXACT))
                             * np.float32(REL_BUCKETS - REL_MAX_EXACT)).astype(np.int32)
    large = np.minimum(large, REL_BUCKETS - 1)
    return np.where(n < REL_MAX_EXACT, n, large).astype(np.int32)


def _bucket_tables():
    s = np.arange(MOBA_BLOCK)[:, None]
    t = np.arange(MOBA_BLOCK)[None, :]
    own = np.where(t >= s, _t5_bucket_np(t - s), -1).astype(np.int32)
    prev = _t5_bucket_np(MOBA_BLOCK + t - s)
    return own, prev


MASKED = -1e30
VT_ROWS = HEAD_DIM + 16
P1_UNROLL = 2
P2_UNROLL = 2


def _moba_kernel(own_b_ref, prev_b_ref, relb_ref, q_ref, k_ref, v_ref, z_ref, o_ref,
                 tab_own, tab_prev, kmean, kaug, vt, scores, mx_ref, acc_ref, *, n_blk):
    hp = pl.program_id(0)
    b = pl.program_id(1)
    qb = pl.program_id(2)
    blk = MOBA_BLOCK
    lane = lax.broadcasted_iota(jnp.int32, (1, LANES), 1)
    head_mask = [lane < HEAD_DIM, lane >= HEAD_DIM]
    flag_base = [HEAD_DIM, 0]
    neg_inf = jnp.float32(-jnp.inf)

    @pl.when((b == 0) & (qb == 0))
    def _build_bias_tables():
        ob = own_b_ref[...]
        pb = prev_b_ref[...]
        for hh in range(2):
            far = relb_ref[REL_BUCKETS - 1, 2 * hp + hh]
            to = jnp.full((blk, blk), neg_inf, F32)
            tp = jnp.zeros((blk, blk), F32)
            for bkt in range(REL_BUCKETS):
                val = relb_ref[bkt, 2 * hp + hh] - far
                to = jnp.where(ob == bkt, val, to)
                tp = jnp.where(pb == bkt, val, tp)
            tab_own[hh] = to
            tab_prev[hh] = tp

    @pl.when(qb == 0)
    def _per_sequence_setup():
        ones_row = (lax.broadcasted_iota(jnp.int32, (VT_ROWS - HEAD_DIM, blk), 0) == 0).astype(BF16)
        for j in range(n_blk):
            kj = k_ref[0, j * blk:(j + 1) * blk, :].astype(F32)
            kmean[j:j + 1, :] = jnp.mean(kj, axis=0, keepdims=True)
            vjt = v_ref[0, j * blk:(j + 1) * blk, :].astype(F32).T
            for hh in range(2):
                flag = (lane == flag_base[hh] + j).astype(F32)
                kaug[hh, j] = jnp.where(head_mask[hh], kj, flag).astype(BF16)
                vt[hh, j, 0:HEAD_DIM, :] = vjt[hh * HEAD_DIM:(hh + 1) * HEAD_DIM, :].astype(BF16)
                vt[hh, j, HEAD_DIM:VT_ROWS, :] = ones_row

    q = q_ref[0].astype(F32) * (HEAD_DIM ** -0.5)

    blk_id = lax.broadcasted_iota(jnp.int32, (n_blk, blk), 0)
    flag_row = lax.broadcasted_iota(jnp.int32, (n_blk, LANES), 0)
    flag_lane = lax.broadcasted_iota(jnp.int32, (n_blk, LANES), 1)
    km_hi, km_lo = _split2(kmean[...])
    q_aug = []
    for hh in range(2):
        q_hi, q_lo = _split2(jnp.where(head_mask[hh], q, 0.0))
        g = _dot_nt(km_hi, q_hi) + _dot_nt(km_lo, q_hi) + _dot_nt(km_hi, q_lo)
        g = jnp.where(blk_id < qb, g, neg_inf)
        allowed = blk_id == qb
        for _ in range(MOBA_TOPK):
            mx = jnp.max(g, axis=0, keepdims=True)
            first = jnp.min(jnp.where(g == mx, blk_id, n_blk), axis=0, keepdims=True)
            hit = (blk_id == first) & (mx > neg_inf)
            allowed = allowed | hit
            g = jnp.where(hit, neg_inf, g)
        pen_t = jnp.where(allowed, 0.0, MASKED).astype(BF16)
        place = (flag_lane == flag_row + flag_base[hh]).astype(BF16)
        pen = _dot_tn(pen_t, place)
        q_aug.append(jnp.where(head_mask[hh], q, pen).astype(BF16))

    def tile_max(s):
        return jnp.max(s.reshape(blk // 8, 8, blk), axis=0)

    for hh in range(2):
        s = _dot_nt(kaug[hh, qb], q_aug[hh]) + tab_own[hh]
        scores[hh, qb] = s
        mx_ref[hh] = tile_max(s)

    @pl.when(qb >= 1)
    def _prev_block():
        for hh in range(2):
            s = _dot_nt(kaug[hh, qb - 1], q_aug[hh]) + tab_prev[hh]
            scores[hh, qb - 1] = s
            mx_ref[hh] = jnp.maximum(mx_ref[hh], tile_max(s))

    n_far = jnp.maximum(qb - 1, 0)

    def far_blocks(j0, count, carry):
        carry = list(carry)
        for u in range(count):
            for hh in range(2):
                s = _dot_nt(kaug[hh, j0 + u], q_aug[hh])
                scores[hh, j0 + u] = s
                carry[hh] = jnp.maximum(carry[hh], tile_max(s))
        return tuple(carry)

    n_single = n_far % P1_UNROLL
    carry = (mx_ref[0], mx_ref[1])
    carry = lax.fori_loop(0, n_single, lambda j, c: far_blocks(j, 1, c), carry)
    carry = lax.fori_loop(0, n_far // P1_UNROLL,
                          lambda i, c: far_blocks(n_single + i * P1_UNROLL, P1_UNROLL, c), carry)
    m_row = [jnp.max(carry[hh], axis=0, keepdims=True) for hh in range(2)]

    acc_ref[...] = jnp.zeros_like(acc_ref)

    def value_blocks(j0, count):
        for hh in range(2):
            part = None
            for u in range(count):
                p = jnp.exp(scores[hh, j0 + u] - m_row[hh]).astype(BF16)
                d = _dot(vt[hh, j0 + u], p)
                part = d if part is None else part + d
            acc_ref[hh] += part

    n_all = qb + 1
    n_single2 = n_all % P2_UNROLL

    def single2(j, c):
        value_blocks(j, 1)
        return c

    def group2(i, c):
        value_blocks(n_single2 + i * P2_UNROLL, P2_UNROLL)
        return c

    lax.fori_loop(0, n_single2, single2, 0)
    lax.fori_loop(0, n_all // P2_UNROLL, group2, 0)

    out_t = jnp.concatenate(
        [acc_ref[hh, 0:HEAD_DIM, :] / acc_ref[hh, HEAD_DIM:HEAD_DIM + 1, :] for hh in range(2)], axis=0)
    z = z_ref[0]
    o_ref[0] = out_t.T * (z * _sigmoid(z))


def _moba(qkv, za, rel_bias):
    bsz, seq, _ = qkv.shape
    n_blk = seq // MOBA_BLOCK
    own_b, prev_b = _bucket_tables()
    blk = MOBA_BLOCK
    kernel = functools.partial(_moba_kernel, n_blk=n_blk)
    return pl.pallas_call(
        kernel,
        grid=(N_PAIRS, bsz, n_blk),
        in_specs=[
            pl.BlockSpec((blk, blk), lambda hp, b, i: (0, 0)),
            pl.BlockSpec((blk, blk), lambda hp, b, i: (0, 0)),
            pl.BlockSpec(memory_space=pltpu.SMEM),
            pl.BlockSpec((1, blk, LANES), lambda hp, b, i: (b, i, hp)),
            pl.BlockSpec((1, seq, LANES), lambda hp, b, i: (b, 0, N_PAIRS + hp)),
            pl.BlockSpec((1, seq, LANES), lambda hp, b, i: (b, 0, 2 * N_PAIRS + hp)),
            pl.BlockSpec((1, blk, LANES), lambda hp, b, i: (b, i, hp)),
        ],
        out_specs=pl.BlockSpec((1, blk, LANES), lambda hp, b, i: (b, i, hp)),
        out_shape=jax.ShapeDtypeStruct((bsz, seq, WIDTH), F32),
        scratch_shapes=[
            pltpu.VMEM((2, blk, blk), F32),
            pltpu.VMEM((2, blk, blk), F32),
            pltpu.VMEM((n_blk, LANES), F32),
            pltpu.VMEM((2, n_blk, blk, LANES), BF16),
            pltpu.VMEM((2, n_blk, VT_ROWS, blk), BF16),
            pltpu.VMEM((2, n_blk, blk, blk), F32),
            pltpu.VMEM((2, 8, blk), F32),
            pltpu.VMEM((2, VT_ROWS, blk), F32),
        ],
        compiler_params=pltpu.CompilerParams(
            dimension_semantics=("arbitrary", "arbitrary", "arbitrary"), vmem_limit_bytes=VMEM_LIMIT),
        name="moba",
    )(jnp.asarray(own_b), jnp.asarray(prev_b), rel_bias, qkv, qkv, qkv, za)


def _rwkv_kernel(rw_ref, mu_ref, w0_ref, lora_hi_ref, lora_lo_ref, a0_ref, kk_ref, ka_ref, rk_ref,
                 lnw_ref, lnb_ref, ltri_ref, ones_ref, o_ref, prev_row, state):
    c = pl.program_id(1)
    C = CHUNK

    @pl.when(c == 0)
    def _reset():
        prev_row[...] = jnp.zeros_like(prev_row)
        state[...] = jnp.zeros_like(state)

    cols = rw_ref[0]
    row = lax.broadcasted_iota(jnp.int32, (C, 1), 0)
    shifted = jnp.where(row == 0, prev_row[0:1, :], pltpu.roll(cols, 1, axis=0))
    prev_row[0:1, :] = cols[C - 1:C, :]
    xs = cols + (shifted - cols) * mu_ref[...]

    r = xs[:, 0:WIDTH]
    k = xs[:, WIDTH:2 * WIDTH]
    v = xs[:, 2 * WIDTH:3 * WIDTH]
    z = xs[:, 3 * WIDTH:4 * WIDTH]
    lo_in = xs[:, 4 * WIDTH:4 * WIDTH + LANES]
    lane = lax.broadcasted_iota(jnp.int32, (1, LANES), 1)
    first_half = lane < HEAD_DIM
    lo_in = jnp.where(first_half, jnp.tanh(lo_in), lo_in)
    x_hi, x_lo = _split2(lo_in)
    lora = (_dot(x_hi, lora_hi_ref[...]) + _dot(x_lo, lora_hi_ref[...]) + _dot(x_hi, lora_lo_ref[...]))
    dw = lora[:, 0:WIDTH]
    da = lora[:, WIDTH:2 * WIDTH]

    t = -(w0_ref[...] + dw)
    softplus = jnp.maximum(t, 0.0) + jnp.log(1.0 + jnp.exp(-jnp.abs(t)))
    w_log = -softplus - 0.5
    lw = -jnp.exp(w_log)
    a_ic = _sigmoid(a0_ref[...] + da)

    ones_bd = ones_ref[...]

    def seg_sum(x):
        outs = []
        for p in range(N_PAIRS):
            hi, lo = _split2(x[:, p * LANES:(p + 1) * LANES])
            outs.append(_dot(hi, ones_bd) + _dot(lo, ones_bd))
        return jnp.concatenate(outs, axis=1)

    kk = k * kk_ref[...]
    kk = kk / jnp.maximum(jnp.sqrt(seg_sum(kk * kk)), 1e-12)
    k2 = k * (1.0 + (a_ic - 1.0) * ka_ref[...])
    a_vec = -kk
    b_vec = kk * a_ic

    l1, l2, l3 = _split3(lw)
    ltri = ltri_ref[...]
    g_inc = _dot(ltri, l1) + _dot(ltri, l2) + _dot(ltri, l3)
    g_exc = g_inc - lw
    g_end = g_inc[C - 1:C, :]
    e_inc = jnp.exp(g_inc)
    e_neg = jnp.exp(-g_inc)
    e_end = jnp.exp(g_end - g_inc)
    rt = r * e_inc
    at = a_vec * jnp.exp(g_exc)
    bt = b_vec * e_neg
    kt = k2 * e_neg
    bh = b_vec * e_end
    kh = k2 * e_end
    decay_end = jnp.exp(g_end)

    second_half = jnp.logical_not(first_half)

    def stack(x):
        return jnp.concatenate([jnp.where(first_half, x, 0.0), jnp.where(second_half, x, 0.0)], axis=0)

    ri = lax.broadcasted_iota(jnp.int32, (2 * C, 2 * C), 0)
    ci = lax.broadcasted_iota(jnp.int32, (2 * C, 2 * C), 1)
    same_head = (ri < C) == (ci < C)
    incl = same_head & (ci <= ri)
    strict = same_head & (ci < ri)
    eye = (ri == ci).astype(F32)

    y_pairs = []
    for p in range(N_PAIRS):
        sl = slice(p * LANES, (p + 1) * LANES)
        ra = jnp.concatenate([stack(rt[:, sl]), stack(at[:, sl])], axis=0).astype(BF16)
        bk = jnp.concatenate([stack(bt[:, sl]), stack(kt[:, sl])], axis=0).astype(BF16)
        v_st = stack(v[:, sl])
        s_t = state[p]
        m_all = _dot_nt(ra, bk)
        a_rb = jnp.where(incl, m_all[0:2 * C, 0:2 * C], 0.0)
        a_rk = jnp.where(incl, m_all[0:2 * C, 2 * C:4 * C], 0.0)
        n_ab = jnp.where(strict, m_all[2 * C:4 * C, 0:2 * C], 0.0)
        a_ak = jnp.where(strict, m_all[2 * C:4 * C, 2 * C:4 * C], 0.0)
        rah = _dot_nt(ra, s_t.astype(BF16))
        tinv = eye + n_ab
        pw = n_ab
        for _ in range(int(math.log2(C)) - 1):
            pw_bf = pw.astype(BF16)
            pw = _dot(pw_bf, pw_bf)
            tinv = tinv + _dot(tinv.astype(BF16), pw.astype(BF16))
        v_bf = v_st.astype(BF16)
        w0m = rah[2 * C:4 * C, :] + _dot(a_ak.astype(BF16), v_bf)
        u_st = _dot(tinv.astype(BF16), w0m.astype(BF16))
        uv = jnp.concatenate([u_st.astype(BF16), v_bf], axis=0)
        y_st = rah[0:2 * C, :] + _dot(jnp.concatenate([a_rb, a_rk], axis=1).astype(BF16), uv)
        y_pairs.append(y_st[0:C, :] + y_st[C:2 * C, :])
        bkh = jnp.concatenate([stack(bh[:, sl]), stack(kh[:, sl])], axis=0).astype(BF16)
        state[p] = s_t * decay_end[:, sl] + _dot_tn(uv, bkh)
    y = jnp.concatenate(y_pairs, axis=1)

    inv_n = 1.0 / HEAD_DIM
    mean = seg_sum(y) * inv_n
    d = y - mean
    var = seg_sum(d * d) * inv_n
    yn = d * lax.rsqrt(var + GN_EPS) * lnw_ref[...] + lnb_ref[...]
    bonus = seg_sum(r * k2 * rk_ref[...]) * v
    o_ref[0] = (yn + bonus) * (z * _sigmoid(z))


def _rwkv(rw, mu, w0, w_up, a0, a_up, k_k, k_a, r_k, ln_w, ln_b):
    bsz, seq, _ = rw.shape
    C = CHUNK
    row = lambda t: t.reshape(1, -1).astype(F32)
    lora = jnp.zeros((LANES, 2 * WIDTH), F32)
    lora = lora.at[:LORA_RANK, :WIDTH].set(w_up).at[LORA_RANK:, WIDTH:].set(a_up)
    lora_hi = lora.astype(BF16)
    lora_lo = (lora - lora_hi.astype(F32)).astype(BF16)
    ltri = jnp.asarray(np.tril(np.ones((C, C), np.float32))).astype(BF16)
    hid = np.arange(LANES) // HEAD_DIM
    ones_bd = jnp.asarray((hid[:, None] == hid[None, :]).astype(np.float32)).astype(BF16)
    vec = lambda n: pl.BlockSpec((1, n), lambda b, c: (0, 0))
    return pl.pallas_call(
        _rwkv_kernel,
        grid=(bsz, seq // C),
        in_specs=[
            pl.BlockSpec((1, C, RW_COLS), lambda b, c: (b, c, 0)),
            vec(RW_COLS), vec(WIDTH),
            pl.BlockSpec((LANES, 2 * WIDTH), lambda b, c: (0, 0)),
            pl.BlockSpec((LANES, 2 * WIDTH), lambda b, c: (0, 0)),
            vec(WIDTH), vec(WIDTH), vec(WIDTH), vec(WIDTH), vec(WIDTH), vec(WIDTH),
            pl.BlockSpec((C, C), lambda b, c: (0, 0)),
            pl.BlockSpec((LANES, LANES), lambda b, c: (0, 0)),
        ],
        out_specs=pl.BlockSpec((1, C, WIDTH), lambda b, c: (b, c, 0)),
        out_shape=jax.ShapeDtypeStruct((bsz, seq, WIDTH), F32),
        scratch_shapes=[
            pltpu.VMEM((8, RW_COLS), F32),
            pltpu.VMEM((N_PAIRS, LANES, LANES), F32),
        ],
        compiler_params=pltpu.CompilerParams(
            dimension_semantics=("arbitrary", "arbitrary"), vmem_limit_bytes=VMEM_LIMIT),
        name="rwkv",
    )(rw, row(mu), row(w0), lora_hi, lora_lo, row(a0), row(k_k), row(k_a), row(r_k), row(ln_w), row(ln_b),
      ltri, ones_bd)


def _merge_kernel(ya_ref, yb_ref, ga_ref, gb_ref, x_ref, p_ref, pa_ref, pb_ref, wo_ref, gpost_ref,
                  wpu_ref, wpg_ref, o_ref):
    ma = _dot(ya_ref[...].astype(BF16), pa_ref[...])
    mb = _dot(yb_ref[...].astype(BF16), pb_ref[...])
    merged = _sigmoid(ga_ref[...]) * ma + _sigmoid(gb_ref[...]) * mb
    y = _dot(merged.astype(BF16), wo_ref[...])
    ms = jnp.mean(y * y, axis=-1, keepdims=True)
    h = x_ref[...] + y * lax.rsqrt(ms + RMS_EPS) * gpost_ref[...]
    e = _dot(p_ref[...].astype(BF16), wpu_ref[...])
    gate = _dot(h.astype(BF16), wpg_ref[...])
    o_ref[...] = h + _sigmoid(gate) * e


def _merge(ya, yb, gates, x2, p2, p_a, p_b, w_out, g_post, w_pu, w_pg, tm):
    n = x2.shape[0]
    full = lambda a: pl.BlockSpec(a.shape, lambda i: (0, 0))
    return pl.pallas_call(
        _merge_kernel,
        grid=(n // tm,),
        in_specs=[
            pl.BlockSpec((tm, WIDTH), lambda i: (i, 0)),
            pl.BlockSpec((tm, WIDTH), lambda i: (i, 0)),
            pl.BlockSpec((tm, D_MODEL), lambda i: (i, 0)),
            pl.BlockSpec((tm, D_MODEL), lambda i: (i, 1)),
            pl.BlockSpec((tm, D_MODEL), lambda i: (i, 0)),
            pl.BlockSpec((tm, PLE_DIM), lambda i: (i, 0)),
            full(p_a), full(p_b), full(w_out), full(g_post), full(w_pu), full(w_pg),
        ],
        out_specs=pl.BlockSpec((tm, D_MODEL), lambda i: (i, 0)),
        out_shape=jax.ShapeDtypeStruct((n, D_MODEL), F32),
        compiler_params=pltpu.CompilerParams(
            dimension_semantics=("arbitrary",), vmem_limit_bytes=VMEM_LIMIT),
        name="merge",
    )(ya, yb, gates, gates, x2, p2, p_a, p_b, w_out, g_post, w_pu, w_pg)


def kernel(x, p, g_pre, w_in, rel_bias, mu_shift, w0, w_up, a0, a_up, k_k, k_a, r_k, ln_x_w, ln_x_b,
           p_a, p_b, w_out, g_post, w_ple_up, w_ple_gate):
    bsz, seq, d = x.shape
    assert d == D_MODEL and seq % MOBA_BLOCK == 0 and seq // MOBA_BLOCK >= MOBA_TOPK
    assert g_pre.shape[0] == 1, "one layer"
    n = bsz * seq
    x2 = x.reshape(n, d).astype(F32)
    qkv, za, rw, gates = _project(x2, g_pre.astype(F32), w_in[0].astype(BF16), tm=256)
    ya = _moba(qkv.reshape(bsz, seq, 3 * WIDTH), za.reshape(bsz, seq, WIDTH), rel_bias.astype(F32))
    yb = _rwkv(rw.reshape(bsz, seq, RW_COLS), mu_shift[0], w0[0], w_up[0], a0[0], a_up[0], k_k[0], k_a[0],
               r_k[0], ln_x_w[0], ln_x_b[0])
    out = _merge(ya.reshape(n, WIDTH), yb.reshape(n, WIDTH), gates, x2, p[0].reshape(n, PLE_DIM),
                 p_a[0].astype(BF16), p_b[0].astype(BF16), w_out[0].astype(BF16), g_post.astype(F32),
                 w_ple_up[0].astype(BF16), w_ple_gate[0].astype(BF16), tm=512)
    return out.reshape(bsz, seq, d).astype(x.dtype)
```

```python
import functools
import math

import jax
import jax.numpy as jnp
import numpy as np
from jax import lax
from jax.experimental import pallas as pl
from jax.experimental.pallas import tpu as pltpu

F32 = jnp.float32
BF16 = jnp.bfloat16

D_MODEL = 1024
PLE_DIM = 256
RMS_EPS = 1e-6
HEAD_DIM = 64
N_HEADS = 8
WIDTH = N_HEADS * HEAD_DIM
MOBA_BLOCK = 256
MOBA_TOPK = 3
REL_BUCKETS = 32
REL_MAX_EXACT = REL_BUCKETS // 2
REL_MAX_DIST = 128
LORA_RANK = 64
GN_EPS = 64e-5
A_COLS = 4 * WIDTH
RW_COLS = 4 * WIDTH + 2 * LORA_RANK
G_COLS = 2 * D_MODEL
IN_COLS = A_COLS + RW_COLS + G_COLS

LANES = 128
N_PAIRS = N_HEADS // 2
CHUNK = 64
VMEM_LIMIT = 48 * 1024 * 1024


def _dot(a, b):
    return jnp.dot(a, b, preferred_element_type=F32)


def _dot_nt(a, b):
    return lax.dot_general(a, b, (((1,), (1,)), ((), ())), preferred_element_type=F32)


def _dot_tn(a, b):
    return lax.dot_general(a, b, (((0,), (0,)), ((), ())), preferred_element_type=F32)


def _split2(x):
    hi = x.astype(BF16)
    lo = (x - hi.astype(F32)).astype(BF16)
    return hi, lo


def _split3(x):
    h1 = x.astype(BF16)
    r1 = x - h1.astype(F32)
    h2 = r1.astype(BF16)
    h3 = (r1 - h2.astype(F32)).astype(BF16)
    return h1, h2, h3


def _sigmoid(x):
    return 1.0 / (1.0 + jnp.exp(-x))


def _proj_kernel(x_ref, g_ref, w_ref, qkv_ref, za_ref, rw_ref, gt_ref):
    x = x_ref[...]
    ms = jnp.mean(x * x, axis=-1, keepdims=True)
    u = (x * lax.rsqrt(ms + RMS_EPS) * g_ref[...]).astype(BF16)
    step = 512

    def emit(out_ref, col0, width):
        for c in range(0, width, step):
            w = min(step, width - c)
            out_ref[:, c:c + w] = _dot(u, w_ref[:, col0 + c:col0 + c + w]).astype(out_ref.dtype)

    emit(qkv_ref, 0, 3 * WIDTH)
    emit(za_ref, 3 * WIDTH, WIDTH)
    emit(rw_ref, A_COLS, RW_COLS)
    emit(gt_ref, A_COLS + RW_COLS, G_COLS)


def _project(x2, g_pre, w_in_bf, tm):
    n = x2.shape[0]
    return pl.pallas_call(
        _proj_kernel,
        grid=(n // tm,),
        in_specs=[
            pl.BlockSpec((tm, D_MODEL), lambda i: (i, 0)),
            pl.BlockSpec((1, D_MODEL), lambda i: (0, 0)),
            pl.BlockSpec((D_MODEL, IN_COLS), lambda i: (0, 0)),
        ],
        out_specs=[
            pl.BlockSpec((tm, 3 * WIDTH), lambda i: (i, 0)),
            pl.BlockSpec((tm, WIDTH), lambda i: (i, 0)),
            pl.BlockSpec((tm, RW_COLS), lambda i: (i, 0)),
            pl.BlockSpec((tm, G_COLS), lambda i: (i, 0)),
        ],
        out_shape=[
            jax.ShapeDtypeStruct((n, 3 * WIDTH), BF16),
            jax.ShapeDtypeStruct((n, WIDTH), F32),
            jax.ShapeDtypeStruct((n, RW_COLS), F32),
            jax.ShapeDtypeStruct((n, G_COLS), F32),
        ],
        compiler_params=pltpu.CompilerParams(
            dimension_semantics=("arbitrary",), vmem_limit_bytes=VMEM_LIMIT),
        name="proj",
    )(x2, g_pre, w_in_bf)


def _t5_bucket_np(dist):
    n = np.maximum(dist, 0)
    nf = np.maximum(n, 1).astype(np.float32)
    large = REL_MAX_EXACT + (np.log(nf / np.float32(REL_MAX_EXACT)) / np.float32(math.log(REL_MAX_DIST / REL_MAX_EXACT))
                             * np.float32(REL_BUCKETS - REL_MAX_EXACT)).astype(np.int32)
    large = np.minimum(large, REL_BUCKETS - 1)
    return np.where(n < REL_MAX_EXACT, n, large).astype(np.int32)


def _bucket_tables():
    s = np.arange(MOBA_BLOCK)[:, None]
    t = np.arange(MOBA_BLOCK)[None, :]
    own = np.where(t >= s, _t5_bucket_np(t - s), -1).astype(np.int32)
    prev = _t5_bucket_np(MOBA_BLOCK + t - s)
    return own, prev


MASKED = -1e30
VT_ROWS = HEAD_DIM + 16
P1_UNROLL = 2
P2_UNROLL = 2


def _moba_kernel(own_b_ref, prev_b_ref, relb_ref, q_ref, k_ref, v_ref, z_ref, o_ref,
                 tab_own, tab_prev, kmean, kaug, vt, scores, mx_ref, acc_ref, *, n_blk):
    hp = pl.program_id(0)
    b = pl.program_id(1)
    qb = pl.program_id(2)
    blk = MOBA_BLOCK
    lane = lax.broadcasted_iota(jnp.int32, (1, LANES), 1)
    head_mask = [lane < HEAD_DIM, lane >= HEAD_DIM]
    flag_base = [HEAD_DIM, 0]
    neg_inf = jnp.float32(-jnp.inf)

    @pl.when((b == 0) & (qb == 0))
    def _build_bias_tables():
        ob = own_b_ref[...]
        pb = prev_b_ref[...]
        for hh in range(2):
            far = relb_ref[REL_BUCKETS - 1, 2 * hp + hh]
            to = jnp.full((blk, blk), neg_inf, F32)
            tp = jnp.zeros((blk, blk), F32)
            for bkt in range(REL_BUCKETS):
                val = relb_ref[bkt, 2 * hp + hh] - far
                to = jnp.where(ob == bkt, val, to)
                tp = jnp.where(pb == bkt, val, tp)
            tab_own[hh] = to
            tab_prev[hh] = tp

    @pl.when(qb == 0)
    def _per_sequence_setup():
        ones_row = (lax.broadcasted_iota(jnp.int32, (VT_ROWS - HEAD_DIM, blk), 0) == 0).astype(BF16)
        for j in range(n_blk):
            kj = k_ref[0, j * blk:(j + 1) * blk, :].astype(F32)
            kmean[j:j + 1, :] = jnp.mean(kj, axis=0, keepdims=True)
            vjt = v_ref[0, j * blk:(j + 1) * blk, :].astype(F32).T
            for hh in range(2):
                flag = (lane == flag_base[hh] + j).astype(F32)
                kaug[hh, j] = jnp.where(head_mask[hh], kj, flag).astype(BF16)
                vt[hh, j, 0:HEAD_DIM, :] = vjt[hh * HEAD_DIM:(hh + 1) * HEAD_DIM, :].astype(BF16)
                vt[hh, j, HEAD_DIM:VT_ROWS, :] = ones_row

    q = q_ref[0].astype(F32) * (HEAD_DIM ** -0.5)

    blk_id = lax.broadcasted_iota(jnp.int32, (n_blk, blk), 0)
    flag_row = lax.broadcasted_iota(jnp.int32, (n_blk, LANES), 0)
    flag_lane = lax.broadcasted_iota(jnp.int32, (n_blk, LANES), 1)
    km_hi, km_lo = _split2(kmean[...])
    q_aug = []
    for hh in range(2):
        q_hi, q_lo = _split2(jnp.where(head_mask[hh], q, 0.0))
        g = _dot_nt(km_hi, q_hi) + _dot_nt(km_lo, q_hi) + _dot_nt(km_hi, q_lo)
        g = jnp.where(blk_id < qb, g, neg_inf)
        allowed = blk_id == qb
        for _ in range(MOBA_TOPK):
            mx = jnp.max(g, axis=0, keepdims=True)
            first = jnp.min(jnp.where(g == mx, blk_id, n_blk), axis=0, keepdims=True)
            hit = (blk_id == first) & (mx > neg_inf)
            allowed = allowed | hit
            g = jnp.where(hit, neg_inf, g)
        pen_t = jnp.where(allowed, 0.0, MASKED).astype(BF16)
        place = (flag_lane == flag_row + flag_base[hh]).astype(BF16)
        pen = _dot_tn(pen_t, place)
        q_aug.append(jnp.where(head_mask[hh], q, pen).astype(BF16))

    def tile_max(s):
        return jnp.max(s.reshape(blk // 8, 8, blk), axis=0)

    for hh in range(2):
        s = _dot_nt(kaug[hh, qb], q_aug[hh]) + tab_own[hh]
        scores[hh, qb] = s
        mx_ref[hh] = tile_max(s)

    @pl.when(qb >= 1)
    def _prev_block():
        for hh in range(2):
            s = _dot_nt(kaug[hh, qb - 1], q_aug[hh]) + tab_prev[hh]
            scores[hh, qb - 1] = s
            mx_ref[hh] = jnp.maximum(mx_ref[hh], tile_max(s))

    n_far = jnp.maximum(qb - 1, 0)

    def far_blocks(j0, count, carry):
        carry = list(carry)
        for u in range(count):
            for hh in range(2):
                s = _dot_nt(kaug[hh, j0 + u], q_aug[hh])
                scores[hh, j0 + u] = s
                carry[hh] = jnp.maximum(carry[hh], tile_max(s))
        return tuple(carry)

    n_single = n_far % P1_UNROLL
    carry = (mx_ref[0], mx_ref[1])
    carry = lax.fori_loop(0, n_single, lambda j, c: far_blocks(j, 1, c), carry)
    carry = lax.fori_loop(0, n_far // P1_UNROLL,
                          lambda i, c: far_blocks(n_single + i * P1_UNROLL, P1_UNROLL, c), carry)
    m_row = [jnp.max(carry[hh], axis=0, keepdims=True) for hh in range(2)]

    acc_ref[...] = jnp.zeros_like(acc_ref)

    def value_blocks(j0, count):
        for hh in range(2):
            part = None
            for u in range(count):
                p = jnp.exp(scores[hh, j0 + u] - m_row[hh]).astype(BF16)
                d = _dot(vt[hh, j0 + u], p)
                part = d if part is None else part + d
            acc_ref[hh] += part

    n_all = qb + 1
    n_single2 = n_all % P2_UNROLL

    def single2(j, c):
        value_blocks(j, 1)
        return c

    def group2(i, c):
        value_blocks(n_single2 + i * P2_UNROLL, P2_UNROLL)
        return c

    lax.fori_loop(0, n_single2, single2, 0)
    lax.fori_loop(0, n_all // P2_UNROLL, group2, 0)

    out_t = jnp.concatenate(
        [acc_ref[hh, 0:HEAD_DIM, :] / acc_ref[hh, HEAD_DIM:HEAD_DIM + 1, :] for hh in range(2)], axis=0)
    z = z_ref[0]
    o_ref[0] = out_t.T * (z * _sigmoid(z))


def _moba(qkv, za, rel_bias):
    bsz, seq, _ = qkv.shape
    n_blk = seq // MOBA_BLOCK
    own_b, prev_b = _bucket_tables()
    blk = MOBA_BLOCK
    kernel = functools.partial(_moba_kernel, n_blk=n_blk)
    return pl.pallas_call(
        kernel,
        grid=(N_PAIRS, bsz, n_blk),
        in_specs=[
            pl.BlockSpec((blk, blk), lambda hp, b, i: (0, 0)),
            pl.BlockSpec((blk, blk), lambda hp, b, i: (0, 0)),
            pl.BlockSpec(memory_space=pltpu.SMEM),
            pl.BlockSpec((1, blk, LANES), lambda hp, b, i: (b, i, hp)),
            pl.BlockSpec((1, seq, LANES), lambda hp, b, i: (b, 0, N_PAIRS + hp)),
            pl.BlockSpec((1, seq, LANES), lambda hp, b, i: (b, 0, 2 * N_PAIRS + hp)),
            pl.BlockSpec((1, blk, LANES), lambda hp, b, i: (b, i, hp)),
        ],
        out_specs=pl.BlockSpec((1, blk, LANES), lambda hp, b, i: (b, i, hp)),
        out_shape=jax.ShapeDtypeStruct((bsz, seq, WIDTH), F32),
        scratch_shapes=[
            pltpu.VMEM((2, blk, blk), F32),
            pltpu.VMEM((2, blk, blk), F32),
            pltpu.VMEM((n_blk, LANES), F32),
            pltpu.VMEM((2, n_blk, blk, LANES), BF16),
            pltpu.VMEM((2, n_blk, VT_ROWS, blk), BF16),
            pltpu.VMEM((2, n_blk, blk, blk), F32),
            pltpu.VMEM((2, 8, blk), F32),
            pltpu.VMEM((2, VT_ROWS, blk), F32),
        ],
        compiler_params=pltpu.CompilerParams(
            dimension_semantics=("arbitrary", "arbitrary", "arbitrary"), vmem_limit_bytes=VMEM_LIMIT),
        name="moba",
    )(jnp.asarray(own_b), jnp.asarray(prev_b), rel_bias, qkv, qkv, qkv, za)


RW_CHUNKS = 2


def _rwkv_kernel(rw_ref, mu_ref, w0_ref, lora_hi_ref, lora_lo_ref, a0_ref, kk_ref, ka_ref, rk_ref,
                 lnw_ref, lnb_ref, ltri_ref, ones_ref, o_ref,
                 prev_row, state, t_s, p_s, arb_s, ayk_s, at_s, vst_s, bkh_s, x1_s, loc_s, y_s):
    c = pl.program_id(1)
    C = CHUNK
    NC = RW_CHUNKS
    R = NC * C

    @pl.when(c == 0)
    def _reset():
        prev_row[...] = jnp.zeros_like(prev_row)
        state[...] = jnp.zeros_like(state)

    cols = rw_ref[0]
    row = lax.broadcasted_iota(jnp.int32, (R, 1), 0)
    shifted = jnp.where(row == 0, prev_row[0:1, :], pltpu.roll(cols, 1, axis=0))
    prev_row[0:1, :] = cols[R - 1:R, :]
    xs = cols + (shifted - cols) * mu_ref[...]

    r = xs[:, 0:WIDTH]
    k = xs[:, WIDTH:2 * WIDTH]
    v = xs[:, 2 * WIDTH:3 * WIDTH]
    z = xs[:, 3 * WIDTH:4 * WIDTH]
    lo_in = xs[:, 4 * WIDTH:4 * WIDTH + LANES]
    lane = lax.broadcasted_iota(jnp.int32, (1, LANES), 1)
    first_half = lane < HEAD_DIM
    lo_in = jnp.where(first_half, jnp.tanh(lo_in), lo_in)
    x_hi, x_lo = _split2(lo_in)
    lora = (_dot(x_hi, lora_hi_ref[...]) + _dot(x_lo, lora_hi_ref[...]) + _dot(x_hi, lora_lo_ref[...]))
    dw = lora[:, 0:WIDTH]
    da = lora[:, WIDTH:2 * WIDTH]

    t = -(w0_ref[...] + dw)
    softplus = jnp.maximum(t, 0.0) + jnp.log(1.0 + jnp.exp(-jnp.abs(t)))
    w_log = -softplus - 0.5
    lw = -jnp.exp(w_log)
    a_ic = _sigmoid(a0_ref[...] + da)

    ones_bd = ones_ref[...]

    def seg_sum(x):
        outs = []
        for p in range(N_PAIRS):
            hi, lo = _split2(x[:, p * LANES:(p + 1) * LANES])
            outs.append(_dot(hi, ones_bd) + _dot(lo, ones_bd))
        return jnp.concatenate(outs, axis=1)

    kk = k * kk_ref[...]
    kk = kk / jnp.maximum(jnp.sqrt(seg_sum(kk * kk)), 1e-12)
    k2 = k * (1.0 + (a_ic - 1.0) * ka_ref[...])
    a_vec = -kk
    b_vec = kk * a_ic

    l1, l2, l3 = _split3(lw)
    ltri = ltri_ref[...]
    g_inc = _dot(ltri, l1) + _dot(ltri, l2) + _dot(ltri, l3)
    g_exc = g_inc - lw
    g_end_rows = [g_inc[(n + 1) * C - 1:(n + 1) * C, :] for n in range(NC)]
    g_end = jnp.concatenate([jnp.broadcast_to(g, (C, WIDTH)) for g in g_end_rows], axis=0)
    e_inc = jnp.exp(g_inc)
    e_neg = jnp.exp(-g_inc)
    e_end = jnp.exp(g_end - g_inc)
    rt = r * e_inc
    at = a_vec * jnp.exp(g_exc)
    bt = b_vec * e_neg
    kt = k2 * e_neg
    bh = b_vec * e_end
    kh = k2 * e_end
    decay_end = [jnp.exp(g) for g in g_end_rows]

    second_half = jnp.logical_not(first_half)

    def stack(x):
        return jnp.concatenate([jnp.where(first_half, x, 0.0), jnp.where(second_half, x, 0.0)], axis=0)

    ri = lax.broadcasted_iota(jnp.int32, (2 * C, 2 * C), 0)
    ci = lax.broadcasted_iota(jnp.int32, (2 * C, 2 * C), 1)
    same_head = (ri < C) == (ci < C)
    incl = same_head & (ci <= ri)
    strict = same_head & (ci < ri)
    eye = (ri == ci).astype(F32)
    items = [(n, p) for n in range(NC) for p in range(N_PAIRS)]
    H = 2 * C

    for i, (n, p) in enumerate(items):
        rows = slice(n * C, (n + 1) * C)
        sl = slice(p * LANES, (p + 1) * LANES)
        r_st = stack(rt[rows, sl]).astype(BF16)
        a_st = stack(at[rows, sl]).astype(BF16)
        ra = jnp.concatenate([r_st, a_st], axis=0)
        bk = jnp.concatenate([stack(bt[rows, sl]), stack(kt[rows, sl])], axis=0).astype(BF16)
        m_all = _dot_nt(ra, bk)
        n_ab = jnp.where(strict, m_all[H:2 * H, 0:H], 0.0)
        t_s[i] = eye + n_ab
        p_s[i] = n_ab.astype(BF16)
        arb_s[i] = jnp.where(incl, m_all[0:H, 0:H], 0.0).astype(BF16)
        ayk_s[i, 0:H, :] = jnp.where(incl, m_all[0:H, H:2 * H], 0.0).astype(BF16)
        ayk_s[i, H:2 * H, :] = jnp.where(strict, m_all[H:2 * H, H:2 * H], 0.0).astype(BF16)
        x1_s[i, 0:H, :] = r_st
        at_s[i] = a_st
        vst_s[i] = stack(v[rows, sl]).astype(BF16)
        bkh_s[i] = jnp.concatenate([stack(bh[rows, sl]), stack(kh[rows, sl])], axis=0).astype(BF16)

    for i in range(len(items)):
        pb = p_s[i]
        p_s[i] = _dot(pb, pb).astype(BF16)
    for _ in range(int(math.log2(C)) - 2):
        for i in range(len(items)):
            pb = p_s[i]
            tb = t_s[i]
            both = _dot(pb, jnp.concatenate([tb.astype(BF16), pb], axis=1))
            t_s[i] = tb + both[:, 0:H]
            p_s[i] = both[:, H:2 * H].astype(BF16)
    for i in range(len(items)):
        tb = t_s[i]
        t_s[i] = tb + _dot(p_s[i], tb.astype(BF16))

    for i in range(len(items)):
        yk = _dot(ayk_s[i], vst_s[i])
        wz = _dot(t_s[i].astype(BF16), jnp.concatenate([at_s[i], yk[H:2 * H, :].astype(BF16)], axis=1))
        x1_s[i, H:2 * H, :] = wz[:, 0:LANES].astype(BF16)
        loc_s[i, 0:H, :] = yk[0:H, :]
        loc_s[i, H:2 * H, :] = wz[:, LANES:2 * LANES]

    for i, (n, p) in enumerate(items):
        sl = slice(p * LANES, (p + 1) * LANES)
        s_t = state[p]
        rs = _dot_nt(x1_s[i], s_t.astype(BF16))
        u_bf = (rs[H:2 * H, :] + loc_s[i, H:2 * H, :]).astype(BF16)
        y_st = rs[0:H, :] + loc_s[i, 0:H, :] + _dot(arb_s[i], u_bf)
        y_s[n * C:(n + 1) * C, sl] = y_st[0:C, :] + y_st[C:H, :]
        uv = jnp.concatenate([u_bf, vst_s[i]], axis=0)
        state[p] = s_t * decay_end[n][:, sl] + _dot_tn(uv, bkh_s[i])
    y = y_s[...]

    inv_n = 1.0 / HEAD_DIM
    mean = seg_sum(y) * inv_n
    d = y - mean
    var = seg_sum(d * d) * inv_n
    yn = d * lax.rsqrt(var + GN_EPS) * lnw_ref[...] + lnb_ref[...]
    bonus = seg_sum(r * k2 * rk_ref[...]) * v
    o_ref[0] = (yn + bonus) * (z * _sigmoid(z))


def _rwkv(rw, mu, w0, w_up, a0, a_up, k_k, k_a, r_k, ln_w, ln_b):
    bsz, seq, _ = rw.shape
    C = CHUNK
    row = lambda t: t.reshape(1, -1).astype(F32)
    lora = jnp.zeros((LANES, 2 * WIDTH), F32)
    lora = lora.at[:LORA_RANK, :WIDTH].set(w_up).at[LORA_RANK:, WIDTH:].set(a_up)
    lora_hi = lora.astype(BF16)
    lora_lo = (lora - lora_hi.astype(F32)).astype(BF16)
    R = RW_CHUNKS * C
    G = RW_CHUNKS * N_PAIRS
    H = 2 * C
    assert seq % R == 0 and H == LANES
    ltri = jnp.asarray(np.kron(np.eye(RW_CHUNKS), np.tril(np.ones((C, C)))).astype(np.float32)).astype(BF16)
    hid = np.arange(LANES) // HEAD_DIM
    ones_bd = jnp.asarray((hid[:, None] == hid[None, :]).astype(np.float32)).astype(BF16)
    vec = lambda n: pl.BlockSpec((1, n), lambda b, c: (0, 0))
    return pl.pallas_call(
        _rwkv_kernel,
        grid=(bsz, seq // R),
        in_specs=[
            pl.BlockSpec((1, R, RW_COLS), lambda b, c: (b, c, 0)),
            vec(RW_COLS), vec(WIDTH),
            pl.BlockSpec((LANES, 2 * WIDTH), lambda b, c: (0, 0)),
            pl.BlockSpec((LANES, 2 * WIDTH), lambda b, c: (0, 0)),
            vec(WIDTH), vec(WIDTH), vec(WIDTH), vec(WIDTH), vec(WIDTH), vec(WIDTH),
            pl.BlockSpec((R, R), lambda b, c: (0, 0)),
            pl.BlockSpec((LANES, LANES), lambda b, c: (0, 0)),
        ],
        out_specs=pl.BlockSpec((1, R, WIDTH), lambda b, c: (b, c, 0)),
        out_shape=jax.ShapeDtypeStruct((bsz, seq, WIDTH), F32),
        scratch_shapes=[
            pltpu.VMEM((8, RW_COLS), F32),
            pltpu.VMEM((N_PAIRS, LANES, LANES), F32),
            pltpu.VMEM((G, H, H), F32),
            pltpu.VMEM((G, H, H), BF16),
            pltpu.VMEM((G, H, H), BF16),
            pltpu.VMEM((G, 2 * H, H), BF16),
            pltpu.VMEM((G, H, LANES), BF16),
            pltpu.VMEM((G, H, LANES), BF16),
            pltpu.VMEM((G, 2 * H, LANES), BF16),
            pltpu.VMEM((G, 2 * H, LANES), BF16),
            pltpu.VMEM((G, 2 * H, LANES), F32),
            pltpu.VMEM((R, WIDTH), F32),
        ],
        compiler_params=pltpu.CompilerParams(
            dimension_semantics=("arbitrary", "arbitrary"), vmem_limit_bytes=VMEM_LIMIT),
        name="rwkv",
    )(rw, row(mu), row(w0), lora_hi, lora_lo, row(a0), row(k_k), row(k_a), row(r_k), row(ln_w), row(ln_b),
      ltri, ones_bd)


def _merge_kernel(ya_ref, yb_ref, ga_ref, gb_ref, x_ref, p_ref, pa_ref, pb_ref, wo_ref, gpost_ref,
                  wpu_ref, wpg_ref, o_ref):
    ma = _dot(ya_ref[...].astype(BF16), pa_ref[...])
    mb = _dot(yb_ref[...].astype(BF16), pb_ref[...])
    merged = _sigmoid(ga_ref[...]) * ma + _sigmoid(gb_ref[...]) * mb
    y = _dot(merged.astype(BF16), wo_ref[...])
    ms = jnp.mean(y * y, axis=-1, keepdims=True)
    h = x_ref[...] + y * lax.rsqrt(ms + RMS_EPS) * gpost_ref[...]
    e = _dot(p_ref[...].astype(BF16), wpu_ref[...])
    gate = _dot(h.astype(BF16), wpg_ref[...])
    o_ref[...] = h + _sigmoid(gate) * e


def _merge(ya, yb, gates, x2, p2, p_a, p_b, w_out, g_post, w_pu, w_pg, tm):
    n = x2.shape[0]
    full = lambda a: pl.BlockSpec(a.shape, lambda i: (0, 0))
    return pl.pallas_call(
        _merge_kernel,
        grid=(n // tm,),
        in_specs=[
            pl.BlockSpec((tm, WIDTH), lambda i: (i, 0)),
            pl.BlockSpec((tm, WIDTH), lambda i: (i, 0)),
            pl.BlockSpec((tm, D_MODEL), lambda i: (i, 0)),
            pl.BlockSpec((tm, D_MODEL), lambda i: (i, 1)),
            pl.BlockSpec((tm, D_MODEL), lambda i: (i, 0)),
            pl.BlockSpec((tm, PLE_DIM), lambda i: (i, 0)),
            full(p_a), full(p_b), full(w_out), full(g_post), full(w_pu), full(w_pg),
        ],
        out_specs=pl.BlockSpec((tm, D_MODEL), lambda i: (i, 0)),
        out_shape=jax.ShapeDtypeStruct((n, D_MODEL), F32),
        compiler_params=pltpu.CompilerParams(
            dimension_semantics=("arbitrary",), vmem_limit_bytes=VMEM_LIMIT),
        name="merge",
    )(ya, yb, gates, gates, x2, p2, p_a, p_b, w_out, g_post, w_pu, w_pg)


def kernel(x, p, g_pre, w_in, rel_bias, mu_shift, w0, w_up, a0, a_up, k_k, k_a, r_k, ln_x_w, ln_x_b,
           p_a, p_b, w_out, g_post, w_ple_up, w_ple_gate):
    bsz, seq, d = x.shape
    assert d == D_MODEL and seq % MOBA_BLOCK == 0 and seq // MOBA_BLOCK >= MOBA_TOPK
    assert g_pre.shape[0] == 1, "one layer"
    n = bsz * seq
    x2 = x.reshape(n, d).astype(F32)
    qkv, za, rw, gates = _project(x2, g_pre.astype(F32), w_in[0].astype(BF16), tm=256)
    ya = _moba(qkv.reshape(bsz, seq, 3 * WIDTH), za.reshape(bsz, seq, WIDTH), rel_bias.astype(F32))
    yb = _rwkv(rw.reshape(bsz, seq, RW_COLS), mu_shift[0], w0[0], w_up[0], a0[0], a_up[0], k_k[0], k_a[0],
               r_k[0], ln_x_w[0], ln_x_b[0])
    out = _merge(ya.reshape(n, WIDTH), yb.reshape(n, WIDTH), gates, x2, p[0].reshape(n, PLE_DIM),
                 p_a[0].astype(BF16), p_b[0].astype(BF16), w_out[0].astype(BF16), g_post.astype(F32),
                 w_ple_up[0].astype(BF16), w_ple_gate[0].astype(BF16), tm=512)
    return out.reshape(bsz, seq, d).astype(x.dtype)
```

```python
import functools
import math

import jax
import jax.numpy as jnp
import numpy as np
from jax import lax
from jax.experimental import pallas as pl
from jax.experimental.pallas import tpu as pltpu

F32 = jnp.float32
BF16 = jnp.bfloat16

D_MODEL = 1024
PLE_DIM = 256
RMS_EPS = 1e-6
HEAD_DIM = 64
N_HEADS = 8
WIDTH = N_HEADS * HEAD_DIM
MOBA_BLOCK = 256
MOBA_TOPK = 3
REL_BUCKETS = 32
REL_MAX_EXACT = REL_BUCKETS // 2
REL_MAX_DIST = 128
LORA_RANK = 64
GN_EPS = 64e-5
A_COLS = 4 * WIDTH
RW_COLS = 4 * WIDTH + 2 * LORA_RANK
G_COLS = 2 * D_MODEL
IN_COLS = A_COLS + RW_COLS + G_COLS

LANES = 128
N_PAIRS = N_HEADS // 2
CHUNK = 64
VMEM_LIMIT = 48 * 1024 * 1024


def _dot(a, b):
    return jnp.dot(a, b, preferred_element_type=F32)


def _dot_nt(a, b):
    return lax.dot_general(a, b, (((1,), (1,)), ((), ())), preferred_element_type=F32)


def _dot_tn(a, b):
    return lax.dot_general(a, b, (((0,), (0,)), ((), ())), preferred_element_type=F32)


def _split2(x):
    hi = x.astype(BF16)
    lo = (x - hi.astype(F32)).astype(BF16)
    return hi, lo


def _split3(x):
    h1 = x.astype(BF16)
    r1 = x - h1.astype(F32)
    h2 = r1.astype(BF16)
    h3 = (r1 - h2.astype(F32)).astype(BF16)
    return h1, h2, h3


def _sigmoid(x):
    return 1.0 / (1.0 + jnp.exp(-x))


def _proj_kernel(x_ref, g_ref, w_ref, qkv_ref, za_ref, rw_ref, gt_ref):
    x = x_ref[...]
    ms = jnp.mean(x * x, axis=-1, keepdims=True)
    u = (x * lax.rsqrt(ms + RMS_EPS) * g_ref[...]).astype(BF16)
    step = 512

    def emit(out_ref, col0, width):
        for c in range(0, width, step):
            w = min(step, width - c)
            out_ref[:, c:c + w] = _dot(u, w_ref[:, col0 + c:col0 + c + w]).astype(out_ref.dtype)

    emit(qkv_ref, 0, 3 * WIDTH)
    emit(za_ref, 3 * WIDTH, WIDTH)
    emit(rw_ref, A_COLS, RW_COLS)
    emit(gt_ref, A_COLS + RW_COLS, G_COLS)


def _project(x2, g_pre, w_in_bf, tm):
    n = x2.shape[0]
    return pl.pallas_call(
        _proj_kernel,
        grid=(n // tm,),
        in_specs=[
            pl.BlockSpec((tm, D_MODEL), lambda i: (i, 0)),
            pl.BlockSpec((1, D_MODEL), lambda i: (0, 0)),
            pl.BlockSpec((D_MODEL, IN_COLS), lambda i: (0, 0)),
        ],
        out_specs=[
            pl.BlockSpec((tm, 3 * WIDTH), lambda i: (i, 0)),
            pl.BlockSpec((tm, WIDTH), lambda i: (i, 0)),
            pl.BlockSpec((tm, RW_COLS), lambda i: (i, 0)),
            pl.BlockSpec((tm, G_COLS), lambda i: (i, 0)),
        ],
        out_shape=[
            jax.ShapeDtypeStruct((n, 3 * WIDTH), BF16),
            jax.ShapeDtypeStruct((n, WIDTH), F32),
            jax.ShapeDtypeStruct((n, RW_COLS), F32),
            jax.ShapeDtypeStruct((n, G_COLS), F32),
        ],
        compiler_params=pltpu.CompilerParams(
            dimension_semantics=("arbitrary",), vmem_limit_bytes=VMEM_LIMIT),
        name="proj",
    )(x2, g_pre, w_in_bf)


def _t5_bucket_np(dist):
    n = np.maximum(dist, 0)
    nf = np.maximum(n, 1).astype(np.float32)
    large = REL_MAX_EXACT + (np.log(nf / np.float32(REL_MAX_EXACT)) / np.float32(math.log(REL_MAX_DIST / REL_MAX_EXACT))
                             * np.float32(REL_BUCKETS - REL_MAX_EXACT)).astype(np.int32)
    large = np.minimum(large, REL_BUCKETS - 1)
    return np.where(n < REL_MAX_EXACT, n, large).astype(np.int32)


def _bucket_tables():
    s = np.arange(MOBA_BLOCK)[:, None]
    t = np.arange(MOBA_BLOCK)[None, :]
    own = np.where(t >= s, _t5_bucket_np(t - s), -1).astype(np.int32)
    prev = _t5_bucket_np(MOBA_BLOCK + t - s)
    return own, prev


MASKED = -1e30
VT_ROWS = HEAD_DIM + 16
KIND_FAR, KIND_PREV, KIND_OWN = 0, 1, 2


def _moba_kernel(own_b_ref, prev_b_ref, relb_ref, q_ref, k_ref, v_ref, za_ref, zb_ref, oa_ref, ob_ref,
                 tabs, kmean, kaug, vt, qaug, scores, mx_ref, acc_ref, *, n_blk):
    hp = pl.program_id(0)
    b = pl.program_id(1)
    i = pl.program_id(2)
    blk = MOBA_BLOCK
    seq = n_blk * blk
    lane = lax.broadcasted_iota(jnp.int32, (1, LANES), 1)
    head_mask = [lane < HEAD_DIM, lane >= HEAD_DIM]
    flag_base = [HEAD_DIM, 0]
    neg_inf = jnp.float32(-jnp.inf)

    @pl.when((b == 0) & (i == 0))
    def _build_bias_tables():
        ob = own_b_ref[...]
        pb = prev_b_ref[...]
        for hh in range(2):
            far = relb_ref[REL_BUCKETS - 1, 2 * hp + hh]
            to = jnp.full((blk, blk), neg_inf, F32)
            tp = jnp.zeros((blk, blk), F32)
            for bkt in range(REL_BUCKETS):
                val = relb_ref[bkt, 2 * hp + hh] - far
                to = jnp.where(ob == bkt, val, to)
                tp = jnp.where(pb == bkt, val, tp)
            tabs[hh, KIND_FAR] = jnp.zeros((blk, blk), F32)
            tabs[hh, KIND_PREV] = tp
            tabs[hh, KIND_OWN] = to

    @pl.when(i == 0)
    def _per_sequence_setup():
        ones_row = (lax.broadcasted_iota(jnp.int32, (VT_ROWS - HEAD_DIM, blk), 0) == 0).astype(BF16)
        for j in range(n_blk):
            kj = k_ref[0, j * blk:(j + 1) * blk, :].astype(F32)
            kmean[j:j + 1, :] = jnp.mean(kj, axis=0, keepdims=True)
            vjt = v_ref[0, j * blk:(j + 1) * blk, :].astype(F32).T
            for hh in range(2):
                flag = (lane == flag_base[hh] + j).astype(F32)
                kaug[hh, j] = jnp.where(head_mask[hh], kj, flag).astype(BF16)
                vt[hh, j, 0:HEAD_DIM, :] = vjt[hh * HEAD_DIM:(hh + 1) * HEAD_DIM, :].astype(BF16)
                vt[hh, j, HEAD_DIM:VT_ROWS, :] = ones_row

        q = q_ref[0].astype(F32) * (HEAD_DIM ** -0.5)
        blk_id = lax.broadcasted_iota(jnp.int32, (n_blk, seq), 0)
        q_blk = lax.broadcasted_iota(jnp.int32, (n_blk, seq), 1) // blk
        flag_row = lax.broadcasted_iota(jnp.int32, (n_blk, LANES), 0)
        flag_lane = lax.broadcasted_iota(jnp.int32, (n_blk, LANES), 1)
        km_hi, km_lo = _split2(kmean[...])
        for hh in range(2):
            q_hi, q_lo = _split2(jnp.where(head_mask[hh], q, 0.0))
            g = _dot_nt(km_hi, q_hi) + _dot_nt(km_lo, q_hi) + _dot_nt(km_hi, q_lo)
            g = jnp.where(blk_id < q_blk, g, neg_inf)
            allowed = blk_id == q_blk
            for _ in range(MOBA_TOPK):
                mx = jnp.max(g, axis=0, keepdims=True)
                first = jnp.min(jnp.where(g == mx, blk_id, n_blk), axis=0, keepdims=True)
                hit = (blk_id == first) & (mx > neg_inf)
                allowed = allowed | hit
                g = jnp.where(hit, neg_inf, g)
            pen_t = jnp.where(allowed, 0.0, MASKED).astype(BF16)
            place = (flag_lane == flag_row + flag_base[hh]).astype(BF16)
            pen = _dot_tn(pen_t, place)
            qa = jnp.where(head_mask[hh], q, pen).astype(BF16)
            for j in range(n_blk):
                qaug[hh, j] = qa[j * blk:(j + 1) * blk, :]

    def tile_max(s):
        return jnp.max(s.reshape(blk // 8, 8, blk), axis=0)

    qb_a = i
    qb_b = n_blk - 1 - i
    n_tiles = n_blk + 1

    def tile_ids(t):
        is_a = t <= qb_a
        qb = jnp.where(is_a, qb_a, qb_b)
        kb = jnp.where(is_a, t, t - qb_a - 1)
        kind = jnp.where(kb == qb, KIND_OWN, jnp.where(kb == qb - 1, KIND_PREV, KIND_FAR))
        return is_a, qb, kb, kind

    mx_ref[...] = jnp.full(mx_ref.shape, neg_inf, F32)
    for t in range(n_tiles):
        is_a, qb, kb, kind = tile_ids(t)
        which = jnp.where(is_a, 0, 1)
        for hh in range(2):
            s = _dot_nt(kaug[hh, kb], qaug[hh, qb]) + tabs[hh, kind]
            scores[hh, t] = s
            mx_ref[hh, which] = jnp.maximum(mx_ref[hh, which], tile_max(s))
    m_rows = [[jnp.max(mx_ref[hh, w], axis=0, keepdims=True) for w in range(2)] for hh in range(2)]

    acc_ref[...] = jnp.zeros_like(acc_ref)
    for t in range(n_tiles):
        is_a, qb, kb, kind = tile_ids(t)
        which = jnp.where(is_a, 0, 1)
        for hh in range(2):
            m_row = jnp.where(is_a, m_rows[hh][0], m_rows[hh][1])
            p = jnp.exp(scores[hh, t] - m_row).astype(BF16)
            acc_ref[hh, which] += _dot(vt[hh, kb], p)

    for w, (z_ref, o_ref) in enumerate(((za_ref, oa_ref), (zb_ref, ob_ref))):
        out_t = jnp.concatenate(
            [acc_ref[hh, w, 0:HEAD_DIM, :] / acc_ref[hh, w, HEAD_DIM:HEAD_DIM + 1, :] for hh in range(2)], axis=0)
        z = z_ref[0]
        o_ref[0] = out_t.T * (z * _sigmoid(z))


def _moba(qkv, za, rel_bias):
    bsz, seq, _ = qkv.shape
    n_blk = seq // MOBA_BLOCK
    assert n_blk % 2 == 0
    half = n_blk // 2
    own_b, prev_b = _bucket_tables()
    blk = MOBA_BLOCK
    kernel = functools.partial(_moba_kernel, n_blk=n_blk)
    return pl.pallas_call(
        kernel,
        grid=(N_PAIRS, bsz, half),
        in_specs=[
            pl.BlockSpec((blk, blk), lambda hp, b, i: (0, 0)),
            pl.BlockSpec((blk, blk), lambda hp, b, i: (0, 0)),
            pl.BlockSpec(memory_space=pltpu.SMEM),
            pl.BlockSpec((1, seq, LANES), lambda hp, b, i: (b, 0, hp)),
            pl.BlockSpec((1, seq, LANES), lambda hp, b, i: (b, 0, N_PAIRS + hp)),
            pl.BlockSpec((1, seq, LANES), lambda hp, b, i: (b, 0, 2 * N_PAIRS + hp)),
            pl.BlockSpec((1, blk, LANES), lambda hp, b, i: (b, i, hp)),
            pl.BlockSpec((1, blk, LANES), lambda hp, b, i: (b, n_blk - 1 - i, hp)),
        ],
        out_specs=[
            pl.BlockSpec((1, blk, LANES), lambda hp, b, i: (b, i, hp)),
            pl.BlockSpec((1, blk, LANES), lambda hp, b, i: (b, half - 1 - i, hp)),
        ],
        out_shape=[
            jax.ShapeDtypeStruct((bsz, seq // 2, WIDTH), F32),
            jax.ShapeDtypeStruct((bsz, seq // 2, WIDTH), F32),
        ],
        scratch_shapes=[
            pltpu.VMEM((2, 3, blk, blk), F32),
            pltpu.VMEM((n_blk, LANES), F32),
            pltpu.VMEM((2, n_blk, blk, LANES), BF16),
            pltpu.VMEM((2, n_blk, VT_ROWS, blk), BF16),
            pltpu.VMEM((2, n_blk, blk, LANES), BF16),
            pltpu.VMEM((2, n_blk + 1, blk, blk), F32),
            pltpu.VMEM((2, 2, 8, blk), F32),
            pltpu.VMEM((2, 2, VT_ROWS, blk), F32),
        ],
        compiler_params=pltpu.CompilerParams(
            dimension_semantics=("arbitrary", "arbitrary", "arbitrary"), vmem_limit_bytes=VMEM_LIMIT),
        name="moba",
    )(jnp.asarray(own_b), jnp.asarray(prev_b), rel_bias, qkv, qkv, qkv, za, za)


RW_CHUNKS = 2


def _rwkv_kernel(rw_ref, mu_ref, w0_ref, lora_hi_ref, lora_lo_ref, a0_ref, kk_ref, ka_ref, rk_ref,
                 lnw_ref, lnb_ref, ltri_ref, ones_ref, o_ref,
                 prev_row, state, t_s, p_s, arb_s, ayk_s, at_s, vst_s, bkh_s, x1_s, loc_s, y_s):
    c = pl.program_id(1)
    C = CHUNK
    NC = RW_CHUNKS
    R = NC * C

    @pl.when(c == 0)
    def _reset():
        prev_row[...] = jnp.zeros_like(prev_row)
        state[...] = jnp.zeros_like(state)

    cols = rw_ref[0]
    row = lax.broadcasted_iota(jnp.int32, (R, 1), 0)
    shifted = jnp.where(row == 0, prev_row[0:1, :], pltpu.roll(cols, 1, axis=0))
    prev_row[0:1, :] = cols[R - 1:R, :]
    xs = cols + (shifted - cols) * mu_ref[...]

    r = xs[:, 0:WIDTH]
    k = xs[:, WIDTH:2 * WIDTH]
    v = xs[:, 2 * WIDTH:3 * WIDTH]
    z = xs[:, 3 * WIDTH:4 * WIDTH]
    lo_in = xs[:, 4 * WIDTH:4 * WIDTH + LANES]
    lane = lax.broadcasted_iota(jnp.int32, (1, LANES), 1)
    first_half = lane < HEAD_DIM
    lo_in = jnp.where(first_half, jnp.tanh(lo_in), lo_in)
    x_hi, x_lo = _split2(lo_in)
    lora = (_dot(x_hi, lora_hi_ref[...]) + _dot(x_lo, lora_hi_ref[...]) + _dot(x_hi, lora_lo_ref[...]))
    dw = lora[:, 0:WIDTH]
    da = lora[:, WIDTH:2 * WIDTH]

    t = -(w0_ref[...] + dw)
    softplus = jnp.maximum(t, 0.0) + jnp.log(1.0 + jnp.exp(-jnp.abs(t)))
    w_log = -softplus - 0.5
    lw = -jnp.exp(w_log)
    a_ic = _sigmoid(a0_ref[...] + da)

    ones_bd = ones_ref[...]

    def seg_sum(x):
        outs = []
        for p in range(N_PAIRS):
            hi, lo = _split2(x[:, p * LANES:(p + 1) * LANES])
            outs.append(_dot(hi, ones_bd) + _dot(lo, ones_bd))
        return jnp.concatenate(outs, axis=1)

    kk = k * kk_ref[...]
    kk = kk / jnp.maximum(jnp.sqrt(seg_sum(kk * kk)), 1e-12)
    k2 = k * (1.0 + (a_ic - 1.0) * ka_ref[...])
    a_vec = -kk
    b_vec = kk * a_ic

    l1, l2, l3 = _split3(lw)
    ltri = ltri_ref[...]
    g_inc = _dot(ltri, l1) + _dot(ltri, l2) + _dot(ltri, l3)
    g_exc = g_inc - lw
    g_end_rows = [g_inc[(n + 1) * C - 1:(n + 1) * C, :] for n in range(NC)]
    g_end = jnp.concatenate([jnp.broadcast_to(g, (C, WIDTH)) for g in g_end_rows], axis=0)
    e_inc = jnp.exp(g_inc)
    e_neg = jnp.exp(-g_inc)
    e_end = jnp.exp(g_end - g_inc)
    rt = r * e_inc
    at = a_vec * jnp.exp(g_exc)
    bt = b_vec * e_neg
    kt = k2 * e_neg
    bh = b_vec * e_end
    kh = k2 * e_end
    decay_end = [jnp.exp(g) for g in g_end_rows]

    second_half = jnp.logical_not(first_half)

    def stack(x):
        return jnp.concatenate([jnp.where(first_half, x, 0.0), jnp.where(second_half, x, 0.0)], axis=0)

    ri = lax.broadcasted_iota(jnp.int32, (2 * C, 2 * C), 0)
    ci = lax.broadcasted_iota(jnp.int32, (2 * C, 2 * C), 1)
    same_head = (ri < C) == (ci < C)
    incl = same_head & (ci <= ri)
    strict = same_head & (ci < ri)
    eye = (ri == ci).astype(F32)
    items = [(n, p) for n in range(NC) for p in range(N_PAIRS)]
    H = 2 * C

    for i, (n, p) in enumerate(items):
        rows = slice(n * C, (n + 1) * C)
        sl = slice(p * LANES, (p + 1) * LANES)
        r_st = stack(rt[rows, sl]).astype(BF16)
        a_st = stack(at[rows, sl]).astype(BF16)
        ra = jnp.concatenate([r_st, a_st], axis=0)
        bk = jnp.concatenate([stack(bt[rows, sl]), stack(kt[rows, sl])], axis=0).astype(BF16)
        m_all = _dot_nt(ra, bk)
        n_ab = jnp.where(strict, m_all[H:2 * H, 0:H], 0.0)
        t_s[i] = eye + n_ab
        p_s[i] = n_ab.astype(BF16)
        arb_s[i] = jnp.where(incl, m_all[0:H, 0:H], 0.0).astype(BF16)
        ayk_s[i, 0:H, :] = jnp.where(incl, m_all[0:H, H:2 * H], 0.0).astype(BF16)
        ayk_s[i, H:2 * H, :] = jnp.where(strict, m_all[H:2 * H, H:2 * H], 0.0).astype(BF16)
        x1_s[i, 0:H, :] = r_st
        at_s[i] = a_st
        vst_s[i] = stack(v[rows, sl]).astype(BF16)
        bkh_s[i] = jnp.concatenate([stack(bh[rows, sl]), stack(kh[rows, sl])], axis=0).astype(BF16)

    for i in range(len(items)):
        pb = p_s[i]
        p_s[i] = _dot(pb, pb).astype(BF16)
    for _ in range(int(math.log2(C)) - 2):
        for i in range(len(items)):
            pb = p_s[i]
            tb = t_s[i]
            both = _dot(pb, jnp.concatenate([tb.astype(BF16), pb], axis=1))
            t_s[i] = tb + both[:, 0:H]
            p_s[i] = both[:, H:2 * H].astype(BF16)
    for i in range(len(items)):
        tb = t_s[i]
        t_s[i] = tb + _dot(p_s[i], tb.astype(BF16))

    for i in range(len(items)):
        yk = _dot(ayk_s[i], vst_s[i])
        wz = _dot(t_s[i].astype(BF16), jnp.concatenate([at_s[i], yk[H:2 * H, :].astype(BF16)], axis=1))
        x1_s[i, H:2 * H, :] = wz[:, 0:LANES].astype(BF16)
        loc_s[i, 0:H, :] = yk[0:H, :]
        loc_s[i, H:2 * H, :] = wz[:, LANES:2 * LANES]

    for i, (n, p) in enumerate(items):
        sl = slice(p * LANES, (p + 1) * LANES)
        s_t = state[p]
        rs = _dot_nt(x1_s[i], s_t.astype(BF16))
        u_bf = (rs[H:2 * H, :] + loc_s[i, H:2 * H, :]).astype(BF16)
        y_st = rs[0:H, :] + loc_s[i, 0:H, :] + _dot(arb_s[i], u_bf)
        y_s[n * C:(n + 1) * C, sl] = y_st[0:C, :] + y_st[C:H, :]
        uv = jnp.concatenate([u_bf, vst_s[i]], axis=0)
        state[p] = s_t * decay_end[n][:, sl] + _dot_tn(uv, bkh_s[i])
    y = y_s[...]

    inv_n = 1.0 / HEAD_DIM
    mean = seg_sum(y) * inv_n
    d = y - mean
    var = seg_sum(d * d) * inv_n
    yn = d * lax.rsqrt(var + GN_EPS) * lnw_ref[...] + lnb_ref[...]
    bonus = seg_sum(r * k2 * rk_ref[...]) * v
    o_ref[0] = (yn + bonus) * (z * _sigmoid(z))


def _rwkv(rw, mu, w0, w_up, a0, a_up, k_k, k_a, r_k, ln_w, ln_b):
    bsz, seq, _ = rw.shape
    C = CHUNK
    row = lambda t: t.reshape(1, -1).astype(F32)
    lora = jnp.zeros((LANES, 2 * WIDTH), F32)
    lora = lora.at[:LORA_RANK, :WIDTH].set(w_up).at[LORA_RANK:, WIDTH:].set(a_up)
    lora_hi = lora.astype(BF16)
    lora_lo = (lora - lora_hi.astype(F32)).astype(BF16)
    R = RW_CHUNKS * C
    G = RW_CHUNKS * N_PAIRS
    H = 2 * C
    assert seq % R == 0 and H == LANES
    ltri = jnp.asarray(np.kron(np.eye(RW_CHUNKS), np.tril(np.ones((C, C)))).astype(np.float32)).astype(BF16)
    hid = np.arange(LANES) // HEAD_DIM
    ones_bd = jnp.asarray((hid[:, None] == hid[None, :]).astype(np.float32)).astype(BF16)
    vec = lambda n: pl.BlockSpec((1, n), lambda b, c: (0, 0))
    return pl.pallas_call(
        _rwkv_kernel,
        grid=(bsz, seq // R),
        in_specs=[
            pl.BlockSpec((1, R, RW_COLS), lambda b, c: (b, c, 0)),
            vec(RW_COLS), vec(WIDTH),
            pl.BlockSpec((LANES, 2 * WIDTH), lambda b, c: (0, 0)),
            pl.BlockSpec((LANES, 2 * WIDTH), lambda b, c: (0, 0)),
            vec(WIDTH), vec(WIDTH), vec(WIDTH), vec(WIDTH), vec(WIDTH), vec(WIDTH),
            pl.BlockSpec((R, R), lambda b, c: (0, 0)),
            pl.BlockSpec((LANES, LANES), lambda b, c: (0, 0)),
        ],
        out_specs=pl.BlockSpec((1, R, WIDTH), lambda b, c: (b, c, 0)),
        out_shape=jax.ShapeDtypeStruct((bsz, seq, WIDTH), F32),
        scratch_shapes=[
            pltpu.VMEM((8, RW_COLS), F32),
            pltpu.VMEM((N_PAIRS, LANES, LANES), F32),
            pltpu.VMEM((G, H, H), F32),
            pltpu.VMEM((G, H, H), BF16),
            pltpu.VMEM((G, H, H), BF16),
            pltpu.VMEM((G, 2 * H, H), BF16),
            pltpu.VMEM((G, H, LANES), BF16),
            pltpu.VMEM((G, H, LANES), BF16),
            pltpu.VMEM((G, 2 * H, LANES), BF16),
            pltpu.VMEM((G, 2 * H, LANES), BF16),
            pltpu.VMEM((G, 2 * H, LANES), F32),
            pltpu.VMEM((R, WIDTH), F32),
        ],
        compiler_params=pltpu.CompilerParams(
            dimension_semantics=("arbitrary", "arbitrary"), vmem_limit_bytes=VMEM_LIMIT),
        name="rwkv",
    )(rw, row(mu), row(w0), lora_hi, lora_lo, row(a0), row(k_k), row(k_a), row(r_k), row(ln_w), row(ln_b),
      ltri, ones_bd)


def _merge_kernel(ya_lo_ref, ya_hi_ref, yb_ref, ga_ref, gb_ref, x_ref, p_ref, pa_ref, pb_ref, wo_ref, gpost_ref,
                  wpu_ref, wpg_ref, o_ref, *, tiles_per_half):
    first_half = pl.program_id(1) < tiles_per_half
    ya = jnp.where(first_half, ya_lo_ref[0], ya_hi_ref[0])
    ma = _dot(ya.astype(BF16), pa_ref[...])
    mb = _dot(yb_ref[0].astype(BF16), pb_ref[...])
    merged = _sigmoid(ga_ref[0]) * ma + _sigmoid(gb_ref[0]) * mb
    y = _dot(merged.astype(BF16), wo_ref[...])
    ms = jnp.mean(y * y, axis=-1, keepdims=True)
    h = x_ref[0] + y * lax.rsqrt(ms + RMS_EPS) * gpost_ref[...]
    e = _dot(p_ref[0].astype(BF16), wpu_ref[...])
    gate = _dot(h.astype(BF16), wpg_ref[...])
    o_ref[0] = h + _sigmoid(gate) * e


def _merge(ya_lo, ya_hi, yb, gates, x, p, p_a, p_b, w_out, g_post, w_pu, w_pg, tm):
    bsz, seq, _ = x.shape
    th = seq // 2 // tm
    full = lambda a: pl.BlockSpec(a.shape, lambda b, t: (0, 0))
    tile = lambda w, col=0: pl.BlockSpec((1, tm, w), lambda b, t: (b, t, col))
    return pl.pallas_call(
        functools.partial(_merge_kernel, tiles_per_half=th),
        grid=(bsz, seq // tm),
        in_specs=[
            pl.BlockSpec((1, tm, WIDTH), lambda b, t: (b, jnp.minimum(t, th - 1), 0)),
            pl.BlockSpec((1, tm, WIDTH), lambda b, t: (b, jnp.maximum(t - th, 0), 0)),
            tile(WIDTH), tile(D_MODEL, 0), tile(D_MODEL, 1), tile(D_MODEL), tile(PLE_DIM),
            full(p_a), full(p_b), full(w_out), full(g_post), full(w_pu), full(w_pg),
        ],
        out_specs=tile(D_MODEL),
        out_shape=jax.ShapeDtypeStruct((bsz, seq, D_MODEL), F32),
        compiler_params=pltpu.CompilerParams(
            dimension_semantics=("arbitrary", "arbitrary"), vmem_limit_bytes=VMEM_LIMIT),
        name="merge",
    )(ya_lo, ya_hi, yb, gates, gates, x, p, p_a, p_b, w_out, g_post, w_pu, w_pg)


def kernel(x, p, g_pre, w_in, rel_bias, mu_shift, w0, w_up, a0, a_up, k_k, k_a, r_k, ln_x_w, ln_x_b,
           p_a, p_b, w_out, g_post, w_ple_up, w_ple_gate):
    bsz, seq, d = x.shape
    assert d == D_MODEL and seq % MOBA_BLOCK == 0 and seq // MOBA_BLOCK >= MOBA_TOPK
    assert g_pre.shape[0] == 1, "one layer"
    n = bsz * seq
    x2 = x.reshape(n, d).astype(F32)
    qkv, za, rw, gates = _project(x2, g_pre.astype(F32), w_in[0].astype(BF16), tm=256)
    ya_lo, ya_hi = _moba(qkv.reshape(bsz, seq, 3 * WIDTH), za.reshape(bsz, seq, WIDTH), rel_bias.astype(F32))
    yb = _rwkv(rw.reshape(bsz, seq, RW_COLS), mu_shift[0], w0[0], w_up[0], a0[0], a_up[0], k_k[0], k_a[0],
               r_k[0], ln_x_w[0], ln_x_b[0])
    out = _merge(ya_lo, ya_hi, yb, gates.reshape(bsz, seq, G_COLS), x.astype(F32), p[0],
                 p_a[0].astype(BF16), p_b[0].astype(BF16), w_out[0].astype(BF16), g_post.astype(F32),
                 w_ple_up[0].astype(BF16), w_ple_gate[0].astype(BF16), tm=512)
    return out.astype(x.dtype)
```

```python
import functools
import math

import jax
import jax.numpy as jnp
import numpy as np
from jax import lax
from jax.experimental import pallas as pl
from jax.experimental.pallas import tpu as pltpu

F32 = jnp.float32
BF16 = jnp.bfloat16

D_MODEL = 1024
PLE_DIM = 256
RMS_EPS = 1e-6
HEAD_DIM = 64
N_HEADS = 8
WIDTH = N_HEADS * HEAD_DIM
MOBA_BLOCK = 256
MOBA_TOPK = 3
REL_BUCKETS = 32
REL_MAX_EXACT = REL_BUCKETS // 2
REL_MAX_DIST = 128
LORA_RANK = 64
GN_EPS = 64e-5
A_COLS = 4 * WIDTH
RW_COLS = 4 * WIDTH + 2 * LORA_RANK
G_COLS = 2 * D_MODEL
IN_COLS = A_COLS + RW_COLS + G_COLS

LANES = 128
N_PAIRS = N_HEADS // 2
CHUNK = 64
VMEM_LIMIT = 48 * 1024 * 1024


def _dot(a, b):
    return jnp.dot(a, b, preferred_element_type=F32)


def _dot_nt(a, b):
    return lax.dot_general(a, b, (((1,), (1,)), ((), ())), preferred_element_type=F32)


def _dot_tn(a, b):
    return lax.dot_general(a, b, (((0,), (0,)), ((), ())), preferred_element_type=F32)


def _split2(x):
    hi = x.astype(BF16)
    lo = (x - hi.astype(F32)).astype(BF16)
    return hi, lo


def _split3(x):
    h1 = x.astype(BF16)
    r1 = x - h1.astype(F32)
    h2 = r1.astype(BF16)
    h3 = (r1 - h2.astype(F32)).astype(BF16)
    return h1, h2, h3


def _sigmoid(x):
    return 1.0 / (1.0 + jnp.exp(-x))


def _proj_kernel(x_ref, g_ref, w_ref, qkv_ref, za_ref, rw_ref, gt_ref):
    x = x_ref[...]
    ms = jnp.mean(x * x, axis=-1, keepdims=True)
    u = (x * lax.rsqrt(ms + RMS_EPS) * g_ref[...]).astype(BF16)
    step = 512

    def emit(out_ref, col0, width):
        for c in range(0, width, step):
            w = min(step, width - c)
            out_ref[:, c:c + w] = _dot(u, w_ref[:, col0 + c:col0 + c + w]).astype(out_ref.dtype)

    emit(qkv_ref, 0, 3 * WIDTH)
    emit(za_ref, 3 * WIDTH, WIDTH)
    emit(rw_ref, A_COLS, RW_COLS)
    emit(gt_ref, A_COLS + RW_COLS, G_COLS)


def _project(x2, g_pre, w_in_bf, tm):
    n = x2.shape[0]
    return pl.pallas_call(
        _proj_kernel,
        grid=(n // tm,),
        in_specs=[
            pl.BlockSpec((tm, D_MODEL), lambda i: (i, 0)),
            pl.BlockSpec((1, D_MODEL), lambda i: (0, 0)),
            pl.BlockSpec((D_MODEL, IN_COLS), lambda i: (0, 0)),
        ],
        out_specs=[
            pl.BlockSpec((tm, 3 * WIDTH), lambda i: (i, 0)),
            pl.BlockSpec((tm, WIDTH), lambda i: (i, 0)),
            pl.BlockSpec((tm, RW_COLS), lambda i: (i, 0)),
            pl.BlockSpec((tm, G_COLS), lambda i: (i, 0)),
        ],
        out_shape=[
            jax.ShapeDtypeStruct((n, 3 * WIDTH), BF16),
            jax.ShapeDtypeStruct((n, WIDTH), F32),
            jax.ShapeDtypeStruct((n, RW_COLS), F32),
            jax.ShapeDtypeStruct((n, G_COLS), F32),
        ],
        compiler_params=pltpu.CompilerParams(
            dimension_semantics=("arbitrary",), vmem_limit_bytes=VMEM_LIMIT),
        name="proj",
    )(x2, g_pre, w_in_bf)


def _t5_bucket_np(dist):
    n = np.maximum(dist, 0)
    nf = np.maximum(n, 1).astype(np.float32)
    large = REL_MAX_EXACT + (np.log(nf / np.float32(REL_MAX_EXACT)) / np.float32(math.log(REL_MAX_DIST / REL_MAX_EXACT))
                             * np.float32(REL_BUCKETS - REL_MAX_EXACT)).astype(np.int32)
    large = np.minimum(large, REL_BUCKETS - 1)
    return np.where(n < REL_MAX_EXACT, n, large).astype(np.int32)


def _bucket_tables():
    s = np.arange(MOBA_BLOCK)[:, None]
    t = np.arange(MOBA_BLOCK)[None, :]
    own = np.where(t >= s, _t5_bucket_np(t - s), -1).astype(np.int32)
    prev = _t5_bucket_np(MOBA_BLOCK + t - s)
    return own, prev


MASKED = -1e30
VT_ROWS = HEAD_DIM + 16
KIND_FAR, KIND_PREV, KIND_OWN = 0, 1, 2


def _moba_kernel(own_b_ref, prev_b_ref, relb_ref, q_ref, k_ref, v_ref, za_ref, zb_ref, oa_ref, ob_ref,
                 tabs, kmean, kaug, vt, qaug, scores, mx_ref, acc_ref, *, n_blk):
    hp = pl.program_id(0)
    b = pl.program_id(1)
    i = pl.program_id(2)
    blk = MOBA_BLOCK
    seq = n_blk * blk
    lane = lax.broadcasted_iota(jnp.int32, (1, LANES), 1)
    head_mask = [lane < HEAD_DIM, lane >= HEAD_DIM]
    flag_base = [HEAD_DIM, 0]
    neg_inf = jnp.float32(-jnp.inf)

    @pl.when((b == 0) & (i == 0))
    def _build_bias_tables():
        ob = own_b_ref[...]
        pb = prev_b_ref[...]
        for hh in range(2):
            far = relb_ref[REL_BUCKETS - 1, 2 * hp + hh]
            to = jnp.full((blk, blk), neg_inf, F32)
            tp = jnp.zeros((blk, blk), F32)
            for bkt in range(REL_BUCKETS):
                val = relb_ref[bkt, 2 * hp + hh] - far
                to = jnp.where(ob == bkt, val, to)
                tp = jnp.where(pb == bkt, val, tp)
            tabs[hh, KIND_FAR] = jnp.zeros((blk, blk), F32)
            tabs[hh, KIND_PREV] = tp
            tabs[hh, KIND_OWN] = to

    @pl.when(i == 0)
    def _per_sequence_setup():
        ones_row = (lax.broadcasted_iota(jnp.int32, (VT_ROWS - HEAD_DIM, blk), 0) == 0).astype(BF16)
        for j in range(n_blk):
            kj = k_ref[0, j * blk:(j + 1) * blk, :].astype(F32)
            kmean[j:j + 1, :] = jnp.mean(kj, axis=0, keepdims=True)
            vjt = v_ref[0, j * blk:(j + 1) * blk, :].astype(F32).T
            for hh in range(2):
                flag = (lane == flag_base[hh] + j).astype(F32)
                kaug[hh, j] = jnp.where(head_mask[hh], kj, flag).astype(BF16)
                vt[hh, j, 0:HEAD_DIM, :] = vjt[hh * HEAD_DIM:(hh + 1) * HEAD_DIM, :].astype(BF16)
                vt[hh, j, HEAD_DIM:VT_ROWS, :] = ones_row

        q = q_ref[0].astype(F32) * (HEAD_DIM ** -0.5)
        blk_id = lax.broadcasted_iota(jnp.int32, (n_blk, seq), 0)
        q_blk = lax.broadcasted_iota(jnp.int32, (n_blk, seq), 1) // blk
        flag_row = lax.broadcasted_iota(jnp.int32, (n_blk, LANES), 0)
        flag_lane = lax.broadcasted_iota(jnp.int32, (n_blk, LANES), 1)
        km_hi, km_lo = _split2(kmean[...])
        for hh in range(2):
            q_hi, q_lo = _split2(jnp.where(head_mask[hh], q, 0.0))
            g = _dot_nt(km_hi, q_hi) + _dot_nt(km_lo, q_hi) + _dot_nt(km_hi, q_lo)
            g = jnp.where(blk_id < q_blk, g, neg_inf)
            allowed = blk_id == q_blk
            for _ in range(MOBA_TOPK):
                mx = jnp.max(g, axis=0, keepdims=True)
                first = jnp.min(jnp.where(g == mx, blk_id, n_blk), axis=0, keepdims=True)
                hit = (blk_id == first) & (mx > neg_inf)
                allowed = allowed | hit
                g = jnp.where(hit, neg_inf, g)
            pen_t = jnp.where(allowed, 0.0, MASKED).astype(BF16)
            place = (flag_lane == flag_row + flag_base[hh]).astype(BF16)
            pen = _dot_tn(pen_t, place)
            qa = jnp.where(head_mask[hh], q, pen).astype(BF16)
            for j in range(n_blk):
                qaug[hh, j] = qa[j * blk:(j + 1) * blk, :]

    def tile_max(s):
        return jnp.max(s.reshape(blk // 8, 8, blk), axis=0)

    qb_a = i
    qb_b = n_blk - 1 - i
    n_tiles = n_blk + 1

    def tile_ids(t):
        is_a = t <= qb_a
        qb = jnp.where(is_a, qb_a, qb_b)
        kb = jnp.where(is_a, t, t - qb_a - 1)
        kind = jnp.where(kb == qb, KIND_OWN, jnp.where(kb == qb - 1, KIND_PREV, KIND_FAR))
        return is_a, qb, kb, kind

    mx_ref[...] = jnp.full(mx_ref.shape, neg_inf, F32)
    for t in range(n_tiles):
        is_a, qb, kb, kind = tile_ids(t)
        which = jnp.where(is_a, 0, 1)
        for hh in range(2):
            s = _dot_nt(kaug[hh, kb], qaug[hh, qb]) + tabs[hh, kind]
            scores[hh, t] = s
            mx_ref[hh, which] = jnp.maximum(mx_ref[hh, which], tile_max(s))
    m_rows = [[jnp.max(mx_ref[hh, w], axis=0, keepdims=True) for w in range(2)] for hh in range(2)]

    acc_ref[...] = jnp.zeros_like(acc_ref)
    for t in range(n_tiles):
        is_a, qb, kb, kind = tile_ids(t)
        which = jnp.where(is_a, 0, 1)
        for hh in range(2):
            m_row = jnp.where(is_a, m_rows[hh][0], m_rows[hh][1])
            p = jnp.exp(scores[hh, t] - m_row).astype(BF16)
            acc_ref[hh, which] += _dot(vt[hh, kb], p)

    for w, (z_ref, o_ref) in enumerate(((za_ref, oa_ref), (zb_ref, ob_ref))):
        out_t = jnp.concatenate(
            [acc_ref[hh, w, 0:HEAD_DIM, :] / acc_ref[hh, w, HEAD_DIM:HEAD_DIM + 1, :] for hh in range(2)], axis=0)
        z = z_ref[0]
        o_ref[0] = out_t.T * (z * _sigmoid(z))


def _moba(qkv, za, rel_bias):
    bsz, seq, _ = qkv.shape
    n_blk = seq // MOBA_BLOCK
    assert n_blk % 2 == 0
    half = n_blk // 2
    own_b, prev_b = _bucket_tables()
    blk = MOBA_BLOCK
    kernel = functools.partial(_moba_kernel, n_blk=n_blk)
    return pl.pallas_call(
        kernel,
        grid=(N_PAIRS, bsz, half),
        in_specs=[
            pl.BlockSpec((blk, blk), lambda hp, b, i: (0, 0)),
            pl.BlockSpec((blk, blk), lambda hp, b, i: (0, 0)),
            pl.BlockSpec(memory_space=pltpu.SMEM),
            pl.BlockSpec((1, seq, LANES), lambda hp, b, i: (b, 0, hp)),
            pl.BlockSpec((1, seq, LANES), lambda hp, b, i: (b, 0, N_PAIRS + hp)),
            pl.BlockSpec((1, seq, LANES), lambda hp, b, i: (b, 0, 2 * N_PAIRS + hp)),
            pl.BlockSpec((1, blk, LANES), lambda hp, b, i: (b, i, hp)),
            pl.BlockSpec((1, blk, LANES), lambda hp, b, i: (b, n_blk - 1 - i, hp)),
        ],
        out_specs=[
            pl.BlockSpec((1, blk, LANES), lambda hp, b, i: (b, i, hp)),
            pl.BlockSpec((1, blk, LANES), lambda hp, b, i: (b, half - 1 - i, hp)),
        ],
        out_shape=[
            jax.ShapeDtypeStruct((bsz, seq // 2, WIDTH), F32),
            jax.ShapeDtypeStruct((bsz, seq // 2, WIDTH), F32),
        ],
        scratch_shapes=[
            pltpu.VMEM((2, 3, blk, blk), F32),
            pltpu.VMEM((n_blk, LANES), F32),
            pltpu.VMEM((2, n_blk, blk, LANES), BF16),
            pltpu.VMEM((2, n_blk, VT_ROWS, blk), BF16),
            pltpu.VMEM((2, n_blk, blk, LANES), BF16),
            pltpu.VMEM((2, n_blk + 1, blk, blk), F32),
            pltpu.VMEM((2, 2, 8, blk), F32),
            pltpu.VMEM((2, 2, VT_ROWS, blk), F32),
        ],
        compiler_params=pltpu.CompilerParams(
            dimension_semantics=("arbitrary", "arbitrary", "arbitrary"), vmem_limit_bytes=VMEM_LIMIT),
        name="moba",
    )(jnp.asarray(own_b), jnp.asarray(prev_b), rel_bias, qkv, qkv, qkv, za, za)


RW_CHUNKS = 4
RW_GROUP_CHUNKS = 4


def _rwkv_kernel(rw_ref, mu_ref, w0_ref, lora_hi_ref, lora_lo_ref, a0_ref, kk_ref, ka_ref, rk_ref,
                 lnw_ref, lnb_ref, ltri_ref, ones_ref, o_ref,
                 prev_row, state, t_s, p_s, arb_s, ayk_s, at_s, vst_s, bkh_s, x1_s, loc_s, y_s,
                 lc_s, nc_s, dm_s, h0_s, akv_s):
    c = pl.program_id(1)
    C = CHUNK
    NC = RW_CHUNKS
    R = NC * C

    @pl.when(c == 0)
    def _reset():
        prev_row[...] = jnp.zeros_like(prev_row)
        state[...] = jnp.zeros_like(state)

    cols = rw_ref[0]
    row = lax.broadcasted_iota(jnp.int32, (R, 1), 0)
    shifted = jnp.where(row == 0, prev_row[0:1, :], pltpu.roll(cols, 1, axis=0))
    prev_row[0:1, :] = cols[R - 1:R, :]
    xs = cols + (shifted - cols) * mu_ref[...]

    r = xs[:, 0:WIDTH]
    k = xs[:, WIDTH:2 * WIDTH]
    v = xs[:, 2 * WIDTH:3 * WIDTH]
    z = xs[:, 3 * WIDTH:4 * WIDTH]
    lo_in = xs[:, 4 * WIDTH:4 * WIDTH + LANES]
    lane = lax.broadcasted_iota(jnp.int32, (1, LANES), 1)
    first_half = lane < HEAD_DIM
    lo_in = jnp.where(first_half, jnp.tanh(lo_in), lo_in)
    x_hi, x_lo = _split2(lo_in)
    lora = (_dot(x_hi, lora_hi_ref[...]) + _dot(x_lo, lora_hi_ref[...]) + _dot(x_hi, lora_lo_ref[...]))
    dw = lora[:, 0:WIDTH]
    da = lora[:, WIDTH:2 * WIDTH]

    lw = (-math.exp(-0.5)) * _sigmoid(w0_ref[...] + dw)
    a_ic = _sigmoid(a0_ref[...] + da)

    ones_bd = ones_ref[...]

    def seg_sum(x):
        outs = []
        for p in range(N_PAIRS):
            outs.append(_dot(x[:, p * LANES:(p + 1) * LANES].astype(BF16), ones_bd))
        return jnp.concatenate(outs, axis=1)

    kk = k * kk_ref[...]
    kk = kk * lax.rsqrt(jnp.maximum(seg_sum(kk * kk), 1e-24))
    k2 = k * (1.0 + (a_ic - 1.0) * ka_ref[...])
    b_vec = kk * a_ic

    l1, l2 = _split2(lw)
    ltri = ltri_ref[...]
    g_inc = _dot(ltri, l1) + _dot(ltri, l2)
    g_end_rows = [g_inc[(n + 1) * C - 1:(n + 1) * C, :] for n in range(NC)]
    decay_end = [jnp.exp(g) for g in g_end_rows]
    d_end = jnp.concatenate([jnp.broadcast_to(d, (C, WIDTH)) for d in decay_end], axis=0)
    e_inc = jnp.exp(g_inc)
    e_neg = 1.0 / e_inc
    e_end = d_end * e_neg
    rt = r * e_inc
    at = kk * (-e_inc * jnp.exp(-lw))
    bt = b_vec * e_neg
    kt = k2 * e_neg
    bh = b_vec * e_end
    kh = k2 * e_end

    second_half = jnp.logical_not(first_half)

    def stack(x):
        return jnp.concatenate([jnp.where(first_half, x, 0.0), jnp.where(second_half, x, 0.0)], axis=0)

    def twice(x):
        return jnp.concatenate([x, x], axis=0)

    ri = lax.broadcasted_iota(jnp.int32, (2 * C, 2 * C), 0)
    ci = lax.broadcasted_iota(jnp.int32, (2 * C, 2 * C), 1)
    same_head = (ri < C) == (ci < C)
    incl = same_head & (ci <= ri)
    strict = same_head & (ci < ri)
    eye = (ri == ci).astype(F32)
    items = [(n, p) for n in range(NC) for p in range(N_PAIRS)]
    H = 2 * C

    def state_independent(group):
        for i, n, p in group:
            rows = slice(n * C, (n + 1) * C)
            sl = slice(p * LANES, (p + 1) * LANES)
            r_st = stack(rt[rows, sl]).astype(BF16)
            a_st = stack(at[rows, sl]).astype(BF16)
            ra = jnp.concatenate([r_st, a_st], axis=0)
            bk = jnp.concatenate([twice(bt[rows, sl].astype(BF16)), twice(kt[rows, sl].astype(BF16))], axis=0)
            m_all = _dot_nt(ra, bk)
            n_ab = jnp.where(strict, m_all[H:2 * H, 0:H], 0.0)
            t_s[i] = eye + n_ab
            p_s[i] = n_ab.astype(BF16)
            arb_s[i] = jnp.where(incl, m_all[0:H, 0:H], 0.0).astype(BF16)
            ayk_s[i, 0:H, :] = jnp.where(incl, m_all[0:H, H:2 * H], 0.0).astype(BF16)
            ayk_s[i, H:2 * H, :] = jnp.where(strict, m_all[H:2 * H, H:2 * H], 0.0).astype(BF16)
            x1_s[i, 0:H, :] = r_st
            at_s[i] = a_st
            vst_s[i] = twice(v[rows, sl].astype(BF16))
            bkh_s[i] = jnp.concatenate([stack(bh[rows, sl]), stack(kh[rows, sl])], axis=0).astype(BF16)

        for i, _, _ in group:
            pb = p_s[i]
            p_s[i] = _dot(pb, pb).astype(BF16)
        for _ in range(int(math.log2(C)) - 2):
            for i, _, _ in group:
                pb = p_s[i]
                tb = t_s[i]
                both = _dot(pb, jnp.concatenate([tb.astype(BF16), pb], axis=1))
                t_s[i] = tb + both[:, 0:H]
                p_s[i] = both[:, H:2 * H].astype(BF16)
        for i, _, _ in group:
            tb = t_s[i]
            t_s[i] = tb + _dot(p_s[i], tb.astype(BF16))

        for i, n, p in group:
            sl = slice(p * LANES, (p + 1) * LANES)
            yk = _dot(ayk_s[i], vst_s[i])
            loc_s[i, 0:H, :] = yk[0:H, :]
            akv_s[i] = yk[H:2 * H, :].astype(BF16)
            dm_s[i] = jnp.broadcast_to(decay_end[n][:, sl], (LANES, LANES)).T
        for i, _, _ in group:
            wz = _dot(t_s[i].astype(BF16), jnp.concatenate([at_s[i], akv_s[i]], axis=1))
            x1_s[i, H:2 * H, :] = wz[:, 0:LANES].astype(BF16)
            loc_s[i, H:2 * H, :] = wz[:, LANES:2 * LANES]
        for i, _, _ in group:
            w_bf = x1_s[i, H:2 * H, :]
            u_loc = loc_s[i, H:2 * H, :]
            lc_s[i] = _dot_tn(bkh_s[i, 0:H, :], w_bf).astype(BF16)
            nc_s[i] = _dot_tn(bkh_s[i], jnp.concatenate([u_loc.astype(BF16), vst_s[i]], axis=0))

    indexed = [(i, n, p) for i, (n, p) in enumerate(items)]
    per_group = RW_GROUP_CHUNKS * N_PAIRS
    for g0 in range(0, len(indexed), per_group):
        state_independent(indexed[g0:g0 + per_group])

    for i, (n, p) in enumerate(items):
        h0 = state[p]
        h0_bf = h0.astype(BF16)
        h0_s[i] = h0_bf
        state[p] = dm_s[i] * h0 + _dot(lc_s[i], h0_bf) + nc_s[i]

    for i, (n, p) in enumerate(items):
        rs = _dot(x1_s[i], h0_s[i])
        akv_s[i] = (rs[H:2 * H, :] + loc_s[i, H:2 * H, :]).astype(BF16)
        loc_s[i, 0:H, :] = rs[0:H, :] + loc_s[i, 0:H, :]
    for i, (n, p) in enumerate(items):
        sl = slice(p * LANES, (p + 1) * LANES)
        y_st = loc_s[i, 0:H, :] + _dot(arb_s[i], akv_s[i])
        y_s[n * C:(n + 1) * C, sl] = jnp.where(first_half, y_st[0:C, :], y_st[C:H, :])
    y = y_s[...]

    inv_n = 1.0 / HEAD_DIM
    mean = seg_sum(y) * inv_n
    d = y - mean
    var = seg_sum(d * d) * inv_n
    yn = d * lax.rsqrt(var + GN_EPS) * lnw_ref[...] + lnb_ref[...]
    bonus = seg_sum(r * k2 * rk_ref[...]) * v
    o_ref[0] = (yn + bonus) * (z * _sigmoid(z))


def _rwkv(rw, mu, w0, w_up, a0, a_up, k_k, k_a, r_k, ln_w, ln_b):
    bsz, seq, _ = rw.shape
    C = CHUNK
    row = lambda t: t.reshape(1, -1).astype(F32)
    lora = jnp.zeros((LANES, 2 * WIDTH), F32)
    lora = lora.at[:LORA_RANK, :WIDTH].set(w_up).at[LORA_RANK:, WIDTH:].set(a_up)
    lora_hi = lora.astype(BF16)
    lora_lo = (lora - lora_hi.astype(F32)).astype(BF16)
    R = RW_CHUNKS * C
    G = RW_CHUNKS * N_PAIRS
    H = 2 * C
    assert seq % R == 0 and H == LANES
    ltri = jnp.asarray(np.kron(np.eye(RW_CHUNKS), np.tril(np.ones((C, C)))).astype(np.float32)).astype(BF16)
    hid = np.arange(LANES) // HEAD_DIM
    ones_bd = jnp.asarray((hid[:, None] == hid[None, :]).astype(np.float32)).astype(BF16)
    vec = lambda n: pl.BlockSpec((1, n), lambda b, c: (0, 0))
    return pl.pallas_call(
        _rwkv_kernel,
        grid=(bsz, seq // R),
        in_specs=[
            pl.BlockSpec((1, R, RW_COLS), lambda b, c: (b, c, 0)),
            vec(RW_COLS), vec(WIDTH),
            pl.BlockSpec((LANES, 2 * WIDTH), lambda b, c: (0, 0)),
            pl.BlockSpec((LANES, 2 * WIDTH), lambda b, c: (0, 0)),
            vec(WIDTH), vec(WIDTH), vec(WIDTH), vec(WIDTH), vec(WIDTH), vec(WIDTH),
            pl.BlockSpec((R, R), lambda b, c: (0, 0)),
            pl.BlockSpec((LANES, LANES), lambda b, c: (0, 0)),
        ],
        out_specs=pl.BlockSpec((1, R, WIDTH), lambda b, c: (b, c, 0)),
        out_shape=jax.ShapeDtypeStruct((bsz, seq, WIDTH), F32),
        scratch_shapes=[
            pltpu.VMEM((8, RW_COLS), F32),
            pltpu.VMEM((N_PAIRS, LANES, LANES), F32),
            pltpu.VMEM((G, H, H), F32),
            pltpu.VMEM((G, H, H), BF16),
            pltpu.VMEM((G, H, H), BF16),
            pltpu.VMEM((G, 2 * H, H), BF16),
            pltpu.VMEM((G, H, LANES), BF16),
            pltpu.VMEM((G, H, LANES), BF16),
            pltpu.VMEM((G, 2 * H, LANES), BF16),
            pltpu.VMEM((G, 2 * H, LANES), BF16),
            pltpu.VMEM((G, 2 * H, LANES), F32),
            pltpu.VMEM((R, WIDTH), F32),
            pltpu.VMEM((G, LANES, LANES), BF16),
            pltpu.VMEM((G, LANES, LANES), F32),
            pltpu.VMEM((G, LANES, LANES), F32),
            pltpu.VMEM((G, LANES, LANES), BF16),
            pltpu.VMEM((G, H, LANES), BF16),
        ],
        compiler_params=pltpu.CompilerParams(
            dimension_semantics=("arbitrary", "arbitrary"), vmem_limit_bytes=VMEM_LIMIT),
        name="rwkv",
    )(rw, row(mu), row(w0), lora_hi, lora_lo, row(a0), row(k_k), row(k_a), row(r_k), row(ln_w), row(ln_b),
      ltri, ones_bd)


def _merge_kernel(ya_lo_ref, ya_hi_ref, yb_ref, ga_ref, gb_ref, x_ref, p_ref, pa_ref, pb_ref, wo_ref, gpost_ref,
                  wpu_ref, wpg_ref, o_ref, *, tiles_per_half):
    first_half = pl.program_id(1) < tiles_per_half
    ya = jnp.where(first_half, ya_lo_ref[0], ya_hi_ref[0])
    ma = _dot(ya.astype(BF16), pa_ref[...])
    mb = _dot(yb_ref[0].astype(BF16), pb_ref[...])
    merged = _sigmoid(ga_ref[0]) * ma + _sigmoid(gb_ref[0]) * mb
    y = _dot(merged.astype(BF16), wo_ref[...])
    ms = jnp.mean(y * y, axis=-1, keepdims=True)
    h = x_ref[0] + y * lax.rsqrt(ms + RMS_EPS) * gpost_ref[...]
    e = _dot(p_ref[0].astype(BF16), wpu_ref[...])
    gate = _dot(h.astype(BF16), wpg_ref[...])
    o_ref[0] = h + _sigmoid(gate) * e


def _merge(ya_lo, ya_hi, yb, gates, x, p, p_a, p_b, w_out, g_post, w_pu, w_pg, tm):
    bsz, seq, _ = x.shape
    th = seq // 2 // tm
    full = lambda a: pl.BlockSpec(a.shape, lambda b, t: (0, 0))
    tile = lambda w, col=0: pl.BlockSpec((1, tm, w), lambda b, t: (b, t, col))
    return pl.pallas_call(
        functools.partial(_merge_kernel, tiles_per_half=th),
        grid=(bsz, seq // tm),
        in_specs=[
            pl.BlockSpec((1, tm, WIDTH), lambda b, t: (b, jnp.minimum(t, th - 1), 0)),
            pl.BlockSpec((1, tm, WIDTH), lambda b, t: (b, jnp.maximum(t - th, 0), 0)),
            tile(WIDTH), tile(D_MODEL, 0), tile(D_MODEL, 1), tile(D_MODEL), tile(PLE_DIM),
            full(p_a), full(p_b), full(w_out), full(g_post), full(w_pu), full(w_pg),
        ],
        out_specs=tile(D_MODEL),
        out_shape=jax.ShapeDtypeStruct((bsz, seq, D_MODEL), F32),
        compiler_params=pltpu.CompilerParams(
            dimension_semantics=("arbitrary", "arbitrary"), vmem_limit_bytes=VMEM_LIMIT),
        name="merge",
    )(ya_lo, ya_hi, yb, gates, gates, x, p, p_a, p_b, w_out, g_post, w_pu, w_pg)


def kernel(x, p, g_pre, w_in, rel_bias, mu_shift, w0, w_up, a0, a_up, k_k, k_a, r_k, ln_x_w, ln_x_b,
           p_a, p_b, w_out, g_post, w_ple_up, w_ple_gate):
    bsz, seq, d = x.shape
    assert d == D_MODEL and seq % MOBA_BLOCK == 0 and seq // MOBA_BLOCK >= MOBA_TOPK
    assert g_pre.shape[0] == 1, "one layer"
    n = bsz * seq
    x2 = x.reshape(n, d).astype(F32)
    qkv, za, rw, gates = _project(x2, g_pre.astype(F32), w_in[0].astype(BF16), tm=256)
    ya_lo, ya_hi = _moba(qkv.reshape(bsz, seq, 3 * WIDTH), za.reshape(bsz, seq, WIDTH), rel_bias.astype(F32))
    yb = _rwkv(rw.reshape(bsz, seq, RW_COLS), mu_shift[0], w0[0], w_up[0], a0[0], a_up[0], k_k[0], k_a[0],
               r_k[0], ln_x_w[0], ln_x_b[0])
    out = _merge(ya_lo, ya_hi, yb, gates.reshape(bsz, seq, G_COLS), x.astype(F32), p[0],
                 p_a[0].astype(BF16), p_b[0].astype(BF16), w_out[0].astype(BF16), g_post.astype(F32),
                 w_ple_up[0].astype(BF16), w_ple_gate[0].astype(BF16), tm=512)
    return out.astype(x.dtype)
```

```python
import functools
import math

import jax
import jax.numpy as jnp
import numpy as np
from jax import lax
from jax.experimental import pallas as pl
from jax.experimental.pallas import tpu as pltpu

F32 = jnp.float32
BF16 = jnp.bfloat16

D_MODEL = 1024
PLE_DIM = 256
RMS_EPS = 1e-6
HEAD_DIM = 64
N_HEADS = 8
WIDTH = N_HEADS * HEAD_DIM
MOBA_BLOCK = 256
MOBA_TOPK = 3
REL_BUCKETS = 32
REL_MAX_EXACT = REL_BUCKETS // 2
REL_MAX_DIST = 128
LORA_RANK = 64
GN_EPS = 64e-5
A_COLS = 4 * WIDTH
RW_COLS = 4 * WIDTH + 2 * LORA_RANK
G_COLS = 2 * D_MODEL
IN_COLS = A_COLS + RW_COLS + G_COLS

LANES = 128
N_PAIRS = N_HEADS // 2
CHUNK = 64
VMEM_LIMIT = 48 * 1024 * 1024


def _dot(a, b):
    return jnp.dot(a, b, preferred_element_type=F32)


def _dot_nt(a, b):
    return lax.dot_general(a, b, (((1,), (1,)), ((), ())), preferred_element_type=F32)


def _dot_tn(a, b):
    return lax.dot_general(a, b, (((0,), (0,)), ((), ())), preferred_element_type=F32)


def _split2(x):
    hi = x.astype(BF16)
    lo = (x - hi.astype(F32)).astype(BF16)
    return hi, lo


def _split3(x):
    h1 = x.astype(BF16)
    r1 = x - h1.astype(F32)
    h2 = r1.astype(BF16)
    h3 = (r1 - h2.astype(F32)).astype(BF16)
    return h1, h2, h3


def _sigmoid(x):
    return 1.0 / (1.0 + jnp.exp(-x))


def _proj_kernel(x_ref, g_ref, w_ref, qkv_ref, za_ref, rw_ref, gt_ref):
    x = x_ref[...]
    ms = jnp.mean(x * x, axis=-1, keepdims=True)
    u = (x * lax.rsqrt(ms + RMS_EPS) * g_ref[...]).astype(BF16)
    step = 512

    def emit(out_ref, col0, width):
        for c in range(0, width, step):
            w = min(step, width - c)
            out_ref[:, c:c + w] = _dot(u, w_ref[:, col0 + c:col0 + c + w]).astype(out_ref.dtype)

    emit(qkv_ref, 0, 3 * WIDTH)
    emit(za_ref, 3 * WIDTH, WIDTH)
    emit(rw_ref, A_COLS, RW_COLS)
    emit(gt_ref, A_COLS + RW_COLS, G_COLS)


def _project(x2, g_pre, w_in_bf, tm):
    n = x2.shape[0]
    return pl.pallas_call(
        _proj_kernel,
        grid=(n // tm,),
        in_specs=[
            pl.BlockSpec((tm, D_MODEL), lambda i: (i, 0)),
            pl.BlockSpec((1, D_MODEL), lambda i: (0, 0)),
            pl.BlockSpec((D_MODEL, IN_COLS), lambda i: (0, 0)),
        ],
        out_specs=[
            pl.BlockSpec((tm, 3 * WIDTH), lambda i: (i, 0)),
            pl.BlockSpec((tm, WIDTH), lambda i: (i, 0)),
            pl.BlockSpec((tm, RW_COLS), lambda i: (i, 0)),
            pl.BlockSpec((tm, G_COLS), lambda i: (i, 0)),
        ],
        out_shape=[
            jax.ShapeDtypeStruct((n, 3 * WIDTH), BF16),
            jax.ShapeDtypeStruct((n, WIDTH), F32),
            jax.ShapeDtypeStruct((n, RW_COLS), F32),
            jax.ShapeDtypeStruct((n, G_COLS), F32),
        ],
        compiler_params=pltpu.CompilerParams(
            dimension_semantics=("arbitrary",), vmem_limit_bytes=VMEM_LIMIT),
        name="proj",
    )(x2, g_pre, w_in_bf)


def _t5_bucket_np(dist):
    n = np.maximum(dist, 0)
    nf = np.maximum(n, 1).astype(np.float32)
    large = REL_MAX_EXACT + (np.log(nf / np.float32(REL_MAX_EXACT)) / np.float32(math.log(REL_MAX_DIST / REL_MAX_EXACT))
                             * np.float32(REL_BUCKETS - REL_MAX_EXACT)).astype(np.int32)
    large = np.minimum(large, REL_BUCKETS - 1)
    return np.where(n < REL_MAX_EXACT, n, large).astype(np.int32)


def _bucket_tables():
    s = np.arange(MOBA_BLOCK)[:, None]
    t = np.arange(MOBA_BLOCK)[None, :]
    own = np.where(t >= s, _t5_bucket_np(t - s), -1).astype(np.int32)
    prev = _t5_bucket_np(MOBA_BLOCK + t - s)
    return own, prev


MASKED = -1e30
VT_ROWS = HEAD_DIM + 16
KIND_FAR, KIND_PREV, KIND_OWN = 0, 1, 2


def _moba_kernel(own_b_ref, prev_b_ref, relb_ref, q_ref, k_ref, v_ref, za_ref, zb_ref, oa_ref, ob_ref,
                 tabs, kmean, kaug, vt, qaug, scores, mx_ref, acc_ref, *, n_blk):
    hp = pl.program_id(0)
    b = pl.program_id(1)
    i = pl.program_id(2)
    blk = MOBA_BLOCK
    seq = n_blk * blk
    lane = lax.broadcasted_iota(jnp.int32, (1, LANES), 1)
    head_mask = [lane < HEAD_DIM, lane >= HEAD_DIM]
    flag_base = [HEAD_DIM, 0]
    neg_inf = jnp.float32(-jnp.inf)

    @pl.when((b == 0) & (i == 0))
    def _build_bias_tables():
        ob = own_b_ref[...]
        pb = prev_b_ref[...]
        for hh in range(2):
            far = relb_ref[REL_BUCKETS - 1, 2 * hp + hh]
            to = jnp.full((blk, blk), neg_inf, F32)
            tp = jnp.zeros((blk, blk), F32)
            for bkt in range(REL_BUCKETS):
                val = relb_ref[bkt, 2 * hp + hh] - far
                to = jnp.where(ob == bkt, val, to)
                tp = jnp.where(pb == bkt, val, tp)
            tabs[hh, KIND_FAR] = jnp.zeros((blk, blk), F32)
            tabs[hh, KIND_PREV] = tp
            tabs[hh, KIND_OWN] = to

    @pl.when(i == 0)
    def _per_sequence_setup():
        ones_row = (lax.broadcasted_iota(jnp.int32, (VT_ROWS - HEAD_DIM, blk), 0) == 0).astype(BF16)
        for j in range(n_blk):
            kj = k_ref[0, j * blk:(j + 1) * blk, :].astype(F32)
            kmean[j:j + 1, :] = jnp.mean(kj, axis=0, keepdims=True)
            vjt = v_ref[0, j * blk:(j + 1) * blk, :].astype(F32).T
            for hh in range(2):
                flag = (lane == flag_base[hh] + j).astype(F32)
                kaug[hh, j] = jnp.where(head_mask[hh], kj, flag).astype(BF16)
                vt[hh, j, 0:HEAD_DIM, :] = vjt[hh * HEAD_DIM:(hh + 1) * HEAD_DIM, :].astype(BF16)
                vt[hh, j, HEAD_DIM:VT_ROWS, :] = ones_row

        q = q_ref[0].astype(F32) * (HEAD_DIM ** -0.5)
        blk_id = lax.broadcasted_iota(jnp.int32, (n_blk, seq), 0)
        q_blk = lax.broadcasted_iota(jnp.int32, (n_blk, seq), 1) // blk
        flag_row = lax.broadcasted_iota(jnp.int32, (n_blk, LANES), 0)
        flag_lane = lax.broadcasted_iota(jnp.int32, (n_blk, LANES), 1)
        km_hi, km_lo = _split2(kmean[...])
        for hh in range(2):
            q_hi, q_lo = _split2(jnp.where(head_mask[hh], q, 0.0))
            g = _dot_nt(km_hi, q_hi) + _dot_nt(km_lo, q_hi) + _dot_nt(km_hi, q_lo)
            g = jnp.where(blk_id < q_blk, g, neg_inf)
            allowed = blk_id == q_blk
            for _ in range(MOBA_TOPK):
                mx = jnp.max(g, axis=0, keepdims=True)
                first = jnp.min(jnp.where(g == mx, blk_id, n_blk), axis=0, keepdims=True)
                hit = (blk_id == first) & (mx > neg_inf)
                allowed = allowed | hit
                g = jnp.where(hit, neg_inf, g)
            pen_t = jnp.where(allowed, 0.0, MASKED).astype(BF16)
            place = (flag_lane == flag_row + flag_base[hh]).astype(BF16)
            pen = _dot_tn(pen_t, place)
            qa = jnp.where(head_mask[hh], q, pen).astype(BF16)
            for j in range(n_blk):
                qaug[hh, j] = qa[j * blk:(j + 1) * blk, :]

    def tile_max(s):
        return jnp.max(s.reshape(blk // 8, 8, blk), axis=0)

    qb_a = i
    qb_b = n_blk - 1 - i
    n_tiles = n_blk + 1

    def tile_ids(t):
        is_a = t <= qb_a
        qb = jnp.where(is_a, qb_a, qb_b)
        kb = jnp.where(is_a, t, t - qb_a - 1)
        kind = jnp.where(kb == qb, KIND_OWN, jnp.where(kb == qb - 1, KIND_PREV, KIND_FAR))
        return is_a, qb, kb, kind

    mx_ref[...] = jnp.full(mx_ref.shape, neg_inf, F32)
    for t in range(n_tiles):
        is_a, qb, kb, kind = tile_ids(t)
        which = jnp.where(is_a, 0, 1)
        for hh in range(2):
            s = _dot_nt(kaug[hh, kb], qaug[hh, qb]) + tabs[hh, kind]
            scores[hh, t] = s
            mx_ref[hh, which] = jnp.maximum(mx_ref[hh, which], tile_max(s))
    m_rows = [[jnp.max(mx_ref[hh, w], axis=0, keepdims=True) for w in range(2)] for hh in range(2)]

    acc_ref[...] = jnp.zeros_like(acc_ref)
    for t in range(n_tiles):
        is_a, qb, kb, kind = tile_ids(t)
        which = jnp.where(is_a, 0, 1)
        for hh in range(2):
            m_row = jnp.where(is_a, m_rows[hh][0], m_rows[hh][1])
            p = jnp.exp(scores[hh, t] - m_row).astype(BF16)
            acc_ref[hh, which] += _dot(vt[hh, kb], p)

    for w, (z_ref, o_ref) in enumerate(((za_ref, oa_ref), (zb_ref, ob_ref))):
        out_t = jnp.concatenate(
            [acc_ref[hh, w, 0:HEAD_DIM, :] / acc_ref[hh, w, HEAD_DIM:HEAD_DIM + 1, :] for hh in range(2)], axis=0)
        z = z_ref[0]
        o_ref[0] = out_t.T * (z * _sigmoid(z))


def _moba(qkv, za, rel_bias):
    bsz, seq, _ = qkv.shape
    n_blk = seq // MOBA_BLOCK
    assert n_blk % 2 == 0
    half = n_blk // 2
    own_b, prev_b = _bucket_tables()
    blk = MOBA_BLOCK
    kernel = functools.partial(_moba_kernel, n_blk=n_blk)
    return pl.pallas_call(
        kernel,
        grid=(N_PAIRS, bsz, half),
        in_specs=[
            pl.BlockSpec((blk, blk), lambda hp, b, i: (0, 0)),
            pl.BlockSpec((blk, blk), lambda hp, b, i: (0, 0)),
            pl.BlockSpec(memory_space=pltpu.SMEM),
            pl.BlockSpec((1, seq, LANES), lambda hp, b, i: (b, 0, hp)),
            pl.BlockSpec((1, seq, LANES), lambda hp, b, i: (b, 0, N_PAIRS + hp)),
            pl.BlockSpec((1, seq, LANES), lambda hp, b, i: (b, 0, 2 * N_PAIRS + hp)),
            pl.BlockSpec((1, blk, LANES), lambda hp, b, i: (b, i, hp)),
            pl.BlockSpec((1, blk, LANES), lambda hp, b, i: (b, n_blk - 1 - i, hp)),
        ],
        out_specs=[
            pl.BlockSpec((1, blk, LANES), lambda hp, b, i: (b, i, hp)),
            pl.BlockSpec((1, blk, LANES), lambda hp, b, i: (b, half - 1 - i, hp)),
        ],
        out_shape=[
            jax.ShapeDtypeStruct((bsz, seq // 2, WIDTH), F32),
            jax.ShapeDtypeStruct((bsz, seq // 2, WIDTH), F32),
        ],
        scratch_shapes=[
            pltpu.VMEM((2, 3, blk, blk), F32),
            pltpu.VMEM((n_blk, LANES), F32),
            pltpu.VMEM((2, n_blk, blk, LANES), BF16),
            pltpu.VMEM((2, n_blk, VT_ROWS, blk), BF16),
            pltpu.VMEM((2, n_blk, blk, LANES), BF16),
            pltpu.VMEM((2, n_blk + 1, blk, blk), F32),
            pltpu.VMEM((2, 2, 8, blk), F32),
            pltpu.VMEM((2, 2, VT_ROWS, blk), F32),
        ],
        compiler_params=pltpu.CompilerParams(
            dimension_semantics=("arbitrary", "arbitrary", "arbitrary"), vmem_limit_bytes=VMEM_LIMIT),
        name="moba",
    )(jnp.asarray(own_b), jnp.asarray(prev_b), rel_bias, qkv, qkv, qkv, za, za)


RW_CHUNKS = 8
RW_GROUP_CHUNKS = 2


def _rwkv_kernel(rw_ref, mu_ref, w0_ref, lora_ref, a0_ref, kk_ref, ka_ref, rk_ref,
                 lnw_ref, lnb_ref, ltri_ref, ones_ref, o_ref,
                 prev_row, state, t_s, p_s, arb_s, ayk_s, at_s, vst_s, bkh_s, x1_s, loc_s, y_s,
                 lc_s, nc_s, dm_s, h0_s, akv_s):
    c = pl.program_id(1)
    C = CHUNK
    NC = RW_CHUNKS
    R = NC * C

    @pl.when(c == 0)
    def _reset():
        prev_row[...] = jnp.zeros_like(prev_row)
        state[...] = jnp.zeros_like(state)

    lane = lax.broadcasted_iota(jnp.int32, (1, LANES), 1)
    first_half = lane < HEAD_DIM
    second_half = jnp.logical_not(first_half)
    ones_bd = ones_ref[...]
    GC = RW_GROUP_CHUNKS
    RG = GC * C

    def seg_sum(x):
        outs = []
        for p in range(N_PAIRS):
            outs.append(_dot(x[:, p * LANES:(p + 1) * LANES].astype(BF16), ones_bd))
        return jnp.concatenate(outs, axis=1)

    def prepare(g, pre):
        start = g * RG
        cols = rw_ref[0, start:start + RG, :]
        prev = prev_row[0:1, :] if g == 0 else rw_ref[0, start - 1:start, :]
        row = lax.broadcasted_iota(jnp.int32, (RG, 1), 0)
        shifted = jnp.where(row == 0, prev, pltpu.roll(cols, 1, axis=0))
        xs = cols + (shifted - cols) * mu_ref[...]
        r = xs[:, 0:WIDTH]
        k = xs[:, WIDTH:2 * WIDTH]
        v = xs[:, 2 * WIDTH:3 * WIDTH]
        z = xs[:, 3 * WIDTH:4 * WIDTH]
        pre.update(r=r, v=v, z=z)
        yield

        lo_in = xs[:, 4 * WIDTH:4 * WIDTH + LANES]
        lo_in = jnp.where(first_half, jnp.tanh(lo_in), lo_in)
        lora = _dot(lo_in.astype(BF16), lora_ref[...])
        dw = lora[:, 0:WIDTH]
        da = lora[:, WIDTH:2 * WIDTH]
        lw = (-math.exp(-0.5)) * _sigmoid(w0_ref[...] + dw)
        a_ic = _sigmoid(a0_ref[...] + da)
        yield

        kk = k * kk_ref[...]
        kk = kk * lax.rsqrt(jnp.maximum(seg_sum(kk * kk), 1e-24))
        k2 = k * (1.0 + (a_ic - 1.0) * ka_ref[...])
        b_vec = kk * a_ic
        pre.update(k2=k2)
        yield

        l1, l2 = _split2(lw)
        ltri = ltri_ref[0:RG, 0:RG]
        g_inc = _dot(ltri, l1) + _dot(ltri, l2)
        g_end_rows = [g_inc[(n + 1) * C - 1:(n + 1) * C, :] for n in range(GC)]
        decay_end = [jnp.exp(g) for g in g_end_rows]
        d_end = jnp.concatenate([jnp.broadcast_to(d, (C, WIDTH)) for d in decay_end], axis=0)
        e_inc = jnp.exp(g_inc)
        e_neg = 1.0 / e_inc
        yield

        e_end = d_end * e_neg
        pre.update(
            decay_end=decay_end,
            rt=r * e_inc,
            at=kk * (-e_inc * jnp.exp(-lw)),
            bt=b_vec * e_neg, kt=k2 * e_neg, bh=b_vec * e_end, kh=k2 * e_end)

    def stack(x):
        return jnp.concatenate([jnp.where(first_half, x, 0.0), jnp.where(second_half, x, 0.0)], axis=0)

    def twice(x):
        return jnp.concatenate([x, x], axis=0)

    ri = lax.broadcasted_iota(jnp.int32, (2 * C, 2 * C), 0)
    ci = lax.broadcasted_iota(jnp.int32, (2 * C, 2 * C), 1)
    same_head = (ri < C) == (ci < C)
    incl = same_head & (ci <= ri)
    strict = same_head & (ci < ri)
    eye = (ri == ci).astype(F32)
    items = [(n, p) for n in range(NC) for p in range(N_PAIRS)]
    H = 2 * C

    def state_independent(group, pre):
        rt, at, bt, kt, bh, kh, v, decay_end = (pre[name] for name in
                                                ("rt", "at", "bt", "kt", "bh", "kh", "v", "decay_end"))
        for i, n, p in group:
            rows = slice(n * C, (n + 1) * C)
            sl = slice(p * LANES, (p + 1) * LANES)
            r_st = stack(rt[rows, sl]).astype(BF16)
            a_st = stack(at[rows, sl]).astype(BF16)
            ra = jnp.concatenate([r_st, a_st], axis=0)
            bk = jnp.concatenate([twice(bt[rows, sl].astype(BF16)), twice(kt[rows, sl].astype(BF16))], axis=0)
            m_all = _dot_nt(ra, bk)
            n_ab = jnp.where(strict, m_all[H:2 * H, 0:H], 0.0)
            t_s[i] = eye + n_ab
            p_s[i] = n_ab.astype(BF16)
            arb_s[i] = jnp.where(incl, m_all[0:H, 0:H], 0.0).astype(BF16)
            ayk_s[i, 0:H, :] = jnp.where(incl, m_all[0:H, H:2 * H], 0.0).astype(BF16)
            ayk_s[i, H:2 * H, :] = jnp.where(strict, m_all[H:2 * H, H:2 * H], 0.0).astype(BF16)
            x1_s[i, 0:H, :] = r_st
            at_s[i] = a_st
            vst_s[i] = twice(v[rows, sl].astype(BF16))
            bkh_s[i] = jnp.concatenate([stack(bh[rows, sl]), stack(kh[rows, sl])], axis=0).astype(BF16)

        yield

        for i, _, _ in group:
            pb = p_s[i]
            p_s[i] = _dot(pb, pb).astype(BF16)
        yield
        for _ in range(int(math.log2(C)) - 2):
            for i, _, _ in group:
                pb = p_s[i]
                tb = t_s[i]
                both = _dot(pb, jnp.concatenate([tb.astype(BF16), pb], axis=1))
                t_s[i] = tb + both[:, 0:H]
                p_s[i] = both[:, H:2 * H].astype(BF16)
            yield
        for i, _, _ in group:
            tb = t_s[i]
            t_s[i] = tb + _dot(p_s[i], tb.astype(BF16))
        yield

        for i, n, p in group:
            sl = slice(p * LANES, (p + 1) * LANES)
            yk = _dot(ayk_s[i], vst_s[i])
            loc_s[i, 0:H, :] = yk[0:H, :]
            akv_s[i] = yk[H:2 * H, :].astype(BF16)
            dm_s[i] = jnp.broadcast_to(decay_end[n][:, sl], (LANES, LANES)).T
        yield
        for i, _, _ in group:
            wz = _dot(t_s[i].astype(BF16), jnp.concatenate([at_s[i], akv_s[i]], axis=1))
            x1_s[i, H:2 * H, :] = wz[:, 0:LANES].astype(BF16)
            loc_s[i, H:2 * H, :] = wz[:, LANES:2 * LANES]
        yield
        for i, _, _ in group:
            w_bf = x1_s[i, H:2 * H, :]
            u_loc = loc_s[i, H:2 * H, :]
            lc_s[i] = _dot_tn(bkh_s[i, 0:H, :], w_bf).astype(BF16)
            nc_s[i] = _dot_tn(bkh_s[i], jnp.concatenate([u_loc.astype(BF16), vst_s[i]], axis=0))

    inv_n = 1.0 / HEAD_DIM

    def finish(g, pre):
        mine = [(i, n, p) for i, (n, p) in enumerate(items) if n // GC == g]
        for i, n, p in mine:
            h0 = state[p]
            h0_bf = h0.astype(BF16)
            h0_s[i] = h0_bf
            state[p] = dm_s[i] * h0 + _dot(lc_s[i], h0_bf) + nc_s[i]
            if p == N_PAIRS - 1:
                yield
        for i, n, p in mine:
            rs = _dot(x1_s[i], h0_s[i])
            akv_s[i] = (rs[H:2 * H, :] + loc_s[i, H:2 * H, :]).astype(BF16)
            loc_s[i, 0:H, :] = rs[0:H, :] + loc_s[i, 0:H, :]
        yield
        for i, n, p in mine:
            sl = slice(p * LANES, (p + 1) * LANES)
            y_st = loc_s[i, 0:H, :] + _dot(arb_s[i], akv_s[i])
            y_s[n * C:(n + 1) * C, sl] = jnp.where(first_half, y_st[0:C, :], y_st[C:H, :])
        yield
        rows = slice(g * RG, (g + 1) * RG)
        y = y_s[rows, :]
        mean = seg_sum(y) * inv_n
        d = y - mean
        var = seg_sum(d * d) * inv_n
        yn = d * lax.rsqrt(var + GN_EPS) * lnw_ref[...] + lnb_ref[...]
        bonus = seg_sum(pre["r"] * pre["k2"] * rk_ref[...]) * pre["v"]
        z = pre["z"]
        o_ref[0, rows, :] = (yn + bonus) * (z * _sigmoid(z))

    n_groups = NC // GC
    pres = [{} for _ in range(n_groups)]
    for _ in prepare(0, pres[0]):
        pass
    fin = iter(())
    for g in range(n_groups):
        group = [(i, n - g * GC, p) for i, (n, p) in enumerate(items) if n // GC == g]
        nxt = prepare(g + 1, pres[g + 1]) if g + 1 < n_groups else iter(())
        for _ in state_independent(group, pres[g]):
            next(nxt, None)
            next(fin, None)
        for _ in nxt:
            pass
        for _ in fin:
            pass
        fin = finish(g, pres[g])
    for _ in fin:
        pass
    prev_row[0:1, :] = rw_ref[0, R - 1:R, :]


def _rwkv(rw, mu, w0, w_up, a0, a_up, k_k, k_a, r_k, ln_w, ln_b):
    bsz, seq, _ = rw.shape
    C = CHUNK
    row = lambda t: t.reshape(1, -1).astype(F32)
    lora = jnp.zeros((LANES, 2 * WIDTH), F32)
    lora = lora.at[:LORA_RANK, :WIDTH].set(w_up).at[LORA_RANK:, WIDTH:].set(a_up)
    lora = lora.astype(BF16)
    R = RW_CHUNKS * C
    G = RW_CHUNKS * N_PAIRS
    H = 2 * C
    assert seq % R == 0 and H == LANES
    ltri = jnp.asarray(np.kron(np.eye(RW_CHUNKS), np.tril(np.ones((C, C)))).astype(np.float32)).astype(BF16)
    hid = np.arange(LANES) // HEAD_DIM
    ones_bd = jnp.asarray((hid[:, None] == hid[None, :]).astype(np.float32)).astype(BF16)
    vec = lambda n: pl.BlockSpec((1, n), lambda b, c: (0, 0))
    return pl.pallas_call(
        _rwkv_kernel,
        grid=(bsz, seq // R),
        in_specs=[
            pl.BlockSpec((1, R, RW_COLS), lambda b, c: (b, c, 0)),
            vec(RW_COLS), vec(WIDTH),
            pl.BlockSpec((LANES, 2 * WIDTH), lambda b, c: (0, 0)),
            vec(WIDTH), vec(WIDTH), vec(WIDTH), vec(WIDTH), vec(WIDTH), vec(WIDTH),
            pl.BlockSpec((R, R), lambda b, c: (0, 0)),
            pl.BlockSpec((LANES, LANES), lambda b, c: (0, 0)),
        ],
        out_specs=pl.BlockSpec((1, R, WIDTH), lambda b, c: (b, c, 0)),
        out_shape=jax.ShapeDtypeStruct((bsz, seq, WIDTH), F32),
        scratch_shapes=[
            pltpu.VMEM((8, RW_COLS), F32),
            pltpu.VMEM((N_PAIRS, LANES, LANES), F32),
            pltpu.VMEM((G, H, H), F32),
            pltpu.VMEM((G, H, H), BF16),
            pltpu.VMEM((G, H, H), BF16),
            pltpu.VMEM((G, 2 * H, H), BF16),
            pltpu.VMEM((G, H, LANES), BF16),
            pltpu.VMEM((G, H, LANES), BF16),
            pltpu.VMEM((G, 2 * H, LANES), BF16),
            pltpu.VMEM((G, 2 * H, LANES), BF16),
            pltpu.VMEM((G, 2 * H, LANES), F32),
            pltpu.VMEM((R, WIDTH), F32),
            pltpu.VMEM((G, LANES, LANES), BF16),
            pltpu.VMEM((G, LANES, LANES), F32),
            pltpu.VMEM((G, LANES, LANES), F32),
            pltpu.VMEM((G, LANES, LANES), BF16),
            pltpu.VMEM((G, H, LANES), BF16),
        ],
        compiler_params=pltpu.CompilerParams(
            dimension_semantics=("arbitrary", "arbitrary"), vmem_limit_bytes=VMEM_LIMIT),
        name="rwkv",
    )(rw, row(mu), row(w0), lora, row(a0), row(k_k), row(k_a), row(r_k), row(ln_w), row(ln_b),
      ltri, ones_bd)


def _merge_kernel(ya_lo_ref, ya_hi_ref, yb_ref, ga_ref, gb_ref, x_ref, p_ref, pa_ref, pb_ref, wo_ref, gpost_ref,
                  wpu_ref, wpg_ref, o_ref, *, tiles_per_half):
    first_half = pl.program_id(1) < tiles_per_half
    ya = jnp.where(first_half, ya_lo_ref[0], ya_hi_ref[0])
    ma = _dot(ya.astype(BF16), pa_ref[...])
    mb = _dot(yb_ref[0].astype(BF16), pb_ref[...])
    merged = _sigmoid(ga_ref[0]) * ma + _sigmoid(gb_ref[0]) * mb
    y = _dot(merged.astype(BF16), wo_ref[...])
    ms = jnp.mean(y * y, axis=-1, keepdims=True)
    h = x_ref[0] + y * lax.rsqrt(ms + RMS_EPS) * gpost_ref[...]
    e = _dot(p_ref[0].astype(BF16), wpu_ref[...])
    gate = _dot(h.astype(BF16), wpg_ref[...])
    o_ref[0] = h + _sigmoid(gate) * e


def _merge(ya_lo, ya_hi, yb, gates, x, p, p_a, p_b, w_out, g_post, w_pu, w_pg, tm):
    bsz, seq, _ = x.shape
    th = seq // 2 // tm
    full = lambda a: pl.BlockSpec(a.shape, lambda b, t: (0, 0))
    tile = lambda w, col=0: pl.BlockSpec((1, tm, w), lambda b, t: (b, t, col))
    return pl.pallas_call(
        functools.partial(_merge_kernel, tiles_per_half=th),
        grid=(bsz, seq // tm),
        in_specs=[
            pl.BlockSpec((1, tm, WIDTH), lambda b, t: (b, jnp.minimum(t, th - 1), 0)),
            pl.BlockSpec((1, tm, WIDTH), lambda b, t: (b, jnp.maximum(t - th, 0), 0)),
            tile(WIDTH), tile(D_MODEL, 0), tile(D_MODEL, 1), tile(D_MODEL), tile(PLE_DIM),
            full(p_a), full(p_b), full(w_out), full(g_post), full(w_pu), full(w_pg),
        ],
        out_specs=tile(D_MODEL),
        out_shape=jax.ShapeDtypeStruct((bsz, seq, D_MODEL), F32),
        compiler_params=pltpu.CompilerParams(
            dimension_semantics=("arbitrary", "arbitrary"), vmem_limit_bytes=VMEM_LIMIT),
        name="merge",
    )(ya_lo, ya_hi, yb, gates, gates, x, p, p_a, p_b, w_out, g_post, w_pu, w_pg)


def kernel(x, p, g_pre, w_in, rel_bias, mu_shift, w0, w_up, a0, a_up, k_k, k_a, r_k, ln_x_w, ln_x_b,
           p_a, p_b, w_out, g_post, w_ple_up, w_ple_gate):
    bsz, seq, d = x.shape
    assert d == D_MODEL and seq % MOBA_BLOCK == 0 and seq // MOBA_BLOCK >= MOBA_TOPK
    assert g_pre.shape[0] == 1, "one layer"
    n = bsz * seq
    x2 = x.reshape(n, d).astype(F32)
    qkv, za, rw, gates = _project(x2, g_pre.astype(F32), w_in[0].astype(BF16), tm=256)
    ya_lo, ya_hi = _moba(qkv.reshape(bsz, seq, 3 * WIDTH), za.reshape(bsz, seq, WIDTH), rel_bias.astype(F32))
    yb = _rwkv(rw.reshape(bsz, seq, RW_COLS), mu_shift[0], w0[0], w_up[0], a0[0], a_up[0], k_k[0], k_a[0],
               r_k[0], ln_x_w[0], ln_x_b[0])
    out = _merge(ya_lo, ya_hi, yb, gates.reshape(bsz, seq, G_COLS), x.astype(F32), p[0],
                 p_a[0].astype(BF16), p_b[0].astype(BF16), w_out[0].astype(BF16), g_post.astype(F32),
                 w_ple_up[0].astype(BF16), w_ple_gate[0].astype(BF16), tm=512)
    return out.astype(x.dtype)
```

```python
import functools
import math

import jax
import jax.numpy as jnp
import numpy as np
from jax import lax
from jax.experimental import pallas as pl
from jax.experimental.pallas import tpu as pltpu

F32 = jnp.float32
BF16 = jnp.bfloat16

D_MODEL = 1024
PLE_DIM = 256
RMS_EPS = 1e-6
HEAD_DIM = 64
N_HEADS = 8
WIDTH = N_HEADS * HEAD_DIM
MOBA_BLOCK = 256
MOBA_TOPK = 3
REL_BUCKETS = 32
REL_MAX_EXACT = REL_BUCKETS // 2
REL_MAX_DIST = 128
LORA_RANK = 64
GN_EPS = 64e-5
A_COLS = 4 * WIDTH
RW_COLS = 4 * WIDTH + 2 * LORA_RANK
G_COLS = 2 * D_MODEL
IN_COLS = A_COLS + RW_COLS + G_COLS

LANES = 128
N_PAIRS = N_HEADS // 2
CHUNK = 64
VMEM_LIMIT = 48 * 1024 * 1024


def _dot(a, b):
    return jnp.dot(a, b, preferred_element_type=F32)


def _dot_nt(a, b):
    return lax.dot_general(a, b, (((1,), (1,)), ((), ())), preferred_element_type=F32)


def _dot_tn(a, b):
    return lax.dot_general(a, b, (((0,), (0,)), ((), ())), preferred_element_type=F32)


def _split2(x):
    hi = x.astype(BF16)
    lo = (x - hi.astype(F32)).astype(BF16)
    return hi, lo


def _split3(x):
    h1 = x.astype(BF16)
    r1 = x - h1.astype(F32)
    h2 = r1.astype(BF16)
    h3 = (r1 - h2.astype(F32)).astype(BF16)
    return h1, h2, h3


def _sigmoid(x):
    return 1.0 / (1.0 + jnp.exp(-x))


def _proj_kernel(x_ref, g_ref, w_ref, qkv_ref, za_ref, rw_ref, gt_ref):
    x = x_ref[...]
    ms = jnp.mean(x * x, axis=-1, keepdims=True)
    u = (x * lax.rsqrt(ms + RMS_EPS) * g_ref[...]).astype(BF16)
    step = 512

    def emit(out_ref, col0, width):
        for c in range(0, width, step):
            w = min(step, width - c)
            out_ref[:, c:c + w] = _dot(u, w_ref[:, col0 + c:col0 + c + w]).astype(out_ref.dtype)

    emit(qkv_ref, 0, 3 * WIDTH)
    emit(za_ref, 3 * WIDTH, WIDTH)
    emit(rw_ref, A_COLS, RW_COLS)
    emit(gt_ref, A_COLS + RW_COLS, G_COLS)


def _project(x2, g_pre, w_in_bf, tm):
    n = x2.shape[0]
    return pl.pallas_call(
        _proj_kernel,
        grid=(n // tm,),
        in_specs=[
            pl.BlockSpec((tm, D_MODEL), lambda i: (i, 0)),
            pl.BlockSpec((1, D_MODEL), lambda i: (0, 0)),
            pl.BlockSpec((D_MODEL, IN_COLS), lambda i: (0, 0)),
        ],
        out_specs=[
            pl.BlockSpec((tm, 3 * WIDTH), lambda i: (i, 0)),
            pl.BlockSpec((tm, WIDTH), lambda i: (i, 0)),
            pl.BlockSpec((tm, RW_COLS), lambda i: (i, 0)),
            pl.BlockSpec((tm, G_COLS), lambda i: (i, 0)),
        ],
        out_shape=[
            jax.ShapeDtypeStruct((n, 3 * WIDTH), BF16),
            jax.ShapeDtypeStruct((n, WIDTH), F32),
            jax.ShapeDtypeStruct((n, RW_COLS), F32),
            jax.ShapeDtypeStruct((n, G_COLS), F32),
        ],
        compiler_params=pltpu.CompilerParams(
            dimension_semantics=("arbitrary",), vmem_limit_bytes=VMEM_LIMIT),
        name="proj",
    )(x2, g_pre, w_in_bf)


def _t5_bucket_np(dist):
    n = np.maximum(dist, 0)
    nf = np.maximum(n, 1).astype(np.float32)
    large = REL_MAX_EXACT + (np.log(nf / np.float32(REL_MAX_EXACT)) / np.float32(math.log(REL_MAX_DIST / REL_MAX_EXACT))
                             * np.float32(REL_BUCKETS - REL_MAX_EXACT)).astype(np.int32)
    large = np.minimum(large, REL_BUCKETS - 1)
    return np.where(n < REL_MAX_EXACT, n, large).astype(np.int32)


def _bucket_tables():
    s = np.arange(MOBA_BLOCK)[:, None]
    t = np.arange(MOBA_BLOCK)[None, :]
    own = np.where(t >= s, _t5_bucket_np(t - s), -1).astype(np.int32)
    prev = _t5_bucket_np(MOBA_BLOCK + t - s)
    return own, prev


MASKED = -1e30
VT_ROWS = HEAD_DIM + 16
KIND_FAR, KIND_PREV, KIND_OWN = 0, 1, 2


def _moba_decode(s, bsz, half):
    per_pair = bsz * half
    return s // per_pair, (s % per_pair) // half, s % half


def _moba_kernel(own_b_ref, prev_b_ref, relb_ref, q_ref, k_ref, v_ref, za_ref, zb_ref, oa_ref, ob_ref,
                 tabs, kmean, kaug, vt, qaug, scores0, scores1, mx0, mx1, acc_ref, *, n_blk, bsz):
    half = n_blk // 2
    n_items = N_PAIRS * bsz * half
    step = pl.program_id(0)
    cur = jnp.minimum(step, n_items - 1)
    prv = jnp.maximum(step - 1, 0)
    hp, b, i = _moba_decode(cur, bsz, half)
    hp_p, b_p, i_p = _moba_decode(prv, bsz, half)
    seq_slot = (hp * bsz + b) % 2
    seq_slot_p = (hp_p * bsz + b_p) % 2
    blk = MOBA_BLOCK
    seq = n_blk * blk
    lane = lax.broadcasted_iota(jnp.int32, (1, LANES), 1)
    head_mask = [lane < HEAD_DIM, lane >= HEAD_DIM]
    flag_base = [HEAD_DIM, 0]
    neg_inf = jnp.float32(-jnp.inf)

    @pl.when((b == 0) & (i == 0))
    def _build_bias_tables():
        ob = own_b_ref[...]
        pb = prev_b_ref[...]
        for hh in range(2):
            far = relb_ref[REL_BUCKETS - 1, 2 * hp + hh]
            to = jnp.full((blk, blk), neg_inf, F32)
            tp = jnp.zeros((blk, blk), F32)
            for bkt in range(REL_BUCKETS):
                val = relb_ref[bkt, 2 * hp + hh] - far
                to = jnp.where(ob == bkt, val, to)
                tp = jnp.where(pb == bkt, val, tp)
            tabs[hh, KIND_FAR] = jnp.zeros((blk, blk), F32)
            tabs[hh, KIND_PREV] = tp
            tabs[hh, KIND_OWN] = to

    @pl.when(step == 0)
    def _placeholders_for_first_pass2():
        scores1[...] = jnp.zeros_like(scores1)
        mx1[...] = jnp.zeros_like(mx1)

    @pl.when((i == 0) & (step < n_items))
    def _per_sequence_setup():
        ones_row = (lax.broadcasted_iota(jnp.int32, (VT_ROWS - HEAD_DIM, blk), 0) == 0).astype(BF16)
        for j in range(n_blk):
            kj = k_ref[0, j * blk:(j + 1) * blk, :].astype(F32)
            kmean[j:j + 1, :] = jnp.mean(kj, axis=0, keepdims=True)
            vjt = v_ref[0, j * blk:(j + 1) * blk, :].astype(F32).T
            for hh in range(2):
                flag = (lane == flag_base[hh] + j).astype(F32)
                kaug[hh, j] = jnp.where(head_mask[hh], kj, flag).astype(BF16)
                vt[seq_slot, hh, j, 0:HEAD_DIM, :] = vjt[hh * HEAD_DIM:(hh + 1) * HEAD_DIM, :].astype(BF16)
                vt[seq_slot, hh, j, HEAD_DIM:VT_ROWS, :] = ones_row

        q = q_ref[0].astype(F32) * (HEAD_DIM ** -0.5)
        blk_id = lax.broadcasted_iota(jnp.int32, (n_blk, seq), 0)
        q_blk = lax.broadcasted_iota(jnp.int32, (n_blk, seq), 1) // blk
        flag_row = lax.broadcasted_iota(jnp.int32, (n_blk, LANES), 0)
        flag_lane = lax.broadcasted_iota(jnp.int32, (n_blk, LANES), 1)
        km_hi, km_lo = _split2(kmean[...])
        for hh in range(2):
            q_hi, q_lo = _split2(jnp.where(head_mask[hh], q, 0.0))
            g = _dot_nt(km_hi, q_hi) + _dot_nt(km_lo, q_hi) + _dot_nt(km_hi, q_lo)
            g = jnp.where(blk_id < q_blk, g, neg_inf)
            allowed = blk_id == q_blk
            for _ in range(MOBA_TOPK):
                mx = jnp.max(g, axis=0, keepdims=True)
                first = jnp.min(jnp.where(g == mx, blk_id, n_blk), axis=0, keepdims=True)
                hit = (blk_id == first) & (mx > neg_inf)
                allowed = allowed | hit
                g = jnp.where(hit, neg_inf, g)
            pen_t = jnp.where(allowed, 0.0, MASKED).astype(BF16)
            place = (flag_lane == flag_row + flag_base[hh]).astype(BF16)
            pen = _dot_tn(pen_t, place)
            qa = jnp.where(head_mask[hh], q, pen).astype(BF16)
            for j in range(n_blk):
                qaug[hh, j] = qa[j * blk:(j + 1) * blk, :]

    def tile_max(s):
        return jnp.max(s.reshape(blk // 8, 8, blk), axis=0)

    n_tiles = n_blk + 1

    def tile_ids(t, qb_a):
        qb_b = n_blk - 1 - qb_a
        is_a = t <= qb_a
        qb = jnp.where(is_a, qb_a, qb_b)
        kb = jnp.where(is_a, t, t - qb_a - 1)
        kind = jnp.where(kb == qb, KIND_OWN, jnp.where(kb == qb - 1, KIND_PREV, KIND_FAR))
        return is_a, qb, kb, kind

    def both_passes(sc_w, mx_w, sc_r, mx_r):
        m_rows = [[jnp.max(mx_r[hh, w], axis=0, keepdims=True) for w in range(2)] for hh in range(2)]
        mx_w[...] = jnp.full(mx_w.shape, neg_inf, F32)
        acc_ref[...] = jnp.zeros_like(acc_ref)
        for t in range(n_tiles):
            is_a, qb, kb, kind = tile_ids(t, i)
            which = jnp.where(is_a, 0, 1)
            for hh in range(2):
                s = _dot_nt(kaug[hh, kb], qaug[hh, qb]) + tabs[hh, kind]
                sc_w[hh, t] = s
                mx_w[hh, which] = jnp.maximum(mx_w[hh, which], tile_max(s))
            is_a, _, kb, _ = tile_ids(t, i_p)
            which = jnp.where(is_a, 0, 1)
            for hh in range(2):
                m_row = jnp.where(is_a, m_rows[hh][0], m_rows[hh][1])
                p = jnp.exp(sc_r[hh, t] - m_row).astype(BF16)
                acc_ref[hh, which] += _dot(vt[seq_slot_p, hh, kb], p)

        for w, (z_ref, o_ref) in enumerate(((za_ref, oa_ref), (zb_ref, ob_ref))):
            out_t = jnp.concatenate(
                [acc_ref[hh, w, 0:HEAD_DIM, :] / acc_ref[hh, w, HEAD_DIM:HEAD_DIM + 1, :] for hh in range(2)],
                axis=0)
            z = z_ref[0]
            o_ref[0] = out_t.T * (z * _sigmoid(z))

    @pl.when(step % 2 == 0)
    def _even_step():
        both_passes(scores0, mx0, scores1, mx1)

    @pl.when(step % 2 == 1)
    def _odd_step():
        both_passes(scores1, mx1, scores0, mx0)


def _moba(qkv, za, rel_bias):
    bsz, seq, _ = qkv.shape
    n_blk = seq // MOBA_BLOCK
    assert n_blk % 2 == 0
    half = n_blk // 2
    own_b, prev_b = _bucket_tables()
    blk = MOBA_BLOCK
    n_items = N_PAIRS * bsz * half
    kernel = functools.partial(_moba_kernel, n_blk=n_blk, bsz=bsz)

    def cur(s):
        return _moba_decode(jnp.minimum(s, n_items - 1), bsz, half)

    def prv(s):
        return _moba_decode(jnp.maximum(s - 1, 0), bsz, half)

    def seq_block(col0):
        def index_map(s):
            hp, b, _ = cur(s)
            return b, 0, col0 + hp
        return pl.BlockSpec((1, seq, LANES), index_map)

    def block_a(s):
        hp, b, i = prv(s)
        return b, i, hp

    def block_b_in(s):
        hp, b, i = prv(s)
        return b, n_blk - 1 - i, hp

    def block_b_out(s):
        hp, b, i = prv(s)
        return b, half - 1 - i, hp

    return pl.pallas_call(
        kernel,
        grid=(n_items + 1,),
        in_specs=[
            pl.BlockSpec((blk, blk), lambda s: (0, 0)),
            pl.BlockSpec((blk, blk), lambda s: (0, 0)),
            pl.BlockSpec(memory_space=pltpu.SMEM),
            seq_block(0), seq_block(N_PAIRS), seq_block(2 * N_PAIRS),
            pl.BlockSpec((1, blk, LANES), block_a),
            pl.BlockSpec((1, blk, LANES), block_b_in),
        ],
        out_specs=[
            pl.BlockSpec((1, blk, LANES), block_a),
            pl.BlockSpec((1, blk, LANES), block_b_out),
        ],
        out_shape=[
            jax.ShapeDtypeStruct((bsz, seq // 2, WIDTH), F32),
            jax.ShapeDtypeStruct((bsz, seq // 2, WIDTH), F32),
        ],
        scratch_shapes=[
            pltpu.VMEM((2, 3, blk, blk), F32),
            pltpu.VMEM((n_blk, LANES), F32),
            pltpu.VMEM((2, n_blk, blk, LANES), BF16),
            pltpu.VMEM((2, 2, n_blk, VT_ROWS, blk), BF16),
            pltpu.VMEM((2, n_blk, blk, LANES), BF16),
            pltpu.VMEM((2, n_blk + 1, blk, blk), F32),
            pltpu.VMEM((2, n_blk + 1, blk, blk), F32),
            pltpu.VMEM((2, 2, 8, blk), F32),
            pltpu.VMEM((2, 2, 8, blk), F32),
            pltpu.VMEM((2, 2, VT_ROWS, blk), F32),
        ],
        compiler_params=pltpu.CompilerParams(
            dimension_semantics=("arbitrary",), vmem_limit_bytes=VMEM_LIMIT),
        name="moba",
    )(jnp.asarray(own_b), jnp.asarray(prev_b), rel_bias, qkv, qkv, qkv, za, za)


RW_CHUNKS = 8
RW_GROUP_CHUNKS = 2


def _rwkv_kernel(rw_ref, mu_ref, w0_ref, lora_ref, a0_ref, kk_ref, ka_ref, rk_ref,
                 lnw_ref, lnb_ref, ltri_ref, ones_ref, o_ref,
                 prev_row, state, t_s, p_s, arb_s, ayk_s, at_s, vst_s, bkh_s, x1_s, loc_s, y_s,
                 lc_s, nc_s, dm_s, h0_s, akv_s):
    c = pl.program_id(1)
    C = CHUNK
    NC = RW_CHUNKS
    R = NC * C

    @pl.when(c == 0)
    def _reset():
        prev_row[...] = jnp.zeros_like(prev_row)
        state[...] = jnp.zeros_like(state)

    lane = lax.broadcasted_iota(jnp.int32, (1, LANES), 1)
    first_half = lane < HEAD_DIM
    second_half = jnp.logical_not(first_half)
    ones_bd = ones_ref[...]
    GC = RW_GROUP_CHUNKS
    RG = GC * C

    def seg_sum(x):
        outs = []
        for p in range(N_PAIRS):
            outs.append(_dot(x[:, p * LANES:(p + 1) * LANES].astype(BF16), ones_bd))
        return jnp.concatenate(outs, axis=1)

    def prepare(g, pre):
        start = g * RG
        cols = rw_ref[0, start:start + RG, :]
        prev = prev_row[0:1, :] if g == 0 else rw_ref[0, start - 1:start, :]
        row = lax.broadcasted_iota(jnp.int32, (RG, 1), 0)
        shifted = jnp.where(row == 0, prev, pltpu.roll(cols, 1, axis=0))
        xs = cols + (shifted - cols) * mu_ref[...]
        r = xs[:, 0:WIDTH]
        k = xs[:, WIDTH:2 * WIDTH]
        v = xs[:, 2 * WIDTH:3 * WIDTH]
        z = xs[:, 3 * WIDTH:4 * WIDTH]
        pre.update(r=r, v=v, z=z)
        yield

        lo_in = xs[:, 4 * WIDTH:4 * WIDTH + LANES]
        lo_in = jnp.where(first_half, jnp.tanh(lo_in), lo_in)
        lora = _dot(lo_in.astype(BF16), lora_ref[...])
        dw = lora[:, 0:WIDTH]
        da = lora[:, WIDTH:2 * WIDTH]
        lw = (-math.exp(-0.5)) * _sigmoid(w0_ref[...] + dw)
        a_ic = _sigmoid(a0_ref[...] + da)
        yield

        kk = k * kk_ref[...]
        kk = kk * lax.rsqrt(jnp.maximum(seg_sum(kk * kk), 1e-24))
        k2 = k * (1.0 + (a_ic - 1.0) * ka_ref[...])
        b_vec = kk * a_ic
        pre.update(k2=k2)
        yield

        l1, l2 = _split2(lw)
        ltri = ltri_ref[0:RG, 0:RG]
        g_inc = _dot(ltri, l1) + _dot(ltri, l2)
        g_end_rows = [g_inc[(n + 1) * C - 1:(n + 1) * C, :] for n in range(GC)]
        decay_end = [jnp.exp(g) for g in g_end_rows]
        d_end = jnp.concatenate([jnp.broadcast_to(d, (C, WIDTH)) for d in decay_end], axis=0)
        e_inc = jnp.exp(g_inc)
        e_neg = 1.0 / e_inc
        yield

        e_end = d_end * e_neg
        pre.update(
            decay_end=decay_end,
            rt=r * e_inc,
            at=kk * (-e_inc * jnp.exp(-lw)),
            bt=b_vec * e_neg, kt=k2 * e_neg, bh=b_vec * e_end, kh=k2 * e_end)

    def stack(x):
        return jnp.concatenate([jnp.where(first_half, x, 0.0), jnp.where(second_half, x, 0.0)], axis=0)

    def twice(x):
        return jnp.concatenate([x, x], axis=0)

    ri = lax.broadcasted_iota(jnp.int32, (2 * C, 2 * C), 0)
    ci = lax.broadcasted_iota(jnp.int32, (2 * C, 2 * C), 1)
    same_head = (ri < C) == (ci < C)
    incl = same_head & (ci <= ri)
    strict = same_head & (ci < ri)
    eye = (ri == ci).astype(F32)
    items = [(n, p) for n in range(NC) for p in range(N_PAIRS)]
    H = 2 * C

    def state_independent(group, pre):
        rt, at, bt, kt, bh, kh, v, decay_end = (pre[name] for name in
                                                ("rt", "at", "bt", "kt", "bh", "kh", "v", "decay_end"))
        for i, n, p in group:
            rows = slice(n * C, (n + 1) * C)
            sl = slice(p * LANES, (p + 1) * LANES)
            r_st = stack(rt[rows, sl]).astype(BF16)
            a_st = stack(at[rows, sl]).astype(BF16)
            ra = jnp.concatenate([r_st, a_st], axis=0)
            bk = jnp.concatenate([twice(bt[rows, sl].astype(BF16)), twice(kt[rows, sl].astype(BF16))], axis=0)
            m_all = _dot_nt(ra, bk)
            n_ab = jnp.where(strict, m_all[H:2 * H, 0:H], 0.0)
            t_s[i] = eye + n_ab
            p_s[i] = n_ab.astype(BF16)
            arb_s[i] = jnp.where(incl, m_all[0:H, 0:H], 0.0).astype(BF16)
            ayk_s[i, 0:H, :] = jnp.where(incl, m_all[0:H, H:2 * H], 0.0).astype(BF16)
            ayk_s[i, H:2 * H, :] = jnp.where(strict, m_all[H:2 * H, H:2 * H], 0.0).astype(BF16)
            x1_s[i, 0:H, :] = r_st
            at_s[i] = a_st
            vst_s[i] = twice(v[rows, sl].astype(BF16))
            bkh_s[i] = jnp.concatenate([stack(bh[rows, sl]), stack(kh[rows, sl])], axis=0).astype(BF16)

        yield

        for i, _, _ in group:
            pb = p_s[i]
            p_s[i] = _dot(pb, pb).astype(BF16)
        yield
        for _ in range(int(math.log2(C)) - 2):
            for i, _, _ in group:
                pb = p_s[i]
                tb = t_s[i]
                both = _dot(pb, jnp.concatenate([tb.astype(BF16), pb], axis=1))
                t_s[i] = tb + both[:, 0:H]
                p_s[i] = both[:, H:2 * H].astype(BF16)
            yield
        for i, _, _ in group:
            tb = t_s[i]
            t_s[i] = tb + _dot(p_s[i], tb.astype(BF16))
        yield

        for i, n, p in group:
            sl = slice(p * LANES, (p + 1) * LANES)
            yk = _dot(ayk_s[i], vst_s[i])
            loc_s[i, 0:H, :] = yk[0:H, :]
            akv_s[i] = yk[H:2 * H, :].astype(BF16)
            dm_s[i] = jnp.broadcast_to(decay_end[n][:, sl], (LANES, LANES)).T
        yield
        for i, _, _ in group:
            wz = _dot(t_s[i].astype(BF16), jnp.concatenate([at_s[i], akv_s[i]], axis=1))
            x1_s[i, H:2 * H, :] = wz[:, 0:LANES].astype(BF16)
            loc_s[i, H:2 * H, :] = wz[:, LANES:2 * LANES]
        yield
        for i, _, _ in group:
            w_bf = x1_s[i, H:2 * H, :]
            u_loc = loc_s[i, H:2 * H, :]
            lc_s[i] = _dot_tn(bkh_s[i, 0:H, :], w_bf).astype(BF16)
            nc_s[i] = _dot_tn(bkh_s[i], jnp.concatenate([u_loc.astype(BF16), vst_s[i]], axis=0))

    inv_n = 1.0 / HEAD_DIM

    def finish(g, pre):
        mine = [(i, n, p) for i, (n, p) in enumerate(items) if n // GC == g]
        for i, n, p in mine:
            h0 = state[p]
            h0_bf = h0.astype(BF16)
            h0_s[i] = h0_bf
            state[p] = dm_s[i] * h0 + _dot(lc_s[i], h0_bf) + nc_s[i]
            if p == N_PAIRS - 1:
                yield
        for i, n, p in mine:
            rs = _dot(x1_s[i], h0_s[i])
            akv_s[i] = (rs[H:2 * H, :] + loc_s[i, H:2 * H, :]).astype(BF16)
            loc_s[i, 0:H, :] = rs[0:H, :] + loc_s[i, 0:H, :]
        yield
        for i, n, p in mine:
            sl = slice(p * LANES, (p + 1) * LANES)
            y_st = loc_s[i, 0:H, :] + _dot(arb_s[i], akv_s[i])
            y_s[n * C:(n + 1) * C, sl] = jnp.where(first_half, y_st[0:C, :], y_st[C:H, :])
        yield
        rows = slice(g * RG, (g + 1) * RG)
        y = y_s[rows, :]
        mean = seg_sum(y) * inv_n
        d = y - mean
        var = seg_sum(d * d) * inv_n
        yn = d * lax.rsqrt(var + GN_EPS) * lnw_ref[...] + lnb_ref[...]
        bonus = seg_sum(pre["r"] * pre["k2"] * rk_ref[...]) * pre["v"]
        z = pre["z"]
        o_ref[0, rows, :] = (yn + bonus) * (z * _sigmoid(z))

    n_groups = NC // GC
    pres = [{} for _ in range(n_groups)]
    for _ in prepare(0, pres[0]):
        pass
    fin = iter(())
    for g in range(n_groups):
        group = [(i, n - g * GC, p) for i, (n, p) in enumerate(items) if n // GC == g]
        nxt = prepare(g + 1, pres[g + 1]) if g + 1 < n_groups else iter(())
        for _ in state_independent(group, pres[g]):
            next(nxt, None)
            next(fin, None)
        for _ in nxt:
            pass
        for _ in fin:
            pass
        fin = finish(g, pres[g])
    for _ in fin:
        pass
    prev_row[0:1, :] = rw_ref[0, R - 1:R, :]


def _rwkv(rw, mu, w0, w_up, a0, a_up, k_k, k_a, r_k, ln_w, ln_b):
    bsz, seq, _ = rw.shape
    C = CHUNK
    row = lambda t: t.reshape(1, -1).astype(F32)
    lora = jnp.zeros((LANES, 2 * WIDTH), F32)
    lora = lora.at[:LORA_RANK, :WIDTH].set(w_up).at[LORA_RANK:, WIDTH:].set(a_up)
    lora = lora.astype(BF16)
    R = RW_CHUNKS * C
    G = RW_CHUNKS * N_PAIRS
    H = 2 * C
    assert seq % R == 0 and H == LANES
    ltri = jnp.asarray(np.kron(np.eye(RW_CHUNKS), np.tril(np.ones((C, C)))).astype(np.float32)).astype(BF16)
    hid = np.arange(LANES) // HEAD_DIM
    ones_bd = jnp.asarray((hid[:, None] == hid[None, :]).astype(np.float32)).astype(BF16)
    vec = lambda n: pl.BlockSpec((1, n), lambda b, c: (0, 0))
    return pl.pallas_call(
        _rwkv_kernel,
        grid=(bsz, seq // R),
        in_specs=[
            pl.BlockSpec((1, R, RW_COLS), lambda b, c: (b, c, 0)),
            vec(RW_COLS), vec(WIDTH),
            pl.BlockSpec((LANES, 2 * WIDTH), lambda b, c: (0, 0)),
            vec(WIDTH), vec(WIDTH), vec(WIDTH), vec(WIDTH), vec(WIDTH), vec(WIDTH),
            pl.BlockSpec((R, R), lambda b, c: (0, 0)),
            pl.BlockSpec((LANES, LANES), lambda b, c: (0, 0)),
        ],
        out_specs=pl.BlockSpec((1, R, WIDTH), lambda b, c: (b, c, 0)),
        out_shape=jax.ShapeDtypeStruct((bsz, seq, WIDTH), F32),
        scratch_shapes=[
            pltpu.VMEM((8, RW_COLS), F32),
            pltpu.VMEM((N_PAIRS, LANES, LANES), F32),
            pltpu.VMEM((G, H, H), F32),
            pltpu.VMEM((G, H, H), BF16),
            pltpu.VMEM((G, H, H), BF16),
            pltpu.VMEM((G, 2 * H, H), BF16),
            pltpu.VMEM((G, H, LANES), BF16),
            pltpu.VMEM((G, H, LANES), BF16),
            pltpu.VMEM((G, 2 * H, LANES), BF16),
            pltpu.VMEM((G, 2 * H, LANES), BF16),
            pltpu.VMEM((G, 2 * H, LANES), F32),
            pltpu.VMEM((R, WIDTH), F32),
            pltpu.VMEM((G, LANES, LANES), BF16),
            pltpu.VMEM((G, LANES, LANES), F32),
            pltpu.VMEM((G, LANES, LANES), F32),
            pltpu.VMEM((G, LANES, LANES), BF16),
            pltpu.VMEM((G, H, LANES), BF16),
        ],
        compiler_params=pltpu.CompilerParams(
            dimension_semantics=("arbitrary", "arbitrary"), vmem_limit_bytes=VMEM_LIMIT),
        name="rwkv",
    )(rw, row(mu), row(w0), lora, row(a0), row(k_k), row(k_a), row(r_k), row(ln_w), row(ln_b),
      ltri, ones_bd)


def _merge_kernel(ya_lo_ref, ya_hi_ref, yb_ref, ga_ref, gb_ref, x_ref, p_ref, pa_ref, pb_ref, wo_ref, gpost_ref,
                  wpu_ref, wpg_ref, o_ref, *, tiles_per_half):
    first_half = pl.program_id(1) < tiles_per_half
    ya = jnp.where(first_half, ya_lo_ref[0], ya_hi_ref[0])
    ma = _dot(ya.astype(BF16), pa_ref[...])
    mb = _dot(yb_ref[0].astype(BF16), pb_ref[...])
    merged = _sigmoid(ga_ref[0]) * ma + _sigmoid(gb_ref[0]) * mb
    y = _dot(merged.astype(BF16), wo_ref[...])
    ms = jnp.mean(y * y, axis=-1, keepdims=True)
    h = x_ref[0] + y * lax.rsqrt(ms + RMS_EPS) * gpost_ref[...]
    e = _dot(p_ref[0].astype(BF16), wpu_ref[...])
    gate = _dot(h.astype(BF16), wpg_ref[...])
    o_ref[0] = h + _sigmoid(gate) * e


def _merge(ya_lo, ya_hi, yb, gates, x, p, p_a, p_b, w_out, g_post, w_pu, w_pg, tm):
    bsz, seq, _ = x.shape
    th = seq // 2 // tm
    full = lambda a: pl.BlockSpec(a.shape, lambda b, t: (0, 0))
    tile = lambda w, col=0: pl.BlockSpec((1, tm, w), lambda b, t: (b, t, col))
    return pl.pallas_call(
        functools.partial(_merge_kernel, tiles_per_half=th),
        grid=(bsz, seq // tm),
        in_specs=[
            pl.BlockSpec((1, tm, WIDTH), lambda b, t: (b, jnp.minimum(t, th - 1), 0)),
            pl.BlockSpec((1, tm, WIDTH), lambda b, t: (b, jnp.maximum(t - th, 0), 0)),
            tile(WIDTH), tile(D_MODEL, 0), tile(D_MODEL, 1), tile(D_MODEL), tile(PLE_DIM),
            full(p_a), full(p_b), full(w_out), full(g_post), full(w_pu), full(w_pg),
        ],
        out_specs=tile(D_MODEL),
        out_shape=jax.ShapeDtypeStruct((bsz, seq, D_MODEL), F32),
        compiler_params=pltpu.CompilerParams(
            dimension_semantics=("arbitrary", "arbitrary"), vmem_limit_bytes=VMEM_LIMIT),
        name="merge",
    )(ya_lo, ya_hi, yb, gates, gates, x, p, p_a, p_b, w_out, g_post, w_pu, w_pg)


def kernel(x, p, g_pre, w_in, rel_bias, mu_shift, w0, w_up, a0, a_up, k_k, k_a, r_k, ln_x_w, ln_x_b,
           p_a, p_b, w_out, g_post, w_ple_up, w_ple_gate):
    bsz, seq, d = x.shape
    assert d == D_MODEL and seq % MOBA_BLOCK == 0 and seq // MOBA_BLOCK >= MOBA_TOPK
    assert g_pre.shape[0] == 1, "one layer"
    n = bsz * seq
    x2 = x.reshape(n, d).astype(F32)
    qkv, za, rw, gates = _project(x2, g_pre.astype(F32), w_in[0].astype(BF16), tm=256)
    ya_lo, ya_hi = _moba(qkv.reshape(bsz, seq, 3 * WIDTH), za.reshape(bsz, seq, WIDTH), rel_bias.astype(F32))
    yb = _rwkv(rw.reshape(bsz, seq, RW_COLS), mu_shift[0], w0[0], w_up[0], a0[0], a_up[0], k_k[0], k_a[0],
               r_k[0], ln_x_w[0], ln_x_b[0])
    out = _merge(ya_lo, ya_hi, yb, gates.reshape(bsz, seq, G_COLS), x.astype(F32), p[0],
                 p_a[0].astype(BF16), p_b[0].astype(BF16), w_out[0].astype(BF16), g_post.astype(F32),
                 w_ple_up[0].astype(BF16), w_ple_gate[0].astype(BF16), tm=512)
    return out.astype(x.dtype)
```

```python
import functools
import math

import jax
import jax.numpy as jnp
import numpy as np
from jax import lax
from jax.experimental import pallas as pl
from jax.experimental.pallas import tpu as pltpu

F32 = jnp.float32
BF16 = jnp.bfloat16

D_MODEL = 1024
PLE_DIM = 256
RMS_EPS = 1e-6
HEAD_DIM = 64
N_HEADS = 8
WIDTH = N_HEADS * HEAD_DIM
MOBA_BLOCK = 256
MOBA_TOPK = 3
REL_BUCKETS = 32
REL_MAX_EXACT = REL_BUCKETS // 2
REL_MAX_DIST = 128
LORA_RANK = 64
GN_EPS = 64e-5
A_COLS = 4 * WIDTH
RW_COLS = 4 * WIDTH + 2 * LORA_RANK
G_COLS = 2 * D_MODEL
IN_COLS = A_COLS + RW_COLS + G_COLS

LANES = 128
N_PAIRS = N_HEADS // 2
CHUNK = 64
VMEM_LIMIT = 48 * 1024 * 1024
LOG2E = math.log2(math.e)
Q_SCALE = LOG2E * HEAD_DIM ** -0.5


def _dot(a, b):
    return jnp.dot(a, b, preferred_element_type=F32)


def _dot_nt(a, b):
    return lax.dot_general(a, b, (((1,), (1,)), ((), ())), preferred_element_type=F32)


def _dot_tn(a, b):
    return lax.dot_general(a, b, (((0,), (0,)), ((), ())), preferred_element_type=F32)


def _split2(x):
    hi = x.astype(BF16)
    lo = (x - hi.astype(F32)).astype(BF16)
    return hi, lo


def _split3(x):
    h1 = x.astype(BF16)
    r1 = x - h1.astype(F32)
    h2 = r1.astype(BF16)
    h3 = (r1 - h2.astype(F32)).astype(BF16)
    return h1, h2, h3


def _sigmoid(x):
    return 1.0 / (1.0 + jnp.exp(-x))


def _proj_kernel(x_ref, g_ref, w_ref, qkv_ref, za_ref, rw_ref, gt_ref):
    x = x_ref[...]
    ms = jnp.mean(x * x, axis=-1, keepdims=True)
    u = (x * lax.rsqrt(ms + RMS_EPS) * g_ref[...]).astype(BF16)
    step = 512

    def emit(out_ref, col0, width, scale_first=None):
        for c in range(0, width, step):
            w = min(step, width - c)
            y = _dot(u, w_ref[:, col0 + c:col0 + c + w])
            if scale_first is not None and c == 0:
                y = y * scale_first
            out_ref[:, c:c + w] = y.astype(out_ref.dtype)

    emit(qkv_ref, 0, 3 * WIDTH, scale_first=Q_SCALE)
    emit(za_ref, 3 * WIDTH, WIDTH)
    emit(rw_ref, A_COLS, RW_COLS)
    emit(gt_ref, A_COLS + RW_COLS, G_COLS)


def _project(x2, g_pre, w_in_bf, tm):
    n = x2.shape[0]
    return pl.pallas_call(
        _proj_kernel,
        grid=(n // tm,),
        in_specs=[
            pl.BlockSpec((tm, D_MODEL), lambda i: (i, 0)),
            pl.BlockSpec((1, D_MODEL), lambda i: (0, 0)),
            pl.BlockSpec((D_MODEL, IN_COLS), lambda i: (0, 0)),
        ],
        out_specs=[
            pl.BlockSpec((tm, 3 * WIDTH), lambda i: (i, 0)),
            pl.BlockSpec((tm, WIDTH), lambda i: (i, 0)),
            pl.BlockSpec((tm, RW_COLS), lambda i: (i, 0)),
            pl.BlockSpec((tm, G_COLS), lambda i: (i, 0)),
        ],
        out_shape=[
            jax.ShapeDtypeStruct((n, 3 * WIDTH), BF16),
            jax.ShapeDtypeStruct((n, WIDTH), F32),
            jax.ShapeDtypeStruct((n, RW_COLS), F32),
            jax.ShapeDtypeStruct((n, G_COLS), F32),
        ],
        compiler_params=pltpu.CompilerParams(
            dimension_semantics=("arbitrary",), vmem_limit_bytes=VMEM_LIMIT),
        name="proj",
    )(x2, g_pre, w_in_bf)


def _t5_bucket_np(dist):
    n = np.maximum(dist, 0)
    nf = np.maximum(n, 1).astype(np.float32)
    large = REL_MAX_EXACT + (np.log(nf / np.float32(REL_MAX_EXACT)) / np.float32(math.log(REL_MAX_DIST / REL_MAX_EXACT))
                             * np.float32(REL_BUCKETS - REL_MAX_EXACT)).astype(np.int32)
    large = np.minimum(large, REL_BUCKETS - 1)
    return np.where(n < REL_MAX_EXACT, n, large).astype(np.int32)


def _bucket_tables():
    s = np.arange(MOBA_BLOCK)[:, None]
    t = np.arange(MOBA_BLOCK)[None, :]
    own = np.where(t >= s, _t5_bucket_np(t - s), -1).astype(np.int32)
    prev = _t5_bucket_np(MOBA_BLOCK + t - s)
    return own, prev


MASKED = -1e30
VT_ROWS = HEAD_DIM + 16
KIND_FAR, KIND_PREV, KIND_OWN = 0, 1, 2


def _moba_decode(s, bsz, half):
    per_pair = bsz * half
    return s // per_pair, (s % per_pair) // half, s % half


def _moba_kernel(own_b_ref, prev_b_ref, relb_ref, q_ref, k_ref, v_ref, za_ref, zb_ref, oa_ref, ob_ref,
                 tabs, kmean, kaug, vt, qaug, scores0, scores1, mx0, mx1, acc_ref, *, n_blk, bsz):
    half = n_blk // 2
    n_items = N_PAIRS * bsz * half
    step = pl.program_id(0)
    cur = jnp.minimum(step, n_items - 1)
    prv = jnp.maximum(step - 1, 0)
    hp, b, i = _moba_decode(cur, bsz, half)
    hp_p, b_p, i_p = _moba_decode(prv, bsz, half)
    seq_slot = (hp * bsz + b) % 2
    seq_slot_p = (hp_p * bsz + b_p) % 2
    blk = MOBA_BLOCK
    seq = n_blk * blk
    lane = lax.broadcasted_iota(jnp.int32, (1, LANES), 1)
    head_mask = [lane < HEAD_DIM, lane >= HEAD_DIM]
    flag_base = [HEAD_DIM, 0]
    neg_inf = jnp.float32(-jnp.inf)

    @pl.when((b == 0) & (i == 0))
    def _build_bias_tables():
        ob = own_b_ref[...]
        pb = prev_b_ref[...]
        for hh in range(2):
            far = relb_ref[REL_BUCKETS - 1, 2 * hp + hh]
            to = jnp.full((blk, blk), neg_inf, F32)
            tp = jnp.zeros((blk, blk), F32)
            for bkt in range(REL_BUCKETS):
                val = (relb_ref[bkt, 2 * hp + hh] - far) * LOG2E
                to = jnp.where(ob == bkt, val, to)
                tp = jnp.where(pb == bkt, val, tp)
            tabs[hh, KIND_FAR] = jnp.zeros((blk, blk), F32)
            tabs[hh, KIND_PREV] = tp
            tabs[hh, KIND_OWN] = to

    @pl.when(step == 0)
    def _placeholders_for_first_pass2():
        scores1[...] = jnp.zeros_like(scores1)
        mx1[...] = jnp.zeros_like(mx1)

    @pl.when((i == 0) & (step < n_items))
    def _per_sequence_setup():
        ones_row = (lax.broadcasted_iota(jnp.int32, (VT_ROWS - HEAD_DIM, blk), 0) == 0).astype(BF16)
        for j in range(n_blk):
            kj = k_ref[0, j * blk:(j + 1) * blk, :].astype(F32)
            kmean[j:j + 1, :] = jnp.mean(kj, axis=0, keepdims=True)
            vjt = v_ref[0, j * blk:(j + 1) * blk, :].astype(F32).T
            for hh in range(2):
                flag = (lane == flag_base[hh] + j).astype(F32)
                kaug[hh, j] = jnp.where(head_mask[hh], kj, flag).astype(BF16)
                vt[seq_slot, hh, j, 0:HEAD_DIM, :] = vjt[hh * HEAD_DIM:(hh + 1) * HEAD_DIM, :].astype(BF16)
                vt[seq_slot, hh, j, HEAD_DIM:VT_ROWS, :] = ones_row

        q = q_ref[0].astype(F32)
        blk_id = lax.broadcasted_iota(jnp.int32, (n_blk, seq), 0)
        q_blk = lax.broadcasted_iota(jnp.int32, (n_blk, seq), 1) // blk
        flag_row = lax.broadcasted_iota(jnp.int32, (n_blk, LANES), 0)
        flag_lane = lax.broadcasted_iota(jnp.int32, (n_blk, LANES), 1)
        km_hi, km_lo = _split2(kmean[...])
        for hh in range(2):
            q_hi, q_lo = _split2(jnp.where(head_mask[hh], q, 0.0))
            g = _dot_nt(km_hi, q_hi) + _dot_nt(km_lo, q_hi) + _dot_nt(km_hi, q_lo)
            g = jnp.where(blk_id < q_blk, g, neg_inf)
            allowed = blk_id == q_blk
            for _ in range(MOBA_TOPK):
                mx = jnp.max(g, axis=0, keepdims=True)
                first = jnp.min(jnp.where(g == mx, blk_id, n_blk), axis=0, keepdims=True)
                hit = (blk_id == first) & (mx > neg_inf)
                allowed = allowed | hit
                g = jnp.where(hit, neg_inf, g)
            pen_t = jnp.where(allowed, 0.0, MASKED).astype(BF16)
            place = (flag_lane == flag_row + flag_base[hh]).astype(BF16)
            pen = _dot_tn(pen_t, place)
            qa = jnp.where(head_mask[hh], q, pen).astype(BF16)
            for j in range(n_blk):
                qaug[hh, j] = qa[j * blk:(j + 1) * blk, :]

    def tile_max(s):
        return jnp.max(s.reshape(blk // 8, 8, blk), axis=0)

    n_tiles = n_blk + 1

    def tile_ids(t, qb_a):
        qb_b = n_blk - 1 - qb_a
        is_a = t <= qb_a
        qb = jnp.where(is_a, qb_a, qb_b)
        kb = jnp.where(is_a, t, t - qb_a - 1)
        kind = jnp.where(kb == qb, KIND_OWN, jnp.where(kb == qb - 1, KIND_PREV, KIND_FAR))
        return is_a, qb, kb, kind

    def both_passes(sc_w, mx_w, sc_r, mx_r):
        m_rows = [[jnp.max(mx_r[hh, w], axis=0, keepdims=True) for w in range(2)] for hh in range(2)]
        mx_w[...] = jnp.full(mx_w.shape, neg_inf, F32)
        acc_ref[...] = jnp.zeros_like(acc_ref)
        for t in range(n_tiles):
            is_a, qb, kb, kind = tile_ids(t, i)
            which = jnp.where(is_a, 0, 1)
            for hh in range(2):
                s = _dot_nt(kaug[hh, kb], qaug[hh, qb]) + tabs[hh, kind]
                sc_w[hh, t] = s
                mx_w[hh, which] = jnp.maximum(mx_w[hh, which], tile_max(s))
            is_a, _, kb, _ = tile_ids(t, i_p)
            which = jnp.where(is_a, 0, 1)
            for hh in range(2):
                m_row = jnp.where(is_a, m_rows[hh][0], m_rows[hh][1])
                p = jnp.exp2(sc_r[hh, t] - m_row).astype(BF16)
                acc_ref[hh, which] += _dot(vt[seq_slot_p, hh, kb], p)

        for w, (z_ref, o_ref) in enumerate(((za_ref, oa_ref), (zb_ref, ob_ref))):
            out_t = jnp.concatenate(
                [acc_ref[hh, w, 0:HEAD_DIM, :] / acc_ref[hh, w, HEAD_DIM:HEAD_DIM + 1, :] for hh in range(2)],
                axis=0)
            z = z_ref[0]
            o_ref[0] = out_t.T * (z * _sigmoid(z))

    @pl.when(step % 2 == 0)
    def _even_step():
        both_passes(scores0, mx0, scores1, mx1)

    @pl.when(step % 2 == 1)
    def _odd_step():
        both_passes(scores1, mx1, scores0, mx0)


def _moba(qkv, za, rel_bias):
    bsz, seq, _ = qkv.shape
    n_blk = seq // MOBA_BLOCK
    assert n_blk % 2 == 0
    half = n_blk // 2
    own_b, prev_b = _bucket_tables()
    blk = MOBA_BLOCK
    n_items = N_PAIRS * bsz * half
    kernel = functools.partial(_moba_kernel, n_blk=n_blk, bsz=bsz)

    def cur(s):
        return _moba_decode(jnp.minimum(s, n_items - 1), bsz, half)

    def prv(s):
        return _moba_decode(jnp.maximum(s - 1, 0), bsz, half)

    def seq_block(col0):
        def index_map(s):
            hp, b, _ = cur(s)
            return b, 0, col0 + hp
        return pl.BlockSpec((1, seq, LANES), index_map)

    def block_a(s):
        hp, b, i = prv(s)
        return b, i, hp

    def block_b_in(s):
        hp, b, i = prv(s)
        return b, n_blk - 1 - i, hp

    def block_b_out(s):
        hp, b, i = prv(s)
        return b, half - 1 - i, hp

    return pl.pallas_call(
        kernel,
        grid=(n_items + 1,),
        in_specs=[
            pl.BlockSpec((blk, blk), lambda s: (0, 0)),
            pl.BlockSpec((blk, blk), lambda s: (0, 0)),
            pl.BlockSpec(memory_space=pltpu.SMEM),
            seq_block(0), seq_block(N_PAIRS), seq_block(2 * N_PAIRS),
            pl.BlockSpec((1, blk, LANES), block_a),
            pl.BlockSpec((1, blk, LANES), block_b_in),
        ],
        out_specs=[
            pl.BlockSpec((1, blk, LANES), block_a),
            pl.BlockSpec((1, blk, LANES), block_b_out),
        ],
        out_shape=[
            jax.ShapeDtypeStruct((bsz, seq // 2, WIDTH), F32),
            jax.ShapeDtypeStruct((bsz, seq // 2, WIDTH), F32),
        ],
        scratch_shapes=[
            pltpu.VMEM((2, 3, blk, blk), F32),
            pltpu.VMEM((n_blk, LANES), F32),
            pltpu.VMEM((2, n_blk, blk, LANES), BF16),
            pltpu.VMEM((2, 2, n_blk, VT_ROWS, blk), BF16),
            pltpu.VMEM((2, n_blk, blk, LANES), BF16),
            pltpu.VMEM((2, n_blk + 1, blk, blk), F32),
            pltpu.VMEM((2, n_blk + 1, blk, blk), F32),
            pltpu.VMEM((2, 2, 8, blk), F32),
            pltpu.VMEM((2, 2, 8, blk), F32),
            pltpu.VMEM((2, 2, VT_ROWS, blk), F32),
        ],
        compiler_params=pltpu.CompilerParams(
            dimension_semantics=("arbitrary",), vmem_limit_bytes=VMEM_LIMIT),
        name="moba",
    )(jnp.asarray(own_b), jnp.asarray(prev_b), rel_bias, qkv, qkv, qkv, za, za)


RW_CHUNKS = 8
RW_GROUP_CHUNKS = 2


def _rwkv_kernel(rw_ref, mu_ref, w0_ref, lora_ref, a0_ref, kk_ref, ka_ref, rk_ref,
                 lnw_ref, lnb_ref, ltri_ref, ones_ref, o_ref,
                 prev_row, state, t_s, p_s, arb_s, ayk_s, at_s, vst_s, bkh_s, x1_s, loc_s, y_s,
                 lc_s, nc_s, dm_s, h0_s, akv_s):
    c = pl.program_id(1)
    C = CHUNK
    NC = RW_CHUNKS
    R = NC * C

    @pl.when(c == 0)
    def _reset():
        prev_row[...] = jnp.zeros_like(prev_row)
        state[...] = jnp.zeros_like(state)

    lane = lax.broadcasted_iota(jnp.int32, (1, LANES), 1)
    first_half = lane < HEAD_DIM
    second_half = jnp.logical_not(first_half)
    ones_bd = ones_ref[...]
    GC = RW_GROUP_CHUNKS
    RG = GC * C

    def seg_sum(x):
        outs = []
        for p in range(N_PAIRS):
            outs.append(_dot(x[:, p * LANES:(p + 1) * LANES].astype(BF16), ones_bd))
        return jnp.concatenate(outs, axis=1)

    def prepare(g, pre):
        start = g * RG
        cols = rw_ref[0, start:start + RG, :]
        prev = prev_row[0:1, :] if g == 0 else rw_ref[0, start - 1:start, :]
        row = lax.broadcasted_iota(jnp.int32, (RG, 1), 0)
        shifted = jnp.where(row == 0, prev, pltpu.roll(cols, 1, axis=0))
        xs = cols + (shifted - cols) * mu_ref[...]
        r = xs[:, 0:WIDTH]
        k = xs[:, WIDTH:2 * WIDTH]
        v = xs[:, 2 * WIDTH:3 * WIDTH]
        z = xs[:, 3 * WIDTH:4 * WIDTH]
        pre.update(r=r, v=v, z=z)
        yield

        lo_in = xs[:, 4 * WIDTH:4 * WIDTH + LANES]
        lo_in = jnp.where(first_half, jnp.tanh(lo_in), lo_in)
        lora = _dot(lo_in.astype(BF16), lora_ref[...])
        dw = lora[:, 0:WIDTH]
        da = lora[:, WIDTH:2 * WIDTH]
        lw = (-math.exp(-0.5)) * _sigmoid(w0_ref[...] + dw)
        a_ic = _sigmoid(a0_ref[...] + da)
        yield

        kk = k * kk_ref[...]
        kk = kk * lax.rsqrt(jnp.maximum(seg_sum(kk * kk), 1e-24))
        k2 = k * (1.0 + (a_ic - 1.0) * ka_ref[...])
        b_vec = kk * a_ic
        pre.update(k2=k2)
        yield

        l1, l2 = _split2(lw)
        ltri = ltri_ref[0:RG, 0:RG]
        g_inc = _dot(ltri, l1) + _dot(ltri, l2)
        g_end_rows = [g_inc[(n + 1) * C - 1:(n + 1) * C, :] for n in range(GC)]
        decay_end = [jnp.exp(g) for g in g_end_rows]
        d_end = jnp.concatenate([jnp.broadcast_to(d, (C, WIDTH)) for d in decay_end], axis=0)
        e_inc = jnp.exp(g_inc)
        e_neg = 1.0 / e_inc
        yield

        e_end = d_end * e_neg
        pre.update(
            decay_end=decay_end,
            rt=r * e_inc,
            at=kk * (-e_inc * jnp.exp(-lw)),
            bt=b_vec * e_neg, kt=k2 * e_neg, bh=b_vec * e_end, kh=k2 * e_end)

    def stack(x):
        return jnp.concatenate([jnp.where(first_half, x, 0.0), jnp.where(second_half, x, 0.0)], axis=0)

    def twice(x):
        return jnp.concatenate([x, x], axis=0)

    ri = lax.broadcasted_iota(jnp.int32, (2 * C, 2 * C), 0)
    ci = lax.broadcasted_iota(jnp.int32, (2 * C, 2 * C), 1)
    same_head = (ri < C) == (ci < C)
    incl = same_head & (ci <= ri)
    strict = same_head & (ci < ri)
    eye = (ri == ci).astype(F32)
    items = [(n, p) for n in range(NC) for p in range(N_PAIRS)]
    H = 2 * C

    def state_independent(group, pre):
        rt, at, bt, kt, bh, kh, v, decay_end = (pre[name] for name in
                                                ("rt", "at", "bt", "kt", "bh", "kh", "v", "decay_end"))
        for i, n, p in group:
            rows = slice(n * C, (n + 1) * C)
            sl = slice(p * LANES, (p + 1) * LANES)
            r_st = stack(rt[rows, sl]).astype(BF16)
            a_st = stack(at[rows, sl]).astype(BF16)
            ra = jnp.concatenate([r_st, a_st], axis=0)
            bk = jnp.concatenate([twice(bt[rows, sl].astype(BF16)), twice(kt[rows, sl].astype(BF16))], axis=0)
            m_all = _dot_nt(ra, bk)
            n_ab = jnp.where(strict, m_all[H:2 * H, 0:H], 0.0)
            t_s[i] = eye + n_ab
            p_s[i] = n_ab.astype(BF16)
            arb_s[i] = jnp.where(incl, m_all[0:H, 0:H], 0.0).astype(BF16)
            ayk_s[i, 0:H, :] = jnp.where(incl, m_all[0:H, H:2 * H], 0.0).astype(BF16)
            ayk_s[i, H:2 * H, :] = jnp.where(strict, m_all[H:2 * H, H:2 * H], 0.0).astype(BF16)
            x1_s[i, 0:H, :] = r_st
            at_s[i] = a_st
            vst_s[i] = twice(v[rows, sl].astype(BF16))
            bkh_s[i] = jnp.concatenate([stack(bh[rows, sl]), stack(kh[rows, sl])], axis=0).astype(BF16)

        yield

        for i, _, _ in group:
            pb = p_s[i]
            p_s[i] = _dot(pb, pb).astype(BF16)
        yield
        for _ in range(int(math.log2(C)) - 2):
            for i, _, _ in group:
                pb = p_s[i]
                tb = t_s[i]
                both = _dot(pb, jnp.concatenate([tb.astype(BF16), pb], axis=1))
                t_s[i] = tb + both[:, 0:H]
                p_s[i] = both[:, H:2 * H].astype(BF16)
            yield
        for i, _, _ in group:
            tb = t_s[i]
            t_s[i] = tb + _dot(p_s[i], tb.astype(BF16))
        yield

        for i, n, p in group:
            sl = slice(p * LANES, (p + 1) * LANES)
            yk = _dot(ayk_s[i], vst_s[i])
            loc_s[i, 0:H, :] = yk[0:H, :]
            akv_s[i] = yk[H:2 * H, :].astype(BF16)
            dm_s[i] = jnp.broadcast_to(decay_end[n][:, sl], (LANES, LANES)).T
        yield
        for i, _, _ in group:
            wz = _dot(t_s[i].astype(BF16), jnp.concatenate([at_s[i], akv_s[i]], axis=1))
            x1_s[i, H:2 * H, :] = wz[:, 0:LANES].astype(BF16)
            loc_s[i, H:2 * H, :] = wz[:, LANES:2 * LANES]
        yield
        for i, _, _ in group:
            w_bf = x1_s[i, H:2 * H, :]
            u_loc = loc_s[i, H:2 * H, :]
            lc_s[i] = _dot_tn(bkh_s[i, 0:H, :], w_bf).astype(BF16)
            nc_s[i] = _dot_tn(bkh_s[i], jnp.concatenate([u_loc.astype(BF16), vst_s[i]], axis=0))

    inv_n = 1.0 / HEAD_DIM

    def finish(g, pre):
        mine = [(i, n, p) for i, (n, p) in enumerate(items) if n // GC == g]
        for i, n, p in mine:
            h0 = state[p]
            h0_bf = h0.astype(BF16)
            h0_s[i] = h0_bf
            state[p] = dm_s[i] * h0 + _dot(lc_s[i], h0_bf) + nc_s[i]
            if p == N_PAIRS - 1:
                yield
        for i, n, p in mine:
            rs = _dot(x1_s[i], h0_s[i])
            akv_s[i] = (rs[H:2 * H, :] + loc_s[i, H:2 * H, :]).astype(BF16)
            loc_s[i, 0:H, :] = rs[0:H, :] + loc_s[i, 0:H, :]
        yield
        for i, n, p in mine:
            sl = slice(p * LANES, (p + 1) * LANES)
            y_st = loc_s[i, 0:H, :] + _dot(arb_s[i], akv_s[i])
            y_s[n * C:(n + 1) * C, sl] = jnp.where(first_half, y_st[0:C, :], y_st[C:H, :])
        yield
        rows = slice(g * RG, (g + 1) * RG)
        y = y_s[rows, :]
        mean = seg_sum(y) * inv_n
        d = y - mean
        var = seg_sum(d * d) * inv_n
        yn = d * lax.rsqrt(var + GN_EPS) * lnw_ref[...] + lnb_ref[...]
        bonus = seg_sum(pre["r"] * pre["k2"] * rk_ref[...]) * pre["v"]
        z = pre["z"]
        o_ref[0, rows, :] = (yn + bonus) * (z * _sigmoid(z))

    n_groups = NC // GC
    pres = [{} for _ in range(n_groups)]
    for _ in prepare(0, pres[0]):
        pass
    fin = iter(())
    for g in range(n_groups):
        group = [(i, n - g * GC, p) for i, (n, p) in enumerate(items) if n // GC == g]
        nxt = prepare(g + 1, pres[g + 1]) if g + 1 < n_groups else iter(())
        for _ in state_independent(group, pres[g]):
            next(nxt, None)
            next(fin, None)
        for _ in nxt:
            pass
        for _ in fin:
            pass
        fin = finish(g, pres[g])
    for _ in fin:
        pass
    prev_row[0:1, :] = rw_ref[0, R - 1:R, :]


def _rwkv(rw, mu, w0, w_up, a0, a_up, k_k, k_a, r_k, ln_w, ln_b):
    bsz, seq, _ = rw.shape
    C = CHUNK
    row = lambda t: t.reshape(1, -1).astype(F32)
    lora = jnp.zeros((LANES, 2 * WIDTH), F32)
    lora = lora.at[:LORA_RANK, :WIDTH].set(w_up).at[LORA_RANK:, WIDTH:].set(a_up)
    lora = lora.astype(BF16)
    R = RW_CHUNKS * C
    G = RW_CHUNKS * N_PAIRS
    H = 2 * C
    assert seq % R == 0 and H == LANES
    ltri = jnp.asarray(np.kron(np.eye(RW_CHUNKS), np.tril(np.ones((C, C)))).astype(np.float32)).astype(BF16)
    hid = np.arange(LANES) // HEAD_DIM
    ones_bd = jnp.asarray((hid[:, None] == hid[None, :]).astype(np.float32)).astype(BF16)
    vec = lambda n: pl.BlockSpec((1, n), lambda b, c: (0, 0))
    return pl.pallas_call(
        _rwkv_kernel,
        grid=(bsz, seq // R),
        in_specs=[
            pl.BlockSpec((1, R, RW_COLS), lambda b, c: (b, c, 0)),
            vec(RW_COLS), vec(WIDTH),
            pl.BlockSpec((LANES, 2 * WIDTH), lambda b, c: (0, 0)),
            vec(WIDTH), vec(WIDTH), vec(WIDTH), vec(WIDTH), vec(WIDTH), vec(WIDTH),
            pl.BlockSpec((R, R), lambda b, c: (0, 0)),
            pl.BlockSpec((LANES, LANES), lambda b, c: (0, 0)),
        ],
        out_specs=pl.BlockSpec((1, R, WIDTH), lambda b, c: (b, c, 0)),
        out_shape=jax.ShapeDtypeStruct((bsz, seq, WIDTH), F32),
        scratch_shapes=[
            pltpu.VMEM((8, RW_COLS), F32),
            pltpu.VMEM((N_PAIRS, LANES, LANES), F32),
            pltpu.VMEM((G, H, H), F32),
            pltpu.VMEM((G, H, H), BF16),
            pltpu.VMEM((G, H, H), BF16),
            pltpu.VMEM((G, 2 * H, H), BF16),
            pltpu.VMEM((G, H, LANES), BF16),
            pltpu.VMEM((G, H, LANES), BF16),
            pltpu.VMEM((G, 2 * H, LANES), BF16),
            pltpu.VMEM((G, 2 * H, LANES), BF16),
            pltpu.VMEM((G, 2 * H, LANES), F32),
            pltpu.VMEM((R, WIDTH), F32),
            pltpu.VMEM((G, LANES, LANES), BF16),
            pltpu.VMEM((G, LANES, LANES), F32),
            pltpu.VMEM((G, LANES, LANES), F32),
            pltpu.VMEM((G, LANES, LANES), BF16),
            pltpu.VMEM((G, H, LANES), BF16),
        ],
        compiler_params=pltpu.CompilerParams(
            dimension_semantics=("arbitrary", "arbitrary"), vmem_limit_bytes=VMEM_LIMIT),
        name="rwkv",
    )(rw, row(mu), row(w0), lora, row(a0), row(k_k), row(k_a), row(r_k), row(ln_w), row(ln_b),
      ltri, ones_bd)


def _merge_kernel(ya_lo_ref, ya_hi_ref, yb_ref, ga_ref, gb_ref, x_ref, p_ref, pa_ref, pb_ref, wo_ref, gpost_ref,
                  wpu_ref, wpg_ref, o_ref, *, tiles_per_half):
    first_half = pl.program_id(1) < tiles_per_half
    ya = jnp.where(first_half, ya_lo_ref[0], ya_hi_ref[0])
    ma = _dot(ya.astype(BF16), pa_ref[...])
    mb = _dot(yb_ref[0].astype(BF16), pb_ref[...])
    merged = _sigmoid(ga_ref[0]) * ma + _sigmoid(gb_ref[0]) * mb
    y = _dot(merged.astype(BF16), wo_ref[...])
    ms = jnp.mean(y * y, axis=-1, keepdims=True)
    h = x_ref[0] + y * lax.rsqrt(ms + RMS_EPS) * gpost_ref[...]
    e = _dot(p_ref[0].astype(BF16), wpu_ref[...])
    gate = _dot(h.astype(BF16), wpg_ref[...])
    o_ref[0] = h + _sigmoid(gate) * e


def _merge(ya_lo, ya_hi, yb, gates, x, p, p_a, p_b, w_out, g_post, w_pu, w_pg, tm):
    bsz, seq, _ = x.shape
    th = seq // 2 // tm
    full = lambda a: pl.BlockSpec(a.shape, lambda b, t: (0, 0))
    tile = lambda w, col=0: pl.BlockSpec((1, tm, w), lambda b, t: (b, t, col))
    return pl.pallas_call(
        functools.partial(_merge_kernel, tiles_per_half=th),
        grid=(bsz, seq // tm),
        in_specs=[
            pl.BlockSpec((1, tm, WIDTH), lambda b, t: (b, jnp.minimum(t, th - 1), 0)),
            pl.BlockSpec((1, tm, WIDTH), lambda b, t: (b, jnp.maximum(t - th, 0), 0)),
            tile(WIDTH), tile(D_MODEL, 0), tile(D_MODEL, 1), tile(D_MODEL), tile(PLE_DIM),
            full(p_a), full(p_b), full(w_out), full(g_post), full(w_pu), full(w_pg),
        ],
        out_specs=tile(D_MODEL),
        out_shape=jax.ShapeDtypeStruct((bsz, seq, D_MODEL), F32),
        compiler_params=pltpu.CompilerParams(
            dimension_semantics=("arbitrary", "arbitrary"), vmem_limit_bytes=VMEM_LIMIT),
        name="merge",
    )(ya_lo, ya_hi, yb, gates, gates, x, p, p_a, p_b, w_out, g_post, w_pu, w_pg)


def kernel(x, p, g_pre, w_in, rel_bias, mu_shift, w0, w_up, a0, a_up, k_k, k_a, r_k, ln_x_w, ln_x_b,
           p_a, p_b, w_out, g_post, w_ple_up, w_ple_gate):
    bsz, seq, d = x.shape
    assert d == D_MODEL and seq % MOBA_BLOCK == 0 and seq // MOBA_BLOCK >= MOBA_TOPK
    assert g_pre.shape[0] == 1, "one layer"
    n = bsz * seq
    x2 = x.reshape(n, d).astype(F32)
    qkv, za, rw, gates = _project(x2, g_pre.astype(F32), w_in[0].astype(BF16), tm=256)
    ya_lo, ya_hi = _moba(qkv.reshape(bsz, seq, 3 * WIDTH), za.reshape(bsz, seq, WIDTH), rel_bias.astype(F32))
    yb = _rwkv(rw.reshape(bsz, seq, RW_COLS), mu_shift[0], w0[0], w_up[0], a0[0], a_up[0], k_k[0], k_a[0],
               r_k[0], ln_x_w[0], ln_x_b[0])
    out = _merge(ya_lo, ya_hi, yb, gates.reshape(bsz, seq, G_COLS), x.astype(F32), p[0],
                 p_a[0].astype(BF16), p_b[0].astype(BF16), w_out[0].astype(BF16), g_post.astype(F32),
                 w_ple_up[0].astype(BF16), w_ple_gate[0].astype(BF16), tm=512)
    return out.astype(x.dtype)
```

```python
import functools
import math

import jax
import jax.numpy as jnp
import numpy as np
from jax import lax
from jax.experimental import pallas as pl
from jax.experimental.pallas import tpu as pltpu

F32 = jnp.float32
BF16 = jnp.bfloat16

D_MODEL = 1024
PLE_DIM = 256
RMS_EPS = 1e-6
HEAD_DIM = 64
N_HEADS = 8
WIDTH = N_HEADS * HEAD_DIM
MOBA_BLOCK = 256
MOBA_TOPK = 3
REL_BUCKETS = 32
REL_MAX_EXACT = REL_BUCKETS // 2
REL_MAX_DIST = 128
LORA_RANK = 64
GN_EPS = 64e-5
A_COLS = 4 * WIDTH
RW_COLS = 4 * WIDTH + 2 * LORA_RANK
G_COLS = 2 * D_MODEL
IN_COLS = A_COLS + RW_COLS + G_COLS

LANES = 128
N_PAIRS = N_HEADS // 2
CHUNK = 64
VMEM_LIMIT = 48 * 1024 * 1024
LOG2E = math.log2(math.e)
Q_SCALE = LOG2E * HEAD_DIM ** -0.5


def _dot(a, b):
    return jnp.dot(a, b, preferred_element_type=F32)


def _dot_nt(a, b):
    return lax.dot_general(a, b, (((1,), (1,)), ((), ())), preferred_element_type=F32)


def _dot_tn(a, b):
    return lax.dot_general(a, b, (((0,), (0,)), ((), ())), preferred_element_type=F32)


def _split2(x):
    hi = x.astype(BF16)
    lo = (x - hi.astype(F32)).astype(BF16)
    return hi, lo


def _split3(x):
    h1 = x.astype(BF16)
    r1 = x - h1.astype(F32)
    h2 = r1.astype(BF16)
    h3 = (r1 - h2.astype(F32)).astype(BF16)
    return h1, h2, h3


def _sigmoid(x):
    return 1.0 / (1.0 + jnp.exp(-x))


def _proj_kernel(x_ref, g_ref, w_ref, qkv_ref, za_ref, rw_ref, gt_ref):
    x = x_ref[...]
    ms = jnp.mean(x * x, axis=-1, keepdims=True)
    u = (x * lax.rsqrt(ms + RMS_EPS) * g_ref[...]).astype(BF16)
    step = 512

    def emit(out_ref, col0, width, scale_first=None):
        for c in range(0, width, step):
            w = min(step, width - c)
            y = _dot(u, w_ref[:, col0 + c:col0 + c + w])
            if scale_first is not None and c == 0:
                y = y * scale_first
            out_ref[:, c:c + w] = y.astype(out_ref.dtype)

    emit(qkv_ref, 0, 3 * WIDTH, scale_first=Q_SCALE)
    emit(za_ref, 3 * WIDTH, WIDTH)
    emit(rw_ref, A_COLS, RW_COLS)
    emit(gt_ref, A_COLS + RW_COLS, G_COLS)


def _project(x2, g_pre, w_in_bf, tm):
    n = x2.shape[0]
    return pl.pallas_call(
        _proj_kernel,
        grid=(n // tm,),
        in_specs=[
            pl.BlockSpec((tm, D_MODEL), lambda i: (i, 0)),
            pl.BlockSpec((1, D_MODEL), lambda i: (0, 0)),
            pl.BlockSpec((D_MODEL, IN_COLS), lambda i: (0, 0)),
        ],
        out_specs=[
            pl.BlockSpec((tm, 3 * WIDTH), lambda i: (i, 0)),
            pl.BlockSpec((tm, WIDTH), lambda i: (i, 0)),
            pl.BlockSpec((tm, RW_COLS), lambda i: (i, 0)),
            pl.BlockSpec((tm, G_COLS), lambda i: (i, 0)),
        ],
        out_shape=[
            jax.ShapeDtypeStruct((n, 3 * WIDTH), BF16),
            jax.ShapeDtypeStruct((n, WIDTH), F32),
            jax.ShapeDtypeStruct((n, RW_COLS), F32),
            jax.ShapeDtypeStruct((n, G_COLS), F32),
        ],
        compiler_params=pltpu.CompilerParams(
            dimension_semantics=("arbitrary",), vmem_limit_bytes=VMEM_LIMIT),
        name="proj",
    )(x2, g_pre, w_in_bf)


def _t5_bucket_np(dist):
    n = np.maximum(dist, 0)
    nf = np.maximum(n, 1).astype(np.float32)
    large = REL_MAX_EXACT + (np.log(nf / np.float32(REL_MAX_EXACT)) / np.float32(math.log(REL_MAX_DIST / REL_MAX_EXACT))
                             * np.float32(REL_BUCKETS - REL_MAX_EXACT)).astype(np.int32)
    large = np.minimum(large, REL_BUCKETS - 1)
    return np.where(n < REL_MAX_EXACT, n, large).astype(np.int32)


def _bucket_tables():
    s = np.arange(MOBA_BLOCK)[:, None]
    t = np.arange(MOBA_BLOCK)[None, :]
    own = np.where(t >= s, _t5_bucket_np(t - s), -1).astype(np.int32)
    prev = _t5_bucket_np(MOBA_BLOCK + t - s)
    return own, prev


MASKED = -1e30
VT_ROWS = HEAD_DIM + 16
KIND_FAR, KIND_PREV, KIND_OWN = 0, 1, 2


def _moba_decode(s, bsz, half):
    per_pair = bsz * half
    return s // per_pair, (s % per_pair) // half, s % half


def _moba_kernel(own_b_ref, prev_b_ref, relb_ref, q_ref, k_ref, v_ref, za_ref, zb_ref, oa_ref, ob_ref,
                 tabs, kmean, kaug, vt, qaug, scores0, scores1, mx0, mx1, acc_ref, *, n_blk, bsz):
    half = n_blk // 2
    n_items = N_PAIRS * bsz * half
    step = pl.program_id(0)
    cur = jnp.minimum(step, n_items - 1)
    prv = jnp.maximum(step - 1, 0)
    hp, b, i = _moba_decode(cur, bsz, half)
    hp_p, b_p, i_p = _moba_decode(prv, bsz, half)
    seq_slot = (hp * bsz + b) % 2
    seq_slot_p = (hp_p * bsz + b_p) % 2
    blk = MOBA_BLOCK
    seq = n_blk * blk
    lane = lax.broadcasted_iota(jnp.int32, (1, LANES), 1)
    head_mask = [lane < HEAD_DIM, lane >= HEAD_DIM]
    flag_base = [HEAD_DIM, 0]
    neg_inf = jnp.float32(-jnp.inf)

    @pl.when((b == 0) & (i == 0))
    def _build_bias_tables():
        ob = own_b_ref[...]
        pb = prev_b_ref[...]
        for hh in range(2):
            far = relb_ref[REL_BUCKETS - 1, 2 * hp + hh]
            to = jnp.full((blk, blk), neg_inf, F32)
            tp = jnp.zeros((blk, blk), F32)
            for bkt in range(REL_BUCKETS):
                val = (relb_ref[bkt, 2 * hp + hh] - far) * LOG2E
                to = jnp.where(ob == bkt, val, to)
                tp = jnp.where(pb == bkt, val, tp)
            tabs[hh, KIND_FAR] = jnp.zeros((blk, blk), F32)
            tabs[hh, KIND_PREV] = tp
            tabs[hh, KIND_OWN] = to

    @pl.when(step == 0)
    def _placeholders_for_first_pass2():
        scores1[...] = jnp.zeros_like(scores1)
        mx1[...] = jnp.zeros_like(mx1)

    @pl.when((i == 0) & (step < n_items))
    def _per_sequence_setup():
        ones_row = (lax.broadcasted_iota(jnp.int32, (VT_ROWS - HEAD_DIM, blk), 0) == 0).astype(BF16)
        for j in range(n_blk):
            kj = k_ref[0, j * blk:(j + 1) * blk, :].astype(F32)
            kmean[j:j + 1, :] = jnp.mean(kj, axis=0, keepdims=True)
            vjt = v_ref[0, j * blk:(j + 1) * blk, :].astype(F32).T
            for hh in range(2):
                flag = (lane == flag_base[hh] + j).astype(F32)
                kaug[hh, j] = jnp.where(head_mask[hh], kj, flag).astype(BF16)
                vt[seq_slot, hh, j, 0:HEAD_DIM, :] = vjt[hh * HEAD_DIM:(hh + 1) * HEAD_DIM, :].astype(BF16)
                vt[seq_slot, hh, j, HEAD_DIM:VT_ROWS, :] = ones_row

        q = q_ref[0].astype(F32)
        blk_id = lax.broadcasted_iota(jnp.int32, (n_blk, seq), 0)
        q_blk = lax.broadcasted_iota(jnp.int32, (n_blk, seq), 1) // blk
        flag_row = lax.broadcasted_iota(jnp.int32, (n_blk, LANES), 0)
        flag_lane = lax.broadcasted_iota(jnp.int32, (n_blk, LANES), 1)
        km_hi, km_lo = _split2(kmean[...])
        for hh in range(2):
            q_hi, q_lo = _split2(jnp.where(head_mask[hh], q, 0.0))
            g = _dot_nt(km_hi, q_hi) + _dot_nt(km_lo, q_hi) + _dot_nt(km_hi, q_lo)
            g = jnp.where(blk_id < q_blk, g, neg_inf)
            allowed = blk_id == q_blk
            for _ in range(MOBA_TOPK):
                mx = jnp.max(g, axis=0, keepdims=True)
                first = jnp.min(jnp.where(g == mx, blk_id, n_blk), axis=0, keepdims=True)
                hit = (blk_id == first) & (mx > neg_inf)
                allowed = allowed | hit
                g = jnp.where(hit, neg_inf, g)
            pen_t = jnp.where(allowed, 0.0, MASKED).astype(BF16)
            place = (flag_lane == flag_row + flag_base[hh]).astype(BF16)
            pen = _dot_tn(pen_t, place)
            qa = jnp.where(head_mask[hh], q, pen).astype(BF16)
            for j in range(n_blk):
                qaug[hh, j] = qa[j * blk:(j + 1) * blk, :]

    def tile_max(s):
        return jnp.max(s.reshape(blk // 8, 8, blk), axis=0)

    n_tiles = n_blk + 1

    def tile_ids(t, qb_a):
        qb_b = n_blk - 1 - qb_a
        is_a = t <= qb_a
        qb = jnp.where(is_a, qb_a, qb_b)
        kb = jnp.where(is_a, t, t - qb_a - 1)
        kind = jnp.where(kb == qb, KIND_OWN, jnp.where(kb == qb - 1, KIND_PREV, KIND_FAR))
        return is_a, qb, kb, kind

    def both_passes(sc_w, mx_w, sc_r, mx_r):
        m_rows = [[jnp.max(mx_r[hh, w], axis=0, keepdims=True) for w in range(2)] for hh in range(2)]
        mx_w[...] = jnp.full(mx_w.shape, neg_inf, F32)
        acc_ref[...] = jnp.zeros_like(acc_ref)
        for t in range(n_tiles):
            is_a, qb, kb, kind = tile_ids(t, i)
            which = jnp.where(is_a, 0, 1)
            if half <= t <= n_blk - 2:
                kind = None
            elif t >= n_blk - 1:
                kind = KIND_PREV if t == n_blk - 1 else KIND_OWN
            for hh in range(2):
                s = _dot_nt(kaug[hh, kb], qaug[hh, qb])
                if kind is not None:
                    s = s + tabs[hh, kind]
                sc_w[hh, t] = s
                mx_w[hh, which] = jnp.maximum(mx_w[hh, which], tile_max(s))
            is_a, _, kb, _ = tile_ids(t, i_p)
            which = jnp.where(is_a, 0, 1)
            for hh in range(2):
                m_row = jnp.where(is_a, m_rows[hh][0], m_rows[hh][1])
                p = jnp.exp2(sc_r[hh, t] - m_row).astype(BF16)
                acc_ref[hh, which] += _dot(vt[seq_slot_p, hh, kb], p)

        for w, (z_ref, o_ref) in enumerate(((za_ref, oa_ref), (zb_ref, ob_ref))):
            out_t = jnp.concatenate(
                [acc_ref[hh, w, 0:HEAD_DIM, :] / acc_ref[hh, w, HEAD_DIM:HEAD_DIM + 1, :] for hh in range(2)],
                axis=0)
            z = z_ref[0]
            o_ref[0] = out_t.T * (z * _sigmoid(z))

    @pl.when(step % 2 == 0)
    def _even_step():
        both_passes(scores0, mx0, scores1, mx1)

    @pl.when(step % 2 == 1)
    def _odd_step():
        both_passes(scores1, mx1, scores0, mx0)


def _moba(qkv, za, rel_bias):
    bsz, seq, _ = qkv.shape
    n_blk = seq // MOBA_BLOCK
    assert n_blk % 2 == 0
    half = n_blk // 2
    own_b, prev_b = _bucket_tables()
    blk = MOBA_BLOCK
    n_items = N_PAIRS * bsz * half
    kernel = functools.partial(_moba_kernel, n_blk=n_blk, bsz=bsz)

    def cur(s):
        return _moba_decode(jnp.minimum(s, n_items - 1), bsz, half)

    def prv(s):
        return _moba_decode(jnp.maximum(s - 1, 0), bsz, half)

    def seq_block(col0):
        def index_map(s):
            hp, b, _ = cur(s)
            return b, 0, col0 + hp
        return pl.BlockSpec((1, seq, LANES), index_map)

    def block_a(s):
        hp, b, i = prv(s)
        return b, i, hp

    def block_b_in(s):
        hp, b, i = prv(s)
        return b, n_blk - 1 - i, hp

    def block_b_out(s):
        hp, b, i = prv(s)
        return b, half - 1 - i, hp

    return pl.pallas_call(
        kernel,
        grid=(n_items + 1,),
        in_specs=[
            pl.BlockSpec((blk, blk), lambda s: (0, 0)),
            pl.BlockSpec((blk, blk), lambda s: (0, 0)),
            pl.BlockSpec(memory_space=pltpu.SMEM),
            seq_block(0), seq_block(N_PAIRS), seq_block(2 * N_PAIRS),
            pl.BlockSpec((1, blk, LANES), block_a),
            pl.BlockSpec((1, blk, LANES), block_b_in),
        ],
        out_specs=[
            pl.BlockSpec((1, blk, LANES), block_a),
            pl.BlockSpec((1, blk, LANES), block_b_out),
        ],
        out_shape=[
            jax.ShapeDtypeStruct((bsz, seq // 2, WIDTH), F32),
            jax.ShapeDtypeStruct((bsz, seq // 2, WIDTH), F32),
        ],
        scratch_shapes=[
            pltpu.VMEM((2, 3, blk, blk), F32),
            pltpu.VMEM((n_blk, LANES), F32),
            pltpu.VMEM((2, n_blk, blk, LANES), BF16),
            pltpu.VMEM((2, 2, n_blk, VT_ROWS, blk), BF16),
            pltpu.VMEM((2, n_blk, blk, LANES), BF16),
            pltpu.VMEM((2, n_blk + 1, blk, blk), F32),
            pltpu.VMEM((2, n_blk + 1, blk, blk), F32),
            pltpu.VMEM((2, 2, 8, blk), F32),
            pltpu.VMEM((2, 2, 8, blk), F32),
            pltpu.VMEM((2, 2, VT_ROWS, blk), F32),
        ],
        compiler_params=pltpu.CompilerParams(
            dimension_semantics=("arbitrary",), vmem_limit_bytes=VMEM_LIMIT),
        name="moba",
    )(jnp.asarray(own_b), jnp.asarray(prev_b), rel_bias, qkv, qkv, qkv, za, za)


RW_CHUNKS = 8
RW_GROUP_CHUNKS = 2


def _rwkv_kernel(rw_ref, mu_ref, w0_ref, lora_ref, a0_ref, kk_ref, ka_ref, rk_ref,
                 lnw_ref, lnb_ref, ltri_ref, ones_ref, o_ref,
                 prev_row, state, t_s, p_s, arb_s, ayk_s, at_s, vst_s, bkh_s, x1_s, loc_s, y_s,
                 lc_s, nc_s, dm_s, h0_s, akv_s):
    c = pl.program_id(1)
    C = CHUNK
    NC = RW_CHUNKS
    R = NC * C

    @pl.when(c == 0)
    def _reset():
        prev_row[...] = jnp.zeros_like(prev_row)
        state[...] = jnp.zeros_like(state)

    lane = lax.broadcasted_iota(jnp.int32, (1, LANES), 1)
    first_half = lane < HEAD_DIM
    second_half = jnp.logical_not(first_half)
    ones_bd = ones_ref[...]
    GC = RW_GROUP_CHUNKS
    RG = GC * C

    def seg_sum(x):
        outs = []
        for p in range(N_PAIRS):
            outs.append(_dot(x[:, p * LANES:(p + 1) * LANES].astype(BF16), ones_bd))
        return jnp.concatenate(outs, axis=1)

    def prepare(g, pre):
        start = g * RG
        cols = rw_ref[0, start:start + RG, :]
        prev = prev_row[0:1, :] if g == 0 else rw_ref[0, start - 1:start, :]
        row = lax.broadcasted_iota(jnp.int32, (RG, 1), 0)
        shifted = jnp.where(row == 0, prev, pltpu.roll(cols, 1, axis=0))
        xs = cols + (shifted - cols) * mu_ref[...]
        r = xs[:, 0:WIDTH]
        k = xs[:, WIDTH:2 * WIDTH]
        v = xs[:, 2 * WIDTH:3 * WIDTH]
        z = xs[:, 3 * WIDTH:4 * WIDTH]
        pre.update(r=r, v=v, z=z)
        yield

        lo_in = xs[:, 4 * WIDTH:4 * WIDTH + LANES]
        lo_in = jnp.where(first_half, jnp.tanh(lo_in), lo_in)
        lora = _dot(lo_in.astype(BF16), lora_ref[...])
        dw = lora[:, 0:WIDTH]
        da = lora[:, WIDTH:2 * WIDTH]
        lw = (-math.exp(-0.5)) * _sigmoid(w0_ref[...] + dw)
        a_ic = _sigmoid(a0_ref[...] + da)
        yield

        kk = k * kk_ref[...]
        kk = kk * lax.rsqrt(jnp.maximum(seg_sum(kk * kk), 1e-24))
        k2 = k * (1.0 + (a_ic - 1.0) * ka_ref[...])
        b_vec = kk * a_ic
        pre.update(k2=k2)
        yield

        l1, l2 = _split2(lw)
        ltri = ltri_ref[0:RG, 0:RG]
        g_inc = _dot(ltri, l1) + _dot(ltri, l2)
        g_end_rows = [g_inc[(n + 1) * C - 1:(n + 1) * C, :] for n in range(GC)]
        decay_end = [jnp.exp(g) for g in g_end_rows]
        d_end = jnp.concatenate([jnp.broadcast_to(d, (C, WIDTH)) for d in decay_end], axis=0)
        e_inc = jnp.exp(g_inc)
        e_neg = 1.0 / e_inc
        yield

        e_end = d_end * e_neg
        pre.update(
            decay_end=decay_end,
            rt=r * e_inc,
            at=kk * (-e_inc * jnp.exp(-lw)),
            bt=b_vec * e_neg, kt=k2 * e_neg, bh=b_vec * e_end, kh=k2 * e_end)

    def stack(x):
        return jnp.concatenate([jnp.where(first_half, x, 0.0), jnp.where(second_half, x, 0.0)], axis=0)

    def twice(x):
        return jnp.concatenate([x, x], axis=0)

    ri = lax.broadcasted_iota(jnp.int32, (2 * C, 2 * C), 0)
    ci = lax.broadcasted_iota(jnp.int32, (2 * C, 2 * C), 1)
    same_head = (ri < C) == (ci < C)
    incl = same_head & (ci <= ri)
    strict = same_head & (ci < ri)
    eye = (ri == ci).astype(F32)
    items = [(n, p) for n in range(NC) for p in range(N_PAIRS)]
    H = 2 * C

    def state_independent(group, pre):
        rt, at, bt, kt, bh, kh, v, decay_end = (pre[name] for name in
                                                ("rt", "at", "bt", "kt", "bh", "kh", "v", "decay_end"))
        for i, n, p in group:
            rows = slice(n * C, (n + 1) * C)
            sl = slice(p * LANES, (p + 1) * LANES)
            r_st = stack(rt[rows, sl]).astype(BF16)
            a_st = stack(at[rows, sl]).astype(BF16)
            ra = jnp.concatenate([r_st, a_st], axis=0)
            bk = jnp.concatenate([twice(bt[rows, sl].astype(BF16)), twice(kt[rows, sl].astype(BF16))], axis=0)
            m_all = _dot_nt(ra, bk)
            n_ab = jnp.where(strict, m_all[H:2 * H, 0:H], 0.0)
            t_s[i] = eye + n_ab
            p_s[i] = n_ab.astype(BF16)
            arb_s[i] = jnp.where(incl, m_all[0:H, 0:H], 0.0).astype(BF16)
            ayk_s[i, 0:H, :] = jnp.where(incl, m_all[0:H, H:2 * H], 0.0).astype(BF16)
            ayk_s[i, H:2 * H, :] = jnp.where(strict, m_all[H:2 * H, H:2 * H], 0.0).astype(BF16)
            x1_s[i, 0:H, :] = r_st
            at_s[i] = a_st
            vst_s[i] = twice(v[rows, sl].astype(BF16))
            bkh_s[i] = jnp.concatenate([stack(bh[rows, sl]), stack(kh[rows, sl])], axis=0).astype(BF16)

        yield

        for i, _, _ in group:
            pb = p_s[i]
            p_s[i] = _dot(pb, pb).astype(BF16)
        yield
        for _ in range(int(math.log2(C)) - 2):
            for i, _, _ in group:
                pb = p_s[i]
                tb = t_s[i]
                both = _dot(pb, jnp.concatenate([tb.astype(BF16), pb], axis=1))
                t_s[i] = tb + both[:, 0:H]
                p_s[i] = both[:, H:2 * H].astype(BF16)
            yield
        for i, _, _ in group:
            tb = t_s[i]
            t_s[i] = tb + _dot(p_s[i], tb.astype(BF16))
        yield

        for i, n, p in group:
            sl = slice(p * LANES, (p + 1) * LANES)
            yk = _dot(ayk_s[i], vst_s[i])
            loc_s[i, 0:H, :] = yk[0:H, :]
            akv_s[i] = yk[H:2 * H, :].astype(BF16)
            dm_s[i] = jnp.broadcast_to(decay_end[n][:, sl], (LANES, LANES)).T
        yield
        for i, _, _ in group:
            wz = _dot(t_s[i].astype(BF16), jnp.concatenate([at_s[i], akv_s[i]], axis=1))
            x1_s[i, H:2 * H, :] = wz[:, 0:LANES].astype(BF16)
            loc_s[i, H:2 * H, :] = wz[:, LANES:2 * LANES]
        yield
        for i, _, _ in group:
            w_bf = x1_s[i, H:2 * H, :]
            u_loc = loc_s[i, H:2 * H, :]
            lc_s[i] = _dot_tn(bkh_s[i, 0:H, :], w_bf).astype(BF16)
            nc_s[i] = _dot_tn(bkh_s[i], jnp.concatenate([u_loc.astype(BF16), vst_s[i]], axis=0))

    inv_n = 1.0 / HEAD_DIM

    def finish(g, pre):
        mine = [(i, n, p) for i, (n, p) in enumerate(items) if n // GC == g]
        for i, n, p in mine:
            h0 = state[p]
            h0_bf = h0.astype(BF16)
            h0_s[i] = h0_bf
            state[p] = dm_s[i] * h0 + _dot(lc_s[i], h0_bf) + nc_s[i]
            if p == N_PAIRS - 1:
                yield
        for i, n, p in mine:
            rs = _dot(x1_s[i], h0_s[i])
            akv_s[i] = (rs[H:2 * H, :] + loc_s[i, H:2 * H, :]).astype(BF16)
            loc_s[i, 0:H, :] = rs[0:H, :] + loc_s[i, 0:H, :]
        yield
        for i, n, p in mine:
            sl = slice(p * LANES, (p + 1) * LANES)
            y_st = loc_s[i, 0:H, :] + _dot(arb_s[i], akv_s[i])
            y_s[n * C:(n + 1) * C, sl] = jnp.where(first_half, y_st[0:C, :], y_st[C:H, :])
        yield
        rows = slice(g * RG, (g + 1) * RG)
        y = y_s[rows, :]
        mean = seg_sum(y) * inv_n
        d = y - mean
        var = seg_sum(d * d) * inv_n
        yn = d * lax.rsqrt(var + GN_EPS) * lnw_ref[...] + lnb_ref[...]
        bonus = seg_sum(pre["r"] * pre["k2"] * rk_ref[...]) * pre["v"]
        z = pre["z"]
        o_ref[0, rows, :] = (yn + bonus) * (z * _sigmoid(z))

    n_groups = NC // GC
    pres = [{} for _ in range(n_groups)]
    for _ in prepare(0, pres[0]):
        pass
    fin = iter(())
    for g in range(n_groups):
        group = [(i, n - g * GC, p) for i, (n, p) in enumerate(items) if n // GC == g]
        nxt = prepare(g + 1, pres[g + 1]) if g + 1 < n_groups else iter(())
        for _ in state_independent(group, pres[g]):
            next(nxt, None)
            next(fin, None)
        for _ in nxt:
            pass
        for _ in fin:
            pass
        fin = finish(g, pres[g])
    for _ in fin:
        pass
    prev_row[0:1, :] = rw_ref[0, R - 1:R, :]


def _rwkv(rw, mu, w0, w_up, a0, a_up, k_k, k_a, r_k, ln_w, ln_b):
    bsz, seq, _ = rw.shape
    C = CHUNK
    row = lambda t: t.reshape(1, -1).astype(F32)
    lora = jnp.zeros((LANES, 2 * WIDTH), F32)
    lora = lora.at[:LORA_RANK, :WIDTH].set(w_up).at[LORA_RANK:, WIDTH:].set(a_up)
    lora = lora.astype(BF16)
    R = RW_CHUNKS * C
    G = RW_CHUNKS * N_PAIRS
    H = 2 * C
    assert seq % R == 0 and H == LANES
    ltri = jnp.asarray(np.kron(np.eye(RW_CHUNKS), np.tril(np.ones((C, C)))).astype(np.float32)).astype(BF16)
    hid = np.arange(LANES) // HEAD_DIM
    ones_bd = jnp.asarray((hid[:, None] == hid[None, :]).astype(np.float32)).astype(BF16)
    vec = lambda n: pl.BlockSpec((1, n), lambda b, c: (0, 0))
    return pl.pallas_call(
        _rwkv_kernel,
        grid=(bsz, seq // R),
        in_specs=[
            pl.BlockSpec((1, R, RW_COLS), lambda b, c: (b, c, 0)),
            vec(RW_COLS), vec(WIDTH),
            pl.BlockSpec((LANES, 2 * WIDTH), lambda b, c: (0, 0)),
            vec(WIDTH), vec(WIDTH), vec(WIDTH), vec(WIDTH), vec(WIDTH), vec(WIDTH),
            pl.BlockSpec((R, R), lambda b, c: (0, 0)),
            pl.BlockSpec((LANES, LANES), lambda b, c: (0, 0)),
        ],
        out_specs=pl.BlockSpec((1, R, WIDTH), lambda b, c: (b, c, 0)),
        out_shape=jax.ShapeDtypeStruct((bsz, seq, WIDTH), F32),
        scratch_shapes=[
            pltpu.VMEM((8, RW_COLS), F32),
            pltpu.VMEM((N_PAIRS, LANES, LANES), F32),
            pltpu.VMEM((G, H, H), F32),
            pltpu.VMEM((G, H, H), BF16),
            pltpu.VMEM((G, H, H), BF16),
            pltpu.VMEM((G, 2 * H, H), BF16),
            pltpu.VMEM((G, H, LANES), BF16),
            pltpu.VMEM((G, H, LANES), BF16),
            pltpu.VMEM((G, 2 * H, LANES), BF16),
            pltpu.VMEM((G, 2 * H, LANES), BF16),
            pltpu.VMEM((G, 2 * H, LANES), F32),
            pltpu.VMEM((R, WIDTH), F32),
            pltpu.VMEM((G, LANES, LANES), BF16),
            pltpu.VMEM((G, LANES, LANES), F32),
            pltpu.VMEM((G, LANES, LANES), F32),
            pltpu.VMEM((G, LANES, LANES), BF16),
            pltpu.VMEM((G, H, LANES), BF16),
        ],
        compiler_params=pltpu.CompilerParams(
            dimension_semantics=("arbitrary", "arbitrary"), vmem_limit_bytes=VMEM_LIMIT),
        name="rwkv",
    )(rw, row(mu), row(w0), lora, row(a0), row(k_k), row(k_a), row(r_k), row(ln_w), row(ln_b),
      ltri, ones_bd)


def _merge_kernel(ya_lo_ref, ya_hi_ref, yb_ref, ga_ref, gb_ref, x_ref, p_ref, pa_ref, pb_ref, wo_ref, gpost_ref,
                  wpu_ref, wpg_ref, o_ref, *, tiles_per_half):
    first_half = pl.program_id(1) < tiles_per_half
    ya = jnp.where(first_half, ya_lo_ref[0], ya_hi_ref[0])
    ma = _dot(ya.astype(BF16), pa_ref[...])
    mb = _dot(yb_ref[0].astype(BF16), pb_ref[...])
    merged = _sigmoid(ga_ref[0]) * ma + _sigmoid(gb_ref[0]) * mb
    y = _dot(merged.astype(BF16), wo_ref[...])
    ms = jnp.mean(y * y, axis=-1, keepdims=True)
    h = x_ref[0] + y * lax.rsqrt(ms + RMS_EPS) * gpost_ref[...]
    e = _dot(p_ref[0].astype(BF16), wpu_ref[...])
    gate = _dot(h.astype(BF16), wpg_ref[...])
    o_ref[0] = h + _sigmoid(gate) * e


def _merge(ya_lo, ya_hi, yb, gates, x, p, p_a, p_b, w_out, g_post, w_pu, w_pg, tm):
    bsz, seq, _ = x.shape
    th = seq // 2 // tm
    full = lambda a: pl.BlockSpec(a.shape, lambda b, t: (0, 0))
    tile = lambda w, col=0: pl.BlockSpec((1, tm, w), lambda b, t: (b, t, col))
    return pl.pallas_call(
        functools.partial(_merge_kernel, tiles_per_half=th),
        grid=(bsz, seq // tm),
        in_specs=[
            pl.BlockSpec((1, tm, WIDTH), lambda b, t: (b, jnp.minimum(t, th - 1), 0)),
            pl.BlockSpec((1, tm, WIDTH), lambda b, t: (b, jnp.maximum(t - th, 0), 0)),
            tile(WIDTH), tile(D_MODEL, 0), tile(D_MODEL, 1), tile(D_MODEL), tile(PLE_DIM),
            full(p_a), full(p_b), full(w_out), full(g_post), full(w_pu), full(w_pg),
        ],
        out_specs=tile(D_MODEL),
        out_shape=jax.ShapeDtypeStruct((bsz, seq, D_MODEL), F32),
        compiler_params=pltpu.CompilerParams(
            dimension_semantics=("arbitrary", "arbitrary"), vmem_limit_bytes=VMEM_LIMIT),
        name="merge",
    )(ya_lo, ya_hi, yb, gates, gates, x, p, p_a, p_b, w_out, g_post, w_pu, w_pg)


def kernel(x, p, g_pre, w_in, rel_bias, mu_shift, w0, w_up, a0, a_up, k_k, k_a, r_k, ln_x_w, ln_x_b,
           p_a, p_b, w_out, g_post, w_ple_up, w_ple_gate):
    bsz, seq, d = x.shape
    assert d == D_MODEL and seq % MOBA_BLOCK == 0 and seq // MOBA_BLOCK >= MOBA_TOPK
    assert g_pre.shape[0] == 1, "one layer"
    n = bsz * seq
    x2 = x.reshape(n, d).astype(F32)
    qkv, za, rw, gates = _project(x2, g_pre.astype(F32), w_in[0].astype(BF16), tm=256)
    ya_lo, ya_hi = _moba(qkv.reshape(bsz, seq, 3 * WIDTH), za.reshape(bsz, seq, WIDTH), rel_bias.astype(F32))
    yb = _rwkv(rw.reshape(bsz, seq, RW_COLS), mu_shift[0], w0[0], w_up[0], a0[0], a_up[0], k_k[0], k_a[0],
               r_k[0], ln_x_w[0], ln_x_b[0])
    out = _merge(ya_lo, ya_hi, yb, gates.reshape(bsz, seq, G_COLS), x.astype(F32), p[0],
                 p_a[0].astype(BF16), p_b[0].astype(BF16), w_out[0].astype(BF16), g_post.astype(F32),
                 w_ple_up[0].astype(BF16), w_ple_gate[0].astype(BF16), tm=512)
    return out.astype(x.dtype)
```

```python
import functools
import math

import jax
import jax.numpy as jnp
import numpy as np
from jax import lax
from jax.experimental import pallas as pl
from jax.experimental.pallas import tpu as pltpu

F32 = jnp.float32
BF16 = jnp.bfloat16

D_MODEL = 1024
PLE_DIM = 256
RMS_EPS = 1e-6
HEAD_DIM = 64
N_HEADS = 8
WIDTH = N_HEADS * HEAD_DIM
MOBA_BLOCK = 256
MOBA_TOPK = 3
REL_BUCKETS = 32
REL_MAX_EXACT = REL_BUCKETS // 2
REL_MAX_DIST = 128
LORA_RANK = 64
GN_EPS = 64e-5
A_COLS = 4 * WIDTH
RW_COLS = 4 * WIDTH + 2 * LORA_RANK
G_COLS = 2 * D_MODEL
IN_COLS = A_COLS + RW_COLS + G_COLS

LANES = 128
N_PAIRS = N_HEADS // 2
CHUNK = 64
VMEM_LIMIT = 48 * 1024 * 1024
LOG2E = math.log2(math.e)
Q_SCALE = LOG2E * HEAD_DIM ** -0.5


def _dot(a, b):
    return jnp.dot(a, b, preferred_element_type=F32)


def _dot_nt(a, b):
    return lax.dot_general(a, b, (((1,), (1,)), ((), ())), preferred_element_type=F32)


def _dot_tn(a, b):
    return lax.dot_general(a, b, (((0,), (0,)), ((), ())), preferred_element_type=F32)


def _split2(x):
    hi = x.astype(BF16)
    lo = (x - hi.astype(F32)).astype(BF16)
    return hi, lo


def _split3(x):
    h1 = x.astype(BF16)
    r1 = x - h1.astype(F32)
    h2 = r1.astype(BF16)
    h3 = (r1 - h2.astype(F32)).astype(BF16)
    return h1, h2, h3


def _sigmoid(x):
    return 1.0 / (1.0 + jnp.exp(-x))


def _proj_kernel(x_ref, g_ref, w_ref, qkv_ref, za_ref, rw_ref, gt_ref):
    x = x_ref[...]
    ms = jnp.mean(x * x, axis=-1, keepdims=True)
    u = (x * lax.rsqrt(ms + RMS_EPS) * g_ref[...]).astype(BF16)
    step = 512

    def emit(out_ref, col0, width, scale_first=None):
        for c in range(0, width, step):
            w = min(step, width - c)
            y = _dot(u, w_ref[:, col0 + c:col0 + c + w])
            if scale_first is not None and c == 0:
                y = y * scale_first
            out_ref[:, c:c + w] = y.astype(out_ref.dtype)

    emit(qkv_ref, 0, 3 * WIDTH, scale_first=Q_SCALE)
    emit(za_ref, 3 * WIDTH, WIDTH)
    emit(rw_ref, A_COLS, RW_COLS)
    emit(gt_ref, A_COLS + RW_COLS, G_COLS)


def _project(x2, g_pre, w_in_bf, tm):
    n = x2.shape[0]
    return pl.pallas_call(
        _proj_kernel,
        grid=(n // tm,),
        in_specs=[
            pl.BlockSpec((tm, D_MODEL), lambda i: (i, 0)),
            pl.BlockSpec((1, D_MODEL), lambda i: (0, 0)),
            pl.BlockSpec((D_MODEL, IN_COLS), lambda i: (0, 0)),
        ],
        out_specs=[
            pl.BlockSpec((tm, 3 * WIDTH), lambda i: (i, 0)),
            pl.BlockSpec((tm, WIDTH), lambda i: (i, 0)),
            pl.BlockSpec((tm, RW_COLS), lambda i: (i, 0)),
            pl.BlockSpec((tm, G_COLS), lambda i: (i, 0)),
        ],
        out_shape=[
            jax.ShapeDtypeStruct((n, 3 * WIDTH), BF16),
            jax.ShapeDtypeStruct((n, WIDTH), F32),
            jax.ShapeDtypeStruct((n, RW_COLS), F32),
            jax.ShapeDtypeStruct((n, G_COLS), F32),
        ],
        compiler_params=pltpu.CompilerParams(
            dimension_semantics=("arbitrary",), vmem_limit_bytes=VMEM_LIMIT),
        name="proj",
    )(x2, g_pre, w_in_bf)


def _t5_bucket_np(dist):
    n = np.maximum(dist, 0)
    nf = np.maximum(n, 1).astype(np.float32)
    large = REL_MAX_EXACT + (np.log(nf / np.float32(REL_MAX_EXACT)) / np.float32(math.log(REL_MAX_DIST / REL_MAX_EXACT))
                             * np.float32(REL_BUCKETS - REL_MAX_EXACT)).astype(np.int32)
    large = np.minimum(large, REL_BUCKETS - 1)
    return np.where(n < REL_MAX_EXACT, n, large).astype(np.int32)


def _bucket_tables():
    s = np.arange(MOBA_BLOCK)[:, None]
    t = np.arange(MOBA_BLOCK)[None, :]
    own = np.where(t >= s, _t5_bucket_np(t - s), -1).astype(np.int32)
    prev = _t5_bucket_np(MOBA_BLOCK + t - s)
    return own, prev


MASKED = -1e30
VT_ROWS = HEAD_DIM + 16
KIND_FAR, KIND_PREV, KIND_OWN = 0, 1, 2


def _moba_decode(s, bsz, half):
    per_pair = bsz * half
    return s // per_pair, (s % per_pair) // half, s % half


def _moba_kernel(own_b_ref, prev_b_ref, relb_ref, q_ref, k_ref, v_ref, za_ref, zb_ref, oa_ref, ob_ref,
                 tabs, kmean, kaug, vt, qaug, scores0, scores1, mx0, mx1, acc_ref, *, n_blk, bsz):
    half = n_blk // 2
    n_items = N_PAIRS * bsz * half
    step = pl.program_id(0)
    cur = jnp.minimum(step, n_items - 1)
    prv = jnp.maximum(step - 1, 0)
    hp, b, i = _moba_decode(cur, bsz, half)
    hp_p, b_p, i_p = _moba_decode(prv, bsz, half)
    seq_slot = (hp * bsz + b) % 2
    seq_slot_p = (hp_p * bsz + b_p) % 2
    blk = MOBA_BLOCK
    seq = n_blk * blk
    lane = lax.broadcasted_iota(jnp.int32, (1, LANES), 1)
    head_mask = [lane < HEAD_DIM, lane >= HEAD_DIM]
    flag_base = [HEAD_DIM, 0]
    neg_inf = jnp.float32(-jnp.inf)

    @pl.when((b == 0) & (i == 0))
    def _build_bias_tables():
        ob = own_b_ref[...]
        pb = prev_b_ref[...]
        for hh in range(2):
            far = relb_ref[REL_BUCKETS - 1, 2 * hp + hh]
            to = jnp.full((blk, blk), neg_inf, F32)
            tp = jnp.zeros((blk, blk), F32)
            for bkt in range(REL_BUCKETS):
                val = (relb_ref[bkt, 2 * hp + hh] - far) * LOG2E
                to = jnp.where(ob == bkt, val, to)
                tp = jnp.where(pb == bkt, val, tp)
            tabs[hh, KIND_FAR] = jnp.zeros((blk, blk), F32)
            tabs[hh, KIND_PREV] = tp
            tabs[hh, KIND_OWN] = to

    @pl.when(step == 0)
    def _placeholders_for_first_pass2():
        scores1[...] = jnp.zeros_like(scores1)
        mx1[...] = jnp.zeros_like(mx1)

    @pl.when((i == 0) & (step < n_items))
    def _per_sequence_setup():
        ones_row = (lax.broadcasted_iota(jnp.int32, (VT_ROWS - HEAD_DIM, blk), 0) == 0).astype(BF16)
        for j in range(n_blk):
            kj = k_ref[0, j * blk:(j + 1) * blk, :].astype(F32)
            kmean[j:j + 1, :] = jnp.mean(kj, axis=0, keepdims=True)
            vjt = v_ref[0, j * blk:(j + 1) * blk, :].astype(F32).T
            for hh in range(2):
                flag = (lane == flag_base[hh] + j).astype(F32)
                kaug[hh, j] = jnp.where(head_mask[hh], kj, flag).astype(BF16)
                vt[seq_slot, hh, j, 0:HEAD_DIM, :] = vjt[hh * HEAD_DIM:(hh + 1) * HEAD_DIM, :].astype(BF16)
                vt[seq_slot, hh, j, HEAD_DIM:VT_ROWS, :] = ones_row

        q = q_ref[0].astype(F32)
        blk_id = lax.broadcasted_iota(jnp.int32, (n_blk, seq), 0)
        q_blk = lax.broadcasted_iota(jnp.int32, (n_blk, seq), 1) // blk
        flag_row = lax.broadcasted_iota(jnp.int32, (n_blk, LANES), 0)
        flag_lane = lax.broadcasted_iota(jnp.int32, (n_blk, LANES), 1)
        km_hi, km_lo = _split2(kmean[...])
        for hh in range(2):
            q_hi, q_lo = _split2(jnp.where(head_mask[hh], q, 0.0))
            g = _dot_nt(km_hi, q_hi) + _dot_nt(km_lo, q_hi) + _dot_nt(km_hi, q_lo)
            g = jnp.where(blk_id < q_blk, g, neg_inf)
            allowed = blk_id == q_blk
            for _ in range(MOBA_TOPK):
                mx = jnp.max(g, axis=0, keepdims=True)
                first = jnp.min(jnp.where(g == mx, blk_id, n_blk), axis=0, keepdims=True)
                hit = (blk_id == first) & (mx > neg_inf)
                allowed = allowed | hit
                g = jnp.where(hit, neg_inf, g)
            pen_t = jnp.where(allowed, 0.0, MASKED).astype(BF16)
            place = (flag_lane == flag_row + flag_base[hh]).astype(BF16)
            pen = _dot_tn(pen_t, place)
            qa = jnp.where(head_mask[hh], q, pen).astype(BF16)
            for j in range(n_blk):
                qaug[hh, j] = qa[j * blk:(j + 1) * blk, :]

    def tile_max(s):
        return jnp.max(s.reshape(blk // 8, 8, blk), axis=0)

    n_tiles = n_blk + 1

    def tile_ids(t, qb_a):
        qb_b = n_blk - 1 - qb_a
        is_a = t <= qb_a
        qb = jnp.where(is_a, qb_a, qb_b)
        kb = jnp.where(is_a, t, t - qb_a - 1)
        kind = jnp.where(kb == qb, KIND_OWN, jnp.where(kb == qb - 1, KIND_PREV, KIND_FAR))
        return is_a, qb, kb, kind

    def both_passes(sc_w, mx_w, sc_r, mx_r):
        m_rows = [[jnp.max(mx_r[hh, w], axis=0, keepdims=True) for w in range(2)] for hh in range(2)]
        mx_w[...] = jnp.full(mx_w.shape, neg_inf, F32)
        acc_ref[...] = jnp.zeros_like(acc_ref)
        tail_b = [None, None]
        q_b = [qaug[hh, n_blk - 1 - i] for hh in range(2)]
        for t in range(n_tiles):
            is_a, qb, kb, kind = tile_ids(t, i)
            which = jnp.where(is_a, 0, 1)
            if half <= t <= n_blk - 2:
                kind = None
            elif t >= n_blk - 1:
                kind = KIND_PREV if t == n_blk - 1 else KIND_OWN
            for hh in range(2):
                s = _dot_nt(kaug[hh, kb], q_b[hh] if t >= half else qaug[hh, qb])
                if kind is not None:
                    s = s + tabs[hh, kind]
                sc_w[hh, t] = s
                mx_w[hh, which] = jnp.maximum(mx_w[hh, which], tile_max(s))
            is_a, _, kb, _ = tile_ids(t, i_p)
            which = jnp.where(is_a, 0, 1)
            for hh in range(2):
                if t >= half:
                    p = jnp.exp2(sc_r[hh, t] - m_rows[hh][1]).astype(BF16)
                    d = _dot(vt[seq_slot_p, hh, kb], p)
                    tail_b[hh] = d if tail_b[hh] is None else tail_b[hh] + d
                else:
                    m_row = jnp.where(is_a, m_rows[hh][0], m_rows[hh][1])
                    p = jnp.exp2(sc_r[hh, t] - m_row).astype(BF16)
                    acc_ref[hh, which] += _dot(vt[seq_slot_p, hh, kb], p)
        for hh in range(2):
            acc_ref[hh, 1] += tail_b[hh]

        for w, (z_ref, o_ref) in enumerate(((za_ref, oa_ref), (zb_ref, ob_ref))):
            out_t = jnp.concatenate(
                [acc_ref[hh, w, 0:HEAD_DIM, :] / acc_ref[hh, w, HEAD_DIM:HEAD_DIM + 1, :] for hh in range(2)],
                axis=0)
            z = z_ref[0]
            o_ref[0] = out_t.T * (z * _sigmoid(z))

    @pl.when(step % 2 == 0)
    def _even_step():
        both_passes(scores0, mx0, scores1, mx1)

    @pl.when(step % 2 == 1)
    def _odd_step():
        both_passes(scores1, mx1, scores0, mx0)


def _moba(qkv, za, rel_bias):
    bsz, seq, _ = qkv.shape
    n_blk = seq // MOBA_BLOCK
    assert n_blk % 2 == 0
    half = n_blk // 2
    own_b, prev_b = _bucket_tables()
    blk = MOBA_BLOCK
    n_items = N_PAIRS * bsz * half
    kernel = functools.partial(_moba_kernel, n_blk=n_blk, bsz=bsz)

    def cur(s):
        return _moba_decode(jnp.minimum(s, n_items - 1), bsz, half)

    def prv(s):
        return _moba_decode(jnp.maximum(s - 1, 0), bsz, half)

    def seq_block(col0):
        def index_map(s):
            hp, b, _ = cur(s)
            return b, 0, col0 + hp
        return pl.BlockSpec((1, seq, LANES), index_map)

    def block_a(s):
        hp, b, i = prv(s)
        return b, i, hp

    def block_b_in(s):
        hp, b, i = prv(s)
        return b, n_blk - 1 - i, hp

    def block_b_out(s):
        hp, b, i = prv(s)
        return b, half - 1 - i, hp

    return pl.pallas_call(
        kernel,
        grid=(n_items + 1,),
        in_specs=[
            pl.BlockSpec((blk, blk), lambda s: (0, 0)),
            pl.BlockSpec((blk, blk), lambda s: (0, 0)),
            pl.BlockSpec(memory_space=pltpu.SMEM),
            seq_block(0), seq_block(N_PAIRS), seq_block(2 * N_PAIRS),
            pl.BlockSpec((1, blk, LANES), block_a),
            pl.BlockSpec((1, blk, LANES), block_b_in),
        ],
        out_specs=[
            pl.BlockSpec((1, blk, LANES), block_a),
            pl.BlockSpec((1, blk, LANES), block_b_out),
        ],
        out_shape=[
            jax.ShapeDtypeStruct((bsz, seq // 2, WIDTH), F32),
            jax.ShapeDtypeStruct((bsz, seq // 2, WIDTH), F32),
        ],
        scratch_shapes=[
            pltpu.VMEM((2, 3, blk, blk), F32),
            pltpu.VMEM((n_blk, LANES), F32),
            pltpu.VMEM((2, n_blk, blk, LANES), BF16),
            pltpu.VMEM((2, 2, n_blk, VT_ROWS, blk), BF16),
            pltpu.VMEM((2, n_blk, blk, LANES), BF16),
            pltpu.VMEM((2, n_blk + 1, blk, blk), F32),
            pltpu.VMEM((2, n_blk + 1, blk, blk), F32),
            pltpu.VMEM((2, 2, 8, blk), F32),
            pltpu.VMEM((2, 2, 8, blk), F32),
            pltpu.VMEM((2, 2, VT_ROWS, blk), F32),
        ],
        compiler_params=pltpu.CompilerParams(
            dimension_semantics=("arbitrary",), vmem_limit_bytes=VMEM_LIMIT),
        name="moba",
    )(jnp.asarray(own_b), jnp.asarray(prev_b), rel_bias, qkv, qkv, qkv, za, za)


RW_CHUNKS = 8
RW_GROUP_CHUNKS = 2


def _rwkv_kernel(rw_ref, mu_ref, w0_ref, lora_ref, a0_ref, kk_ref, ka_ref, rk_ref,
                 lnw_ref, lnb_ref, ltri_ref, ones_ref, o_ref,
                 prev_row, state, t_s, p_s, arb_s, ayk_s, at_s, vst_s, bkh_s, x1_s, loc_s, y_s,
                 lc_s, nc_s, dm_s, h0_s, akv_s):
    c = pl.program_id(1)
    C = CHUNK
    NC = RW_CHUNKS
    R = NC * C

    @pl.when(c == 0)
    def _reset():
        prev_row[...] = jnp.zeros_like(prev_row)
        state[...] = jnp.zeros_like(state)

    lane = lax.broadcasted_iota(jnp.int32, (1, LANES), 1)
    first_half = lane < HEAD_DIM
    second_half = jnp.logical_not(first_half)
    ones_bd = ones_ref[...]
    GC = RW_GROUP_CHUNKS
    RG = GC * C

    def seg_sum(x):
        outs = []
        for p in range(N_PAIRS):
            outs.append(_dot(x[:, p * LANES:(p + 1) * LANES].astype(BF16), ones_bd))
        return jnp.concatenate(outs, axis=1)

    def prepare(g, pre):
        start = g * RG
        cols = rw_ref[0, start:start + RG, :]
        prev = prev_row[0:1, :] if g == 0 else rw_ref[0, start - 1:start, :]
        row = lax.broadcasted_iota(jnp.int32, (RG, 1), 0)
        shifted = jnp.where(row == 0, prev, pltpu.roll(cols, 1, axis=0))
        xs = cols + (shifted - cols) * mu_ref[...]
        r = xs[:, 0:WIDTH]
        k = xs[:, WIDTH:2 * WIDTH]
        v = xs[:, 2 * WIDTH:3 * WIDTH]
        z = xs[:, 3 * WIDTH:4 * WIDTH]
        pre.update(r=r, v=v, z=z)
        yield

        lo_in = xs[:, 4 * WIDTH:4 * WIDTH + LANES]
        lo_in = jnp.where(first_half, jnp.tanh(lo_in), lo_in)
        lora = _dot(lo_in.astype(BF16), lora_ref[...])
        dw = lora[:, 0:WIDTH]
        da = lora[:, WIDTH:2 * WIDTH]
        lw = (-math.exp(-0.5)) * _sigmoid(w0_ref[...] + dw)
        a_ic = _sigmoid(a0_ref[...] + da)
        yield

        kk = k * kk_ref[...]
        kk = kk * lax.rsqrt(jnp.maximum(seg_sum(kk * kk), 1e-24))
        k2 = k * (1.0 + (a_ic - 1.0) * ka_ref[...])
        b_vec = kk * a_ic
        pre.update(k2=k2)
        yield

        l1, l2 = _split2(lw)
        ltri = ltri_ref[0:RG, 0:RG]
        g_inc = _dot(ltri, l1) + _dot(ltri, l2)
        g_end_rows = [g_inc[(n + 1) * C - 1:(n + 1) * C, :] for n in range(GC)]
        decay_end = [jnp.exp(g) for g in g_end_rows]
        d_end = jnp.concatenate([jnp.broadcast_to(d, (C, WIDTH)) for d in decay_end], axis=0)
        e_inc = jnp.exp(g_inc)
        e_neg = 1.0 / e_inc
        yield

        e_end = d_end * e_neg
        pre.update(
            decay_end=decay_end,
            rt=r * e_inc,
            at=kk * (-e_inc * jnp.exp(-lw)),
            bt=b_vec * e_neg, kt=k2 * e_neg, bh=b_vec * e_end, kh=k2 * e_end)

    def stack(x):
        return jnp.concatenate([jnp.where(first_half, x, 0.0), jnp.where(second_half, x, 0.0)], axis=0)

    def twice(x):
        return jnp.concatenate([x, x], axis=0)

    ri = lax.broadcasted_iota(jnp.int32, (2 * C, 2 * C), 0)
    ci = lax.broadcasted_iota(jnp.int32, (2 * C, 2 * C), 1)
    same_head = (ri < C) == (ci < C)
    incl = same_head & (ci <= ri)
    strict = same_head & (ci < ri)
    eye = (ri == ci).astype(F32)
    items = [(n, p) for n in range(NC) for p in range(N_PAIRS)]
    H = 2 * C

    def state_independent(group, pre):
        rt, at, bt, kt, bh, kh, v, decay_end = (pre[name] for name in
                                                ("rt", "at", "bt", "kt", "bh", "kh", "v", "decay_end"))
        for i, n, p in group:
            rows = slice(n * C, (n + 1) * C)
            sl = slice(p * LANES, (p + 1) * LANES)
            r_st = stack(rt[rows, sl]).astype(BF16)
            a_st = stack(at[rows, sl]).astype(BF16)
            ra = jnp.concatenate([r_st, a_st], axis=0)
            bk = jnp.concatenate([twice(bt[rows, sl].astype(BF16)), twice(kt[rows, sl].astype(BF16))], axis=0)
            m_all = _dot_nt(ra, bk)
            n_ab = jnp.where(strict, m_all[H:2 * H, 0:H], 0.0)
            t_s[i] = eye + n_ab
            p_s[i] = n_ab.astype(BF16)
            arb_s[i] = jnp.where(incl, m_all[0:H, 0:H], 0.0).astype(BF16)
            ayk_s[i, 0:H, :] = jnp.where(incl, m_all[0:H, H:2 * H], 0.0).astype(BF16)
            ayk_s[i, H:2 * H, :] = jnp.where(strict, m_all[H:2 * H, H:2 * H], 0.0).astype(BF16)
            x1_s[i, 0:H, :] = r_st
            at_s[i] = a_st
            vst_s[i] = twice(v[rows, sl].astype(BF16))
            bkh_s[i] = jnp.concatenate([stack(bh[rows, sl]), stack(kh[rows, sl])], axis=0).astype(BF16)

        yield

        for i, _, _ in group:
            pb = p_s[i]
            p_s[i] = _dot(pb, pb).astype(BF16)
        yield
        for _ in range(int(math.log2(C)) - 2):
            for i, _, _ in group:
                pb = p_s[i]
                tb = t_s[i]
                both = _dot(pb, jnp.concatenate([tb.astype(BF16), pb], axis=1))
                t_s[i] = tb + both[:, 0:H]
                p_s[i] = both[:, H:2 * H].astype(BF16)
            yield
        for i, _, _ in group:
            tb = t_s[i]
            t_s[i] = tb + _dot(p_s[i], tb.astype(BF16))
        yield

        for i, n, p in group:
            sl = slice(p * LANES, (p + 1) * LANES)
            yk = _dot(ayk_s[i], vst_s[i])
            loc_s[i, 0:H, :] = yk[0:H, :]
            akv_s[i] = yk[H:2 * H, :].astype(BF16)
            dm_s[i] = jnp.broadcast_to(decay_end[n][:, sl], (LANES, LANES)).T
        yield
        for i, _, _ in group:
            wz = _dot(t_s[i].astype(BF16), jnp.concatenate([at_s[i], akv_s[i]], axis=1))
            x1_s[i, H:2 * H, :] = wz[:, 0:LANES].astype(BF16)
            loc_s[i, H:2 * H, :] = wz[:, LANES:2 * LANES]
        yield
        for i, _, _ in group:
            w_bf = x1_s[i, H:2 * H, :]
            u_loc = loc_s[i, H:2 * H, :]
            lc_s[i] = _dot_tn(bkh_s[i, 0:H, :], w_bf).astype(BF16)
            nc_s[i] = _dot_tn(bkh_s[i], jnp.concatenate([u_loc.astype(BF16), vst_s[i]], axis=0))

    inv_n = 1.0 / HEAD_DIM

    def finish(g, pre):
        mine = [(i, n, p) for i, (n, p) in enumerate(items) if n // GC == g]
        for i, n, p in mine:
            h0 = state[p]
            h0_bf = h0.astype(BF16)
            h0_s[i] = h0_bf
            state[p] = dm_s[i] * h0 + _dot(lc_s[i], h0_bf) + nc_s[i]
            if p == N_PAIRS - 1:
                yield
        for i, n, p in mine:
            rs = _dot(x1_s[i], h0_s[i])
            akv_s[i] = (rs[H:2 * H, :] + loc_s[i, H:2 * H, :]).astype(BF16)
            loc_s[i, 0:H, :] = rs[0:H, :] + loc_s[i, 0:H, :]
        yield
        for i, n, p in mine:
            sl = slice(p * LANES, (p + 1) * LANES)
            y_st = loc_s[i, 0:H, :] + _dot(arb_s[i], akv_s[i])
            y_s[n * C:(n + 1) * C, sl] = jnp.where(first_half, y_st[0:C, :], y_st[C:H, :])
        yield
        rows = slice(g * RG, (g + 1) * RG)
        y = y_s[rows, :]
        mean = seg_sum(y) * inv_n
        d = y - mean
        var = seg_sum(d * d) * inv_n
        yn = d * lax.rsqrt(var + GN_EPS) * lnw_ref[...] + lnb_ref[...]
        bonus = seg_sum(pre["r"] * pre["k2"] * rk_ref[...]) * pre["v"]
        z = pre["z"]
        o_ref[0, rows, :] = (yn + bonus) * (z * _sigmoid(z))

    n_groups = NC // GC
    pres = [{} for _ in range(n_groups)]
    for _ in prepare(0, pres[0]):
        pass
    fin = iter(())
    for g in range(n_groups):
        group = [(i, n - g * GC, p) for i, (n, p) in enumerate(items) if n // GC == g]
        nxt = prepare(g + 1, pres[g + 1]) if g + 1 < n_groups else iter(())
        for _ in state_independent(group, pres[g]):
            next(nxt, None)
            next(fin, None)
        for _ in nxt:
            pass
        for _ in fin:
            pass
        fin = finish(g, pres[g])
    for _ in fin:
        pass
    prev_row[0:1, :] = rw_ref[0, R - 1:R, :]


def _rwkv(rw, mu, w0, w_up, a0, a_up, k_k, k_a, r_k, ln_w, ln_b):
    bsz, seq, _ = rw.shape
    C = CHUNK
    row = lambda t: t.reshape(1, -1).astype(F32)
    lora = jnp.zeros((LANES, 2 * WIDTH), F32)
    lora = lora.at[:LORA_RANK, :WIDTH].set(w_up).at[LORA_RANK:, WIDTH:].set(a_up)
    lora = lora.astype(BF16)
    R = RW_CHUNKS * C
    G = RW_CHUNKS * N_PAIRS
    H = 2 * C
    assert seq % R == 0 and H == LANES
    ltri = jnp.asarray(np.kron(np.eye(RW_CHUNKS), np.tril(np.ones((C, C)))).astype(np.float32)).astype(BF16)
    hid = np.arange(LANES) // HEAD_DIM
    ones_bd = jnp.asarray((hid[:, None] == hid[None, :]).astype(np.float32)).astype(BF16)
    vec = lambda n: pl.BlockSpec((1, n), lambda b, c: (0, 0))
    return pl.pallas_call(
        _rwkv_kernel,
        grid=(bsz, seq // R),
        in_specs=[
            pl.BlockSpec((1, R, RW_COLS), lambda b, c: (b, c, 0)),
            vec(RW_COLS), vec(WIDTH),
            pl.BlockSpec((LANES, 2 * WIDTH), lambda b, c: (0, 0)),
            vec(WIDTH), vec(WIDTH), vec(WIDTH), vec(WIDTH), vec(WIDTH), vec(WIDTH),
            pl.BlockSpec((R, R), lambda b, c: (0, 0)),
            pl.BlockSpec((LANES, LANES), lambda b, c: (0, 0)),
        ],
        out_specs=pl.BlockSpec((1, R, WIDTH), lambda b, c: (b, c, 0)),
        out_shape=jax.ShapeDtypeStruct((bsz, seq, WIDTH), F32),
        scratch_shapes=[
            pltpu.VMEM((8, RW_COLS), F32),
            pltpu.VMEM((N_PAIRS, LANES, LANES), F32),
            pltpu.VMEM((G, H, H), F32),
            pltpu.VMEM((G, H, H), BF16),
            pltpu.VMEM((G, H, H), BF16),
            pltpu.VMEM((G, 2 * H, H), BF16),
            pltpu.VMEM((G, H, LANES), BF16),
            pltpu.VMEM((G, H, LANES), BF16),
            pltpu.VMEM((G, 2 * H, LANES), BF16),
            pltpu.VMEM((G, 2 * H, LANES), BF16),
            pltpu.VMEM((G, 2 * H, LANES), F32),
            pltpu.VMEM((R, WIDTH), F32),
            pltpu.VMEM((G, LANES, LANES), BF16),
            pltpu.VMEM((G, LANES, LANES), F32),
            pltpu.VMEM((G, LANES, LANES), F32),
            pltpu.VMEM((G, LANES, LANES), BF16),
            pltpu.VMEM((G, H, LANES), BF16),
        ],
        compiler_params=pltpu.CompilerParams(
            dimension_semantics=("arbitrary", "arbitrary"), vmem_limit_bytes=VMEM_LIMIT),
        name="rwkv",
    )(rw, row(mu), row(w0), lora, row(a0), row(k_k), row(k_a), row(r_k), row(ln_w), row(ln_b),
      ltri, ones_bd)


def _merge_kernel(ya_lo_ref, ya_hi_ref, yb_ref, ga_ref, gb_ref, x_ref, p_ref, pa_ref, pb_ref, wo_ref, gpost_ref,
                  wpu_ref, wpg_ref, o_ref, *, tiles_per_half):
    first_half = pl.program_id(1) < tiles_per_half
    ya = jnp.where(first_half, ya_lo_ref[0], ya_hi_ref[0])
    ma = _dot(ya.astype(BF16), pa_ref[...])
    mb = _dot(yb_ref[0].astype(BF16), pb_ref[...])
    merged = _sigmoid(ga_ref[0]) * ma + _sigmoid(gb_ref[0]) * mb
    y = _dot(merged.astype(BF16), wo_ref[...])
    ms = jnp.mean(y * y, axis=-1, keepdims=True)
    h = x_ref[0] + y * lax.rsqrt(ms + RMS_EPS) * gpost_ref[...]
    e = _dot(p_ref[0].astype(BF16), wpu_ref[...])
    gate = _dot(h.astype(BF16), wpg_ref[...])
    o_ref[0] = h + _sigmoid(gate) * e


def _merge(ya_lo, ya_hi, yb, gates, x, p, p_a, p_b, w_out, g_post, w_pu, w_pg, tm):
    bsz, seq, _ = x.shape
    th = seq // 2 // tm
    full = lambda a: pl.BlockSpec(a.shape, lambda b, t: (0, 0))
    tile = lambda w, col=0: pl.BlockSpec((1, tm, w), lambda b, t: (b, t, col))
    return pl.pallas_call(
        functools.partial(_merge_kernel, tiles_per_half=th),
        grid=(bsz, seq // tm),
        in_specs=[
            pl.BlockSpec((1, tm, WIDTH), lambda b, t: (b, jnp.minimum(t, th - 1), 0)),
            pl.BlockSpec((1, tm, WIDTH), lambda b, t: (b, jnp.maximum(t - th, 0), 0)),
            tile(WIDTH), tile(D_MODEL, 0), tile(D_MODEL, 1), tile(D_MODEL), tile(PLE_DIM),
            full(p_a), full(p_b), full(w_out), full(g_post), full(w_pu), full(w_pg),
        ],
        out_specs=tile(D_MODEL),
        out_shape=jax.ShapeDtypeStruct((bsz, seq, D_MODEL), F32),
        compiler_params=pltpu.CompilerParams(
            dimension_semantics=("arbitrary", "arbitrary"), vmem_limit_bytes=VMEM_LIMIT),
        name="merge",
    )(ya_lo, ya_hi, yb, gates, gates, x, p, p_a, p_b, w_out, g_post, w_pu, w_pg)


def kernel(x, p, g_pre, w_in, rel_bias, mu_shift, w0, w_up, a0, a_up, k_k, k_a, r_k, ln_x_w, ln_x_b,
           p_a, p_b, w_out, g_post, w_ple_up, w_ple_gate):
    bsz, seq, d = x.shape
    assert d == D_MODEL and seq % MOBA_BLOCK == 0 and seq // MOBA_BLOCK >= MOBA_TOPK
    assert g_pre.shape[0] == 1, "one layer"
    n = bsz * seq
    x2 = x.reshape(n, d).astype(F32)
    qkv, za, rw, gates = _project(x2, g_pre.astype(F32), w_in[0].astype(BF16), tm=256)
    ya_lo, ya_hi = _moba(qkv.reshape(bsz, seq, 3 * WIDTH), za.reshape(bsz, seq, WIDTH), rel_bias.astype(F32))
    yb = _rwkv(rw.reshape(bsz, seq, RW_COLS), mu_shift[0], w0[0], w_up[0], a0[0], a_up[0], k_k[0], k_a[0],
               r_k[0], ln_x_w[0], ln_x_b[0])
    out = _merge(ya_lo, ya_hi, yb, gates.reshape(bsz, seq, G_COLS), x.astype(F32), p[0],
                 p_a[0].astype(BF16), p_b[0].astype(BF16), w_out[0].astype(BF16), g_post.astype(F32),
                 w_ple_up[0].astype(BF16), w_ple_gate[0].astype(BF16), tm=512)
    return out.astype(x.dtype)
```

```python
import functools
import math

import jax
import jax.numpy as jnp
import numpy as np
from jax import lax
from jax.experimental import pallas as pl
from jax.experimental.pallas import tpu as pltpu

F32 = jnp.float32
BF16 = jnp.bfloat16

D_MODEL = 1024
PLE_DIM = 256
RMS_EPS = 1e-6
HEAD_DIM = 64
N_HEADS = 8
WIDTH = N_HEADS * HEAD_DIM
MOBA_BLOCK = 256
MOBA_TOPK = 3
REL_BUCKETS = 32
REL_MAX_EXACT = REL_BUCKETS // 2
REL_MAX_DIST = 128
LORA_RANK = 64
GN_EPS = 64e-5
A_COLS = 4 * WIDTH
RW_COLS = 4 * WIDTH + 2 * LORA_RANK
G_COLS = 2 * D_MODEL
IN_COLS = A_COLS + RW_COLS + G_COLS

LANES = 128
N_PAIRS = N_HEADS // 2
CHUNK = 64
VMEM_LIMIT = 48 * 1024 * 1024
LOG2E = math.log2(math.e)
Q_SCALE = LOG2E * HEAD_DIM ** -0.5


def _dot(a, b):
    return jnp.dot(a, b, preferred_element_type=F32)


def _dot_nt(a, b):
    return lax.dot_general(a, b, (((1,), (1,)), ((), ())), preferred_element_type=F32)


def _dot_tn(a, b):
    return lax.dot_general(a, b, (((0,), (0,)), ((), ())), preferred_element_type=F32)


def _split2(x):
    hi = x.astype(BF16)
    lo = (x - hi.astype(F32)).astype(BF16)
    return hi, lo


def _split3(x):
    h1 = x.astype(BF16)
    r1 = x - h1.astype(F32)
    h2 = r1.astype(BF16)
    h3 = (r1 - h2.astype(F32)).astype(BF16)
    return h1, h2, h3


def _sigmoid(x):
    return 1.0 / (1.0 + jnp.exp(-x))


def _proj_kernel(x_ref, g_ref, w_ref, qkv_ref, za_ref, rw_ref, gt_ref):
    x = x_ref[...]
    ms = jnp.mean(x * x, axis=-1, keepdims=True)
    u = (x * lax.rsqrt(ms + RMS_EPS) * g_ref[...]).astype(BF16)
    step = 512

    def emit(out_ref, col0, width, scale_first=None):
        for c in range(0, width, step):
            w = min(step, width - c)
            y = _dot(u, w_ref[:, col0 + c:col0 + c + w])
            if scale_first is not None and c == 0:
                y = y * scale_first
            out_ref[:, c:c + w] = y.astype(out_ref.dtype)

    emit(qkv_ref, 0, 3 * WIDTH, scale_first=Q_SCALE)
    emit(za_ref, 3 * WIDTH, WIDTH)
    emit(rw_ref, A_COLS, RW_COLS)
    emit(gt_ref, A_COLS + RW_COLS, G_COLS)


def _project(x2, g_pre, w_in_bf, tm):
    n = x2.shape[0]
    return pl.pallas_call(
        _proj_kernel,
        grid=(n // tm,),
        in_specs=[
            pl.BlockSpec((tm, D_MODEL), lambda i: (i, 0)),
            pl.BlockSpec((1, D_MODEL), lambda i: (0, 0)),
            pl.BlockSpec((D_MODEL, IN_COLS), lambda i: (0, 0), pipeline_mode=pl.Buffered(1)),
        ],
        out_specs=[
            pl.BlockSpec((tm, 3 * WIDTH), lambda i: (i, 0)),
            pl.BlockSpec((tm, WIDTH), lambda i: (i, 0)),
            pl.BlockSpec((tm, RW_COLS), lambda i: (i, 0)),
            pl.BlockSpec((tm, G_COLS), lambda i: (i, 0)),
        ],
        out_shape=[
            jax.ShapeDtypeStruct((n, 3 * WIDTH), BF16),
            jax.ShapeDtypeStruct((n, WIDTH), F32),
            jax.ShapeDtypeStruct((n, RW_COLS), F32),
            jax.ShapeDtypeStruct((n, G_COLS), F32),
        ],
        compiler_params=pltpu.CompilerParams(
            dimension_semantics=("arbitrary",), vmem_limit_bytes=VMEM_LIMIT),
        name="proj",
    )(x2, g_pre, w_in_bf)


def _t5_bucket_np(dist):
    n = np.maximum(dist, 0)
    nf = np.maximum(n, 1).astype(np.float32)
    large = REL_MAX_EXACT + (np.log(nf / np.float32(REL_MAX_EXACT)) / np.float32(math.log(REL_MAX_DIST / REL_MAX_EXACT))
                             * np.float32(REL_BUCKETS - REL_MAX_EXACT)).astype(np.int32)
    large = np.minimum(large, REL_BUCKETS - 1)
    return np.where(n < REL_MAX_EXACT, n, large).astype(np.int32)


def _bucket_tables():
    s = np.arange(MOBA_BLOCK)[:, None]
    t = np.arange(MOBA_BLOCK)[None, :]
    own = np.where(t >= s, _t5_bucket_np(t - s), -1).astype(np.int32)
    prev = _t5_bucket_np(MOBA_BLOCK + t - s)
    return own, prev


MASKED = -1e30
VT_ROWS = HEAD_DIM + 16
KIND_FAR, KIND_PREV, KIND_OWN = 0, 1, 2


def _moba_decode(s, bsz, half):
    per_pair = bsz * half
    return s // per_pair, (s % per_pair) // half, s % half


def _moba_kernel(own_b_ref, prev_b_ref, relb_ref, q_ref, k_ref, v_ref, za_ref, zb_ref, oa_ref, ob_ref,
                 tabs, kmean, kaug, vt, qaug, scores0, scores1, mx0, mx1, acc_ref, *, n_blk, bsz):
    half = n_blk // 2
    n_items = N_PAIRS * bsz * half
    step = pl.program_id(0)
    cur = jnp.minimum(step, n_items - 1)
    prv = jnp.maximum(step - 1, 0)
    hp, b, i = _moba_decode(cur, bsz, half)
    hp_p, b_p, i_p = _moba_decode(prv, bsz, half)
    seq_slot = (hp * bsz + b) % 2
    seq_slot_p = (hp_p * bsz + b_p) % 2
    blk = MOBA_BLOCK
    seq = n_blk * blk
    lane = lax.broadcasted_iota(jnp.int32, (1, LANES), 1)
    head_mask = [lane < HEAD_DIM, lane >= HEAD_DIM]
    flag_base = [HEAD_DIM, 0]
    neg_inf = jnp.float32(-jnp.inf)

    @pl.when((b == 0) & (i == 0))
    def _build_bias_tables():
        ob = own_b_ref[...]
        pb = prev_b_ref[...]
        for hh in range(2):
            far = relb_ref[REL_BUCKETS - 1, 2 * hp + hh]
            to = jnp.full((blk, blk), neg_inf, F32)
            tp = jnp.zeros((blk, blk), F32)
            for bkt in range(REL_BUCKETS):
                val = (relb_ref[bkt, 2 * hp + hh] - far) * LOG2E
                to = jnp.where(ob == bkt, val, to)
                tp = jnp.where(pb == bkt, val, tp)
            tabs[hh, KIND_FAR] = jnp.zeros((blk, blk), F32)
            tabs[hh, KIND_PREV] = tp
            tabs[hh, KIND_OWN] = to

    @pl.when(step == 0)
    def _placeholders_for_first_pass2():
        scores1[...] = jnp.zeros_like(scores1)
        mx1[...] = jnp.zeros_like(mx1)

    @pl.when((i == 0) & (step < n_items))
    def _per_sequence_setup():
        ones_row = (lax.broadcasted_iota(jnp.int32, (VT_ROWS - HEAD_DIM, blk), 0) == 0).astype(BF16)
        for j in range(n_blk):
            kj = k_ref[0, j * blk:(j + 1) * blk, :].astype(F32)
            kmean[j:j + 1, :] = jnp.mean(kj, axis=0, keepdims=True)
            vjt = v_ref[0, j * blk:(j + 1) * blk, :].astype(F32).T
            for hh in range(2):
                flag = (lane == flag_base[hh] + j).astype(F32)
                kaug[hh, j] = jnp.where(head_mask[hh], kj, flag).astype(BF16)
                vt[seq_slot, hh, j, 0:HEAD_DIM, :] = vjt[hh * HEAD_DIM:(hh + 1) * HEAD_DIM, :].astype(BF16)
                vt[seq_slot, hh, j, HEAD_DIM:VT_ROWS, :] = ones_row

        q = q_ref[0].astype(F32)
        blk_id = lax.broadcasted_iota(jnp.int32, (n_blk, seq), 0)
        q_blk = lax.broadcasted_iota(jnp.int32, (n_blk, seq), 1) // blk
        flag_row = lax.broadcasted_iota(jnp.int32, (n_blk, LANES), 0)
        flag_lane = lax.broadcasted_iota(jnp.int32, (n_blk, LANES), 1)
        km_hi, km_lo = _split2(kmean[...])
        for hh in range(2):
            q_hi, q_lo = _split2(jnp.where(head_mask[hh], q, 0.0))
            g = _dot_nt(km_hi, q_hi) + _dot_nt(km_lo, q_hi) + _dot_nt(km_hi, q_lo)
            g = jnp.where(blk_id < q_blk, g, neg_inf)
            allowed = blk_id == q_blk
            for _ in range(MOBA_TOPK):
                mx = jnp.max(g, axis=0, keepdims=True)
                first = jnp.min(jnp.where(g == mx, blk_id, n_blk), axis=0, keepdims=True)
                hit = (blk_id == first) & (mx > neg_inf)
                allowed = allowed | hit
                g = jnp.where(hit, neg_inf, g)
            pen_t = jnp.where(allowed, 0.0, MASKED).astype(BF16)
            place = (flag_lane == flag_row + flag_base[hh]).astype(BF16)
            pen = _dot_tn(pen_t, place)
            qa = jnp.where(head_mask[hh], q, pen).astype(BF16)
            for j in range(n_blk):
                qaug[hh, j] = qa[j * blk:(j + 1) * blk, :]

    def tile_max(s):
        return jnp.max(s.reshape(blk // 8, 8, blk), axis=0)

    n_tiles = n_blk + 1

    def tile_ids(t, qb_a):
        qb_b = n_blk - 1 - qb_a
        is_a = t <= qb_a
        qb = jnp.where(is_a, qb_a, qb_b)
        kb = jnp.where(is_a, t, t - qb_a - 1)
        kind = jnp.where(kb == qb, KIND_OWN, jnp.where(kb == qb - 1, KIND_PREV, KIND_FAR))
        return is_a, qb, kb, kind

    def both_passes(sc_w, mx_w, sc_r, mx_r):
        m_rows = [[jnp.max(mx_r[hh, w], axis=0, keepdims=True) for w in range(2)] for hh in range(2)]
        mx_w[...] = jnp.full(mx_w.shape, neg_inf, F32)
        acc_ref[...] = jnp.zeros_like(acc_ref)
        tail_b = [None, None]
        q_b = [qaug[hh, n_blk - 1 - i] for hh in range(2)]
        for t in range(n_tiles):
            is_a, qb, kb, kind = tile_ids(t, i)
            which = jnp.where(is_a, 0, 1)
            if half <= t <= n_blk - 2:
                kind = None
            elif t >= n_blk - 1:
                kind = KIND_PREV if t == n_blk - 1 else KIND_OWN
            for hh in range(2):
                s = _dot_nt(kaug[hh, kb], q_b[hh] if t >= half else qaug[hh, qb])
                if kind is not None:
                    s = s + tabs[hh, kind]
                sc_w[hh, t] = s
                mx_w[hh, which] = jnp.maximum(mx_w[hh, which], tile_max(s))
            is_a, _, kb, _ = tile_ids(t, i_p)
            which = jnp.where(is_a, 0, 1)
            for hh in range(2):
                if t >= half:
                    p = jnp.exp2(sc_r[hh, t] - m_rows[hh][1]).astype(BF16)
                    d = _dot(vt[seq_slot_p, hh, kb], p)
                    tail_b[hh] = d if tail_b[hh] is None else tail_b[hh] + d
                else:
                    m_row = jnp.where(is_a, m_rows[hh][0], m_rows[hh][1])
                    p = jnp.exp2(sc_r[hh, t] - m_row).astype(BF16)
                    acc_ref[hh, which] += _dot(vt[seq_slot_p, hh, kb], p)
        for hh in range(2):
            acc_ref[hh, 1] += tail_b[hh]

        for w, (z_ref, o_ref) in enumerate(((za_ref, oa_ref), (zb_ref, ob_ref))):
            out_t = jnp.concatenate(
                [acc_ref[hh, w, 0:HEAD_DIM, :] / acc_ref[hh, w, HEAD_DIM:HEAD_DIM + 1, :] for hh in range(2)],
                axis=0)
            z = z_ref[0]
            o_ref[0] = out_t.T * (z * _sigmoid(z))

    @pl.when(step % 2 == 0)
    def _even_step():
        both_passes(scores0, mx0, scores1, mx1)

    @pl.when(step % 2 == 1)
    def _odd_step():
        both_passes(scores1, mx1, scores0, mx0)


def _moba(qkv, za, rel_bias):
    bsz, seq, _ = qkv.shape
    n_blk = seq // MOBA_BLOCK
    assert n_blk % 2 == 0
    half = n_blk // 2
    own_b, prev_b = _bucket_tables()
    blk = MOBA_BLOCK
    n_items = N_PAIRS * bsz * half
    kernel = functools.partial(_moba_kernel, n_blk=n_blk, bsz=bsz)

    def cur(s):
        return _moba_decode(jnp.minimum(s, n_items - 1), bsz, half)

    def prv(s):
        return _moba_decode(jnp.maximum(s - 1, 0), bsz, half)

    def seq_block(col0):
        def index_map(s):
            hp, b, _ = cur(s)
            return b, 0, col0 + hp
        return pl.BlockSpec((1, seq, LANES), index_map)

    def block_a(s):
        hp, b, i = prv(s)
        return b, i, hp

    def block_b_in(s):
        hp, b, i = prv(s)
        return b, n_blk - 1 - i, hp

    def block_b_out(s):
        hp, b, i = prv(s)
        return b, half - 1 - i, hp

    return pl.pallas_call(
        kernel,
        grid=(n_items + 1,),
        in_specs=[
            pl.BlockSpec((blk, blk), lambda s: (0, 0)),
            pl.BlockSpec((blk, blk), lambda s: (0, 0)),
            pl.BlockSpec(memory_space=pltpu.SMEM),
            seq_block(0), seq_block(N_PAIRS), seq_block(2 * N_PAIRS),
            pl.BlockSpec((1, blk, LANES), block_a),
            pl.BlockSpec((1, blk, LANES), block_b_in),
        ],
        out_specs=[
            pl.BlockSpec((1, blk, LANES), block_a),
            pl.BlockSpec((1, blk, LANES), block_b_out),
        ],
        out_shape=[
            jax.ShapeDtypeStruct((bsz, seq // 2, WIDTH), F32),
            jax.ShapeDtypeStruct((bsz, seq // 2, WIDTH), F32),
        ],
        scratch_shapes=[
            pltpu.VMEM((2, 3, blk, blk), F32),
            pltpu.VMEM((n_blk, LANES), F32),
            pltpu.VMEM((2, n_blk, blk, LANES), BF16),
            pltpu.VMEM((2, 2, n_blk, VT_ROWS, blk), BF16),
            pltpu.VMEM((2, n_blk, blk, LANES), BF16),
            pltpu.VMEM((2, n_blk + 1, blk, blk), F32),
            pltpu.VMEM((2, n_blk + 1, blk, blk), F32),
            pltpu.VMEM((2, 2, 8, blk), F32),
            pltpu.VMEM((2, 2, 8, blk), F32),
            pltpu.VMEM((2, 2, VT_ROWS, blk), F32),
        ],
        compiler_params=pltpu.CompilerParams(
            dimension_semantics=("arbitrary",), vmem_limit_bytes=VMEM_LIMIT),
        name="moba",
    )(jnp.asarray(own_b), jnp.asarray(prev_b), rel_bias, qkv, qkv, qkv, za, za)


RW_CHUNKS = 8
RW_GROUP_CHUNKS = 2


def _rwkv_kernel(rw_ref, mu_ref, w0_ref, lora_ref, a0_ref, kk_ref, ka_ref, rk_ref,
                 lnw_ref, lnb_ref, ltri_ref, ones_ref, o_ref,
                 prev_row, state, t_s, p_s, arb_s, ayk_s, at_s, vst_s, bkh_s, x1_s, loc_s, y_s,
                 lc_s, nc_s, dm_s, h0_s, akv_s):
    c = pl.program_id(1)
    C = CHUNK
    NC = RW_CHUNKS
    R = NC * C

    @pl.when(c == 0)
    def _reset():
        prev_row[...] = jnp.zeros_like(prev_row)
        state[...] = jnp.zeros_like(state)

    lane = lax.broadcasted_iota(jnp.int32, (1, LANES), 1)
    first_half = lane < HEAD_DIM
    second_half = jnp.logical_not(first_half)
    ones_bd = ones_ref[...]
    GC = RW_GROUP_CHUNKS
    RG = GC * C

    def seg_sum(x):
        outs = []
        for p in range(N_PAIRS):
            outs.append(_dot(x[:, p * LANES:(p + 1) * LANES].astype(BF16), ones_bd))
        return jnp.concatenate(outs, axis=1)

    def prepare(g, pre):
        start = g * RG
        cols = rw_ref[0, start:start + RG, :]
        prev = prev_row[0:1, :] if g == 0 else rw_ref[0, start - 1:start, :]
        row = lax.broadcasted_iota(jnp.int32, (RG, 1), 0)
        shifted = jnp.where(row == 0, prev, pltpu.roll(cols, 1, axis=0))
        xs = cols + (shifted - cols) * mu_ref[...]
        r = xs[:, 0:WIDTH]
        k = xs[:, WIDTH:2 * WIDTH]
        v = xs[:, 2 * WIDTH:3 * WIDTH]
        z = xs[:, 3 * WIDTH:4 * WIDTH]
        pre.update(r=r, v=v, z=z)
        yield

        lo_in = xs[:, 4 * WIDTH:4 * WIDTH + LANES]
        lo_in = jnp.where(first_half, jnp.tanh(lo_in), lo_in)
        lora = _dot(lo_in.astype(BF16), lora_ref[...])
        dw = lora[:, 0:WIDTH]
        da = lora[:, WIDTH:2 * WIDTH]
        lw = (-math.exp(-0.5)) * _sigmoid(w0_ref[...] + dw)
        a_ic = _sigmoid(a0_ref[...] + da)
        yield

        kk = k * kk_ref[...]
        kk = kk * lax.rsqrt(jnp.maximum(seg_sum(kk * kk), 1e-24))
        k2 = k * (1.0 + (a_ic - 1.0) * ka_ref[...])
        b_vec = kk * a_ic
        pre.update(k2=k2)
        yield

        l1, l2 = _split2(lw)
        ltri = ltri_ref[0:RG, 0:RG]
        g_inc = _dot(ltri, l1) + _dot(ltri, l2)
        g_end_rows = [g_inc[(n + 1) * C - 1:(n + 1) * C, :] for n in range(GC)]
        decay_end = [jnp.exp(g) for g in g_end_rows]
        d_end = jnp.concatenate([jnp.broadcast_to(d, (C, WIDTH)) for d in decay_end], axis=0)
        e_inc = jnp.exp(g_inc)
        e_neg = 1.0 / e_inc
        yield

        e_end = d_end * e_neg
        pre.update(
            decay_end=decay_end,
            rt=r * e_inc,
            at=kk * (-e_inc * jnp.exp(-lw)),
            bt=b_vec * e_neg, kt=k2 * e_neg, bh=b_vec * e_end, kh=k2 * e_end)

    def stack(x):
        return jnp.concatenate([jnp.where(first_half, x, 0.0), jnp.where(second_half, x, 0.0)], axis=0)

    def twice(x):
        return jnp.concatenate([x, x], axis=0)

    ri = lax.broadcasted_iota(jnp.int32, (2 * C, 2 * C), 0)
    ci = lax.broadcasted_iota(jnp.int32, (2 * C, 2 * C), 1)
    same_head = (ri < C) == (ci < C)
    incl = same_head & (ci <= ri)
    strict = same_head & (ci < ri)
    eye = (ri == ci).astype(F32)
    items = [(n, p) for n in range(NC) for p in range(N_PAIRS)]
    H = 2 * C

    def state_independent(group, pre):
        rt, at, bt, kt, bh, kh, v, decay_end = (pre[name] for name in
                                                ("rt", "at", "bt", "kt", "bh", "kh", "v", "decay_end"))
        for i, n, p in group:
            rows = slice(n * C, (n + 1) * C)
            sl = slice(p * LANES, (p + 1) * LANES)
            r_st = stack(rt[rows, sl]).astype(BF16)
            a_st = stack(at[rows, sl]).astype(BF16)
            ra = jnp.concatenate([r_st, a_st], axis=0)
            bk = jnp.concatenate([twice(bt[rows, sl].astype(BF16)), twice(kt[rows, sl].astype(BF16))], axis=0)
            m_all = _dot_nt(ra, bk)
            n_ab = jnp.where(strict, m_all[H:2 * H, 0:H], 0.0)
            t_s[i] = eye + n_ab
            p_s[i] = n_ab.astype(BF16)
            arb_s[i] = jnp.where(incl, m_all[0:H, 0:H], 0.0).astype(BF16)
            ayk_s[i, 0:H, :] = jnp.where(incl, m_all[0:H, H:2 * H], 0.0).astype(BF16)
            ayk_s[i, H:2 * H, :] = jnp.where(strict, m_all[H:2 * H, H:2 * H], 0.0).astype(BF16)
            x1_s[i, 0:H, :] = r_st
            at_s[i] = a_st
            vst_s[i] = twice(v[rows, sl].astype(BF16))
            bkh_s[i] = jnp.concatenate([stack(bh[rows, sl]), stack(kh[rows, sl])], axis=0).astype(BF16)

        yield

        for i, _, _ in group:
            pb = p_s[i]
            p_s[i] = _dot(pb, pb).astype(BF16)
        yield
        for _ in range(int(math.log2(C)) - 2):
            for i, _, _ in group:
                pb = p_s[i]
                tb = t_s[i]
                both = _dot(pb, jnp.concatenate([tb.astype(BF16), pb], axis=1))
                t_s[i] = tb + both[:, 0:H]
                p_s[i] = both[:, H:2 * H].astype(BF16)
            yield
        for i, _, _ in group:
            tb = t_s[i]
            t_s[i] = tb + _dot(p_s[i], tb.astype(BF16))
        yield

        for i, n, p in group:
            sl = slice(p * LANES, (p + 1) * LANES)
            yk = _dot(ayk_s[i], vst_s[i])
            loc_s[i, 0:H, :] = yk[0:H, :]
            akv_s[i] = yk[H:2 * H, :].astype(BF16)
            dm_s[i] = jnp.broadcast_to(decay_end[n][:, sl], (LANES, LANES)).T
        yield
        for i, _, _ in group:
            wz = _dot(t_s[i].astype(BF16), jnp.concatenate([at_s[i], akv_s[i]], axis=1))
            x1_s[i, H:2 * H, :] = wz[:, 0:LANES].astype(BF16)
            loc_s[i, H:2 * H, :] = wz[:, LANES:2 * LANES]
        yield
        for i, _, _ in group:
            w_bf = x1_s[i, H:2 * H, :]
            u_loc = loc_s[i, H:2 * H, :]
            lc_s[i] = _dot_tn(bkh_s[i, 0:H, :], w_bf).astype(BF16)
            nc_s[i] = _dot_tn(bkh_s[i], jnp.concatenate([u_loc.astype(BF16), vst_s[i]], axis=0))

    inv_n = 1.0 / HEAD_DIM

    def finish(g, pre):
        mine = [(i, n, p) for i, (n, p) in enumerate(items) if n // GC == g]
        for i, n, p in mine:
            h0 = state[p]
            h0_bf = h0.astype(BF16)
            h0_s[i] = h0_bf
            state[p] = dm_s[i] * h0 + _dot(lc_s[i], h0_bf) + nc_s[i]
            if p == N_PAIRS - 1:
                yield
        for i, n, p in mine:
            rs = _dot(x1_s[i], h0_s[i])
            akv_s[i] = (rs[H:2 * H, :] + loc_s[i, H:2 * H, :]).astype(BF16)
            loc_s[i, 0:H, :] = rs[0:H, :] + loc_s[i, 0:H, :]
        yield
        for i, n, p in mine:
            sl = slice(p * LANES, (p + 1) * LANES)
            y_st = loc_s[i, 0:H, :] + _dot(arb_s[i], akv_s[i])
            y_s[n * C:(n + 1) * C, sl] = jnp.where(first_half, y_st[0:C, :], y_st[C:H, :])
        yield
        rows = slice(g * RG, (g + 1) * RG)
        y = y_s[rows, :]
        mean = seg_sum(y) * inv_n
        d = y - mean
        var = seg_sum(d * d) * inv_n
        yn = d * lax.rsqrt(var + GN_EPS) * lnw_ref[...] + lnb_ref[...]
        bonus = seg_sum(pre["r"] * pre["k2"] * rk_ref[...]) * pre["v"]
        z = pre["z"]
        o_ref[0, rows, :] = (yn + bonus) * (z * _sigmoid(z))

    n_groups = NC // GC
    pres = [{} for _ in range(n_groups)]
    for _ in prepare(0, pres[0]):
        pass
    fin = iter(())
    for g in range(n_groups):
        group = [(i, n - g * GC, p) for i, (n, p) in enumerate(items) if n // GC == g]
        nxt = prepare(g + 1, pres[g + 1]) if g + 1 < n_groups else iter(())
        for _ in state_independent(group, pres[g]):
            next(nxt, None)
            next(fin, None)
        for _ in nxt:
            pass
        for _ in fin:
            pass
        fin = finish(g, pres[g])
    for _ in fin:
        pass
    prev_row[0:1, :] = rw_ref[0, R - 1:R, :]


def _rwkv(rw, mu, w0, w_up, a0, a_up, k_k, k_a, r_k, ln_w, ln_b):
    bsz, seq, _ = rw.shape
    C = CHUNK
    row = lambda t: t.reshape(1, -1).astype(F32)
    lora = jnp.zeros((LANES, 2 * WIDTH), F32)
    lora = lora.at[:LORA_RANK, :WIDTH].set(w_up).at[LORA_RANK:, WIDTH:].set(a_up)
    lora = lora.astype(BF16)
    R = RW_CHUNKS * C
    G = RW_CHUNKS * N_PAIRS
    H = 2 * C
    assert seq % R == 0 and H == LANES
    ltri = jnp.asarray(np.kron(np.eye(RW_CHUNKS), np.tril(np.ones((C, C)))).astype(np.float32)).astype(BF16)
    hid = np.arange(LANES) // HEAD_DIM
    ones_bd = jnp.asarray((hid[:, None] == hid[None, :]).astype(np.float32)).astype(BF16)
    vec = lambda n: pl.BlockSpec((1, n), lambda b, c: (0, 0))
    return pl.pallas_call(
        _rwkv_kernel,
        grid=(bsz, seq // R),
        in_specs=[
            pl.BlockSpec((1, R, RW_COLS), lambda b, c: (b, c, 0)),
            vec(RW_COLS), vec(WIDTH),
            pl.BlockSpec((LANES, 2 * WIDTH), lambda b, c: (0, 0)),
            vec(WIDTH), vec(WIDTH), vec(WIDTH), vec(WIDTH), vec(WIDTH), vec(WIDTH),
            pl.BlockSpec((R, R), lambda b, c: (0, 0)),
            pl.BlockSpec((LANES, LANES), lambda b, c: (0, 0)),
        ],
        out_specs=pl.BlockSpec((1, R, WIDTH), lambda b, c: (b, c, 0)),
        out_shape=jax.ShapeDtypeStruct((bsz, seq, WIDTH), F32),
        scratch_shapes=[
            pltpu.VMEM((8, RW_COLS), F32),
            pltpu.VMEM((N_PAIRS, LANES, LANES), F32),
            pltpu.VMEM((G, H, H), F32),
            pltpu.VMEM((G, H, H), BF16),
            pltpu.VMEM((G, H, H), BF16),
            pltpu.VMEM((G, 2 * H, H), BF16),
            pltpu.VMEM((G, H, LANES), BF16),
            pltpu.VMEM((G, H, LANES), BF16),
            pltpu.VMEM((G, 2 * H, LANES), BF16),
            pltpu.VMEM((G, 2 * H, LANES), BF16),
            pltpu.VMEM((G, 2 * H, LANES), F32),
            pltpu.VMEM((R, WIDTH), F32),
            pltpu.VMEM((G, LANES, LANES), BF16),
            pltpu.VMEM((G, LANES, LANES), F32),
            pltpu.VMEM((G, LANES, LANES), F32),
            pltpu.VMEM((G, LANES, LANES), BF16),
            pltpu.VMEM((G, H, LANES), BF16),
        ],
        compiler_params=pltpu.CompilerParams(
            dimension_semantics=("arbitrary", "arbitrary"), vmem_limit_bytes=VMEM_LIMIT),
        name="rwkv",
    )(rw, row(mu), row(w0), lora, row(a0), row(k_k), row(k_a), row(r_k), row(ln_w), row(ln_b),
      ltri, ones_bd)


def _merge_kernel(ya_lo_ref, ya_hi_ref, yb_ref, ga_ref, gb_ref, x_ref, p_ref, pa_ref, pb_ref, wo_ref, gpost_ref,
                  wpu_ref, wpg_ref, o_ref, *, tiles_per_half):
    first_half = pl.program_id(1) < tiles_per_half
    ya = jnp.where(first_half, ya_lo_ref[0], ya_hi_ref[0])
    ma = _dot(ya.astype(BF16), pa_ref[...])
    mb = _dot(yb_ref[0].astype(BF16), pb_ref[...])
    merged = _sigmoid(ga_ref[0]) * ma + _sigmoid(gb_ref[0]) * mb
    y = _dot(merged.astype(BF16), wo_ref[...])
    ms = jnp.mean(y * y, axis=-1, keepdims=True)
    h = x_ref[0] + y * lax.rsqrt(ms + RMS_EPS) * gpost_ref[...]
    e = _dot(p_ref[0].astype(BF16), wpu_ref[...])
    gate = _dot(h.astype(BF16), wpg_ref[...])
    o_ref[0] = h + _sigmoid(gate) * e


def _merge(ya_lo, ya_hi, yb, gates, x, p, p_a, p_b, w_out, g_post, w_pu, w_pg, tm):
    bsz, seq, _ = x.shape
    th = seq // 2 // tm
    full = lambda a: pl.BlockSpec(a.shape, lambda b, t: (0, 0))
    tile = lambda w, col=0: pl.BlockSpec((1, tm, w), lambda b, t: (b, t, col))
    return pl.pallas_call(
        functools.partial(_merge_kernel, tiles_per_half=th),
        grid=(bsz, seq // tm),
        in_specs=[
            pl.BlockSpec((1, tm, WIDTH), lambda b, t: (b, jnp.minimum(t, th - 1), 0)),
            pl.BlockSpec((1, tm, WIDTH), lambda b, t: (b, jnp.maximum(t - th, 0), 0)),
            tile(WIDTH), tile(D_MODEL, 0), tile(D_MODEL, 1), tile(D_MODEL), tile(PLE_DIM),
            full(p_a), full(p_b), full(w_out), full(g_post), full(w_pu), full(w_pg),
        ],
        out_specs=tile(D_MODEL),
        out_shape=jax.ShapeDtypeStruct((bsz, seq, D_MODEL), F32),
        compiler_params=pltpu.CompilerParams(
            dimension_semantics=("arbitrary", "arbitrary"), vmem_limit_bytes=VMEM_LIMIT),
        name="merge",
    )(ya_lo, ya_hi, yb, gates, gates, x, p, p_a, p_b, w_out, g_post, w_pu, w_pg)


def kernel(x, p, g_pre, w_in, rel_bias, mu_shift, w0, w_up, a0, a_up, k_k, k_a, r_k, ln_x_w, ln_x_b,
           p_a, p_b, w_out, g_post, w_ple_up, w_ple_gate):
    bsz, seq, d = x.shape
    assert d == D_MODEL and seq % MOBA_BLOCK == 0 and seq // MOBA_BLOCK >= MOBA_TOPK
    assert g_pre.shape[0] == 1, "one layer"
    n = bsz * seq
    x2 = x.reshape(n, d).astype(F32)
    qkv, za, rw, gates = _project(x2, g_pre.astype(F32), w_in[0].astype(BF16), tm=512)
    ya_lo, ya_hi = _moba(qkv.reshape(bsz, seq, 3 * WIDTH), za.reshape(bsz, seq, WIDTH), rel_bias.astype(F32))
    yb = _rwkv(rw.reshape(bsz, seq, RW_COLS), mu_shift[0], w0[0], w_up[0], a0[0], a_up[0], k_k[0], k_a[0],
               r_k[0], ln_x_w[0], ln_x_b[0])
    out = _merge(ya_lo, ya_hi, yb, gates.reshape(bsz, seq, G_COLS), x.astype(F32), p[0],
                 p_a[0].astype(BF16), p_b[0].astype(BF16), w_out[0].astype(BF16), g_post.astype(F32),
                 w_ple_up[0].astype(BF16), w_ple_gate[0].astype(BF16), tm=512)
    return out.astype(x.dtype)
```

```python
import functools
import math

import jax
import jax.numpy as jnp
import numpy as np
from jax import lax
from jax.experimental import pallas as pl
from jax.experimental.pallas import tpu as pltpu

F32 = jnp.float32
BF16 = jnp.bfloat16

D_MODEL = 1024
PLE_DIM = 256
RMS_EPS = 1e-6
HEAD_DIM = 64
N_HEADS = 8
WIDTH = N_HEADS * HEAD_DIM
MOBA_BLOCK = 256
MOBA_TOPK = 3
REL_BUCKETS = 32
REL_MAX_EXACT = REL_BUCKETS // 2
REL_MAX_DIST = 128
LORA_RANK = 64
GN_EPS = 64e-5
A_COLS = 4 * WIDTH
RW_COLS = 4 * WIDTH + 2 * LORA_RANK
G_COLS = 2 * D_MODEL
IN_COLS = A_COLS + RW_COLS + G_COLS

LANES = 128
BF16_SUBLANES = 16
N_PAIRS = N_HEADS // 2
CHUNK = 64
PROJ_TOKENS = 512
MERGE_TOKENS = 512
VMEM_LIMIT = 48 * 1024 * 1024
LOG2E = math.log2(math.e)
Q_SCALE = LOG2E * HEAD_DIM ** -0.5


def _dot(a, b):
    return jnp.dot(a, b, preferred_element_type=F32)


def _dot_nt(a, b):
    return lax.dot_general(a, b, (((1,), (1,)), ((), ())), preferred_element_type=F32)


def _dot_tn(a, b):
    return lax.dot_general(a, b, (((0,), (0,)), ((), ())), preferred_element_type=F32)


def _split2(x):
    hi = x.astype(BF16)
    lo = (x - hi.astype(F32)).astype(BF16)
    return hi, lo


def _sigmoid(x):
    return 1.0 / (1.0 + jnp.exp(-x))


def _proj_kernel(x_ref, g_ref, w_ref, qkv_ref, za_ref, rw_ref, gt_ref):
    x = x_ref[...]
    ms = jnp.mean(x * x, axis=-1, keepdims=True)
    u = (x * lax.rsqrt(ms + RMS_EPS) * g_ref[...]).astype(BF16)
    step = 512

    def emit(out_ref, col0, width, scale_first=None):
        for c in range(0, width, step):
            w = min(step, width - c)
            y = _dot(u, w_ref[:, col0 + c:col0 + c + w])
            if scale_first is not None and c == 0:
                y = y * scale_first
            out_ref[:, c:c + w] = y.astype(out_ref.dtype)

    emit(qkv_ref, 0, 3 * WIDTH, scale_first=Q_SCALE)
    emit(za_ref, 3 * WIDTH, WIDTH)
    emit(rw_ref, A_COLS, RW_COLS)
    emit(gt_ref, A_COLS + RW_COLS, G_COLS)


def _project(x2, g_pre, w_in_bf):
    n = x2.shape[0]
    tm = math.gcd(PROJ_TOKENS, n)
    return pl.pallas_call(
        _proj_kernel,
        grid=(n // tm,),
        in_specs=[
            pl.BlockSpec((tm, D_MODEL), lambda i: (i, 0)),
            pl.BlockSpec((1, D_MODEL), lambda i: (0, 0)),
            pl.BlockSpec((D_MODEL, IN_COLS), lambda i: (0, 0), pipeline_mode=pl.Buffered(1)),
        ],
        out_specs=[
            pl.BlockSpec((tm, 3 * WIDTH), lambda i: (i, 0)),
            pl.BlockSpec((tm, WIDTH), lambda i: (i, 0)),
            pl.BlockSpec((tm, RW_COLS), lambda i: (i, 0)),
            pl.BlockSpec((tm, G_COLS), lambda i: (i, 0)),
        ],
        out_shape=[
            jax.ShapeDtypeStruct((n, 3 * WIDTH), BF16),
            jax.ShapeDtypeStruct((n, WIDTH), F32),
            jax.ShapeDtypeStruct((n, RW_COLS), F32),
            jax.ShapeDtypeStruct((n, G_COLS), F32),
        ],
        compiler_params=pltpu.CompilerParams(
            dimension_semantics=("arbitrary",), vmem_limit_bytes=VMEM_LIMIT),
        name="proj",
    )(x2, g_pre, w_in_bf)


def _t5_bucket_np(dist):
    n = np.maximum(dist, 0)
    nf = np.maximum(n, 1).astype(np.float32)
    large = REL_MAX_EXACT + (np.log(nf / np.float32(REL_MAX_EXACT)) / np.float32(math.log(REL_MAX_DIST / REL_MAX_EXACT))
                             * np.float32(REL_BUCKETS - REL_MAX_EXACT)).astype(np.int32)
    large = np.minimum(large, REL_BUCKETS - 1)
    return np.where(n < REL_MAX_EXACT, n, large).astype(np.int32)


def _bucket_tables():
    s = np.arange(MOBA_BLOCK)[:, None]
    t = np.arange(MOBA_BLOCK)[None, :]
    own = np.where(t >= s, _t5_bucket_np(t - s), -1).astype(np.int32)
    prev = _t5_bucket_np(MOBA_BLOCK + t - s)
    return own, prev


MASKED = -1e30
VT_ROWS = HEAD_DIM + BF16_SUBLANES
KIND_FAR, KIND_PREV, KIND_OWN = 0, 1, 2


def _moba_decode(s, bsz, half):
    per_pair = bsz * half
    return s // per_pair, (s % per_pair) // half, s % half


def _moba_kernel(own_b_ref, prev_b_ref, relb_ref, q_ref, k_ref, v_ref, za_ref, zb_ref, oa_ref, ob_ref,
                 tabs, kmean, kaug, vt, qaug, scores0, scores1, mx0, mx1, acc_ref, *, n_blk, bsz):
    half = n_blk // 2
    n_items = N_PAIRS * bsz * half
    step = pl.program_id(0)
    cur = jnp.minimum(step, n_items - 1)
    prv = jnp.maximum(step - 1, 0)
    hp, b, i = _moba_decode(cur, bsz, half)
    hp_p, b_p, i_p = _moba_decode(prv, bsz, half)
    seq_slot = (hp * bsz + b) % 2
    seq_slot_p = (hp_p * bsz + b_p) % 2
    blk = MOBA_BLOCK
    seq = n_blk * blk
    lane = lax.broadcasted_iota(jnp.int32, (1, LANES), 1)
    head_mask = [lane < HEAD_DIM, lane >= HEAD_DIM]
    flag_base = [HEAD_DIM, 0]
    neg_inf = jnp.float32(-jnp.inf)

    @pl.when((b == 0) & (i == 0))
    def _build_bias_tables():
        ob = own_b_ref[...]
        pb = prev_b_ref[...]
        for hh in range(2):
            far = relb_ref[REL_BUCKETS - 1, 2 * hp + hh]
            to = jnp.full((blk, blk), neg_inf, F32)
            tp = jnp.zeros((blk, blk), F32)
            for bkt in range(REL_BUCKETS):
                val = (relb_ref[bkt, 2 * hp + hh] - far) * LOG2E
                to = jnp.where(ob == bkt, val, to)
                tp = jnp.where(pb == bkt, val, tp)
            tabs[hh, KIND_FAR] = jnp.zeros((blk, blk), F32)
            tabs[hh, KIND_PREV] = tp
            tabs[hh, KIND_OWN] = to

    @pl.when(step == 0)
    def _placeholders_for_first_pass2():
        scores1[...] = jnp.zeros_like(scores1)
        mx1[...] = jnp.zeros_like(mx1)

    @pl.when((i == 0) & (step < n_items))
    def _per_sequence_setup():
        ones_row = (lax.broadcasted_iota(jnp.int32, (VT_ROWS - HEAD_DIM, blk), 0) == 0).astype(BF16)
        for j in range(n_blk):
            kj = k_ref[0, j * blk:(j + 1) * blk, :].astype(F32)
            kmean[j:j + 1, :] = jnp.mean(kj, axis=0, keepdims=True)
            vjt = v_ref[0, j * blk:(j + 1) * blk, :].astype(F32).T
            for hh in range(2):
                flag = (lane == flag_base[hh] + j).astype(F32)
                kaug[hh, j] = jnp.where(head_mask[hh], kj, flag).astype(BF16)
                vt[seq_slot, hh, j, 0:HEAD_DIM, :] = vjt[hh * HEAD_DIM:(hh + 1) * HEAD_DIM, :].astype(BF16)
                vt[seq_slot, hh, j, HEAD_DIM:VT_ROWS, :] = ones_row

        q = q_ref[0]
        blk_id = lax.broadcasted_iota(jnp.int32, (n_blk, seq), 0)
        q_blk = lax.broadcasted_iota(jnp.int32, (n_blk, seq), 1) // blk
        flag_row = lax.broadcasted_iota(jnp.int32, (n_blk, LANES), 0)
        flag_lane = lax.broadcasted_iota(jnp.int32, (n_blk, LANES), 1)
        km_hi, km_lo = _split2(kmean[...])
        for hh in range(2):
            q_h = jnp.where(head_mask[hh], q, jnp.zeros_like(q))
            g = _dot_nt(km_hi, q_h) + _dot_nt(km_lo, q_h)
            g = jnp.where(blk_id < q_blk, g, neg_inf)
            allowed = blk_id == q_blk
            for _ in range(MOBA_TOPK):
                mx = jnp.max(g, axis=0, keepdims=True)
                first = jnp.min(jnp.where(g == mx, blk_id, n_blk), axis=0, keepdims=True)
                hit = (blk_id == first) & (mx > neg_inf)
                allowed = allowed | hit
                g = jnp.where(hit, neg_inf, g)
            pen_t = jnp.where(allowed, 0.0, MASKED).astype(BF16)
            place = (flag_lane == flag_row + flag_base[hh]).astype(BF16)
            pen = _dot_tn(pen_t, place)
            qa = jnp.where(head_mask[hh], q, pen.astype(BF16))
            for j in range(n_blk):
                qaug[hh, j] = qa[j * blk:(j + 1) * blk, :]

    def tile_max(s):
        return jnp.max(s.reshape(blk // 8, 8, blk), axis=0)

    n_tiles = n_blk + 1

    def tile_ids(t, qb_a):
        qb_b = n_blk - 1 - qb_a
        is_a = t <= qb_a
        qb = jnp.where(is_a, qb_a, qb_b)
        kb = jnp.where(is_a, t, t - qb_a - 1)
        kind = jnp.where(kb == qb, KIND_OWN, jnp.where(kb == qb - 1, KIND_PREV, KIND_FAR))
        return is_a, qb, kb, kind

    def both_passes(sc_w, mx_w, sc_r, mx_r):
        m_rows = [[jnp.max(mx_r[hh, w], axis=0, keepdims=True) for w in range(2)] for hh in range(2)]
        mx_w[...] = jnp.full(mx_w.shape, neg_inf, F32)
        acc_ref[...] = jnp.zeros_like(acc_ref)
        tail_b = [None, None]
        q_b = [qaug[hh, n_blk - 1 - i] for hh in range(2)]
        for t in range(n_tiles):
            is_a, qb, kb, kind = tile_ids(t, i)
            which = jnp.where(is_a, 0, 1)
            if half <= t <= n_blk - 2:
                kind = None
            elif t >= n_blk - 1:
                kind = KIND_PREV if t == n_blk - 1 else KIND_OWN
            for hh in range(2):
                s = _dot_nt(kaug[hh, kb], q_b[hh] if t >= half else qaug[hh, qb])
                if kind is not None:
                    s = s + tabs[hh, kind]
                sc_w[hh, t] = s
                mx_w[hh, which] = jnp.maximum(mx_w[hh, which], tile_max(s))
            is_a, _, kb, _ = tile_ids(t, i_p)
            which = jnp.where(is_a, 0, 1)
            for hh in range(2):
                if t >= half:
                    p = jnp.exp2(sc_r[hh, t] - m_rows[hh][1]).astype(BF16)
                    d = _dot(vt[seq_slot_p, hh, kb], p)
                    tail_b[hh] = d if tail_b[hh] is None else tail_b[hh] + d
                else:
                    m_row = jnp.where(is_a, m_rows[hh][0], m_rows[hh][1])
                    p = jnp.exp2(sc_r[hh, t] - m_row).astype(BF16)
                    acc_ref[hh, which] += _dot(vt[seq_slot_p, hh, kb], p)
        for hh in range(2):
            acc_ref[hh, 1] += tail_b[hh]

        for w, (z_ref, o_ref) in enumerate(((za_ref, oa_ref), (zb_ref, ob_ref))):
            out_t = jnp.concatenate(
                [acc_ref[hh, w, 0:HEAD_DIM, :] / acc_ref[hh, w, HEAD_DIM:HEAD_DIM + 1, :] for hh in range(2)],
                axis=0)
            z = z_ref[0]
            o_ref[0] = out_t.T * (z * _sigmoid(z))

    @pl.when(step % 2 == 0)
    def _even_step():
        both_passes(scores0, mx0, scores1, mx1)

    @pl.when(step % 2 == 1)
    def _odd_step():
        both_passes(scores1, mx1, scores0, mx0)


def _moba(qkv, za, rel_bias):
    bsz, seq, _ = qkv.shape
    n_blk = seq // MOBA_BLOCK
    assert n_blk % 2 == 0
    half = n_blk // 2
    own_b, prev_b = _bucket_tables()
    blk = MOBA_BLOCK
    n_items = N_PAIRS * bsz * half
    kernel = functools.partial(_moba_kernel, n_blk=n_blk, bsz=bsz)

    def cur(s):
        return _moba_decode(jnp.minimum(s, n_items - 1), bsz, half)

    def prv(s):
        return _moba_decode(jnp.maximum(s - 1, 0), bsz, half)

    def seq_block(col0):
        def index_map(s):
            hp, b, _ = cur(s)
            return b, 0, col0 + hp
        return pl.BlockSpec((1, seq, LANES), index_map)

    def block_a(s):
        hp, b, i = prv(s)
        return b, i, hp

    def block_b_in(s):
        hp, b, i = prv(s)
        return b, n_blk - 1 - i, hp

    def block_b_out(s):
        hp, b, i = prv(s)
        return b, half - 1 - i, hp

    return pl.pallas_call(
        kernel,
        grid=(n_items + 1,),
        in_specs=[
            pl.BlockSpec((blk, blk), lambda s: (0, 0)),
            pl.BlockSpec((blk, blk), lambda s: (0, 0)),
            pl.BlockSpec(memory_space=pltpu.SMEM),
            seq_block(0), seq_block(N_PAIRS), seq_block(2 * N_PAIRS),
            pl.BlockSpec((1, blk, LANES), block_a),
            pl.BlockSpec((1, blk, LANES), block_b_in),
        ],
        out_specs=[
            pl.BlockSpec((1, blk, LANES), block_a),
            pl.BlockSpec((1, blk, LANES), block_b_out),
        ],
        out_shape=[
            jax.ShapeDtypeStruct((bsz, seq // 2, WIDTH), F32),
            jax.ShapeDtypeStruct((bsz, seq // 2, WIDTH), F32),
        ],
        scratch_shapes=[
            pltpu.VMEM((2, 3, blk, blk), F32),
            pltpu.VMEM((n_blk, LANES), F32),
            pltpu.VMEM((2, n_blk, blk, LANES), BF16),
            pltpu.VMEM((2, 2, n_blk, VT_ROWS, blk), BF16),
            pltpu.VMEM((2, n_blk, blk, LANES), BF16),
            pltpu.VMEM((2, n_blk + 1, blk, blk), F32),
            pltpu.VMEM((2, n_blk + 1, blk, blk), F32),
            pltpu.VMEM((2, 2, 8, blk), F32),
            pltpu.VMEM((2, 2, 8, blk), F32),
            pltpu.VMEM((2, 2, VT_ROWS, blk), F32),
        ],
        compiler_params=pltpu.CompilerParams(
            dimension_semantics=("arbitrary",), vmem_limit_bytes=VMEM_LIMIT),
        name="moba",
    )(jnp.asarray(own_b), jnp.asarray(prev_b), rel_bias, qkv, qkv, qkv, za, za)


RW_CHUNKS = 8
RW_GROUP_CHUNKS = 2


def _rwkv_kernel(rw_ref, mu_ref, w0_ref, lora_ref, a0_ref, kk_ref, ka_ref, rk_ref,
                 lnw_ref, lnb_ref, ltri_ref, ones_ref, o_ref,
                 prev_row, state, t_s, p_s, arb_s, ayk_s, at_s, vst_s, bkh_s, x1_s, loc_s, y_s,
                 lc_s, nc_s, dm_s, h0_s, akv_s):
    c = pl.program_id(1)
    C = CHUNK
    NC = RW_CHUNKS
    R = NC * C

    @pl.when(c == 0)
    def _reset():
        prev_row[...] = jnp.zeros_like(prev_row)
        state[...] = jnp.zeros_like(state)

    lane = lax.broadcasted_iota(jnp.int32, (1, LANES), 1)
    first_half = lane < HEAD_DIM
    second_half = jnp.logical_not(first_half)
    ones_bd = ones_ref[...]
    GC = RW_GROUP_CHUNKS
    RG = GC * C

    def seg_sum(x):
        outs = []
        for p in range(N_PAIRS):
            outs.append(_dot(x[:, p * LANES:(p + 1) * LANES].astype(BF16), ones_bd))
        return jnp.concatenate(outs, axis=1)

    def prepare(g, pre):
        start = g * RG
        cols = rw_ref[0, start:start + RG, :]
        prev = prev_row[0:1, :] if g == 0 else rw_ref[0, start - 1:start, :]
        row = lax.broadcasted_iota(jnp.int32, (RG, 1), 0)
        shifted = jnp.where(row == 0, prev, pltpu.roll(cols, 1, axis=0))
        xs = cols + (shifted - cols) * mu_ref[...]
        r = xs[:, 0:WIDTH]
        k = xs[:, WIDTH:2 * WIDTH]
        v = xs[:, 2 * WIDTH:3 * WIDTH]
        z = xs[:, 3 * WIDTH:4 * WIDTH]
        pre.update(r=r, v=v, z=z)
        yield

        lo_in = xs[:, 4 * WIDTH:4 * WIDTH + LANES]
        lo_in = jnp.where(first_half, jnp.tanh(lo_in), lo_in)
        lora = _dot(lo_in.astype(BF16), lora_ref[...])
        dw = lora[:, 0:WIDTH]
        da = lora[:, WIDTH:2 * WIDTH]
        lw = (-math.exp(-0.5)) * _sigmoid(w0_ref[...] + dw)
        a_ic = _sigmoid(a0_ref[...] + da)
        yield

        kk = k * kk_ref[...]
        kk = kk * lax.rsqrt(jnp.maximum(seg_sum(kk * kk), 1e-24))
        k2 = k * (1.0 + (a_ic - 1.0) * ka_ref[...])
        b_vec = kk * a_ic
        pre.update(k2=k2)
        yield

        l1, l2 = _split2(lw)
        ltri = ltri_ref[0:RG, 0:RG]
        g_inc = _dot(ltri, l1) + _dot(ltri, l2)
        g_end_rows = [g_inc[(n + 1) * C - 1:(n + 1) * C, :] for n in range(GC)]
        decay_end = [jnp.exp(g) for g in g_end_rows]
        d_end = jnp.concatenate([jnp.broadcast_to(d, (C, WIDTH)) for d in decay_end], axis=0)
        e_inc = jnp.exp(g_inc)
        e_neg = 1.0 / e_inc
        yield

        e_end = d_end * e_neg
        pre.update(
            decay_end=decay_end,
            rt=r * e_inc,
            at=kk * (-e_inc * jnp.exp(-lw)),
            bt=b_vec * e_neg, kt=k2 * e_neg, bh=b_vec * e_end, kh=k2 * e_end)

    def stack(x):
        return jnp.concatenate([jnp.where(first_half, x, 0.0), jnp.where(second_half, x, 0.0)], axis=0)

    def twice(x):
        return jnp.concatenate([x, x], axis=0)

    ri = lax.broadcasted_iota(jnp.int32, (2 * C, 2 * C), 0)
    ci = lax.broadcasted_iota(jnp.int32, (2 * C, 2 * C), 1)
    same_head = (ri < C) == (ci < C)
    incl = same_head & (ci <= ri)
    strict = same_head & (ci < ri)
    eye = (ri == ci).astype(F32)
    items = [(n, p) for n in range(NC) for p in range(N_PAIRS)]
    H = 2 * C

    def state_independent(group, pre):
        rt, at, bt, kt, bh, kh, v, decay_end = (pre[name] for name in
                                                ("rt", "at", "bt", "kt", "bh", "kh", "v", "decay_end"))
        for i, n, p in group:
            rows = slice(n * C, (n + 1) * C)
            sl = slice(p * LANES, (p + 1) * LANES)
            r_st = stack(rt[rows, sl]).astype(BF16)
            a_st = stack(at[rows, sl]).astype(BF16)
            ra = jnp.concatenate([r_st, a_st], axis=0)
            bk = jnp.concatenate([twice(bt[rows, sl].astype(BF16)), twice(kt[rows, sl].astype(BF16))], axis=0)
            m_all = _dot_nt(ra, bk)
            n_ab = jnp.where(strict, m_all[H:2 * H, 0:H], 0.0)
            t_s[i] = eye + n_ab
            p_s[i] = n_ab.astype(BF16)
            arb_s[i] = jnp.where(incl, m_all[0:H, 0:H], 0.0).astype(BF16)
            ayk_s[i, 0:H, :] = jnp.where(incl, m_all[0:H, H:2 * H], 0.0).astype(BF16)
            ayk_s[i, H:2 * H, :] = jnp.where(strict, m_all[H:2 * H, H:2 * H], 0.0).astype(BF16)
            x1_s[i, 0:H, :] = r_st
            at_s[i] = a_st
            vst_s[i] = twice(v[rows, sl].astype(BF16))
            bkh_s[i] = jnp.concatenate([stack(bh[rows, sl]), stack(kh[rows, sl])], axis=0).astype(BF16)

        yield

        for i, _, _ in group:
            pb = p_s[i]
            p_s[i] = _dot(pb, pb).astype(BF16)
        yield
        for _ in range(int(math.log2(C)) - 2):
            for i, _, _ in group:
                pb = p_s[i]
                tb = t_s[i]
                both = _dot(pb, jnp.concatenate([tb.astype(BF16), pb], axis=1))
                t_s[i] = tb + both[:, 0:H]
                p_s[i] = both[:, H:2 * H].astype(BF16)
            yield
        for i, _, _ in group:
            tb = t_s[i]
            t_s[i] = tb + _dot(p_s[i], tb.astype(BF16))
        yield

        for i, n, p in group:
            sl = slice(p * LANES, (p + 1) * LANES)
            yk = _dot(ayk_s[i], vst_s[i])
            loc_s[i, 0:H, :] = yk[0:H, :]
            akv_s[i] = yk[H:2 * H, :].astype(BF16)
            dm_s[i] = jnp.broadcast_to(decay_end[n][:, sl], (LANES, LANES)).T
        yield
        for i, _, _ in group:
            wz = _dot(t_s[i].astype(BF16), jnp.concatenate([at_s[i], akv_s[i]], axis=1))
            x1_s[i, H:2 * H, :] = wz[:, 0:LANES].astype(BF16)
            loc_s[i, H:2 * H, :] = wz[:, LANES:2 * LANES]
        yield
        for i, _, _ in group:
            w_bf = x1_s[i, H:2 * H, :]
            u_loc = loc_s[i, H:2 * H, :]
            lc_s[i] = _dot_tn(bkh_s[i, 0:H, :], w_bf).astype(BF16)
            nc_s[i] = _dot_tn(bkh_s[i], jnp.concatenate([u_loc.astype(BF16), vst_s[i]], axis=0))

    inv_n = 1.0 / HEAD_DIM

    def finish(g, pre):
        mine = [(i, n, p) for i, (n, p) in enumerate(items) if n // GC == g]
        for i, n, p in mine:
            h0 = state[p]
            h0_bf = h0.astype(BF16)
            h0_s[i] = h0_bf
            state[p] = dm_s[i] * h0 + _dot(lc_s[i], h0_bf) + nc_s[i]
            if p == N_PAIRS - 1:
                yield
        for i, n, p in mine:
            rs = _dot(x1_s[i], h0_s[i])
            akv_s[i] = (rs[H:2 * H, :] + loc_s[i, H:2 * H, :]).astype(BF16)
            loc_s[i, 0:H, :] = rs[0:H, :] + loc_s[i, 0:H, :]
        yield
        for i, n, p in mine:
            sl = slice(p * LANES, (p + 1) * LANES)
            y_st = loc_s[i, 0:H, :] + _dot(arb_s[i], akv_s[i])
            y_s[n * C:(n + 1) * C, sl] = jnp.where(first_half, y_st[0:C, :], y_st[C:H, :])
        yield
        rows = slice(g * RG, (g + 1) * RG)
        y = y_s[rows, :]
        mean = seg_sum(y) * inv_n
        d = y - mean
        var = seg_sum(d * d) * inv_n
        yn = d * lax.rsqrt(var + GN_EPS) * lnw_ref[...] + lnb_ref[...]
        bonus = seg_sum(pre["r"] * pre["k2"] * rk_ref[...]) * pre["v"]
        z = pre["z"]
        o_ref[0, rows, :] = (yn + bonus) * (z * _sigmoid(z))

    n_groups = NC // GC
    pres = [{} for _ in range(n_groups)]
    for _ in prepare(0, pres[0]):
        pass
    fin = iter(())
    for g in range(n_groups):
        group = [(i, n - g * GC, p) for i, (n, p) in enumerate(items) if n // GC == g]
        nxt = prepare(g + 1, pres[g + 1]) if g + 1 < n_groups else iter(())
        for _ in state_independent(group, pres[g]):
            next(nxt, None)
            next(fin, None)
        for _ in nxt:
            pass
        for _ in fin:
            pass
        fin = finish(g, pres[g])
    for _ in fin:
        pass
    prev_row[0:1, :] = rw_ref[0, R - 1:R, :]


def _rwkv(rw, mu, w0, w_up, a0, a_up, k_k, k_a, r_k, ln_w, ln_b):
    bsz, seq, _ = rw.shape
    C = CHUNK
    row = lambda t: t.reshape(1, -1).astype(F32)
    lora = jnp.zeros((LANES, 2 * WIDTH), F32)
    lora = lora.at[:LORA_RANK, :WIDTH].set(w_up).at[LORA_RANK:, WIDTH:].set(a_up)
    lora = lora.astype(BF16)
    R = RW_CHUNKS * C
    G = RW_CHUNKS * N_PAIRS
    H = 2 * C
    assert seq % R == 0 and H == LANES
    ltri = jnp.asarray(np.kron(np.eye(RW_CHUNKS), np.tril(np.ones((C, C)))).astype(np.float32)).astype(BF16)
    hid = np.arange(LANES) // HEAD_DIM
    ones_bd = jnp.asarray((hid[:, None] == hid[None, :]).astype(np.float32)).astype(BF16)
    vec = lambda n: pl.BlockSpec((1, n), lambda b, c: (0, 0))
    return pl.pallas_call(
        _rwkv_kernel,
        grid=(bsz, seq // R),
        in_specs=[
            pl.BlockSpec((1, R, RW_COLS), lambda b, c: (b, c, 0)),
            vec(RW_COLS), vec(WIDTH),
            pl.BlockSpec((LANES, 2 * WIDTH), lambda b, c: (0, 0)),
            vec(WIDTH), vec(WIDTH), vec(WIDTH), vec(WIDTH), vec(WIDTH), vec(WIDTH),
            pl.BlockSpec((R, R), lambda b, c: (0, 0)),
            pl.BlockSpec((LANES, LANES), lambda b, c: (0, 0)),
        ],
        out_specs=pl.BlockSpec((1, R, WIDTH), lambda b, c: (b, c, 0)),
        out_shape=jax.ShapeDtypeStruct((bsz, seq, WIDTH), F32),
        scratch_shapes=[
            pltpu.VMEM((8, RW_COLS), F32),
            pltpu.VMEM((N_PAIRS, LANES, LANES), F32),
            pltpu.VMEM((G, H, H), F32),
            pltpu.VMEM((G, H, H), BF16),
            pltpu.VMEM((G, H, H), BF16),
            pltpu.VMEM((G, 2 * H, H), BF16),
            pltpu.VMEM((G, H, LANES), BF16),
            pltpu.VMEM((G, H, LANES), BF16),
            pltpu.VMEM((G, 2 * H, LANES), BF16),
            pltpu.VMEM((G, 2 * H, LANES), BF16),
            pltpu.VMEM((G, 2 * H, LANES), F32),
            pltpu.VMEM((R, WIDTH), F32),
            pltpu.VMEM((G, LANES, LANES), BF16),
            pltpu.VMEM((G, LANES, LANES), F32),
            pltpu.VMEM((G, LANES, LANES), F32),
            pltpu.VMEM((G, LANES, LANES), BF16),
            pltpu.VMEM((G, H, LANES), BF16),
        ],
        compiler_params=pltpu.CompilerParams(
            dimension_semantics=("arbitrary", "arbitrary"), vmem_limit_bytes=VMEM_LIMIT),
        name="rwkv",
    )(rw, row(mu), row(w0), lora, row(a0), row(k_k), row(k_a), row(r_k), row(ln_w), row(ln_b),
      ltri, ones_bd)


def _merge_kernel(ya_lo_ref, ya_hi_ref, yb_ref, ga_ref, gb_ref, x_ref, p_ref, pa_ref, pb_ref, wo_ref, gpost_ref,
                  wpu_ref, wpg_ref, o_ref, *, tiles_per_half):
    first_half = pl.program_id(1) < tiles_per_half
    ya = jnp.where(first_half, ya_lo_ref[0], ya_hi_ref[0])
    ma = _dot(ya.astype(BF16), pa_ref[...])
    mb = _dot(yb_ref[0].astype(BF16), pb_ref[...])
    merged = _sigmoid(ga_ref[0]) * ma + _sigmoid(gb_ref[0]) * mb
    y = _dot(merged.astype(BF16), wo_ref[...])
    ms = jnp.mean(y * y, axis=-1, keepdims=True)
    h = x_ref[0] + y * lax.rsqrt(ms + RMS_EPS) * gpost_ref[...]
    e = _dot(p_ref[0].astype(BF16), wpu_ref[...])
    gate = _dot(h.astype(BF16), wpg_ref[...])
    o_ref[0] = h + _sigmoid(gate) * e


def _merge(ya_lo, ya_hi, yb, gates, x, p, p_a, p_b, w_out, g_post, w_pu, w_pg):
    bsz, seq, _ = x.shape
    tm = math.gcd(MERGE_TOKENS, seq // 2)
    th = seq // 2 // tm
    full = lambda a: pl.BlockSpec(a.shape, lambda b, t: (0, 0))
    tile = lambda w, col=0: pl.BlockSpec((1, tm, w), lambda b, t: (b, t, col))
    return pl.pallas_call(
        functools.partial(_merge_kernel, tiles_per_half=th),
        grid=(bsz, seq // tm),
        in_specs=[
            pl.BlockSpec((1, tm, WIDTH), lambda b, t: (b, jnp.minimum(t, th - 1), 0)),
            pl.BlockSpec((1, tm, WIDTH), lambda b, t: (b, jnp.maximum(t - th, 0), 0)),
            tile(WIDTH), tile(D_MODEL, 0), tile(D_MODEL, 1), tile(D_MODEL), tile(PLE_DIM),
            full(p_a), full(p_b), full(w_out), full(g_post), full(w_pu), full(w_pg),
        ],
        out_specs=tile(D_MODEL),
        out_shape=jax.ShapeDtypeStruct((bsz, seq, D_MODEL), F32),
        compiler_params=pltpu.CompilerParams(
            dimension_semantics=("arbitrary", "arbitrary"), vmem_limit_bytes=VMEM_LIMIT),
        name="merge",
    )(ya_lo, ya_hi, yb, gates, gates, x, p, p_a, p_b, w_out, g_post, w_pu, w_pg)


def kernel(x, p, g_pre, w_in, rel_bias, mu_shift, w0, w_up, a0, a_up, k_k, k_a, r_k, ln_x_w, ln_x_b,
           p_a, p_b, w_out, g_post, w_ple_up, w_ple_gate):
    bsz, seq, d = x.shape
    assert d == D_MODEL and seq % MOBA_BLOCK == 0 and seq // MOBA_BLOCK >= MOBA_TOPK
    assert g_pre.shape[0] == 1, "one layer"
    n = bsz * seq
    x2 = x.reshape(n, d).astype(F32)
    qkv, za, rw, gates = _project(x2, g_pre.astype(F32), w_in[0].astype(BF16))
    ya_lo, ya_hi = _moba(qkv.reshape(bsz, seq, 3 * WIDTH), za.reshape(bsz, seq, WIDTH), rel_bias.astype(F32))
    yb = _rwkv(rw.reshape(bsz, seq, RW_COLS), mu_shift[0], w0[0], w_up[0], a0[0], a_up[0], k_k[0], k_a[0],
               r_k[0], ln_x_w[0], ln_x_b[0])
    out = _merge(ya_lo, ya_hi, yb, gates.reshape(bsz, seq, G_COLS), x.astype(F32), p[0],
                 p_a[0].astype(BF16), p_b[0].astype(BF16), w_out[0].astype(BF16), g_post.astype(F32),
                 w_ple_up[0].astype(BF16), w_ple_gate[0].astype(BF16))
    return out.astype(x.dtype)
```

```python
import functools
import math

import jax
import jax.numpy as jnp
import numpy as np
from jax import lax
from jax.experimental import pallas as pl
from jax.experimental.pallas import tpu as pltpu

F32 = jnp.float32
BF16 = jnp.bfloat16

D_MODEL = 1024
PLE_DIM = 256
RMS_EPS = 1e-6
HEAD_DIM = 64
N_HEADS = 8
WIDTH = N_HEADS * HEAD_DIM
MOBA_BLOCK = 256
MOBA_TOPK = 3
REL_BUCKETS = 32
REL_MAX_EXACT = REL_BUCKETS // 2
REL_MAX_DIST = 128
LORA_RANK = 64
GN_EPS = 64e-5
A_COLS = 4 * WIDTH
RW_COLS = 4 * WIDTH + 2 * LORA_RANK
G_COLS = 2 * D_MODEL
IN_COLS = A_COLS + RW_COLS + G_COLS

LANES = 128
BF16_SUBLANES = 16
N_PAIRS = N_HEADS // 2
CHUNK = 64
PROJ_TOKENS = 512
MERGE_TOKENS = 512
VMEM_LIMIT = 48 * 1024 * 1024
LOG2E = math.log2(math.e)
Q_SCALE = LOG2E * HEAD_DIM ** -0.5


def _dot(a, b):
    return jnp.dot(a, b, preferred_element_type=F32)


def _dot_nt(a, b):
    return lax.dot_general(a, b, (((1,), (1,)), ((), ())), preferred_element_type=F32)


def _dot_tn(a, b):
    return lax.dot_general(a, b, (((0,), (0,)), ((), ())), preferred_element_type=F32)


def _split2(x):
    hi = x.astype(BF16)
    lo = (x - hi.astype(F32)).astype(BF16)
    return hi, lo


def _sigmoid(x):
    return 1.0 / (1.0 + jnp.exp(-x))


def _proj_kernel(x_ref, g_ref, w_ref, qkv_ref, za_ref, rw_ref, gt_ref):
    x = x_ref[...]
    ms = jnp.mean(x * x, axis=-1, keepdims=True)
    u = (x * lax.rsqrt(ms + RMS_EPS) * g_ref[...]).astype(BF16)
    step = 512

    def emit(out_ref, col0, width, scale_first=None):
        for c in range(0, width, step):
            w = min(step, width - c)
            y = _dot(u, w_ref[:, col0 + c:col0 + c + w])
            if scale_first is not None and c == 0:
                y = y * scale_first
            out_ref[:, c:c + w] = y.astype(out_ref.dtype)

    emit(qkv_ref, 0, 3 * WIDTH, scale_first=Q_SCALE)
    emit(za_ref, 3 * WIDTH, WIDTH)
    emit(rw_ref, A_COLS, RW_COLS)
    emit(gt_ref, A_COLS + RW_COLS, G_COLS)


def _project(x2, g_pre, w_in_bf):
    n = x2.shape[0]
    tm = math.gcd(PROJ_TOKENS, n)
    return pl.pallas_call(
        _proj_kernel,
        grid=(n // tm,),
        in_specs=[
            pl.BlockSpec((tm, D_MODEL), lambda i: (i, 0)),
            pl.BlockSpec((1, D_MODEL), lambda i: (0, 0)),
            pl.BlockSpec((D_MODEL, IN_COLS), lambda i: (0, 0), pipeline_mode=pl.Buffered(1)),
        ],
        out_specs=[
            pl.BlockSpec((tm, 3 * WIDTH), lambda i: (i, 0)),
            pl.BlockSpec((tm, WIDTH), lambda i: (i, 0)),
            pl.BlockSpec((tm, RW_COLS), lambda i: (i, 0)),
            pl.BlockSpec((tm, G_COLS), lambda i: (i, 0)),
        ],
        out_shape=[
            jax.ShapeDtypeStruct((n, 3 * WIDTH), BF16),
            jax.ShapeDtypeStruct((n, WIDTH), F32),
            jax.ShapeDtypeStruct((n, RW_COLS), F32),
            jax.ShapeDtypeStruct((n, G_COLS), F32),
        ],
        compiler_params=pltpu.CompilerParams(
            dimension_semantics=("arbitrary",), vmem_limit_bytes=VMEM_LIMIT),
        name="proj",
    )(x2, g_pre, w_in_bf)


def _t5_bucket_np(dist):
    n = np.maximum(dist, 0)
    nf = np.maximum(n, 1).astype(np.float32)
    large = REL_MAX_EXACT + (np.log(nf / np.float32(REL_MAX_EXACT)) / np.float32(math.log(REL_MAX_DIST / REL_MAX_EXACT))
                             * np.float32(REL_BUCKETS - REL_MAX_EXACT)).astype(np.int32)
    large = np.minimum(large, REL_BUCKETS - 1)
    return np.where(n < REL_MAX_EXACT, n, large).astype(np.int32)


def _bucket_tables():
    s = np.arange(MOBA_BLOCK)[:, None]
    t = np.arange(MOBA_BLOCK)[None, :]
    own = np.where(t >= s, _t5_bucket_np(t - s), -1).astype(np.int32)
    prev = _t5_bucket_np(MOBA_BLOCK + t - s)
    return own, prev


MASKED = -1e30
VT_ROWS = HEAD_DIM + BF16_SUBLANES
KIND_FAR, KIND_PREV, KIND_OWN = 0, 1, 2


def _moba_decode(s, bsz, half):
    per_pair = bsz * half
    return s // per_pair, (s % per_pair) // half, s % half


def _moba_kernel(own_b_ref, prev_b_ref, relb_ref, q_ref, k_ref, v_ref, za_ref, zb_ref, oa_ref, ob_ref,
                 tabs, kmean, kaug, vt, qaug, scores0, scores1, mx0, mx1, acc_ref, *, n_blk, bsz):
    half = n_blk // 2
    n_items = N_PAIRS * bsz * half
    step = pl.program_id(0)
    cur = jnp.minimum(step, n_items - 1)
    prv = jnp.maximum(step - 1, 0)
    hp, b, i = _moba_decode(cur, bsz, half)
    hp_p, b_p, i_p = _moba_decode(prv, bsz, half)
    seq_slot = (hp * bsz + b) % 2
    seq_slot_p = (hp_p * bsz + b_p) % 2
    blk = MOBA_BLOCK
    seq = n_blk * blk
    lane = lax.broadcasted_iota(jnp.int32, (1, LANES), 1)
    head_mask = [lane < HEAD_DIM, lane >= HEAD_DIM]
    flag_base = [HEAD_DIM, 0]
    neg_inf = jnp.float32(-jnp.inf)

    @pl.when((b == 0) & (i == 0))
    def _build_bias_tables():
        ob = own_b_ref[...]
        pb = prev_b_ref[...]
        for hh in range(2):
            far = relb_ref[REL_BUCKETS - 1, 2 * hp + hh]
            to = jnp.full((blk, blk), neg_inf, F32)
            tp = jnp.zeros((blk, blk), F32)
            for bkt in range(REL_BUCKETS):
                val = (relb_ref[bkt, 2 * hp + hh] - far) * LOG2E
                to = jnp.where(ob == bkt, val, to)
                tp = jnp.where(pb == bkt, val, tp)
            tabs[hh, KIND_FAR] = jnp.zeros((blk, blk), F32)
            tabs[hh, KIND_PREV] = tp
            tabs[hh, KIND_OWN] = to

    @pl.when(step == 0)
    def _placeholders_for_first_pass2():
        scores1[...] = jnp.zeros_like(scores1)
        mx1[...] = jnp.zeros_like(mx1)

    @pl.when((i == 0) & (step < n_items))
    def _per_sequence_setup():
        ones_row = (lax.broadcasted_iota(jnp.int32, (VT_ROWS - HEAD_DIM, blk), 0) == 0).astype(BF16)
        for j in range(n_blk):
            kj = k_ref[0, j * blk:(j + 1) * blk, :].astype(F32)
            kmean[j:j + 1, :] = jnp.mean(kj, axis=0, keepdims=True)
            vjt = v_ref[0, j * blk:(j + 1) * blk, :].astype(F32).T
            for hh in range(2):
                flag = (lane == flag_base[hh] + j).astype(F32)
                kaug[hh, j] = jnp.where(head_mask[hh], kj, flag).astype(BF16)
                vt[seq_slot, hh, j, 0:HEAD_DIM, :] = vjt[hh * HEAD_DIM:(hh + 1) * HEAD_DIM, :].astype(BF16)
                vt[seq_slot, hh, j, HEAD_DIM:VT_ROWS, :] = ones_row

        q = q_ref[0]
        blk_id = lax.broadcasted_iota(jnp.int32, (n_blk, seq), 0)
        q_blk = lax.broadcasted_iota(jnp.int32, (n_blk, seq), 1) // blk
        flag_row = lax.broadcasted_iota(jnp.int32, (n_blk, LANES), 0)
        flag_lane = lax.broadcasted_iota(jnp.int32, (n_blk, LANES), 1)
        km_hi, km_lo = _split2(kmean[...])
        for hh in range(2):
            q_h = jnp.where(head_mask[hh], q, jnp.zeros_like(q))
            g = _dot_nt(km_hi, q_h) + _dot_nt(km_lo, q_h)
            g = jnp.where(blk_id < q_blk, g, neg_inf)
            allowed = blk_id == q_blk
            for _ in range(MOBA_TOPK):
                mx = jnp.max(g, axis=0, keepdims=True)
                first = jnp.min(jnp.where(g == mx, blk_id, n_blk), axis=0, keepdims=True)
                hit = (blk_id == first) & (mx > neg_inf)
                allowed = allowed | hit
                g = jnp.where(hit, neg_inf, g)
            pen_t = jnp.where(allowed, 0.0, MASKED).astype(BF16)
            place = (flag_lane == flag_row + flag_base[hh]).astype(BF16)
            pen = _dot_tn(pen_t, place)
            qa = jnp.where(head_mask[hh], q, pen.astype(BF16))
            for j in range(n_blk):
                qaug[hh, j] = qa[j * blk:(j + 1) * blk, :]

    def tile_max(s):
        return jnp.max(s.reshape(blk // 8, 8, blk), axis=0)

    n_tiles = n_blk + 1

    def tile_ids(t, qb_a):
        qb_b = n_blk - 1 - qb_a
        is_a = t <= qb_a
        qb = jnp.where(is_a, qb_a, qb_b)
        kb = jnp.where(is_a, t, t - qb_a - 1)
        kind = jnp.where(kb == qb, KIND_OWN, jnp.where(kb == qb - 1, KIND_PREV, KIND_FAR))
        return is_a, qb, kb, kind

    def both_passes(sc_w, mx_w, sc_r, mx_r):
        m_rows = [[jnp.max(mx_r[hh, w], axis=0, keepdims=True) for w in range(2)] for hh in range(2)]
        mx_w[...] = jnp.full(mx_w.shape, neg_inf, F32)
        acc_ref[...] = jnp.zeros_like(acc_ref)
        tail_b = [None, None]
        q_b = [qaug[hh, n_blk - 1 - i] for hh in range(2)]
        for t in range(n_tiles):
            is_a, qb, kb, kind = tile_ids(t, i)
            which = jnp.where(is_a, 0, 1)
            if half <= t <= n_blk - 2:
                kind = None
            elif t >= n_blk - 1:
                kind = KIND_PREV if t == n_blk - 1 else KIND_OWN
            for hh in range(2):
                s = _dot_nt(kaug[hh, kb], q_b[hh] if t >= half else qaug[hh, qb])
                if kind is not None:
                    s = s + tabs[hh, kind]
                sc_w[hh, t] = s
                mx_w[hh, which] = jnp.maximum(mx_w[hh, which], tile_max(s))
            is_a, _, kb, _ = tile_ids(t, i_p)
            which = jnp.where(is_a, 0, 1)
            for hh in range(2):
                if t >= half:
                    p = jnp.exp2(sc_r[hh, t] - m_rows[hh][1]).astype(BF16)
                    d = _dot(vt[seq_slot_p, hh, kb], p)
                    tail_b[hh] = d if tail_b[hh] is None else tail_b[hh] + d
                else:
                    m_row = jnp.where(is_a, m_rows[hh][0], m_rows[hh][1])
                    p = jnp.exp2(sc_r[hh, t] - m_row).astype(BF16)
                    acc_ref[hh, which] += _dot(vt[seq_slot_p, hh, kb], p)
        for hh in range(2):
            acc_ref[hh, 1] += tail_b[hh]

        for w, (z_ref, o_ref) in enumerate(((za_ref, oa_ref), (zb_ref, ob_ref))):
            out_t = jnp.concatenate(
                [acc_ref[hh, w, 0:HEAD_DIM, :] / acc_ref[hh, w, HEAD_DIM:HEAD_DIM + 1, :] for hh in range(2)],
                axis=0)
            z = z_ref[0]
            o_ref[0] = (out_t.T * (z * _sigmoid(z))).astype(o_ref.dtype)

    @pl.when(step % 2 == 0)
    def _even_step():
        both_passes(scores0, mx0, scores1, mx1)

    @pl.when(step % 2 == 1)
    def _odd_step():
        both_passes(scores1, mx1, scores0, mx0)


def _moba(qkv, za, rel_bias):
    bsz, seq, _ = qkv.shape
    n_blk = seq // MOBA_BLOCK
    assert n_blk % 2 == 0
    half = n_blk // 2
    own_b, prev_b = _bucket_tables()
    blk = MOBA_BLOCK
    n_items = N_PAIRS * bsz * half
    kernel = functools.partial(_moba_kernel, n_blk=n_blk, bsz=bsz)

    def cur(s):
        return _moba_decode(jnp.minimum(s, n_items - 1), bsz, half)

    def prv(s):
        return _moba_decode(jnp.maximum(s - 1, 0), bsz, half)

    def seq_block(col0):
        def index_map(s):
            hp, b, _ = cur(s)
            return b, 0, col0 + hp
        return pl.BlockSpec((1, seq, LANES), index_map)

    def block_a(s):
        hp, b, i = prv(s)
        return b, i, hp

    def block_b_in(s):
        hp, b, i = prv(s)
        return b, n_blk - 1 - i, hp

    def block_b_out(s):
        hp, b, i = prv(s)
        return b, half - 1 - i, hp

    return pl.pallas_call(
        kernel,
        grid=(n_items + 1,),
        in_specs=[
            pl.BlockSpec((blk, blk), lambda s: (0, 0)),
            pl.BlockSpec((blk, blk), lambda s: (0, 0)),
            pl.BlockSpec(memory_space=pltpu.SMEM),
            seq_block(0), seq_block(N_PAIRS), seq_block(2 * N_PAIRS),
            pl.BlockSpec((1, blk, LANES), block_a),
            pl.BlockSpec((1, blk, LANES), block_b_in),
        ],
        out_specs=[
            pl.BlockSpec((1, blk, LANES), block_a),
            pl.BlockSpec((1, blk, LANES), block_b_out),
        ],
        out_shape=[
            jax.ShapeDtypeStruct((bsz, seq // 2, WIDTH), BF16),
            jax.ShapeDtypeStruct((bsz, seq // 2, WIDTH), BF16),
        ],
        scratch_shapes=[
            pltpu.VMEM((2, 3, blk, blk), F32),
            pltpu.VMEM((n_blk, LANES), F32),
            pltpu.VMEM((2, n_blk, blk, LANES), BF16),
            pltpu.VMEM((2, 2, n_blk, VT_ROWS, blk), BF16),
            pltpu.VMEM((2, n_blk, blk, LANES), BF16),
            pltpu.VMEM((2, n_blk + 1, blk, blk), F32),
            pltpu.VMEM((2, n_blk + 1, blk, blk), F32),
            pltpu.VMEM((2, 2, 8, blk), F32),
            pltpu.VMEM((2, 2, 8, blk), F32),
            pltpu.VMEM((2, 2, VT_ROWS, blk), F32),
        ],
        compiler_params=pltpu.CompilerParams(
            dimension_semantics=("arbitrary",), vmem_limit_bytes=VMEM_LIMIT),
        name="moba",
    )(jnp.asarray(own_b), jnp.asarray(prev_b), rel_bias, qkv, qkv, qkv, za, za)


RW_CHUNKS = 8
RW_GROUP_CHUNKS = 2


def _rwkv_kernel(rw_ref, mu_ref, w0_ref, lora_ref, a0_ref, kk_ref, ka_ref, rk_ref,
                 lnw_ref, lnb_ref, ltri_ref, ones_ref, o_ref,
                 prev_row, state, t_s, p_s, arb_s, ayk_s, at_s, vst_s, bkh_s, x1_s, loc_s, y_s,
                 lc_s, nc_s, dm_s, h0_s, akv_s):
    c = pl.program_id(1)
    C = CHUNK
    NC = RW_CHUNKS
    R = NC * C

    @pl.when(c == 0)
    def _reset():
        prev_row[...] = jnp.zeros_like(prev_row)
        state[...] = jnp.zeros_like(state)

    lane = lax.broadcasted_iota(jnp.int32, (1, LANES), 1)
    first_half = lane < HEAD_DIM
    second_half = jnp.logical_not(first_half)
    ones_bd = ones_ref[...]
    GC = RW_GROUP_CHUNKS
    RG = GC * C

    def seg_sum(x):
        outs = []
        for p in range(N_PAIRS):
            outs.append(_dot(x[:, p * LANES:(p + 1) * LANES].astype(BF16), ones_bd))
        return jnp.concatenate(outs, axis=1)

    def prepare(g, pre):
        start = g * RG
        cols = rw_ref[0, start:start + RG, :]
        prev = prev_row[0:1, :] if g == 0 else rw_ref[0, start - 1:start, :]
        row = lax.broadcasted_iota(jnp.int32, (RG, 1), 0)
        shifted = jnp.where(row == 0, prev, pltpu.roll(cols, 1, axis=0))
        xs = cols + (shifted - cols) * mu_ref[...]
        r = xs[:, 0:WIDTH]
        k = xs[:, WIDTH:2 * WIDTH]
        v = xs[:, 2 * WIDTH:3 * WIDTH]
        z = xs[:, 3 * WIDTH:4 * WIDTH]
        pre.update(r=r, v=v, z=z)
        yield

        lo_in = xs[:, 4 * WIDTH:4 * WIDTH + LANES]
        lo_in = jnp.where(first_half, jnp.tanh(lo_in), lo_in)
        lora = _dot(lo_in.astype(BF16), lora_ref[...])
        dw = lora[:, 0:WIDTH]
        da = lora[:, WIDTH:2 * WIDTH]
        lw = (-math.exp(-0.5)) * _sigmoid(w0_ref[...] + dw)
        a_ic = _sigmoid(a0_ref[...] + da)
        yield

        kk = k * kk_ref[...]
        kk = kk * lax.rsqrt(jnp.maximum(seg_sum(kk * kk), 1e-24))
        k2 = k * (1.0 + (a_ic - 1.0) * ka_ref[...])
        b_vec = kk * a_ic
        pre.update(k2=k2)
        yield

        l1, l2 = _split2(lw)
        ltri = ltri_ref[0:RG, 0:RG]
        g_inc = _dot(ltri, l1) + _dot(ltri, l2)
        g_end_rows = [g_inc[(n + 1) * C - 1:(n + 1) * C, :] for n in range(GC)]
        decay_end = [jnp.exp(g) for g in g_end_rows]
        d_end = jnp.concatenate([jnp.broadcast_to(d, (C, WIDTH)) for d in decay_end], axis=0)
        e_inc = jnp.exp(g_inc)
        e_neg = 1.0 / e_inc
        yield

        e_end = d_end * e_neg
        pre.update(
            decay_end=decay_end,
            rt=r * e_inc,
            at=kk * (-e_inc * jnp.exp(-lw)),
            bt=b_vec * e_neg, kt=k2 * e_neg, bh=b_vec * e_end, kh=k2 * e_end)

    def stack(x):
        return jnp.concatenate([jnp.where(first_half, x, 0.0), jnp.where(second_half, x, 0.0)], axis=0)

    def twice(x):
        return jnp.concatenate([x, x], axis=0)

    ri = lax.broadcasted_iota(jnp.int32, (2 * C, 2 * C), 0)
    ci = lax.broadcasted_iota(jnp.int32, (2 * C, 2 * C), 1)
    same_head = (ri < C) == (ci < C)
    incl = same_head & (ci <= ri)
    strict = same_head & (ci < ri)
    eye = (ri == ci).astype(F32)
    items = [(n, p) for n in range(NC) for p in range(N_PAIRS)]
    H = 2 * C

    def state_independent(group, pre):
        rt, at, bt, kt, bh, kh, v, decay_end = (pre[name] for name in
                                                ("rt", "at", "bt", "kt", "bh", "kh", "v", "decay_end"))
        for i, n, p in group:
            rows = slice(n * C, (n + 1) * C)
            sl = slice(p * LANES, (p + 1) * LANES)
            r_st = stack(rt[rows, sl]).astype(BF16)
            a_st = stack(at[rows, sl]).astype(BF16)
            ra = jnp.concatenate([r_st, a_st], axis=0)
            bk = jnp.concatenate([twice(bt[rows, sl].astype(BF16)), twice(kt[rows, sl].astype(BF16))], axis=0)
            m_all = _dot_nt(ra, bk)
            n_ab = jnp.where(strict, m_all[H:2 * H, 0:H], 0.0)
            t_s[i] = eye + n_ab
            p_s[i] = n_ab.astype(BF16)
            arb_s[i] = jnp.where(incl, m_all[0:H, 0:H], 0.0).astype(BF16)
            ayk_s[i, 0:H, :] = jnp.where(incl, m_all[0:H, H:2 * H], 0.0).astype(BF16)
            ayk_s[i, H:2 * H, :] = jnp.where(strict, m_all[H:2 * H, H:2 * H], 0.0).astype(BF16)
            x1_s[i, 0:H, :] = r_st
            at_s[i] = a_st
            vst_s[i] = twice(v[rows, sl].astype(BF16))
            bkh_s[i] = jnp.concatenate([stack(bh[rows, sl]), stack(kh[rows, sl])], axis=0).astype(BF16)

        yield

        for i, _, _ in group:
            pb = p_s[i]
            p_s[i] = _dot(pb, pb).astype(BF16)
        yield
        for _ in range(int(math.log2(C)) - 2):
            for i, _, _ in group:
                pb = p_s[i]
                tb = t_s[i]
                both = _dot(pb, jnp.concatenate([tb.astype(BF16), pb], axis=1))
                t_s[i] = tb + both[:, 0:H]
                p_s[i] = both[:, H:2 * H].astype(BF16)
            yield
        for i, _, _ in group:
            tb = t_s[i]
            t_s[i] = tb + _dot(p_s[i], tb.astype(BF16))
        yield

        for i, n, p in group:
            sl = slice(p * LANES, (p + 1) * LANES)
            yk = _dot(ayk_s[i], vst_s[i])
            loc_s[i, 0:H, :] = yk[0:H, :]
            akv_s[i] = yk[H:2 * H, :].astype(BF16)
            dm_s[i] = jnp.broadcast_to(decay_end[n][:, sl], (LANES, LANES)).T
        yield
        for i, _, _ in group:
            wz = _dot(t_s[i].astype(BF16), jnp.concatenate([at_s[i], akv_s[i]], axis=1))
            x1_s[i, H:2 * H, :] = wz[:, 0:LANES].astype(BF16)
            loc_s[i, H:2 * H, :] = wz[:, LANES:2 * LANES]
        yield
        for i, _, _ in group:
            w_bf = x1_s[i, H:2 * H, :]
            u_loc = loc_s[i, H:2 * H, :]
            lc_s[i] = _dot_tn(bkh_s[i, 0:H, :], w_bf).astype(BF16)
            nc_s[i] = _dot_tn(bkh_s[i], jnp.concatenate([u_loc.astype(BF16), vst_s[i]], axis=0))

    inv_n = 1.0 / HEAD_DIM

    def finish(g, pre):
        mine = [(i, n, p) for i, (n, p) in enumerate(items) if n // GC == g]
        for i, n, p in mine:
            h0 = state[p]
            h0_bf = h0.astype(BF16)
            h0_s[i] = h0_bf
            state[p] = dm_s[i] * h0 + _dot(lc_s[i], h0_bf) + nc_s[i]
            if p == N_PAIRS - 1:
                yield
        for i, n, p in mine:
            rs = _dot(x1_s[i], h0_s[i])
            akv_s[i] = (rs[H:2 * H, :] + loc_s[i, H:2 * H, :]).astype(BF16)
            loc_s[i, 0:H, :] = rs[0:H, :] + loc_s[i, 0:H, :]
        yield
        for i, n, p in mine:
            sl = slice(p * LANES, (p + 1) * LANES)
            y_st = loc_s[i, 0:H, :] + _dot(arb_s[i], akv_s[i])
            y_s[n * C:(n + 1) * C, sl] = jnp.where(first_half, y_st[0:C, :], y_st[C:H, :])
        yield
        rows = slice(g * RG, (g + 1) * RG)
        y = y_s[rows, :]
        mean = seg_sum(y) * inv_n
        d = y - mean
        var = seg_sum(d * d) * inv_n
        yn = d * lax.rsqrt(var + GN_EPS) * lnw_ref[...] + lnb_ref[...]
        bonus = seg_sum(pre["r"] * pre["k2"] * rk_ref[...]) * pre["v"]
        z = pre["z"]
        o_ref[0, rows, :] = ((yn + bonus) * (z * _sigmoid(z))).astype(o_ref.dtype)

    n_groups = NC // GC
    pres = [{} for _ in range(n_groups)]
    for _ in prepare(0, pres[0]):
        pass
    fin = iter(())
    for g in range(n_groups):
        group = [(i, n - g * GC, p) for i, (n, p) in enumerate(items) if n // GC == g]
        nxt = prepare(g + 1, pres[g + 1]) if g + 1 < n_groups else iter(())
        for _ in state_independent(group, pres[g]):
            next(nxt, None)
            next(fin, None)
        for _ in nxt:
            pass
        for _ in fin:
            pass
        fin = finish(g, pres[g])
    for _ in fin:
        pass
    prev_row[0:1, :] = rw_ref[0, R - 1:R, :]


def _rwkv(rw, mu, w0, w_up, a0, a_up, k_k, k_a, r_k, ln_w, ln_b):
    bsz, seq, _ = rw.shape
    C = CHUNK
    row = lambda t: t.reshape(1, -1).astype(F32)
    lora = jnp.zeros((LANES, 2 * WIDTH), F32)
    lora = lora.at[:LORA_RANK, :WIDTH].set(w_up).at[LORA_RANK:, WIDTH:].set(a_up)
    lora = lora.astype(BF16)
    R = RW_CHUNKS * C
    G = RW_CHUNKS * N_PAIRS
    H = 2 * C
    assert seq % R == 0 and H == LANES
    ltri = jnp.asarray(np.kron(np.eye(RW_CHUNKS), np.tril(np.ones((C, C)))).astype(np.float32)).astype(BF16)
    hid = np.arange(LANES) // HEAD_DIM
    ones_bd = jnp.asarray((hid[:, None] == hid[None, :]).astype(np.float32)).astype(BF16)
    vec = lambda n: pl.BlockSpec((1, n), lambda b, c: (0, 0))
    return pl.pallas_call(
        _rwkv_kernel,
        grid=(bsz, seq // R),
        in_specs=[
            pl.BlockSpec((1, R, RW_COLS), lambda b, c: (b, c, 0)),
            vec(RW_COLS), vec(WIDTH),
            pl.BlockSpec((LANES, 2 * WIDTH), lambda b, c: (0, 0)),
            vec(WIDTH), vec(WIDTH), vec(WIDTH), vec(WIDTH), vec(WIDTH), vec(WIDTH),
            pl.BlockSpec((R, R), lambda b, c: (0, 0)),
            pl.BlockSpec((LANES, LANES), lambda b, c: (0, 0)),
        ],
        out_specs=pl.BlockSpec((1, R, WIDTH), lambda b, c: (b, c, 0)),
        out_shape=jax.ShapeDtypeStruct((bsz, seq, WIDTH), BF16),
        scratch_shapes=[
            pltpu.VMEM((8, RW_COLS), F32),
            pltpu.VMEM((N_PAIRS, LANES, LANES), F32),
            pltpu.VMEM((G, H, H), F32),
            pltpu.VMEM((G, H, H), BF16),
            pltpu.VMEM((G, H, H), BF16),
            pltpu.VMEM((G, 2 * H, H), BF16),
            pltpu.VMEM((G, H, LANES), BF16),
            pltpu.VMEM((G, H, LANES), BF16),
            pltpu.VMEM((G, 2 * H, LANES), BF16),
            pltpu.VMEM((G, 2 * H, LANES), BF16),
            pltpu.VMEM((G, 2 * H, LANES), F32),
            pltpu.VMEM((R, WIDTH), F32),
            pltpu.VMEM((G, LANES, LANES), BF16),
            pltpu.VMEM((G, LANES, LANES), F32),
            pltpu.VMEM((G, LANES, LANES), F32),
            pltpu.VMEM((G, LANES, LANES), BF16),
            pltpu.VMEM((G, H, LANES), BF16),
        ],
        compiler_params=pltpu.CompilerParams(
            dimension_semantics=("arbitrary", "arbitrary"), vmem_limit_bytes=VMEM_LIMIT),
        name="rwkv",
    )(rw, row(mu), row(w0), lora, row(a0), row(k_k), row(k_a), row(r_k), row(ln_w), row(ln_b),
      ltri, ones_bd)


def _merge_kernel(ya_lo_ref, ya_hi_ref, yb_ref, ga_ref, gb_ref, x_ref, p_ref, pa_ref, pb_ref, wo_ref, gpost_ref,
                  wpu_ref, wpg_ref, o_ref, *, tiles_per_half):
    first_half = pl.program_id(1) < tiles_per_half
    ya = jnp.where(first_half, ya_lo_ref[0], ya_hi_ref[0])
    ma = _dot(ya.astype(BF16), pa_ref[...])
    mb = _dot(yb_ref[0].astype(BF16), pb_ref[...])
    merged = _sigmoid(ga_ref[0]) * ma + _sigmoid(gb_ref[0]) * mb
    y = _dot(merged.astype(BF16), wo_ref[...])
    ms = jnp.mean(y * y, axis=-1, keepdims=True)
    h = x_ref[0] + y * lax.rsqrt(ms + RMS_EPS) * gpost_ref[...]
    e = _dot(p_ref[0].astype(BF16), wpu_ref[...])
    gate = _dot(h.astype(BF16), wpg_ref[...])
    o_ref[0] = h + _sigmoid(gate) * e


def _merge(ya_lo, ya_hi, yb, gates, x, p, p_a, p_b, w_out, g_post, w_pu, w_pg):
    bsz, seq, _ = x.shape
    tm = math.gcd(MERGE_TOKENS, seq // 2)
    th = seq // 2 // tm
    full = lambda a: pl.BlockSpec(a.shape, lambda b, t: (0, 0))
    tile = lambda w, col=0: pl.BlockSpec((1, tm, w), lambda b, t: (b, t, col))
    return pl.pallas_call(
        functools.partial(_merge_kernel, tiles_per_half=th),
        grid=(bsz, seq // tm),
        in_specs=[
            pl.BlockSpec((1, tm, WIDTH), lambda b, t: (b, jnp.minimum(t, th - 1), 0)),
            pl.BlockSpec((1, tm, WIDTH), lambda b, t: (b, jnp.maximum(t - th, 0), 0)),
            tile(WIDTH), tile(D_MODEL, 0), tile(D_MODEL, 1), tile(D_MODEL), tile(PLE_DIM),
            full(p_a), full(p_b), full(w_out), full(g_post), full(w_pu), full(w_pg),
        ],
        out_specs=tile(D_MODEL),
        out_shape=jax.ShapeDtypeStruct((bsz, seq, D_MODEL), F32),
        compiler_params=pltpu.CompilerParams(
            dimension_semantics=("arbitrary", "arbitrary"), vmem_limit_bytes=VMEM_LIMIT),
        name="merge",
    )(ya_lo, ya_hi, yb, gates, gates, x, p, p_a, p_b, w_out, g_post, w_pu, w_pg)


def kernel(x, p, g_pre, w_in, rel_bias, mu_shift, w0, w_up, a0, a_up, k_k, k_a, r_k, ln_x_w, ln_x_b,
           p_a, p_b, w_out, g_post, w_ple_up, w_ple_gate):
    bsz, seq, d = x.shape
    assert d == D_MODEL and seq % MOBA_BLOCK == 0 and seq // MOBA_BLOCK >= MOBA_TOPK
    assert g_pre.shape[0] == 1, "one layer"
    n = bsz * seq
    x2 = x.reshape(n, d).astype(F32)
    qkv, za, rw, gates = _project(x2, g_pre.astype(F32), w_in[0].astype(BF16))
    ya_lo, ya_hi = _moba(qkv.reshape(bsz, seq, 3 * WIDTH), za.reshape(bsz, seq, WIDTH), rel_bias.astype(F32))
    yb = _rwkv(rw.reshape(bsz, seq, RW_COLS), mu_shift[0], w0[0], w_up[0], a0[0], a_up[0], k_k[0], k_a[0],
               r_k[0], ln_x_w[0], ln_x_b[0])
    out = _merge(ya_lo, ya_hi, yb, gates.reshape(bsz, seq, G_COLS), x.astype(F32), p[0],
                 p_a[0].astype(BF16), p_b[0].astype(BF16), w_out[0].astype(BF16), g_post.astype(F32),
                 w_ple_up[0].astype(BF16), w_ple_gate[0].astype(BF16))
    return out.astype(x.dtype)
```

```python
import functools
import math

import jax
import jax.numpy as jnp
import numpy as np
from jax import lax
from jax.experimental import pallas as pl
from jax.experimental.pallas import tpu as pltpu

F32 = jnp.float32
BF16 = jnp.bfloat16

D_MODEL = 1024
PLE_DIM = 256
RMS_EPS = 1e-6
HEAD_DIM = 64
N_HEADS = 8
WIDTH = N_HEADS * HEAD_DIM
MOBA_BLOCK = 256
MOBA_TOPK = 3
REL_BUCKETS = 32
REL_MAX_EXACT = REL_BUCKETS // 2
REL_MAX_DIST = 128
LORA_RANK = 64
GN_EPS = 64e-5
A_COLS = 4 * WIDTH
RW_COLS = 4 * WIDTH + 2 * LORA_RANK
G_COLS = 2 * D_MODEL
IN_COLS = A_COLS + RW_COLS + G_COLS

LANES = 128
BF16_SUBLANES = 16
N_PAIRS = N_HEADS // 2
CHUNK = 64
PROJ_TOKENS = 512
PROJ_COLS = WIDTH
MERGE_TOKENS = 512
VMEM_LIMIT = 48 * 1024 * 1024
LOG2E = math.log2(math.e)
Q_SCALE = LOG2E * HEAD_DIM ** -0.5


def _dot(a, b):
    return jnp.dot(a, b, preferred_element_type=F32)


def _dot_nt(a, b):
    return lax.dot_general(a, b, (((1,), (1,)), ((), ())), preferred_element_type=F32)


def _dot_tn(a, b):
    return lax.dot_general(a, b, (((0,), (0,)), ((), ())), preferred_element_type=F32)


def _split2(x):
    hi = x.astype(BF16)
    lo = (x - hi.astype(F32)).astype(BF16)
    return hi, lo


def _sigmoid(x):
    return 1.0 / (1.0 + jnp.exp(-x))


def _proj_kernel(x_ref, g_ref, w_ref, qkv_ref, za_ref, rw_ref, gt_ref):
    x = x_ref[...]
    ms = jnp.mean(x * x, axis=-1, keepdims=True)
    u = (x * lax.rsqrt(ms + RMS_EPS) * g_ref[...]).astype(BF16)

    def emit(out_ref, col0, width, scale_first=None):
        for c in range(0, width, PROJ_COLS):
            w = min(PROJ_COLS, width - c)
            y = _dot(u, w_ref[:, col0 + c:col0 + c + w])
            if scale_first is not None and c == 0:
                y = y * scale_first
            out_ref[:, c:c + w] = y.astype(out_ref.dtype)

    emit(qkv_ref, 0, 3 * WIDTH, scale_first=Q_SCALE)
    emit(za_ref, 3 * WIDTH, WIDTH)
    emit(rw_ref, A_COLS, RW_COLS)
    emit(gt_ref, A_COLS + RW_COLS, G_COLS)


def _project(x2, g_pre, w_in_bf):
    n = x2.shape[0]
    tm = math.gcd(PROJ_TOKENS, n)
    return pl.pallas_call(
        _proj_kernel,
        grid=(n // tm,),
        in_specs=[
            pl.BlockSpec((tm, D_MODEL), lambda i: (i, 0)),
            pl.BlockSpec((1, D_MODEL), lambda i: (0, 0)),
            pl.BlockSpec((D_MODEL, IN_COLS), lambda i: (0, 0), pipeline_mode=pl.Buffered(1)),
        ],
        out_specs=[
            pl.BlockSpec((tm, 3 * WIDTH), lambda i: (i, 0)),
            pl.BlockSpec((tm, WIDTH), lambda i: (i, 0)),
            pl.BlockSpec((tm, RW_COLS), lambda i: (i, 0)),
            pl.BlockSpec((tm, G_COLS), lambda i: (i, 0)),
        ],
        out_shape=[
            jax.ShapeDtypeStruct((n, 3 * WIDTH), BF16),
            jax.ShapeDtypeStruct((n, WIDTH), F32),
            jax.ShapeDtypeStruct((n, RW_COLS), F32),
            jax.ShapeDtypeStruct((n, G_COLS), F32),
        ],
        compiler_params=pltpu.CompilerParams(
            dimension_semantics=("arbitrary",), vmem_limit_bytes=VMEM_LIMIT),
        name="proj",
    )(x2, g_pre, w_in_bf)


def _t5_bucket_np(dist):
    n = np.maximum(dist, 0)
    nf = np.maximum(n, 1).astype(np.float32)
    large = REL_MAX_EXACT + (np.log(nf / np.float32(REL_MAX_EXACT)) / np.float32(math.log(REL_MAX_DIST / REL_MAX_EXACT))
                             * np.float32(REL_BUCKETS - REL_MAX_EXACT)).astype(np.int32)
    large = np.minimum(large, REL_BUCKETS - 1)
    return np.where(n < REL_MAX_EXACT, n, large).astype(np.int32)


def _bucket_tables():
    s = np.arange(MOBA_BLOCK)[:, None]
    t = np.arange(MOBA_BLOCK)[None, :]
    own = np.where(t >= s, _t5_bucket_np(t - s), -1).astype(np.int32)
    prev = _t5_bucket_np(MOBA_BLOCK + t - s)
    assert _t5_bucket_np(np.array(MOBA_BLOCK + 1)) == REL_BUCKETS - 1
    return own, prev


MASKED = -1e30
VT_ROWS = HEAD_DIM + BF16_SUBLANES
KIND_FAR, KIND_PREV, KIND_OWN = 0, 1, 2


def _moba_decode(s, bsz, half):
    per_pair = bsz * half
    return s // per_pair, (s % per_pair) // half, s % half


def _moba_kernel(own_b_ref, prev_b_ref, relb_ref, q_ref, k_ref, v_ref, za_ref, zb_ref, oa_ref, ob_ref,
                 tabs, kmean, kaug, vt, qaug, scores0, scores1, mx0, mx1, acc_ref, *, n_blk, bsz):
    half = n_blk // 2
    n_items = N_PAIRS * bsz * half
    step = pl.program_id(0)
    cur = jnp.minimum(step, n_items - 1)
    prv = jnp.maximum(step - 1, 0)
    hp, b, i = _moba_decode(cur, bsz, half)
    hp_p, b_p, i_p = _moba_decode(prv, bsz, half)
    seq_slot = (hp * bsz + b) % 2
    seq_slot_p = (hp_p * bsz + b_p) % 2
    blk = MOBA_BLOCK
    seq = n_blk * blk
    lane = lax.broadcasted_iota(jnp.int32, (1, LANES), 1)
    head_mask = [lane < HEAD_DIM, lane >= HEAD_DIM]
    flag_base = [HEAD_DIM, 0]
    neg_inf = jnp.float32(-jnp.inf)

    @pl.when((b == 0) & (i == 0))
    def _build_bias_tables():
        ob = own_b_ref[...]
        pb = prev_b_ref[...]
        for hh in range(2):
            far = relb_ref[REL_BUCKETS - 1, 2 * hp + hh]
            to = jnp.full((blk, blk), neg_inf, F32)
            tp = jnp.zeros((blk, blk), F32)
            for bkt in range(REL_BUCKETS):
                val = (relb_ref[bkt, 2 * hp + hh] - far) * LOG2E
                to = jnp.where(ob == bkt, val, to)
                tp = jnp.where(pb == bkt, val, tp)
            tabs[hh, KIND_FAR] = jnp.zeros((blk, blk), F32)
            tabs[hh, KIND_PREV] = tp
            tabs[hh, KIND_OWN] = to

    @pl.when(step == 0)
    def _placeholders_for_first_pass2():
        scores1[...] = jnp.zeros_like(scores1)
        mx1[...] = jnp.zeros_like(mx1)

    @pl.when((i == 0) & (step < n_items))
    def _per_sequence_setup():
        ones_row = (lax.broadcasted_iota(jnp.int32, (VT_ROWS - HEAD_DIM, blk), 0) == 0).astype(BF16)
        for j in range(n_blk):
            kj = k_ref[0, j * blk:(j + 1) * blk, :].astype(F32)
            kmean[j:j + 1, :] = jnp.mean(kj, axis=0, keepdims=True)
            vjt = v_ref[0, j * blk:(j + 1) * blk, :].astype(F32).T
            for hh in range(2):
                flag = (lane == flag_base[hh] + j).astype(F32)
                kaug[hh, j] = jnp.where(head_mask[hh], kj, flag).astype(BF16)
                vt[seq_slot, hh, j, 0:HEAD_DIM, :] = vjt[hh * HEAD_DIM:(hh + 1) * HEAD_DIM, :].astype(BF16)
                vt[seq_slot, hh, j, HEAD_DIM:VT_ROWS, :] = ones_row

        q = q_ref[0]
        blk_id = lax.broadcasted_iota(jnp.int32, (n_blk, seq), 0)
        q_blk = lax.broadcasted_iota(jnp.int32, (n_blk, seq), 1) // blk
        flag_row = lax.broadcasted_iota(jnp.int32, (n_blk, LANES), 0)
        flag_lane = lax.broadcasted_iota(jnp.int32, (n_blk, LANES), 1)
        km_hi, km_lo = _split2(kmean[...])
        for hh in range(2):
            q_h = jnp.where(head_mask[hh], q, jnp.zeros_like(q))
            g = _dot_nt(km_hi, q_h) + _dot_nt(km_lo, q_h)
            g = jnp.where(blk_id < q_blk, g, neg_inf)
            allowed = blk_id == q_blk
            for _ in range(MOBA_TOPK):
                mx = jnp.max(g, axis=0, keepdims=True)
                first = jnp.min(jnp.where(g == mx, blk_id, n_blk), axis=0, keepdims=True)
                hit = (blk_id == first) & (mx > neg_inf)
                allowed = allowed | hit
                g = jnp.where(hit, neg_inf, g)
            pen_t = jnp.where(allowed, 0.0, MASKED).astype(BF16)
            place = (flag_lane == flag_row + flag_base[hh]).astype(BF16)
            pen = _dot_tn(pen_t, place)
            qa = jnp.where(head_mask[hh], q, pen.astype(BF16))
            for j in range(n_blk):
                qaug[hh, j] = qa[j * blk:(j + 1) * blk, :]

    def tile_max(s):
        return jnp.max(s.reshape(blk // 8, 8, blk), axis=0)

    n_tiles = n_blk + 1

    def tile_ids(t, qb_a):
        qb_b = n_blk - 1 - qb_a
        is_a = t <= qb_a
        qb = jnp.where(is_a, qb_a, qb_b)
        kb = jnp.where(is_a, t, t - qb_a - 1)
        kind = jnp.where(kb == qb, KIND_OWN, jnp.where(kb == qb - 1, KIND_PREV, KIND_FAR))
        return is_a, qb, kb, kind

    def both_passes(sc_w, mx_w, sc_r, mx_r):
        m_rows = [[jnp.max(mx_r[hh, w], axis=0, keepdims=True) for w in range(2)] for hh in range(2)]
        mx_w[...] = jnp.full(mx_w.shape, neg_inf, F32)
        acc_ref[...] = jnp.zeros_like(acc_ref)
        tail_b = [None, None]
        q_b = [qaug[hh, n_blk - 1 - i] for hh in range(2)]
        for t in range(n_tiles):
            is_a, qb, kb, kind = tile_ids(t, i)
            which = jnp.where(is_a, 0, 1)
            if half <= t <= n_blk - 2:
                kind = None
            elif t >= n_blk - 1:
                kind = KIND_PREV if t == n_blk - 1 else KIND_OWN
            for hh in range(2):
                s = _dot_nt(kaug[hh, kb], q_b[hh] if t >= half else qaug[hh, qb])
                if kind is not None:
                    s = s + tabs[hh, kind]
                sc_w[hh, t] = s
                mx_w[hh, which] = jnp.maximum(mx_w[hh, which], tile_max(s))
            is_a, _, kb, _ = tile_ids(t, i_p)
            which = jnp.where(is_a, 0, 1)
            for hh in range(2):
                if t >= half:
                    p = jnp.exp2(sc_r[hh, t] - m_rows[hh][1]).astype(BF16)
                    d = _dot(vt[seq_slot_p, hh, kb], p)
                    tail_b[hh] = d if tail_b[hh] is None else tail_b[hh] + d
                else:
                    m_row = jnp.where(is_a, m_rows[hh][0], m_rows[hh][1])
                    p = jnp.exp2(sc_r[hh, t] - m_row).astype(BF16)
                    acc_ref[hh, which] += _dot(vt[seq_slot_p, hh, kb], p)
        for hh in range(2):
            acc_ref[hh, 1] += tail_b[hh]

        for w, (z_ref, o_ref) in enumerate(((za_ref, oa_ref), (zb_ref, ob_ref))):
            out_t = jnp.concatenate(
                [acc_ref[hh, w, 0:HEAD_DIM, :] / acc_ref[hh, w, HEAD_DIM:HEAD_DIM + 1, :] for hh in range(2)],
                axis=0)
            z = z_ref[0]
            o_ref[0] = out_t.T * (z * _sigmoid(z))

    @pl.when(step % 2 == 0)
    def _even_step():
        both_passes(scores0, mx0, scores1, mx1)

    @pl.when(step % 2 == 1)
    def _odd_step():
        both_passes(scores1, mx1, scores0, mx0)


def _moba(qkv, za, rel_bias):
    bsz, seq, _ = qkv.shape
    n_blk = seq // MOBA_BLOCK
    assert n_blk % 2 == 0
    half = n_blk // 2
    own_b, prev_b = _bucket_tables()
    blk = MOBA_BLOCK
    n_items = N_PAIRS * bsz * half
    kernel = functools.partial(_moba_kernel, n_blk=n_blk, bsz=bsz)

    def cur(s):
        return _moba_decode(jnp.minimum(s, n_items - 1), bsz, half)

    def prv(s):
        return _moba_decode(jnp.maximum(s - 1, 0), bsz, half)

    def seq_block(col0):
        def index_map(s):
            hp, b, _ = cur(s)
            return b, 0, col0 + hp
        return pl.BlockSpec((1, seq, LANES), index_map)

    def block_a(s):
        hp, b, i = prv(s)
        return b, i, hp

    def block_b_in(s):
        hp, b, i = prv(s)
        return b, n_blk - 1 - i, hp

    def block_b_out(s):
        hp, b, i = prv(s)
        return b, half - 1 - i, hp

    return pl.pallas_call(
        kernel,
        grid=(n_items + 1,),
        in_specs=[
            pl.BlockSpec((blk, blk), lambda s: (0, 0)),
            pl.BlockSpec((blk, blk), lambda s: (0, 0)),
            pl.BlockSpec(memory_space=pltpu.SMEM),
            seq_block(0), seq_block(N_PAIRS), seq_block(2 * N_PAIRS),
            pl.BlockSpec((1, blk, LANES), block_a),
            pl.BlockSpec((1, blk, LANES), block_b_in),
        ],
        out_specs=[
            pl.BlockSpec((1, blk, LANES), block_a),
            pl.BlockSpec((1, blk, LANES), block_b_out),
        ],
        out_shape=[
            jax.ShapeDtypeStruct((bsz, seq // 2, WIDTH), F32),
            jax.ShapeDtypeStruct((bsz, seq // 2, WIDTH), F32),
        ],
        scratch_shapes=[
            pltpu.VMEM((2, 3, blk, blk), F32),
            pltpu.VMEM((n_blk, LANES), F32),
            pltpu.VMEM((2, n_blk, blk, LANES), BF16),
            pltpu.VMEM((2, 2, n_blk, VT_ROWS, blk), BF16),
            pltpu.VMEM((2, n_blk, blk, LANES), BF16),
            pltpu.VMEM((2, n_blk + 1, blk, blk), F32),
            pltpu.VMEM((2, n_blk + 1, blk, blk), F32),
            pltpu.VMEM((2, 2, 8, blk), F32),
            pltpu.VMEM((2, 2, 8, blk), F32),
            pltpu.VMEM((2, 2, VT_ROWS, blk), F32),
        ],
        compiler_params=pltpu.CompilerParams(
            dimension_semantics=("arbitrary",), vmem_limit_bytes=VMEM_LIMIT),
        name="moba",
    )(jnp.asarray(own_b), jnp.asarray(prev_b), rel_bias, qkv, qkv, qkv, za, za)


RW_CHUNKS = 8
RW_GROUP_CHUNKS = 2


def _rwkv_kernel(rw_ref, mu_ref, w0_ref, lora_ref, a0_ref, kk_ref, ka_ref, rk_ref,
                 lnw_ref, lnb_ref, ltri_ref, ones_ref, o_ref,
                 prev_row, state, t_s, p_s, arb_s, ayk_s, at_s, vst_s, bkh_s, x1_s, loc_s, y_s,
                 lc_s, nc_s, dm_s, h0_s, akv_s):
    c = pl.program_id(1)
    C = CHUNK
    NC = RW_CHUNKS
    R = NC * C

    @pl.when(c == 0)
    def _reset():
        prev_row[...] = jnp.zeros_like(prev_row)
        state[...] = jnp.zeros_like(state)

    lane = lax.broadcasted_iota(jnp.int32, (1, LANES), 1)
    first_half = lane < HEAD_DIM
    second_half = jnp.logical_not(first_half)
    ones_bd = ones_ref[...]
    GC = RW_GROUP_CHUNKS
    RG = GC * C

    def seg_sum(x):
        outs = []
        for p in range(N_PAIRS):
            outs.append(_dot(x[:, p * LANES:(p + 1) * LANES].astype(BF16), ones_bd))
        return jnp.concatenate(outs, axis=1)

    def prepare(g, pre):
        start = g * RG
        cols = rw_ref[0, start:start + RG, :]
        prev = prev_row[0:1, :] if g == 0 else rw_ref[0, start - 1:start, :]
        row = lax.broadcasted_iota(jnp.int32, (RG, 1), 0)
        shifted = jnp.where(row == 0, prev, pltpu.roll(cols, 1, axis=0))
        xs = cols + (shifted - cols) * mu_ref[...]
        r = xs[:, 0:WIDTH]
        k = xs[:, WIDTH:2 * WIDTH]
        v = xs[:, 2 * WIDTH:3 * WIDTH]
        z = xs[:, 3 * WIDTH:4 * WIDTH]
        pre.update(r=r, v=v, z=z)
        yield

        lo_in = xs[:, 4 * WIDTH:4 * WIDTH + LANES]
        lo_in = jnp.where(first_half, jnp.tanh(lo_in), lo_in)
        lora = _dot(lo_in.astype(BF16), lora_ref[...])
        dw = lora[:, 0:WIDTH]
        da = lora[:, WIDTH:2 * WIDTH]
        lw = (-math.exp(-0.5)) * _sigmoid(w0_ref[...] + dw)
        a_ic = _sigmoid(a0_ref[...] + da)
        yield

        kk = k * kk_ref[...]
        kk = kk * lax.rsqrt(jnp.maximum(seg_sum(kk * kk), 1e-24))
        k2 = k * (1.0 + (a_ic - 1.0) * ka_ref[...])
        b_vec = kk * a_ic
        pre.update(k2=k2)
        yield

        l1, l2 = _split2(lw)
        ltri = ltri_ref[0:RG, 0:RG]
        g_inc = _dot(ltri, l1) + _dot(ltri, l2)
        g_end_rows = [g_inc[(n + 1) * C - 1:(n + 1) * C, :] for n in range(GC)]
        decay_end = [jnp.exp(g) for g in g_end_rows]
        d_end = jnp.concatenate([jnp.broadcast_to(d, (C, WIDTH)) for d in decay_end], axis=0)
        e_inc = jnp.exp(g_inc)
        e_neg = 1.0 / e_inc
        yield

        e_end = d_end * e_neg
        pre.update(
            decay_end=decay_end,
            rt=r * e_inc,
            at=kk * (-e_inc * jnp.exp(-lw)),
            bt=b_vec * e_neg, kt=k2 * e_neg, bh=b_vec * e_end, kh=k2 * e_end)

    def stack(x):
        return jnp.concatenate([jnp.where(first_half, x, 0.0), jnp.where(second_half, x, 0.0)], axis=0)

    def twice(x):
        return jnp.concatenate([x, x], axis=0)

    ri = lax.broadcasted_iota(jnp.int32, (2 * C, 2 * C), 0)
    ci = lax.broadcasted_iota(jnp.int32, (2 * C, 2 * C), 1)
    same_head = (ri < C) == (ci < C)
    incl = same_head & (ci <= ri)
    strict = same_head & (ci < ri)
    eye = (ri == ci).astype(F32)
    items = [(n, p) for n in range(NC) for p in range(N_PAIRS)]
    H = 2 * C

    def state_independent(group, pre):
        rt, at, bt, kt, bh, kh, v, decay_end = (pre[name] for name in
                                                ("rt", "at", "bt", "kt", "bh", "kh", "v", "decay_end"))
        for i, n, p in group:
            rows = slice(n * C, (n + 1) * C)
            sl = slice(p * LANES, (p + 1) * LANES)
            r_st = stack(rt[rows, sl]).astype(BF16)
            a_st = stack(at[rows, sl]).astype(BF16)
            ra = jnp.concatenate([r_st, a_st], axis=0)
            bk = jnp.concatenate([twice(bt[rows, sl].astype(BF16)), twice(kt[rows, sl].astype(BF16))], axis=0)
            m_all = _dot_nt(ra, bk)
            n_ab = jnp.where(strict, m_all[H:2 * H, 0:H], 0.0)
            t_s[i] = eye + n_ab
            p_s[i] = n_ab.astype(BF16)
            arb_s[i] = jnp.where(incl, m_all[0:H, 0:H], 0.0).astype(BF16)
            ayk_s[i, 0:H, :] = jnp.where(incl, m_all[0:H, H:2 * H], 0.0).astype(BF16)
            ayk_s[i, H:2 * H, :] = jnp.where(strict, m_all[H:2 * H, H:2 * H], 0.0).astype(BF16)
            x1_s[i, 0:H, :] = r_st
            at_s[i] = a_st
            vst_s[i] = twice(v[rows, sl].astype(BF16))
            bkh_s[i] = jnp.concatenate([stack(bh[rows, sl]), stack(kh[rows, sl])], axis=0).astype(BF16)

        yield

        for i, _, _ in group:
            pb = p_s[i]
            p_s[i] = _dot(pb, pb).astype(BF16)
        yield
        for _ in range(int(math.log2(C)) - 2):
            for i, _, _ in group:
                pb = p_s[i]
                tb = t_s[i]
                both = _dot(pb, jnp.concatenate([tb.astype(BF16), pb], axis=1))
                t_s[i] = tb + both[:, 0:H]
                p_s[i] = both[:, H:2 * H].astype(BF16)
            yield
        for i, _, _ in group:
            tb = t_s[i]
            t_s[i] = tb + _dot(p_s[i], tb.astype(BF16))
        yield

        for i, n, p in group:
            sl = slice(p * LANES, (p + 1) * LANES)
            yk = _dot(ayk_s[i], vst_s[i])
            loc_s[i, 0:H, :] = yk[0:H, :]
            akv_s[i] = yk[H:2 * H, :].astype(BF16)
            dm_s[i] = jnp.broadcast_to(decay_end[n][:, sl], (LANES, LANES)).T
        yield
        for i, _, _ in group:
            wz = _dot(t_s[i].astype(BF16), jnp.concatenate([at_s[i], akv_s[i]], axis=1))
            x1_s[i, H:2 * H, :] = wz[:, 0:LANES].astype(BF16)
            loc_s[i, H:2 * H, :] = wz[:, LANES:2 * LANES]
        yield
        for i, _, _ in group:
            w_bf = x1_s[i, H:2 * H, :]
            u_loc = loc_s[i, H:2 * H, :]
            lc_s[i] = _dot_tn(bkh_s[i, 0:H, :], w_bf).astype(BF16)
            nc_s[i] = _dot_tn(bkh_s[i], jnp.concatenate([u_loc.astype(BF16), vst_s[i]], axis=0))

    inv_n = 1.0 / HEAD_DIM

    def finish(g, pre):
        mine = [(i, n, p) for i, (n, p) in enumerate(items) if n // GC == g]
        for i, n, p in mine:
            h0 = state[p]
            h0_bf = h0.astype(BF16)
            h0_s[i] = h0_bf
            state[p] = dm_s[i] * h0 + _dot(lc_s[i], h0_bf) + nc_s[i]
            if p == N_PAIRS - 1:
                yield
        for i, n, p in mine:
            rs = _dot(x1_s[i], h0_s[i])
            akv_s[i] = (rs[H:2 * H, :] + loc_s[i, H:2 * H, :]).astype(BF16)
            loc_s[i, 0:H, :] = rs[0:H, :] + loc_s[i, 0:H, :]
        yield
        for i, n, p in mine:
            sl = slice(p * LANES, (p + 1) * LANES)
            y_st = loc_s[i, 0:H, :] + _dot(arb_s[i], akv_s[i])
            y_s[n * C:(n + 1) * C, sl] = jnp.where(first_half, y_st[0:C, :], y_st[C:H, :])
        yield
        rows = slice(g * RG, (g + 1) * RG)
        y = y_s[rows, :]
        mean = seg_sum(y) * inv_n
        d = y - mean
        var = seg_sum(d * d) * inv_n
        yn = d * lax.rsqrt(var + GN_EPS) * lnw_ref[...] + lnb_ref[...]
        bonus = seg_sum(pre["r"] * pre["k2"] * rk_ref[...]) * pre["v"]
        z = pre["z"]
        o_ref[0, rows, :] = (yn + bonus) * (z * _sigmoid(z))

    n_groups = NC // GC
    pres = [{} for _ in range(n_groups)]
    for _ in prepare(0, pres[0]):
        pass
    fin = iter(())
    for g in range(n_groups):
        group = [(i, n - g * GC, p) for i, (n, p) in enumerate(items) if n // GC == g]
        nxt = prepare(g + 1, pres[g + 1]) if g + 1 < n_groups else iter(())
        for _ in state_independent(group, pres[g]):
            next(nxt, None)
            next(fin, None)
        for _ in nxt:
            pass
        for _ in fin:
            pass
        fin = finish(g, pres[g])
    for _ in fin:
        pass
    prev_row[0:1, :] = rw_ref[0, R - 1:R, :]


def _rwkv(rw, mu, w0, w_up, a0, a_up, k_k, k_a, r_k, ln_w, ln_b):
    bsz, seq, _ = rw.shape
    C = CHUNK
    row = lambda t: t.reshape(1, -1).astype(F32)
    lora = jnp.zeros((LANES, 2 * WIDTH), F32)
    lora = lora.at[:LORA_RANK, :WIDTH].set(w_up).at[LORA_RANK:, WIDTH:].set(a_up)
    lora = lora.astype(BF16)
    R = RW_CHUNKS * C
    G = RW_CHUNKS * N_PAIRS
    H = 2 * C
    assert seq % R == 0 and H == LANES and RW_CHUNKS % RW_GROUP_CHUNKS == 0
    ltri = jnp.asarray(np.kron(np.eye(RW_CHUNKS), np.tril(np.ones((C, C)))).astype(np.float32)).astype(BF16)
    hid = np.arange(LANES) // HEAD_DIM
    ones_bd = jnp.asarray((hid[:, None] == hid[None, :]).astype(np.float32)).astype(BF16)
    vec = lambda n: pl.BlockSpec((1, n), lambda b, c: (0, 0))
    return pl.pallas_call(
        _rwkv_kernel,
        grid=(bsz, seq // R),
        in_specs=[
            pl.BlockSpec((1, R, RW_COLS), lambda b, c: (b, c, 0)),
            vec(RW_COLS), vec(WIDTH),
            pl.BlockSpec((LANES, 2 * WIDTH), lambda b, c: (0, 0)),
            vec(WIDTH), vec(WIDTH), vec(WIDTH), vec(WIDTH), vec(WIDTH), vec(WIDTH),
            pl.BlockSpec((R, R), lambda b, c: (0, 0)),
            pl.BlockSpec((LANES, LANES), lambda b, c: (0, 0)),
        ],
        out_specs=pl.BlockSpec((1, R, WIDTH), lambda b, c: (b, c, 0)),
        out_shape=jax.ShapeDtypeStruct((bsz, seq, WIDTH), F32),
        scratch_shapes=[
            pltpu.VMEM((8, RW_COLS), F32),
            pltpu.VMEM((N_PAIRS, LANES, LANES), F32),
            pltpu.VMEM((G, H, H), F32),
            pltpu.VMEM((G, H, H), BF16),
            pltpu.VMEM((G, H, H), BF16),
            pltpu.VMEM((G, 2 * H, H), BF16),
            pltpu.VMEM((G, H, LANES), BF16),
            pltpu.VMEM((G, H, LANES), BF16),
            pltpu.VMEM((G, 2 * H, LANES), BF16),
            pltpu.VMEM((G, 2 * H, LANES), BF16),
            pltpu.VMEM((G, 2 * H, LANES), F32),
            pltpu.VMEM((R, WIDTH), F32),
            pltpu.VMEM((G, LANES, LANES), BF16),
            pltpu.VMEM((G, LANES, LANES), F32),
            pltpu.VMEM((G, LANES, LANES), F32),
            pltpu.VMEM((G, LANES, LANES), BF16),
            pltpu.VMEM((G, H, LANES), BF16),
        ],
        compiler_params=pltpu.CompilerParams(
            dimension_semantics=("arbitrary", "arbitrary"), vmem_limit_bytes=VMEM_LIMIT),
        name="rwkv",
    )(rw, row(mu), row(w0), lora, row(a0), row(k_k), row(k_a), row(r_k), row(ln_w), row(ln_b),
      ltri, ones_bd)


def _merge_kernel(ya_lo_ref, ya_hi_ref, yb_ref, ga_ref, gb_ref, x_ref, p_ref, pa_ref, pb_ref, wo_ref, gpost_ref,
                  wpu_ref, wpg_ref, o_ref, *, tiles_per_half):
    first_half = pl.program_id(1) < tiles_per_half
    ya = jnp.where(first_half, ya_lo_ref[0], ya_hi_ref[0])
    ma = _dot(ya.astype(BF16), pa_ref[...])
    mb = _dot(yb_ref[0].astype(BF16), pb_ref[...])
    merged = _sigmoid(ga_ref[0]) * ma + _sigmoid(gb_ref[0]) * mb
    y = _dot(merged.astype(BF16), wo_ref[...])
    ms = jnp.mean(y * y, axis=-1, keepdims=True)
    h = x_ref[0] + y * lax.rsqrt(ms + RMS_EPS) * gpost_ref[...]
    e = _dot(p_ref[0].astype(BF16), wpu_ref[...])
    gate = _dot(h.astype(BF16), wpg_ref[...])
    o_ref[0] = h + _sigmoid(gate) * e


def _merge(ya_lo, ya_hi, yb, gates, x, p, p_a, p_b, w_out, g_post, w_pu, w_pg):
    bsz, seq, _ = x.shape
    tm = math.gcd(MERGE_TOKENS, seq // 2)
    th = seq // 2 // tm
    full = lambda a: pl.BlockSpec(a.shape, lambda b, t: (0, 0))
    tile = lambda w, col=0: pl.BlockSpec((1, tm, w), lambda b, t: (b, t, col))
    return pl.pallas_call(
        functools.partial(_merge_kernel, tiles_per_half=th),
        grid=(bsz, seq // tm),
        in_specs=[
            pl.BlockSpec((1, tm, WIDTH), lambda b, t: (b, jnp.minimum(t, th - 1), 0)),
            pl.BlockSpec((1, tm, WIDTH), lambda b, t: (b, jnp.maximum(t - th, 0), 0)),
            tile(WIDTH), tile(D_MODEL, 0), tile(D_MODEL, 1), tile(D_MODEL), tile(PLE_DIM),
            full(p_a), full(p_b), full(w_out), full(g_post), full(w_pu), full(w_pg),
        ],
        out_specs=tile(D_MODEL),
        out_shape=jax.ShapeDtypeStruct((bsz, seq, D_MODEL), F32),
        compiler_params=pltpu.CompilerParams(
            dimension_semantics=("arbitrary", "arbitrary"), vmem_limit_bytes=VMEM_LIMIT),
        name="merge",
    )(ya_lo, ya_hi, yb, gates, gates, x, p, p_a, p_b, w_out, g_post, w_pu, w_pg)


def kernel(x, p, g_pre, w_in, rel_bias, mu_shift, w0, w_up, a0, a_up, k_k, k_a, r_k, ln_x_w, ln_x_b,
           p_a, p_b, w_out, g_post, w_ple_up, w_ple_gate):
    bsz, seq, d = x.shape
    assert d == D_MODEL and seq % MOBA_BLOCK == 0 and seq // MOBA_BLOCK >= MOBA_TOPK
    assert g_pre.shape[0] == 1, "one layer"
    n = bsz * seq
    x2 = x.reshape(n, d).astype(F32)
    qkv, za, rw, gates = _project(x2, g_pre.astype(F32), w_in[0].astype(BF16))
    ya_lo, ya_hi = _moba(qkv.reshape(bsz, seq, 3 * WIDTH), za.reshape(bsz, seq, WIDTH), rel_bias.astype(F32))
    yb = _rwkv(rw.reshape(bsz, seq, RW_COLS), mu_shift[0], w0[0], w_up[0], a0[0], a_up[0], k_k[0], k_a[0],
               r_k[0], ln_x_w[0], ln_x_b[0])
    out = _merge(ya_lo, ya_hi, yb, gates.reshape(bsz, seq, G_COLS), x.astype(F32), p[0],
                 p_a[0].astype(BF16), p_b[0].astype(BF16), w_out[0].astype(BF16), g_post.astype(F32),
                 w_ple_up[0].astype(BF16), w_ple_gate[0].astype(BF16))
    return out.astype(x.dtype)
```

```python
import functools
import math

import jax
import jax.numpy as jnp
import numpy as np
from jax import lax
from jax.experimental import pallas as pl
from jax.experimental.pallas import tpu as pltpu

F32 = jnp.float32
BF16 = jnp.bfloat16

D_MODEL = 1024
PLE_DIM = 256
RMS_EPS = 1e-6
HEAD_DIM = 64
N_HEADS = 8
WIDTH = N_HEADS * HEAD_DIM
MOBA_BLOCK = 256
MOBA_TOPK = 3
REL_BUCKETS = 32
REL_MAX_EXACT = REL_BUCKETS // 2
REL_MAX_DIST = 128
LORA_RANK = 64
GN_EPS = 64e-5
A_COLS = 4 * WIDTH
RW_COLS = 4 * WIDTH + 2 * LORA_RANK
G_COLS = 2 * D_MODEL
IN_COLS = A_COLS + RW_COLS + G_COLS

LANES = 128
BF16_SUBLANES = 16
N_PAIRS = N_HEADS // 2
CHUNK = 64
PROJ_TOKENS = 512
PROJ_COLS = WIDTH
MERGE_TOKENS = 512
VMEM_LIMIT = 48 * 1024 * 1024
LOG2E = math.log2(math.e)
Q_SCALE = LOG2E * HEAD_DIM ** -0.5


def _dot(a, b):
    return jnp.dot(a, b, preferred_element_type=F32)


def _dot_nt(a, b):
    return lax.dot_general(a, b, (((1,), (1,)), ((), ())), preferred_element_type=F32)


def _dot_tn(a, b):
    return lax.dot_general(a, b, (((0,), (0,)), ((), ())), preferred_element_type=F32)


def _split2(x):
    hi = x.astype(BF16)
    lo = (x - hi.astype(F32)).astype(BF16)
    return hi, lo


def _sigmoid(x):
    return 1.0 / (1.0 + jnp.exp(-x))


def _proj_kernel(x_ref, g_ref, w_ref, qkv_ref, za_ref, rw_ref, gt_ref):
    x = x_ref[...]
    ms = jnp.mean(x * x, axis=-1, keepdims=True)
    u = (x * lax.rsqrt(ms + RMS_EPS) * g_ref[...]).astype(BF16)

    def emit(out_ref, col0, width, scale_first=None):
        for c in range(0, width, PROJ_COLS):
            w = min(PROJ_COLS, width - c)
            y = _dot(u, w_ref[:, col0 + c:col0 + c + w])
            if scale_first is not None and c == 0:
                y = y * scale_first
            out_ref[:, c:c + w] = y.astype(out_ref.dtype)

    emit(qkv_ref, 0, 3 * WIDTH, scale_first=Q_SCALE)
    emit(za_ref, 3 * WIDTH, WIDTH)
    emit(rw_ref, A_COLS, RW_COLS)
    emit(gt_ref, A_COLS + RW_COLS, G_COLS)


def _project(x2, g_pre, w_in_bf):
    n = x2.shape[0]
    tm = math.gcd(PROJ_TOKENS, n)
    return pl.pallas_call(
        _proj_kernel,
        grid=(n // tm,),
        in_specs=[
            pl.BlockSpec((tm, D_MODEL), lambda i: (i, 0)),
            pl.BlockSpec((1, D_MODEL), lambda i: (0, 0)),
            pl.BlockSpec((D_MODEL, IN_COLS), lambda i: (0, 0), pipeline_mode=pl.Buffered(1)),
        ],
        out_specs=[
            pl.BlockSpec((tm, 3 * WIDTH), lambda i: (i, 0)),
            pl.BlockSpec((tm, WIDTH), lambda i: (i, 0)),
            pl.BlockSpec((tm, RW_COLS), lambda i: (i, 0)),
            pl.BlockSpec((tm, G_COLS), lambda i: (i, 0)),
        ],
        out_shape=[
            jax.ShapeDtypeStruct((n, 3 * WIDTH), BF16),
            jax.ShapeDtypeStruct((n, WIDTH), F32),
            jax.ShapeDtypeStruct((n, RW_COLS), F32),
            jax.ShapeDtypeStruct((n, G_COLS), F32),
        ],
        compiler_params=pltpu.CompilerParams(
            dimension_semantics=("arbitrary",), vmem_limit_bytes=VMEM_LIMIT),
        name="proj",
    )(x2, g_pre, w_in_bf)


def _t5_bucket_np(dist):
    n = np.maximum(dist, 0)
    nf = np.maximum(n, 1).astype(np.float32)
    large = REL_MAX_EXACT + (np.log(nf / np.float32(REL_MAX_EXACT)) / np.float32(math.log(REL_MAX_DIST / REL_MAX_EXACT))
                             * np.float32(REL_BUCKETS - REL_MAX_EXACT)).astype(np.int32)
    large = np.minimum(large, REL_BUCKETS - 1)
    return np.where(n < REL_MAX_EXACT, n, large).astype(np.int32)


def _bucket_tables():
    s = np.arange(MOBA_BLOCK)[:, None]
    t = np.arange(MOBA_BLOCK)[None, :]
    own = np.where(t >= s, _t5_bucket_np(t - s), -1).astype(np.int32)
    prev = _t5_bucket_np(MOBA_BLOCK + t - s)
    assert _t5_bucket_np(np.array(MOBA_BLOCK + 1)) == REL_BUCKETS - 1
    return own, prev


MASKED = -1e30
VT_ROWS = HEAD_DIM + BF16_SUBLANES
KIND_FAR, KIND_PREV, KIND_OWN = 0, 1, 2


def _moba_decode(s, bsz, half):
    per_pair = bsz * half
    return s // per_pair, (s % per_pair) // half, s % half


def _moba_kernel(own_b_ref, prev_b_ref, relb_ref, q_ref, k_ref, v_ref, za_ref, zb_ref, oa_ref, ob_ref,
                 tabs, kmean, kaug, vt, qaug, scores0, scores1, mx0, mx1, acc_ref, *, n_blk, bsz):
    half = n_blk // 2
    n_items = N_PAIRS * bsz * half
    step = pl.program_id(0)
    cur = jnp.minimum(step, n_items - 1)
    prv = jnp.maximum(step - 1, 0)
    hp, b, i = _moba_decode(cur, bsz, half)
    hp_p, b_p, i_p = _moba_decode(prv, bsz, half)
    seq_slot = (hp * bsz + b) % 2
    seq_slot_p = (hp_p * bsz + b_p) % 2
    blk = MOBA_BLOCK
    seq = n_blk * blk
    lane = lax.broadcasted_iota(jnp.int32, (1, LANES), 1)
    head_mask = [lane < HEAD_DIM, lane >= HEAD_DIM]
    flag_base = [HEAD_DIM, 0]
    neg_inf = jnp.float32(-jnp.inf)

    @pl.when((b == 0) & (i == 0))
    def _build_bias_tables():
        ob = own_b_ref[...]
        pb = prev_b_ref[...]
        for hh in range(2):
            far = relb_ref[REL_BUCKETS - 1, 2 * hp + hh]
            to = jnp.full((blk, blk), neg_inf, F32)
            tp = jnp.zeros((blk, blk), F32)
            for bkt in range(REL_BUCKETS):
                val = (relb_ref[bkt, 2 * hp + hh] - far) * LOG2E
                to = jnp.where(ob == bkt, val, to)
                tp = jnp.where(pb == bkt, val, tp)
            tabs[hh, KIND_FAR] = jnp.zeros((blk, blk), F32)
            tabs[hh, KIND_PREV] = tp
            tabs[hh, KIND_OWN] = to

    @pl.when(step == 0)
    def _placeholders_for_first_pass2():
        scores1[...] = jnp.zeros_like(scores1)
        mx1[...] = jnp.zeros_like(mx1)

    @pl.when((i == 0) & (step < n_items))
    def _per_sequence_setup():
        ones_row = (lax.broadcasted_iota(jnp.int32, (VT_ROWS - HEAD_DIM, blk), 0) == 0).astype(BF16)
        for j in range(n_blk):
            kj = k_ref[0, j * blk:(j + 1) * blk, :].astype(F32)
            kmean[j:j + 1, :] = jnp.mean(kj, axis=0, keepdims=True)
            vjt = v_ref[0, j * blk:(j + 1) * blk, :].astype(F32).T
            for hh in range(2):
                flag = (lane == flag_base[hh] + j).astype(F32)
                kaug[hh, j] = jnp.where(head_mask[hh], kj, flag).astype(BF16)
                vt[seq_slot, hh, j, 0:HEAD_DIM, :] = vjt[hh * HEAD_DIM:(hh + 1) * HEAD_DIM, :].astype(BF16)
                vt[seq_slot, hh, j, HEAD_DIM:VT_ROWS, :] = ones_row

        q = q_ref[0]
        blk_id = lax.broadcasted_iota(jnp.int32, (n_blk, seq), 0)
        q_blk = lax.broadcasted_iota(jnp.int32, (n_blk, seq), 1) // blk
        flag_row = lax.broadcasted_iota(jnp.int32, (n_blk, LANES), 0)
        flag_lane = lax.broadcasted_iota(jnp.int32, (n_blk, LANES), 1)
        km_hi, km_lo = _split2(kmean[...])
        channel = lax.broadcasted_iota(jnp.int32, (LANES, 1), 0)
        head_rows = [channel < HEAD_DIM, channel >= HEAD_DIM]
        eye = (lax.broadcasted_iota(jnp.int32, (LANES, LANES), 0)
               == lax.broadcasted_iota(jnp.int32, (LANES, LANES), 1)).astype(BF16)
        q_t = _dot_nt(eye, q)
        for hh in range(2):
            q_h = jnp.where(head_mask[hh], q, jnp.zeros_like(q))
            g = _dot_nt(km_hi, q_h) + _dot_nt(km_lo, q_h)
            g = jnp.where(blk_id < q_blk, g, neg_inf)
            allowed = blk_id == q_blk
            for _ in range(MOBA_TOPK):
                mx = jnp.max(g, axis=0, keepdims=True)
                first = jnp.min(jnp.where(g == mx, blk_id, n_blk), axis=0, keepdims=True)
                hit = (blk_id == first) & (mx > neg_inf)
                allowed = allowed | hit
                g = jnp.where(hit, neg_inf, g)
            pen_t = jnp.where(allowed, 0.0, MASKED).astype(BF16)
            place = (flag_lane == flag_row + flag_base[hh]).astype(BF16)
            pen = _dot_tn(place, pen_t)
            qa = jnp.where(head_rows[hh], q_t, pen).astype(BF16)
            for j in range(n_blk):
                qaug[hh, j] = qa[:, j * blk:(j + 1) * blk]

    def tile_max(s):
        return jnp.max(s.reshape(blk // 8, 8, blk), axis=0)

    n_tiles = n_blk + 1

    def tile_ids(t, qb_a):
        qb_b = n_blk - 1 - qb_a
        is_a = t <= qb_a
        qb = jnp.where(is_a, qb_a, qb_b)
        kb = jnp.where(is_a, t, t - qb_a - 1)
        kind = jnp.where(kb == qb, KIND_OWN, jnp.where(kb == qb - 1, KIND_PREV, KIND_FAR))
        return is_a, qb, kb, kind

    def both_passes(sc_w, mx_w, sc_r, mx_r):
        m_rows = [[jnp.max(mx_r[hh, w], axis=0, keepdims=True) for w in range(2)] for hh in range(2)]
        mx_w[...] = jnp.full(mx_w.shape, neg_inf, F32)
        acc_ref[...] = jnp.zeros_like(acc_ref)
        tail_b = [None, None]
        q_b = [qaug[hh, n_blk - 1 - i] for hh in range(2)]
        for t in range(n_tiles):
            is_a, qb, kb, kind = tile_ids(t, i)
            which = jnp.where(is_a, 0, 1)
            if half <= t <= n_blk - 2:
                kind = None
            elif t >= n_blk - 1:
                kind = KIND_PREV if t == n_blk - 1 else KIND_OWN
            for hh in range(2):
                s = _dot(kaug[hh, kb], q_b[hh] if t >= half else qaug[hh, qb])
                if kind is not None:
                    s = s + tabs[hh, kind]
                sc_w[hh, t] = s
                mx_w[hh, which] = jnp.maximum(mx_w[hh, which], tile_max(s))
            is_a, _, kb, _ = tile_ids(t, i_p)
            which = jnp.where(is_a, 0, 1)
            for hh in range(2):
                if t >= half:
                    p = jnp.exp2(sc_r[hh, t] - m_rows[hh][1]).astype(BF16)
                    d = _dot(vt[seq_slot_p, hh, kb], p)
                    tail_b[hh] = d if tail_b[hh] is None else tail_b[hh] + d
                else:
                    m_row = jnp.where(is_a, m_rows[hh][0], m_rows[hh][1])
                    p = jnp.exp2(sc_r[hh, t] - m_row).astype(BF16)
                    acc_ref[hh, which] += _dot(vt[seq_slot_p, hh, kb], p)
        for hh in range(2):
            acc_ref[hh, 1] += tail_b[hh]

        for w, (z_ref, o_ref) in enumerate(((za_ref, oa_ref), (zb_ref, ob_ref))):
            out_t = jnp.concatenate(
                [acc_ref[hh, w, 0:HEAD_DIM, :] / acc_ref[hh, w, HEAD_DIM:HEAD_DIM + 1, :] for hh in range(2)],
                axis=0)
            z = z_ref[0]
            o_ref[0] = out_t.T * (z * _sigmoid(z))

    @pl.when(step % 2 == 0)
    def _even_step():
        both_passes(scores0, mx0, scores1, mx1)

    @pl.when(step % 2 == 1)
    def _odd_step():
        both_passes(scores1, mx1, scores0, mx0)


def _moba(qkv, za, rel_bias):
    bsz, seq, _ = qkv.shape
    n_blk = seq // MOBA_BLOCK
    assert n_blk % 2 == 0
    half = n_blk // 2
    own_b, prev_b = _bucket_tables()
    blk = MOBA_BLOCK
    n_items = N_PAIRS * bsz * half
    kernel = functools.partial(_moba_kernel, n_blk=n_blk, bsz=bsz)

    def cur(s):
        return _moba_decode(jnp.minimum(s, n_items - 1), bsz, half)

    def prv(s):
        return _moba_decode(jnp.maximum(s - 1, 0), bsz, half)

    def seq_block(col0):
        def index_map(s):
            hp, b, _ = cur(s)
            return b, 0, col0 + hp
        return pl.BlockSpec((1, seq, LANES), index_map)

    def block_a(s):
        hp, b, i = prv(s)
        return b, i, hp

    def block_b_in(s):
        hp, b, i = prv(s)
        return b, n_blk - 1 - i, hp

    def block_b_out(s):
        hp, b, i = prv(s)
        return b, half - 1 - i, hp

    return pl.pallas_call(
        kernel,
        grid=(n_items + 1,),
        in_specs=[
            pl.BlockSpec((blk, blk), lambda s: (0, 0)),
            pl.BlockSpec((blk, blk), lambda s: (0, 0)),
            pl.BlockSpec(memory_space=pltpu.SMEM),
            seq_block(0), seq_block(N_PAIRS), seq_block(2 * N_PAIRS),
            pl.BlockSpec((1, blk, LANES), block_a),
            pl.BlockSpec((1, blk, LANES), block_b_in),
        ],
        out_specs=[
            pl.BlockSpec((1, blk, LANES), block_a),
            pl.BlockSpec((1, blk, LANES), block_b_out),
        ],
        out_shape=[
            jax.ShapeDtypeStruct((bsz, seq // 2, WIDTH), F32),
            jax.ShapeDtypeStruct((bsz, seq // 2, WIDTH), F32),
        ],
        scratch_shapes=[
            pltpu.VMEM((2, 3, blk, blk), F32),
            pltpu.VMEM((n_blk, LANES), F32),
            pltpu.VMEM((2, n_blk, blk, LANES), BF16),
            pltpu.VMEM((2, 2, n_blk, VT_ROWS, blk), BF16),
            pltpu.VMEM((2, n_blk, LANES, blk), BF16),
            pltpu.VMEM((2, n_blk + 1, blk, blk), F32),
            pltpu.VMEM((2, n_blk + 1, blk, blk), F32),
            pltpu.VMEM((2, 2, 8, blk), F32),
            pltpu.VMEM((2, 2, 8, blk), F32),
            pltpu.VMEM((2, 2, VT_ROWS, blk), F32),
        ],
        compiler_params=pltpu.CompilerParams(
            dimension_semantics=("arbitrary",), vmem_limit_bytes=VMEM_LIMIT),
        name="moba",
    )(jnp.asarray(own_b), jnp.asarray(prev_b), rel_bias, qkv, qkv, qkv, za, za)


RW_CHUNKS = 8
RW_GROUP_CHUNKS = 2


def _rwkv_kernel(rw_ref, mu_ref, w0_ref, lora_ref, a0_ref, kk_ref, ka_ref, rk_ref,
                 lnw_ref, lnb_ref, ltri_ref, ones_ref, o_ref,
                 prev_row, state, t_s, p_s, arb_s, ayk_s, at_s, vst_s, bkh_s, x1_s, loc_s, y_s,
                 lc_s, nc_s, dm_s, h0_s, akv_s):
    c = pl.program_id(1)
    C = CHUNK
    NC = RW_CHUNKS
    R = NC * C

    @pl.when(c == 0)
    def _reset():
        prev_row[...] = jnp.zeros_like(prev_row)
        state[...] = jnp.zeros_like(state)

    lane = lax.broadcasted_iota(jnp.int32, (1, LANES), 1)
    first_half = lane < HEAD_DIM
    second_half = jnp.logical_not(first_half)
    ones_bd = ones_ref[...]
    GC = RW_GROUP_CHUNKS
    RG = GC * C

    def seg_sum(x):
        outs = []
        for p in range(N_PAIRS):
            outs.append(_dot(x[:, p * LANES:(p + 1) * LANES].astype(BF16), ones_bd))
        return jnp.concatenate(outs, axis=1)

    def prepare(g, pre):
        start = g * RG
        cols = rw_ref[0, start:start + RG, :]
        prev = prev_row[0:1, :] if g == 0 else rw_ref[0, start - 1:start, :]
        row = lax.broadcasted_iota(jnp.int32, (RG, 1), 0)
        shifted = jnp.where(row == 0, prev, pltpu.roll(cols, 1, axis=0))
        xs = cols + (shifted - cols) * mu_ref[...]
        r = xs[:, 0:WIDTH]
        k = xs[:, WIDTH:2 * WIDTH]
        v = xs[:, 2 * WIDTH:3 * WIDTH]
        z = xs[:, 3 * WIDTH:4 * WIDTH]
        pre.update(r=r, v=v, z=z)
        yield

        lo_in = xs[:, 4 * WIDTH:4 * WIDTH + LANES]
        lo_in = jnp.where(first_half, jnp.tanh(lo_in), lo_in)
        lora = _dot(lo_in.astype(BF16), lora_ref[...])
        dw = lora[:, 0:WIDTH]
        da = lora[:, WIDTH:2 * WIDTH]
        lw = (-math.exp(-0.5)) * _sigmoid(w0_ref[...] + dw)
        a_ic = _sigmoid(a0_ref[...] + da)
        yield

        kk = k * kk_ref[...]
        kk = kk * lax.rsqrt(jnp.maximum(seg_sum(kk * kk), 1e-24))
        k2 = k * (1.0 + (a_ic - 1.0) * ka_ref[...])
        b_vec = kk * a_ic
        pre.update(k2=k2)
        yield

        l1, l2 = _split2(lw)
        ltri = ltri_ref[0:RG, 0:RG]
        g_inc = _dot(ltri, l1) + _dot(ltri, l2)
        g_end_rows = [g_inc[(n + 1) * C - 1:(n + 1) * C, :] for n in range(GC)]
        decay_end = [jnp.exp(g) for g in g_end_rows]
        d_end = jnp.concatenate([jnp.broadcast_to(d, (C, WIDTH)) for d in decay_end], axis=0)
        e_inc = jnp.exp(g_inc)
        e_neg = 1.0 / e_inc
        yield

        e_end = d_end * e_neg
        pre.update(
            decay_end=decay_end,
            rt=r * e_inc,
            at=kk * (-e_inc * jnp.exp(-lw)),
            bt=b_vec * e_neg, kt=k2 * e_neg, bh=b_vec * e_end, kh=k2 * e_end)

    def stack(x):
        return jnp.concatenate([jnp.where(first_half, x, 0.0), jnp.where(second_half, x, 0.0)], axis=0)

    def twice(x):
        return jnp.concatenate([x, x], axis=0)

    ri = lax.broadcasted_iota(jnp.int32, (2 * C, 2 * C), 0)
    ci = lax.broadcasted_iota(jnp.int32, (2 * C, 2 * C), 1)
    same_head = (ri < C) == (ci < C)
    incl = same_head & (ci <= ri)
    strict = same_head & (ci < ri)
    eye = (ri == ci).astype(F32)
    items = [(n, p) for n in range(NC) for p in range(N_PAIRS)]
    H = 2 * C

    def state_independent(group, pre):
        rt, at, bt, kt, bh, kh, v, decay_end = (pre[name] for name in
                                                ("rt", "at", "bt", "kt", "bh", "kh", "v", "decay_end"))
        for i, n, p in group:
            rows = slice(n * C, (n + 1) * C)
            sl = slice(p * LANES, (p + 1) * LANES)
            r_st = stack(rt[rows, sl]).astype(BF16)
            a_st = stack(at[rows, sl]).astype(BF16)
            ra = jnp.concatenate([r_st, a_st], axis=0)
            bk = jnp.concatenate([twice(bt[rows, sl].astype(BF16)), twice(kt[rows, sl].astype(BF16))], axis=0)
            m_all = _dot_nt(ra, bk)
            n_ab = jnp.where(strict, m_all[H:2 * H, 0:H], 0.0)
            t_s[i] = eye + n_ab
            p_s[i] = n_ab.astype(BF16)
            arb_s[i] = jnp.where(incl, m_all[0:H, 0:H], 0.0).astype(BF16)
            ayk_s[i, 0:H, :] = jnp.where(incl, m_all[0:H, H:2 * H], 0.0).astype(BF16)
            ayk_s[i, H:2 * H, :] = jnp.where(strict, m_all[H:2 * H, H:2 * H], 0.0).astype(BF16)
            x1_s[i, 0:H, :] = r_st
            at_s[i] = a_st
            vst_s[i] = twice(v[rows, sl].astype(BF16))
            bkh_s[i] = jnp.concatenate([stack(bh[rows, sl]), stack(kh[rows, sl])], axis=0).astype(BF16)

        yield

        for i, _, _ in group:
            pb = p_s[i]
            p_s[i] = _dot(pb, pb).astype(BF16)
        yield
        for _ in range(int(math.log2(C)) - 2):
            for i, _, _ in group:
                pb = p_s[i]
                tb = t_s[i]
                both = _dot(pb, jnp.concatenate([tb.astype(BF16), pb], axis=1))
                t_s[i] = tb + both[:, 0:H]
                p_s[i] = both[:, H:2 * H].astype(BF16)
            yield
        for i, _, _ in group:
            tb = t_s[i]
            t_s[i] = tb + _dot(p_s[i], tb.astype(BF16))
        yield

        for i, n, p in group:
            sl = slice(p * LANES, (p + 1) * LANES)
            yk = _dot(ayk_s[i], vst_s[i])
            loc_s[i, 0:H, :] = yk[0:H, :]
            akv_s[i] = yk[H:2 * H, :].astype(BF16)
            dm_s[i] = jnp.broadcast_to(decay_end[n][:, sl], (LANES, LANES)).T
        yield
        for i, _, _ in group:
            wz = _dot(t_s[i].astype(BF16), jnp.concatenate([at_s[i], akv_s[i]], axis=1))
            x1_s[i, H:2 * H, :] = wz[:, 0:LANES].astype(BF16)
            loc_s[i, H:2 * H, :] = wz[:, LANES:2 * LANES]
        yield
        for i, _, _ in group:
            w_bf = x1_s[i, H:2 * H, :]
            u_loc = loc_s[i, H:2 * H, :]
            lc_s[i] = _dot_tn(bkh_s[i, 0:H, :], w_bf).astype(BF16)
            nc_s[i] = _dot_tn(bkh_s[i], jnp.concatenate([u_loc.astype(BF16), vst_s[i]], axis=0))

    inv_n = 1.0 / HEAD_DIM

    def finish(g, pre):
        mine = [(i, n, p) for i, (n, p) in enumerate(items) if n // GC == g]
        for i, n, p in mine:
            h0 = state[p]
            h0_bf = h0.astype(BF16)
            h0_s[i] = h0_bf
            state[p] = dm_s[i] * h0 + _dot(lc_s[i], h0_bf) + nc_s[i]
            if p == N_PAIRS - 1:
                yield
        for i, n, p in mine:
            rs = _dot(x1_s[i], h0_s[i])
            akv_s[i] = (rs[H:2 * H, :] + loc_s[i, H:2 * H, :]).astype(BF16)
            loc_s[i, 0:H, :] = rs[0:H, :] + loc_s[i, 0:H, :]
        yield
        for i, n, p in mine:
            sl = slice(p * LANES, (p + 1) * LANES)
            y_st = loc_s[i, 0:H, :] + _dot(arb_s[i], akv_s[i])
            y_s[n * C:(n + 1) * C, sl] = jnp.where(first_half, y_st[0:C, :], y_st[C:H, :])
        yield
        rows = slice(g * RG, (g + 1) * RG)
        y = y_s[rows, :]
        mean = seg_sum(y) * inv_n
        d = y - mean
        var = seg_sum(d * d) * inv_n
        yn = d * lax.rsqrt(var + GN_EPS) * lnw_ref[...] + lnb_ref[...]
        bonus = seg_sum(pre["r"] * pre["k2"] * rk_ref[...]) * pre["v"]
        z = pre["z"]
        o_ref[0, rows, :] = (yn + bonus) * (z * _sigmoid(z))

    n_groups = NC // GC
    pres = [{} for _ in range(n_groups)]
    for _ in prepare(0, pres[0]):
        pass
    fin = iter(())
    for g in range(n_groups):
        group = [(i, n - g * GC, p) for i, (n, p) in enumerate(items) if n // GC == g]
        nxt = prepare(g + 1, pres[g + 1]) if g + 1 < n_groups else iter(())
        for _ in state_independent(group, pres[g]):
            next(nxt, None)
            next(fin, None)
        for _ in nxt:
            pass
        for _ in fin:
            pass
        fin = finish(g, pres[g])
    for _ in fin:
        pass
    prev_row[0:1, :] = rw_ref[0, R - 1:R, :]


def _rwkv(rw, mu, w0, w_up, a0, a_up, k_k, k_a, r_k, ln_w, ln_b):
    bsz, seq, _ = rw.shape
    C = CHUNK
    row = lambda t: t.reshape(1, -1).astype(F32)
    lora = jnp.zeros((LANES, 2 * WIDTH), F32)
    lora = lora.at[:LORA_RANK, :WIDTH].set(w_up).at[LORA_RANK:, WIDTH:].set(a_up)
    lora = lora.astype(BF16)
    R = RW_CHUNKS * C
    G = RW_CHUNKS * N_PAIRS
    H = 2 * C
    assert seq % R == 0 and H == LANES and RW_CHUNKS % RW_GROUP_CHUNKS == 0
    ltri = jnp.asarray(np.kron(np.eye(RW_CHUNKS), np.tril(np.ones((C, C)))).astype(np.float32)).astype(BF16)
    hid = np.arange(LANES) // HEAD_DIM
    ones_bd = jnp.asarray((hid[:, None] == hid[None, :]).astype(np.float32)).astype(BF16)
    vec = lambda n: pl.BlockSpec((1, n), lambda b, c: (0, 0))
    return pl.pallas_call(
        _rwkv_kernel,
        grid=(bsz, seq // R),
        in_specs=[
            pl.BlockSpec((1, R, RW_COLS), lambda b, c: (b, c, 0)),
            vec(RW_COLS), vec(WIDTH),
            pl.BlockSpec((LANES, 2 * WIDTH), lambda b, c: (0, 0)),
            vec(WIDTH), vec(WIDTH), vec(WIDTH), vec(WIDTH), vec(WIDTH), vec(WIDTH),
            pl.BlockSpec((R, R), lambda b, c: (0, 0)),
            pl.BlockSpec((LANES, LANES), lambda b, c: (0, 0)),
        ],
        out_specs=pl.BlockSpec((1, R, WIDTH), lambda b, c: (b, c, 0)),
        out_shape=jax.ShapeDtypeStruct((bsz, seq, WIDTH), F32),
        scratch_shapes=[
            pltpu.VMEM((8, RW_COLS), F32),
            pltpu.VMEM((N_PAIRS, LANES, LANES), F32),
            pltpu.VMEM((G, H, H), F32),
            pltpu.VMEM((G, H, H), BF16),
            pltpu.VMEM((G, H, H), BF16),
            pltpu.VMEM((G, 2 * H, H), BF16),
            pltpu.VMEM((G, H, LANES), BF16),
            pltpu.VMEM((G, H, LANES), BF16),
            pltpu.VMEM((G, 2 * H, LANES), BF16),
            pltpu.VMEM((G, 2 * H, LANES), BF16),
            pltpu.VMEM((G, 2 * H, LANES), F32),
            pltpu.VMEM((R, WIDTH), F32),
            pltpu.VMEM((G, LANES, LANES), BF16),
            pltpu.VMEM((G, LANES, LANES), F32),
            pltpu.VMEM((G, LANES, LANES), F32),
            pltpu.VMEM((G, LANES, LANES), BF16),
            pltpu.VMEM((G, H, LANES), BF16),
        ],
        compiler_params=pltpu.CompilerParams(
            dimension_semantics=("arbitrary", "arbitrary"), vmem_limit_bytes=VMEM_LIMIT),
        name="rwkv",
    )(rw, row(mu), row(w0), lora, row(a0), row(k_k), row(k_a), row(r_k), row(ln_w), row(ln_b),
      ltri, ones_bd)


def _merge_kernel(ya_lo_ref, ya_hi_ref, yb_ref, ga_ref, gb_ref, x_ref, p_ref, pa_ref, pb_ref, wo_ref, gpost_ref,
                  wpu_ref, wpg_ref, o_ref, *, tiles_per_half):
    first_half = pl.program_id(1) < tiles_per_half
    ya = jnp.where(first_half, ya_lo_ref[0], ya_hi_ref[0])
    ma = _dot(ya.astype(BF16), pa_ref[...])
    mb = _dot(yb_ref[0].astype(BF16), pb_ref[...])
    merged = _sigmoid(ga_ref[0]) * ma + _sigmoid(gb_ref[0]) * mb
    y = _dot(merged.astype(BF16), wo_ref[...])
    ms = jnp.mean(y * y, axis=-1, keepdims=True)
    h = x_ref[0] + y * lax.rsqrt(ms + RMS_EPS) * gpost_ref[...]
    e = _dot(p_ref[0].astype(BF16), wpu_ref[...])
    gate = _dot(h.astype(BF16), wpg_ref[...])
    o_ref[0] = h + _sigmoid(gate) * e


def _merge(ya_lo, ya_hi, yb, gates, x, p, p_a, p_b, w_out, g_post, w_pu, w_pg):
    bsz, seq, _ = x.shape
    tm = math.gcd(MERGE_TOKENS, seq // 2)
    th = seq // 2 // tm
    full = lambda a: pl.BlockSpec(a.shape, lambda b, t: (0, 0))
    tile = lambda w, col=0: pl.BlockSpec((1, tm, w), lambda b, t: (b, t, col))
    return pl.pallas_call(
        functools.partial(_merge_kernel, tiles_per_half=th),
        grid=(bsz, seq // tm),
        in_specs=[
            pl.BlockSpec((1, tm, WIDTH), lambda b, t: (b, jnp.minimum(t, th - 1), 0)),
            pl.BlockSpec((1, tm, WIDTH), lambda b, t: (b, jnp.maximum(t - th, 0), 0)),
            tile(WIDTH), tile(D_MODEL, 0), tile(D_MODEL, 1), tile(D_MODEL), tile(PLE_DIM),
            full(p_a), full(p_b), full(w_out), full(g_post), full(w_pu), full(w_pg),
        ],
        out_specs=tile(D_MODEL),
        out_shape=jax.ShapeDtypeStruct((bsz, seq, D_MODEL), F32),
        compiler_params=pltpu.CompilerParams(
            dimension_semantics=("arbitrary", "arbitrary"), vmem_limit_bytes=VMEM_LIMIT),
        name="merge",
    )(ya_lo, ya_hi, yb, gates, gates, x, p, p_a, p_b, w_out, g_post, w_pu, w_pg)


def kernel(x, p, g_pre, w_in, rel_bias, mu_shift, w0, w_up, a0, a_up, k_k, k_a, r_k, ln_x_w, ln_x_b,
           p_a, p_b, w_out, g_post, w_ple_up, w_ple_gate):
    bsz, seq, d = x.shape
    assert d == D_MODEL and seq % MOBA_BLOCK == 0 and seq // MOBA_BLOCK >= MOBA_TOPK
    assert g_pre.shape[0] == 1, "one layer"
    n = bsz * seq
    x2 = x.reshape(n, d).astype(F32)
    qkv, za, rw, gates = _project(x2, g_pre.astype(F32), w_in[0].astype(BF16))
    ya_lo, ya_hi = _moba(qkv.reshape(bsz, seq, 3 * WIDTH), za.reshape(bsz, seq, WIDTH), rel_bias.astype(F32))
    yb = _rwkv(rw.reshape(bsz, seq, RW_COLS), mu_shift[0], w0[0], w_up[0], a0[0], a_up[0], k_k[0], k_a[0],
               r_k[0], ln_x_w[0], ln_x_b[0])
    out = _merge(ya_lo, ya_hi, yb, gates.reshape(bsz, seq, G_COLS), x.astype(F32), p[0],
                 p_a[0].astype(BF16), p_b[0].astype(BF16), w_out[0].astype(BF16), g_post.astype(F32),
                 w_ple_up[0].astype(BF16), w_ple_gate[0].astype(BF16))
    return out.astype(x.dtype)
```

```python
import functools
import math

import jax
import jax.numpy as jnp
import numpy as np
from jax import lax
from jax.experimental import pallas as pl
from jax.experimental.pallas import tpu as pltpu

F32 = jnp.float32
BF16 = jnp.bfloat16

D_MODEL = 1024
PLE_DIM = 256
RMS_EPS = 1e-6
HEAD_DIM = 64
N_HEADS = 8
WIDTH = N_HEADS * HEAD_DIM
MOBA_BLOCK = 256
MOBA_TOPK = 3
REL_BUCKETS = 32
REL_MAX_EXACT = REL_BUCKETS // 2
REL_MAX_DIST = 128
LORA_RANK = 64
GN_EPS = 64e-5
A_COLS = 4 * WIDTH
RW_COLS = 4 * WIDTH + 2 * LORA_RANK
G_COLS = 2 * D_MODEL
IN_COLS = A_COLS + RW_COLS + G_COLS

LANES = 128
BF16_SUBLANES = 16
N_PAIRS = N_HEADS // 2
CHUNK = 64
PROJ_TOKENS = 512
PROJ_COLS = WIDTH
MERGE_TOKENS = 512
VMEM_LIMIT = 48 * 1024 * 1024
LOG2E = math.log2(math.e)
Q_SCALE = LOG2E * HEAD_DIM ** -0.5


def _dot(a, b):
    return jnp.dot(a, b, preferred_element_type=F32)


def _dot_nt(a, b):
    return lax.dot_general(a, b, (((1,), (1,)), ((), ())), preferred_element_type=F32)


def _dot_tn(a, b):
    return lax.dot_general(a, b, (((0,), (0,)), ((), ())), preferred_element_type=F32)


def _split2(x):
    hi = x.astype(BF16)
    lo = (x - hi.astype(F32)).astype(BF16)
    return hi, lo


def _sigmoid(x):
    return 1.0 / (1.0 + jnp.exp(-x))


def _proj_kernel(x_ref, g_ref, w_ref, qkv_ref, za_ref, rw_ref, gt_ref):
    x = x_ref[...]
    ms = jnp.mean(x * x, axis=-1, keepdims=True)
    u = (x * lax.rsqrt(ms + RMS_EPS) * g_ref[...]).astype(BF16)

    def emit(out_ref, col0, width, scale_first=None):
        for c in range(0, width, PROJ_COLS):
            w = min(PROJ_COLS, width - c)
            y = _dot(u, w_ref[:, col0 + c:col0 + c + w])
            if scale_first is not None and c == 0:
                y = y * scale_first
            out_ref[:, c:c + w] = y.astype(out_ref.dtype)

    emit(qkv_ref, 0, 3 * WIDTH, scale_first=Q_SCALE)
    emit(za_ref, 3 * WIDTH, WIDTH)
    emit(rw_ref, A_COLS, RW_COLS)
    emit(gt_ref, A_COLS + RW_COLS, G_COLS)


def _project(x2, g_pre, w_in_bf):
    n = x2.shape[0]
    tm = math.gcd(PROJ_TOKENS, n)
    return pl.pallas_call(
        _proj_kernel,
        grid=(n // tm,),
        in_specs=[
            pl.BlockSpec((tm, D_MODEL), lambda i: (i, 0)),
            pl.BlockSpec((1, D_MODEL), lambda i: (0, 0)),
            pl.BlockSpec((D_MODEL, IN_COLS), lambda i: (0, 0), pipeline_mode=pl.Buffered(1)),
        ],
        out_specs=[
            pl.BlockSpec((tm, 3 * WIDTH), lambda i: (i, 0)),
            pl.BlockSpec((tm, WIDTH), lambda i: (i, 0)),
            pl.BlockSpec((tm, RW_COLS), lambda i: (i, 0)),
            pl.BlockSpec((tm, G_COLS), lambda i: (i, 0)),
        ],
        out_shape=[
            jax.ShapeDtypeStruct((n, 3 * WIDTH), BF16),
            jax.ShapeDtypeStruct((n, WIDTH), F32),
            jax.ShapeDtypeStruct((n, RW_COLS), F32),
            jax.ShapeDtypeStruct((n, G_COLS), F32),
        ],
        compiler_params=pltpu.CompilerParams(
            dimension_semantics=("arbitrary",), vmem_limit_bytes=VMEM_LIMIT),
        name="proj",
    )(x2, g_pre, w_in_bf)


def _t5_bucket_np(dist):
    n = np.maximum(dist, 0)
    nf = np.maximum(n, 1).astype(np.float32)
    large = REL_MAX_EXACT + (np.log(nf / np.float32(REL_MAX_EXACT)) / np.float32(math.log(REL_MAX_DIST / REL_MAX_EXACT))
                             * np.float32(REL_BUCKETS - REL_MAX_EXACT)).astype(np.int32)
    large = np.minimum(large, REL_BUCKETS - 1)
    return np.where(n < REL_MAX_EXACT, n, large).astype(np.int32)


def _bucket_tables():
    s = np.arange(MOBA_BLOCK)[:, None]
    t = np.arange(MOBA_BLOCK)[None, :]
    own = np.where(t >= s, _t5_bucket_np(t - s), -1).astype(np.int32)
    prev = _t5_bucket_np(MOBA_BLOCK + t - s)
    assert _t5_bucket_np(np.array(MOBA_BLOCK + 1)) == REL_BUCKETS - 1
    return own, prev


MASKED = -1e30
VT_ROWS = HEAD_DIM + BF16_SUBLANES
KIND_FAR, KIND_PREV, KIND_OWN = 0, 1, 2


def _moba_decode(s, bsz, half):
    per_pair = bsz * half
    return s // per_pair, (s % per_pair) // half, s % half


def _moba_kernel(own_b_ref, prev_b_ref, relb_ref, q_ref, k_ref, v_ref, za_ref, zb_ref, oa_ref, ob_ref,
                 tabs, kmean, kaug, vt, qaug, scores0, scores1, mx0, mx1, acc_ref, *, n_blk, bsz):
    half = n_blk // 2
    n_items = N_PAIRS * bsz * half
    step = pl.program_id(0)
    cur = jnp.minimum(step, n_items - 1)
    prv = jnp.maximum(step - 1, 0)
    hp, b, i = _moba_decode(cur, bsz, half)
    hp_p, b_p, i_p = _moba_decode(prv, bsz, half)
    seq_slot = (hp * bsz + b) % 2
    seq_slot_p = (hp_p * bsz + b_p) % 2
    blk = MOBA_BLOCK
    seq = n_blk * blk
    lane = lax.broadcasted_iota(jnp.int32, (1, LANES), 1)
    head_mask = [lane < HEAD_DIM, lane >= HEAD_DIM]
    flag_base = [HEAD_DIM, 0]
    neg_inf = jnp.float32(-jnp.inf)

    @pl.when((b == 0) & (i == 0))
    def _build_bias_tables():
        ob = own_b_ref[...]
        pb = prev_b_ref[...]
        for hh in range(2):
            far = relb_ref[REL_BUCKETS - 1, 2 * hp + hh]
            to = jnp.full((blk, blk), neg_inf, F32)
            tp = jnp.zeros((blk, blk), F32)
            for bkt in range(REL_BUCKETS):
                val = (relb_ref[bkt, 2 * hp + hh] - far) * LOG2E
                to = jnp.where(ob == bkt, val, to)
                tp = jnp.where(pb == bkt, val, tp)
            tabs[hh, KIND_FAR] = jnp.zeros((blk, blk), F32)
            tabs[hh, KIND_PREV] = tp
            tabs[hh, KIND_OWN] = to

    @pl.when(step == 0)
    def _placeholders_for_first_pass2():
        scores1[...] = jnp.zeros_like(scores1)
        mx1[...] = jnp.zeros_like(mx1)

    @pl.when((i == 0) & (step < n_items))
    def _per_sequence_setup():
        ones_row = (lax.broadcasted_iota(jnp.int32, (VT_ROWS - HEAD_DIM, blk), 0) == 0).astype(BF16)
        for j in range(n_blk):
            kj = k_ref[0, j * blk:(j + 1) * blk, :].astype(F32)
            kmean[j:j + 1, :] = jnp.mean(kj, axis=0, keepdims=True)
            vjt = v_ref[0, j * blk:(j + 1) * blk, :].astype(F32).T
            for hh in range(2):
                flag = (lane == flag_base[hh] + j).astype(F32)
                kaug[hh, j] = jnp.where(head_mask[hh], kj, flag).astype(BF16)
                vt[seq_slot, hh, j, 0:HEAD_DIM, :] = vjt[hh * HEAD_DIM:(hh + 1) * HEAD_DIM, :].astype(BF16)
                vt[seq_slot, hh, j, HEAD_DIM:VT_ROWS, :] = ones_row

        q = q_ref[0]
        blk_id = lax.broadcasted_iota(jnp.int32, (n_blk, seq), 0)
        q_blk = lax.broadcasted_iota(jnp.int32, (n_blk, seq), 1) // blk
        flag_row = lax.broadcasted_iota(jnp.int32, (n_blk, LANES), 0)
        flag_lane = lax.broadcasted_iota(jnp.int32, (n_blk, LANES), 1)
        km_hi, km_lo = _split2(kmean[...])
        channel = lax.broadcasted_iota(jnp.int32, (LANES, 1), 0)
        head_rows = [channel < HEAD_DIM, channel >= HEAD_DIM]
        eye = (lax.broadcasted_iota(jnp.int32, (LANES, LANES), 0)
               == lax.broadcasted_iota(jnp.int32, (LANES, LANES), 1)).astype(BF16)
        q_t = _dot_nt(eye, q)
        for hh in range(2):
            q_h = jnp.where(head_rows[hh], q_t, 0.0).astype(BF16)
            g = _dot(km_hi, q_h) + _dot(km_lo, q_h)
            g = jnp.where(blk_id < q_blk, g, neg_inf)
            allowed = blk_id == q_blk
            for _ in range(MOBA_TOPK):
                mx = jnp.max(g, axis=0, keepdims=True)
                first = jnp.min(jnp.where(g == mx, blk_id, n_blk), axis=0, keepdims=True)
                hit = (blk_id == first) & (mx > neg_inf)
                allowed = allowed | hit
                g = jnp.where(hit, neg_inf, g)
            pen_t = jnp.where(allowed, 0.0, MASKED).astype(BF16)
            place = (flag_lane == flag_row + flag_base[hh]).astype(BF16)
            pen = _dot_tn(place, pen_t)
            qa = jnp.where(head_rows[hh], q_t, pen).astype(BF16)
            for j in range(n_blk):
                qaug[hh, j] = qa[:, j * blk:(j + 1) * blk]

    def tile_max(s):
        return jnp.max(s.reshape(blk // 8, 8, blk), axis=0)

    n_tiles = n_blk + 1

    def tile_ids(t, qb_a):
        qb_b = n_blk - 1 - qb_a
        is_a = t <= qb_a
        qb = jnp.where(is_a, qb_a, qb_b)
        kb = jnp.where(is_a, t, t - qb_a - 1)
        kind = jnp.where(kb == qb, KIND_OWN, jnp.where(kb == qb - 1, KIND_PREV, KIND_FAR))
        return is_a, qb, kb, kind

    def both_passes(sc_w, mx_w, sc_r, mx_r):
        m_rows = [[jnp.max(mx_r[hh, w], axis=0, keepdims=True) for w in range(2)] for hh in range(2)]
        mx_w[...] = jnp.full(mx_w.shape, neg_inf, F32)
        acc_ref[...] = jnp.zeros_like(acc_ref)
        tail_b = [None, None]
        q_b = [qaug[hh, n_blk - 1 - i] for hh in range(2)]
        for t in range(n_tiles):
            is_a, qb, kb, kind = tile_ids(t, i)
            which = jnp.where(is_a, 0, 1)
            if half <= t <= n_blk - 2:
                kind = None
            elif t >= n_blk - 1:
                kind = KIND_PREV if t == n_blk - 1 else KIND_OWN
            for hh in range(2):
                s = _dot(kaug[hh, kb], q_b[hh] if t >= half else qaug[hh, qb])
                if kind is not None:
                    s = s + tabs[hh, kind]
                sc_w[hh, t] = s
                mx_w[hh, which] = jnp.maximum(mx_w[hh, which], tile_max(s))
            is_a, _, kb, _ = tile_ids(t, i_p)
            which = jnp.where(is_a, 0, 1)
            for hh in range(2):
                if t >= half:
                    p = jnp.exp2(sc_r[hh, t] - m_rows[hh][1]).astype(BF16)
                    d = _dot(vt[seq_slot_p, hh, kb], p)
                    tail_b[hh] = d if tail_b[hh] is None else tail_b[hh] + d
                else:
                    m_row = jnp.where(is_a, m_rows[hh][0], m_rows[hh][1])
                    p = jnp.exp2(sc_r[hh, t] - m_row).astype(BF16)
                    acc_ref[hh, which] += _dot(vt[seq_slot_p, hh, kb], p)
        for hh in range(2):
            acc_ref[hh, 1] += tail_b[hh]

        for w, (z_ref, o_ref) in enumerate(((za_ref, oa_ref), (zb_ref, ob_ref))):
            out_t = jnp.concatenate(
                [acc_ref[hh, w, 0:HEAD_DIM, :] / acc_ref[hh, w, HEAD_DIM:HEAD_DIM + 1, :] for hh in range(2)],
                axis=0)
            z = z_ref[0]
            o_ref[0] = out_t.T * (z * _sigmoid(z))

    @pl.when(step % 2 == 0)
    def _even_step():
        both_passes(scores0, mx0, scores1, mx1)

    @pl.when(step % 2 == 1)
    def _odd_step():
        both_passes(scores1, mx1, scores0, mx0)


def _moba(qkv, za, rel_bias):
    bsz, seq, _ = qkv.shape
    n_blk = seq // MOBA_BLOCK
    assert n_blk % 2 == 0
    half = n_blk // 2
    own_b, prev_b = _bucket_tables()
    blk = MOBA_BLOCK
    n_items = N_PAIRS * bsz * half
    kernel = functools.partial(_moba_kernel, n_blk=n_blk, bsz=bsz)

    def cur(s):
        return _moba_decode(jnp.minimum(s, n_items - 1), bsz, half)

    def prv(s):
        return _moba_decode(jnp.maximum(s - 1, 0), bsz, half)

    def seq_block(col0):
        def index_map(s):
            hp, b, _ = cur(s)
            return b, 0, col0 + hp
        return pl.BlockSpec((1, seq, LANES), index_map)

    def block_a(s):
        hp, b, i = prv(s)
        return b, i, hp

    def block_b_in(s):
        hp, b, i = prv(s)
        return b, n_blk - 1 - i, hp

    def block_b_out(s):
        hp, b, i = prv(s)
        return b, half - 1 - i, hp

    return pl.pallas_call(
        kernel,
        grid=(n_items + 1,),
        in_specs=[
            pl.BlockSpec((blk, blk), lambda s: (0, 0)),
            pl.BlockSpec((blk, blk), lambda s: (0, 0)),
            pl.BlockSpec(memory_space=pltpu.SMEM),
            seq_block(0), seq_block(N_PAIRS), seq_block(2 * N_PAIRS),
            pl.BlockSpec((1, blk, LANES), block_a),
            pl.BlockSpec((1, blk, LANES), block_b_in),
        ],
        out_specs=[
            pl.BlockSpec((1, blk, LANES), block_a),
            pl.BlockSpec((1, blk, LANES), block_b_out),
        ],
        out_shape=[
            jax.ShapeDtypeStruct((bsz, seq // 2, WIDTH), F32),
            jax.ShapeDtypeStruct((bsz, seq // 2, WIDTH), F32),
        ],
        scratch_shapes=[
            pltpu.VMEM((2, 3, blk, blk), F32),
            pltpu.VMEM((n_blk, LANES), F32),
            pltpu.VMEM((2, n_blk, blk, LANES), BF16),
            pltpu.VMEM((2, 2, n_blk, VT_ROWS, blk), BF16),
            pltpu.VMEM((2, n_blk, LANES, blk), BF16),
            pltpu.VMEM((2, n_blk + 1, blk, blk), F32),
            pltpu.VMEM((2, n_blk + 1, blk, blk), F32),
            pltpu.VMEM((2, 2, 8, blk), F32),
            pltpu.VMEM((2, 2, 8, blk), F32),
            pltpu.VMEM((2, 2, VT_ROWS, blk), F32),
        ],
        compiler_params=pltpu.CompilerParams(
            dimension_semantics=("arbitrary",), vmem_limit_bytes=VMEM_LIMIT),
        name="moba",
    )(jnp.asarray(own_b), jnp.asarray(prev_b), rel_bias, qkv, qkv, qkv, za, za)


RW_CHUNKS = 8
RW_GROUP_CHUNKS = 2


def _rwkv_kernel(rw_ref, mu_ref, w0_ref, lora_ref, a0_ref, kk_ref, ka_ref, rk_ref,
                 lnw_ref, lnb_ref, ltri_ref, ones_ref, o_ref,
                 prev_row, state, t_s, p_s, arb_s, ayk_s, at_s, vst_s, bkh_s, x1_s, loc_s, y_s,
                 lc_s, nc_s, dm_s, h0_s, akv_s):
    c = pl.program_id(1)
    C = CHUNK
    NC = RW_CHUNKS
    R = NC * C

    @pl.when(c == 0)
    def _reset():
        prev_row[...] = jnp.zeros_like(prev_row)
        state[...] = jnp.zeros_like(state)

    lane = lax.broadcasted_iota(jnp.int32, (1, LANES), 1)
    first_half = lane < HEAD_DIM
    second_half = jnp.logical_not(first_half)
    ones_bd = ones_ref[...]
    GC = RW_GROUP_CHUNKS
    RG = GC * C

    def seg_sum(x):
        outs = []
        for p in range(N_PAIRS):
            outs.append(_dot(x[:, p * LANES:(p + 1) * LANES].astype(BF16), ones_bd))
        return jnp.concatenate(outs, axis=1)

    def prepare(g, pre):
        start = g * RG
        cols = rw_ref[0, start:start + RG, :]
        prev = prev_row[0:1, :] if g == 0 else rw_ref[0, start - 1:start, :]
        row = lax.broadcasted_iota(jnp.int32, (RG, 1), 0)
        shifted = jnp.where(row == 0, prev, pltpu.roll(cols, 1, axis=0))
        xs = cols + (shifted - cols) * mu_ref[...]
        r = xs[:, 0:WIDTH]
        k = xs[:, WIDTH:2 * WIDTH]
        v = xs[:, 2 * WIDTH:3 * WIDTH]
        z = xs[:, 3 * WIDTH:4 * WIDTH]
        pre.update(r=r, v=v, z=z)
        yield

        lo_in = xs[:, 4 * WIDTH:4 * WIDTH + LANES]
        lo_in = jnp.where(first_half, jnp.tanh(lo_in), lo_in)
        lora = _dot(lo_in.astype(BF16), lora_ref[...])
        dw = lora[:, 0:WIDTH]
        da = lora[:, WIDTH:2 * WIDTH]
        lw = (-math.exp(-0.5)) * _sigmoid(w0_ref[...] + dw)
        a_ic = _sigmoid(a0_ref[...] + da)
        yield

        kk = k * kk_ref[...]
        kk = kk * lax.rsqrt(jnp.maximum(seg_sum(kk * kk), 1e-24))
        k2 = k * (1.0 + (a_ic - 1.0) * ka_ref[...])
        b_vec = kk * a_ic
        pre.update(k2=k2)
        yield

        l1, l2 = _split2(lw)
        ltri = ltri_ref[0:RG, 0:RG]
        g_inc = _dot(ltri, l1) + _dot(ltri, l2)
        g_end_rows = [g_inc[(n + 1) * C - 1:(n + 1) * C, :] for n in range(GC)]
        decay_end = [jnp.exp(g) for g in g_end_rows]
        d_end = jnp.concatenate([jnp.broadcast_to(d, (C, WIDTH)) for d in decay_end], axis=0)
        e_inc = jnp.exp(g_inc)
        e_neg = 1.0 / e_inc
        yield

        e_end = d_end * e_neg
        pre.update(
            decay_end=decay_end,
            rt=r * e_inc,
            at=kk * (-e_inc * jnp.exp(-lw)),
            bt=b_vec * e_neg, kt=k2 * e_neg, bh=b_vec * e_end, kh=k2 * e_end)

    def stack(x):
        return jnp.concatenate([jnp.where(first_half, x, 0.0), jnp.where(second_half, x, 0.0)], axis=0)

    def twice(x):
        return jnp.concatenate([x, x], axis=0)

    ri = lax.broadcasted_iota(jnp.int32, (2 * C, 2 * C), 0)
    ci = lax.broadcasted_iota(jnp.int32, (2 * C, 2 * C), 1)
    same_head = (ri < C) == (ci < C)
    incl = same_head & (ci <= ri)
    strict = same_head & (ci < ri)
    eye = (ri == ci).astype(F32)
    items = [(n, p) for n in range(NC) for p in range(N_PAIRS)]
    H = 2 * C

    def state_independent(group, pre):
        rt, at, bt, kt, bh, kh, v, decay_end = (pre[name] for name in
                                                ("rt", "at", "bt", "kt", "bh", "kh", "v", "decay_end"))
        for i, n, p in group:
            rows = slice(n * C, (n + 1) * C)
            sl = slice(p * LANES, (p + 1) * LANES)
            r_st = stack(rt[rows, sl]).astype(BF16)
            a_st = stack(at[rows, sl]).astype(BF16)
            ra = jnp.concatenate([r_st, a_st], axis=0)
            bk_t = jnp.concatenate([twice(bt[rows, sl]), twice(kt[rows, sl])], axis=0).T.astype(BF16)
            m_all = _dot(ra, bk_t)
            n_ab = jnp.where(strict, m_all[H:2 * H, 0:H], 0.0)
            t_s[i] = eye + n_ab
            p_s[i] = n_ab.astype(BF16)
            arb_s[i] = jnp.where(incl, m_all[0:H, 0:H], 0.0).astype(BF16)
            ayk_s[i, 0:H, :] = jnp.where(incl, m_all[0:H, H:2 * H], 0.0).astype(BF16)
            ayk_s[i, H:2 * H, :] = jnp.where(strict, m_all[H:2 * H, H:2 * H], 0.0).astype(BF16)
            x1_s[i, 0:H, :] = r_st
            at_s[i] = a_st
            vst_s[i] = twice(v[rows, sl].astype(BF16))
            bkh_s[i] = jnp.concatenate([stack(bh[rows, sl]), stack(kh[rows, sl])], axis=0).astype(BF16)

        yield

        for i, _, _ in group:
            pb = p_s[i]
            p_s[i] = _dot(pb, pb).astype(BF16)
        yield
        for _ in range(int(math.log2(C)) - 2):
            for i, _, _ in group:
                pb = p_s[i]
                tb = t_s[i]
                both = _dot(pb, jnp.concatenate([tb.astype(BF16), pb], axis=1))
                t_s[i] = tb + both[:, 0:H]
                p_s[i] = both[:, H:2 * H].astype(BF16)
            yield
        for i, _, _ in group:
            tb = t_s[i]
            t_s[i] = tb + _dot(p_s[i], tb.astype(BF16))
        yield

        for i, n, p in group:
            sl = slice(p * LANES, (p + 1) * LANES)
            yk = _dot(ayk_s[i], vst_s[i])
            loc_s[i, 0:H, :] = yk[0:H, :]
            akv_s[i] = yk[H:2 * H, :].astype(BF16)
            dm_s[i] = jnp.broadcast_to(decay_end[n][:, sl], (LANES, LANES)).T
        yield
        for i, _, _ in group:
            wz = _dot(t_s[i].astype(BF16), jnp.concatenate([at_s[i], akv_s[i]], axis=1))
            x1_s[i, H:2 * H, :] = wz[:, 0:LANES].astype(BF16)
            loc_s[i, H:2 * H, :] = wz[:, LANES:2 * LANES]
        yield
        for i, _, _ in group:
            w_bf = x1_s[i, H:2 * H, :]
            u_loc = loc_s[i, H:2 * H, :]
            lc_s[i] = _dot_tn(bkh_s[i, 0:H, :], w_bf).astype(BF16)
            nc_s[i] = _dot_tn(bkh_s[i], jnp.concatenate([u_loc.astype(BF16), vst_s[i]], axis=0))

    inv_n = 1.0 / HEAD_DIM

    def finish(g, pre):
        mine = [(i, n, p) for i, (n, p) in enumerate(items) if n // GC == g]
        for i, n, p in mine:
            h0 = state[p]
            h0_bf = h0.astype(BF16)
            h0_s[i] = h0_bf
            state[p] = dm_s[i] * h0 + _dot(lc_s[i], h0_bf) + nc_s[i]
            if p == N_PAIRS - 1:
                yield
        for i, n, p in mine:
            rs = _dot(x1_s[i], h0_s[i])
            akv_s[i] = (rs[H:2 * H, :] + loc_s[i, H:2 * H, :]).astype(BF16)
            loc_s[i, 0:H, :] = rs[0:H, :] + loc_s[i, 0:H, :]
        yield
        for i, n, p in mine:
            sl = slice(p * LANES, (p + 1) * LANES)
            y_st = loc_s[i, 0:H, :] + _dot(arb_s[i], akv_s[i])
            y_s[n * C:(n + 1) * C, sl] = jnp.where(first_half, y_st[0:C, :], y_st[C:H, :])
        yield
        rows = slice(g * RG, (g + 1) * RG)
        y = y_s[rows, :]
        mean = seg_sum(y) * inv_n
        d = y - mean
        var = seg_sum(d * d) * inv_n
        yn = d * lax.rsqrt(var + GN_EPS) * lnw_ref[...] + lnb_ref[...]
        bonus = seg_sum(pre["r"] * pre["k2"] * rk_ref[...]) * pre["v"]
        z = pre["z"]
        o_ref[0, rows, :] = (yn + bonus) * (z * _sigmoid(z))

    n_groups = NC // GC
    pres = [{} for _ in range(n_groups)]
    for _ in prepare(0, pres[0]):
        pass
    fin = iter(())
    for g in range(n_groups):
        group = [(i, n - g * GC, p) for i, (n, p) in enumerate(items) if n // GC == g]
        nxt = prepare(g + 1, pres[g + 1]) if g + 1 < n_groups else iter(())
        for _ in state_independent(group, pres[g]):
            next(nxt, None)
            next(fin, None)
        for _ in nxt:
            pass
        for _ in fin:
            pass
        fin = finish(g, pres[g])
    for _ in fin:
        pass
    prev_row[0:1, :] = rw_ref[0, R - 1:R, :]


def _rwkv(rw, mu, w0, w_up, a0, a_up, k_k, k_a, r_k, ln_w, ln_b):
    bsz, seq, _ = rw.shape
    C = CHUNK
    row = lambda t: t.reshape(1, -1).astype(F32)
    lora = jnp.zeros((LANES, 2 * WIDTH), F32)
    lora = lora.at[:LORA_RANK, :WIDTH].set(w_up).at[LORA_RANK:, WIDTH:].set(a_up)
    lora = lora.astype(BF16)
    R = RW_CHUNKS * C
    G = RW_CHUNKS * N_PAIRS
    H = 2 * C
    assert seq % R == 0 and H == LANES and RW_CHUNKS % RW_GROUP_CHUNKS == 0
    ltri = jnp.asarray(np.kron(np.eye(RW_CHUNKS), np.tril(np.ones((C, C)))).astype(np.float32)).astype(BF16)
    hid = np.arange(LANES) // HEAD_DIM
    ones_bd = jnp.asarray((hid[:, None] == hid[None, :]).astype(np.float32)).astype(BF16)
    vec = lambda n: pl.BlockSpec((1, n), lambda b, c: (0, 0))
    return pl.pallas_call(
        _rwkv_kernel,
        grid=(bsz, seq // R),
        in_specs=[
            pl.BlockSpec((1, R, RW_COLS), lambda b, c: (b, c, 0)),
            vec(RW_COLS), vec(WIDTH),
            pl.BlockSpec((LANES, 2 * WIDTH), lambda b, c: (0, 0)),
            vec(WIDTH), vec(WIDTH), vec(WIDTH), vec(WIDTH), vec(WIDTH), vec(WIDTH),
            pl.BlockSpec((R, R), lambda b, c: (0, 0)),
            pl.BlockSpec((LANES, LANES), lambda b, c: (0, 0)),
        ],
        out_specs=pl.BlockSpec((1, R, WIDTH), lambda b, c: (b, c, 0)),
        out_shape=jax.ShapeDtypeStruct((bsz, seq, WIDTH), F32),
        scratch_shapes=[
            pltpu.VMEM((8, RW_COLS), F32),
            pltpu.VMEM((N_PAIRS, LANES, LANES), F32),
            pltpu.VMEM((G, H, H), F32),
            pltpu.VMEM((G, H, H), BF16),
            pltpu.VMEM((G, H, H), BF16),
            pltpu.VMEM((G, 2 * H, H), BF16),
            pltpu.VMEM((G, H, LANES), BF16),
            pltpu.VMEM((G, H, LANES), BF16),
            pltpu.VMEM((G, 2 * H, LANES), BF16),
            pltpu.VMEM((G, 2 * H, LANES), BF16),
            pltpu.VMEM((G, 2 * H, LANES), F32),
            pltpu.VMEM((R, WIDTH), F32),
            pltpu.VMEM((G, LANES, LANES), BF16),
            pltpu.VMEM((G, LANES, LANES), F32),
            pltpu.VMEM((G, LANES, LANES), F32),
            pltpu.VMEM((G, LANES, LANES), BF16),
            pltpu.VMEM((G, H, LANES), BF16),
        ],
        compiler_params=pltpu.CompilerParams(
            dimension_semantics=("arbitrary", "arbitrary"), vmem_limit_bytes=VMEM_LIMIT),
        name="rwkv",
    )(rw, row(mu), row(w0), lora, row(a0), row(k_k), row(k_a), row(r_k), row(ln_w), row(ln_b),
      ltri, ones_bd)


def _merge_kernel(ya_lo_ref, ya_hi_ref, yb_ref, ga_ref, gb_ref, x_ref, p_ref, pa_ref, pb_ref, wo_ref, gpost_ref,
                  wpu_ref, wpg_ref, o_ref, *, tiles_per_half):
    first_half = pl.program_id(1) < tiles_per_half
    ya = jnp.where(first_half, ya_lo_ref[0], ya_hi_ref[0])
    ma = _dot(ya.astype(BF16), pa_ref[...])
    mb = _dot(yb_ref[0].astype(BF16), pb_ref[...])
    merged = _sigmoid(ga_ref[0]) * ma + _sigmoid(gb_ref[0]) * mb
    y = _dot(merged.astype(BF16), wo_ref[...])
    ms = jnp.mean(y * y, axis=-1, keepdims=True)
    h = x_ref[0] + y * lax.rsqrt(ms + RMS_EPS) * gpost_ref[...]
    e = _dot(p_ref[0].astype(BF16), wpu_ref[...])
    gate = _dot(h.astype(BF16), wpg_ref[...])
    o_ref[0] = h + _sigmoid(gate) * e


def _merge(ya_lo, ya_hi, yb, gates, x, p, p_a, p_b, w_out, g_post, w_pu, w_pg):
    bsz, seq, _ = x.shape
    tm = math.gcd(MERGE_TOKENS, seq // 2)
    th = seq // 2 // tm
    full = lambda a: pl.BlockSpec(a.shape, lambda b, t: (0, 0))
    tile = lambda w, col=0: pl.BlockSpec((1, tm, w), lambda b, t: (b, t, col))
    return pl.pallas_call(
        functools.partial(_merge_kernel, tiles_per_half=th),
        grid=(bsz, seq // tm),
        in_specs=[
            pl.BlockSpec((1, tm, WIDTH), lambda b, t: (b, jnp.minimum(t, th - 1), 0)),
            pl.BlockSpec((1, tm, WIDTH), lambda b, t: (b, jnp.maximum(t - th, 0), 0)),
            tile(WIDTH), tile(D_MODEL, 0), tile(D_MODEL, 1), tile(D_MODEL), tile(PLE_DIM),
            full(p_a), full(p_b), full(w_out), full(g_post), full(w_pu), full(w_pg),
        ],
        out_specs=tile(D_MODEL),
        out_shape=jax.ShapeDtypeStruct((bsz, seq, D_MODEL), F32),
        compiler_params=pltpu.CompilerParams(
            dimension_semantics=("arbitrary", "arbitrary"), vmem_limit_bytes=VMEM_LIMIT),
        name="merge",
    )(ya_lo, ya_hi, yb, gates, gates, x, p, p_a, p_b, w_out, g_post, w_pu, w_pg)


def kernel(x, p, g_pre, w_in, rel_bias, mu_shift, w0, w_up, a0, a_up, k_k, k_a, r_k, ln_x_w, ln_x_b,
           p_a, p_b, w_out, g_post, w_ple_up, w_ple_gate):
    bsz, seq, d = x.shape
    assert d == D_MODEL and seq % MOBA_BLOCK == 0 and seq // MOBA_BLOCK >= MOBA_TOPK
    assert g_pre.shape[0] == 1, "one layer"
    n = bsz * seq
    x2 = x.reshape(n, d).astype(F32)
    qkv, za, rw, gates = _project(x2, g_pre.astype(F32), w_in[0].astype(BF16))
    ya_lo, ya_hi = _moba(qkv.reshape(bsz, seq, 3 * WIDTH), za.reshape(bsz, seq, WIDTH), rel_bias.astype(F32))
    yb = _rwkv(rw.reshape(bsz, seq, RW_COLS), mu_shift[0], w0[0], w_up[0], a0[0], a_up[0], k_k[0], k_a[0],
               r_k[0], ln_x_w[0], ln_x_b[0])
    out = _merge(ya_lo, ya_hi, yb, gates.reshape(bsz, seq, G_COLS), x.astype(F32), p[0],
                 p_a[0].astype(BF16), p_b[0].astype(BF16), w_out[0].astype(BF16), g_post.astype(F32),
                 w_ple_up[0].astype(BF16), w_ple_gate[0].astype(BF16))
    return out.astype(x.dtype)
```

```python
import functools
import math

import jax
import jax.numpy as jnp
import numpy as np
from jax import lax
from jax.experimental import pallas as pl
from jax.experimental.pallas import tpu as pltpu

F32 = jnp.float32
BF16 = jnp.bfloat16

D_MODEL = 1024
PLE_DIM = 256
RMS_EPS = 1e-6
HEAD_DIM = 64
N_HEADS = 8
WIDTH = N_HEADS * HEAD_DIM
MOBA_BLOCK = 256
MOBA_TOPK = 3
REL_BUCKETS = 32
REL_MAX_EXACT = REL_BUCKETS // 2
REL_MAX_DIST = 128
LORA_RANK = 64
GN_EPS = 64e-5
A_COLS = 4 * WIDTH
RW_COLS = 4 * WIDTH + 2 * LORA_RANK
G_COLS = 2 * D_MODEL
IN_COLS = A_COLS + RW_COLS + G_COLS

LANES = 128
BF16_SUBLANES = 16
N_PAIRS = N_HEADS // 2
CHUNK = 64
PROJ_TOKENS = 512
PROJ_COLS = WIDTH
MERGE_TOKENS = 512
VMEM_LIMIT = 48 * 1024 * 1024
LOG2E = math.log2(math.e)
Q_SCALE = LOG2E * HEAD_DIM ** -0.5


def _dot(a, b):
    return jnp.dot(a, b, preferred_element_type=F32)


def _dot_nt(a, b):
    return lax.dot_general(a, b, (((1,), (1,)), ((), ())), preferred_element_type=F32)


def _dot_tn(a, b):
    return lax.dot_general(a, b, (((0,), (0,)), ((), ())), preferred_element_type=F32)


def _split2(x):
    hi = x.astype(BF16)
    lo = (x - hi.astype(F32)).astype(BF16)
    return hi, lo


def _sigmoid(x):
    return 1.0 / (1.0 + jnp.exp(-x))


def _proj_kernel(x_ref, g_ref, w_ref, qkv_ref, za_ref, rw_ref, gt_ref):
    x = x_ref[...]
    ms = jnp.mean(x * x, axis=-1, keepdims=True)
    u = (x * lax.rsqrt(ms + RMS_EPS) * g_ref[...]).astype(BF16)

    def emit(out_ref, col0, width, scale_first=None):
        for c in range(0, width, PROJ_COLS):
            w = min(PROJ_COLS, width - c)
            y = _dot(u, w_ref[:, col0 + c:col0 + c + w])
            if scale_first is not None and c == 0:
                y = y * scale_first
            out_ref[:, c:c + w] = y.astype(out_ref.dtype)

    emit(qkv_ref, 0, 3 * WIDTH, scale_first=Q_SCALE)
    emit(za_ref, 3 * WIDTH, WIDTH)
    emit(rw_ref, A_COLS, RW_COLS)
    emit(gt_ref, A_COLS + RW_COLS, G_COLS)


def _project(x2, g_pre, w_in_bf):
    n = x2.shape[0]
    tm = math.gcd(PROJ_TOKENS, n)
    return pl.pallas_call(
        _proj_kernel,
        grid=(n // tm,),
        in_specs=[
            pl.BlockSpec((tm, D_MODEL), lambda i: (i, 0)),
            pl.BlockSpec((1, D_MODEL), lambda i: (0, 0)),
            pl.BlockSpec((D_MODEL, IN_COLS), lambda i: (0, 0), pipeline_mode=pl.Buffered(1)),
        ],
        out_specs=[
            pl.BlockSpec((tm, 3 * WIDTH), lambda i: (i, 0)),
            pl.BlockSpec((tm, WIDTH), lambda i: (i, 0)),
            pl.BlockSpec((tm, RW_COLS), lambda i: (i, 0)),
            pl.BlockSpec((tm, G_COLS), lambda i: (i, 0)),
        ],
        out_shape=[
            jax.ShapeDtypeStruct((n, 3 * WIDTH), BF16),
            jax.ShapeDtypeStruct((n, WIDTH), F32),
            jax.ShapeDtypeStruct((n, RW_COLS), F32),
            jax.ShapeDtypeStruct((n, G_COLS), F32),
        ],
        compiler_params=pltpu.CompilerParams(
            dimension_semantics=("arbitrary",), vmem_limit_bytes=VMEM_LIMIT),
        name="proj",
    )(x2, g_pre, w_in_bf)


def _t5_bucket_np(dist):
    n = np.maximum(dist, 0)
    nf = np.maximum(n, 1).astype(np.float32)
    large = REL_MAX_EXACT + (np.log(nf / np.float32(REL_MAX_EXACT)) / np.float32(math.log(REL_MAX_DIST / REL_MAX_EXACT))
                             * np.float32(REL_BUCKETS - REL_MAX_EXACT)).astype(np.int32)
    large = np.minimum(large, REL_BUCKETS - 1)
    return np.where(n < REL_MAX_EXACT, n, large).astype(np.int32)


def _bucket_tables():
    s = np.arange(MOBA_BLOCK)[:, None]
    t = np.arange(MOBA_BLOCK)[None, :]
    own = np.where(t >= s, _t5_bucket_np(t - s), -1).astype(np.int32)
    prev = _t5_bucket_np(MOBA_BLOCK + t - s)
    assert _t5_bucket_np(np.array(MOBA_BLOCK + 1)) == REL_BUCKETS - 1
    return own, prev


MASKED = -1e30
VT_ROWS = HEAD_DIM + BF16_SUBLANES
KIND_FAR, KIND_PREV, KIND_OWN = 0, 1, 2


def _moba_decode(s, bsz, half):
    per_pair = bsz * half
    return s // per_pair, (s % per_pair) // half, s % half


def _moba_kernel(own_b_ref, prev_b_ref, relb_ref, q_ref, k_ref, v_ref, za_ref, zb_ref, oa_ref, ob_ref,
                 tabs, kmean, kaug, vt, qaug, scores0, scores1, mx0, mx1, acc_ref, *, n_blk, bsz):
    half = n_blk // 2
    n_items = N_PAIRS * bsz * half
    step = pl.program_id(0)
    cur = jnp.minimum(step, n_items - 1)
    prv = jnp.maximum(step - 1, 0)
    hp, b, i = _moba_decode(cur, bsz, half)
    hp_p, b_p, i_p = _moba_decode(prv, bsz, half)
    seq_slot = (hp * bsz + b) % 2
    seq_slot_p = (hp_p * bsz + b_p) % 2
    blk = MOBA_BLOCK
    seq = n_blk * blk
    lane = lax.broadcasted_iota(jnp.int32, (1, LANES), 1)
    head_mask = [lane < HEAD_DIM, lane >= HEAD_DIM]
    flag_base = [HEAD_DIM, 0]
    neg_inf = jnp.float32(-jnp.inf)

    @pl.when((b == 0) & (i == 0))
    def _build_bias_tables():
        ob = own_b_ref[...]
        pb = prev_b_ref[...]
        for hh in range(2):
            far = relb_ref[REL_BUCKETS - 1, 2 * hp + hh]
            to = jnp.full((blk, blk), neg_inf, F32)
            tp = jnp.zeros((blk, blk), F32)
            for bkt in range(REL_BUCKETS):
                val = (relb_ref[bkt, 2 * hp + hh] - far) * LOG2E
                to = jnp.where(ob == bkt, val, to)
                tp = jnp.where(pb == bkt, val, tp)
            tabs[hh, KIND_FAR] = jnp.zeros((blk, blk), F32)
            tabs[hh, KIND_PREV] = tp
            tabs[hh, KIND_OWN] = to

    @pl.when(step == 0)
    def _placeholders_for_first_pass2():
        scores1[...] = jnp.zeros_like(scores1)
        mx1[...] = jnp.zeros_like(mx1)

    @pl.when((i == 0) & (step < n_items))
    def _per_sequence_setup():
        ones_row = (lax.broadcasted_iota(jnp.int32, (VT_ROWS - HEAD_DIM, blk), 0) == 0).astype(BF16)
        for j in range(n_blk):
            kj = k_ref[0, j * blk:(j + 1) * blk, :].astype(F32)
            kmean[j:j + 1, :] = jnp.mean(kj, axis=0, keepdims=True)
            vjt = v_ref[0, j * blk:(j + 1) * blk, :].astype(F32).T
            for hh in range(2):
                flag = (lane == flag_base[hh] + j).astype(F32)
                kaug[hh, j] = jnp.where(head_mask[hh], kj, flag).astype(BF16)
                vt[seq_slot, hh, j, 0:HEAD_DIM, :] = vjt[hh * HEAD_DIM:(hh + 1) * HEAD_DIM, :].astype(BF16)
                vt[seq_slot, hh, j, HEAD_DIM:VT_ROWS, :] = ones_row

        q = q_ref[0]
        blk_id = lax.broadcasted_iota(jnp.int32, (n_blk, seq), 0)
        q_blk = lax.broadcasted_iota(jnp.int32, (n_blk, seq), 1) // blk
        flag_row = lax.broadcasted_iota(jnp.int32, (n_blk, LANES), 0)
        flag_lane = lax.broadcasted_iota(jnp.int32, (n_blk, LANES), 1)
        km_hi, km_lo = _split2(kmean[...])
        channel = lax.broadcasted_iota(jnp.int32, (LANES, 1), 0)
        head_rows = [channel < HEAD_DIM, channel >= HEAD_DIM]
        eye = (lax.broadcasted_iota(jnp.int32, (LANES, LANES), 0)
               == lax.broadcasted_iota(jnp.int32, (LANES, LANES), 1)).astype(BF16)
        q_t = _dot_nt(eye, q)
        for hh in range(2):
            q_h = jnp.where(head_rows[hh], q_t, 0.0).astype(BF16)
            g = _dot(km_hi, q_h) + _dot(km_lo, q_h)
            g = jnp.where(blk_id < q_blk, g, neg_inf)
            allowed = blk_id == q_blk
            for _ in range(MOBA_TOPK):
                mx = jnp.max(g, axis=0, keepdims=True)
                first = jnp.min(jnp.where(g == mx, blk_id, n_blk), axis=0, keepdims=True)
                hit = (blk_id == first) & (mx > neg_inf)
                allowed = allowed | hit
                g = jnp.where(hit, neg_inf, g)
            pen_t = jnp.where(allowed, 0.0, MASKED).astype(BF16)
            place = (flag_lane == flag_row + flag_base[hh]).astype(BF16)
            pen = _dot_tn(place, pen_t)
            qa = jnp.where(head_rows[hh], q_t, pen).astype(BF16)
            for j in range(n_blk):
                qaug[hh, j] = qa[:, j * blk:(j + 1) * blk]

    def tile_max(s):
        return jnp.max(s.reshape(blk // 8, 8, blk), axis=0)

    n_tiles = n_blk + 1

    def tile_ids(t, qb_a):
        qb_b = n_blk - 1 - qb_a
        is_a = t <= qb_a
        qb = jnp.where(is_a, qb_a, qb_b)
        kb = jnp.where(is_a, t, t - qb_a - 1)
        kind = jnp.where(kb == qb, KIND_OWN, jnp.where(kb == qb - 1, KIND_PREV, KIND_FAR))
        return is_a, qb, kb, kind

    def both_passes(sc_w, mx_w, sc_r, mx_r):
        m_rows = [[jnp.max(mx_r[hh, w], axis=0, keepdims=True) for w in range(2)] for hh in range(2)]
        mx_w[...] = jnp.full(mx_w.shape, neg_inf, F32)
        acc_ref[...] = jnp.zeros_like(acc_ref)
        tail_b = [None, None]
        q_b = [qaug[hh, n_blk - 1 - i] for hh in range(2)]
        for t in range(n_tiles):
            is_a, qb, kb, kind = tile_ids(t, i)
            which = jnp.where(is_a, 0, 1)
            if half <= t <= n_blk - 2:
                kind = None
            elif t >= n_blk - 1:
                kind = KIND_PREV if t == n_blk - 1 else KIND_OWN
            for hh in range(2):
                s = _dot(kaug[hh, kb], q_b[hh] if t >= half else qaug[hh, qb])
                if kind is not None:
                    s = s + tabs[hh, kind]
                sc_w[hh, t] = s
                mx_w[hh, which] = jnp.maximum(mx_w[hh, which], tile_max(s))
            is_a, _, kb, _ = tile_ids(t, i_p)
            which = jnp.where(is_a, 0, 1)
            for hh in range(2):
                if t >= half:
                    p = jnp.exp2(sc_r[hh, t] - m_rows[hh][1]).astype(BF16)
                    d = _dot(vt[seq_slot_p, hh, kb], p)
                    tail_b[hh] = d if tail_b[hh] is None else tail_b[hh] + d
                else:
                    m_row = jnp.where(is_a, m_rows[hh][0], m_rows[hh][1])
                    p = jnp.exp2(sc_r[hh, t] - m_row).astype(BF16)
                    acc_ref[hh, which] += _dot(vt[seq_slot_p, hh, kb], p)
        for hh in range(2):
            acc_ref[hh, 1] += tail_b[hh]

        for w, (z_ref, o_ref) in enumerate(((za_ref, oa_ref), (zb_ref, ob_ref))):
            out_t = jnp.concatenate(
                [acc_ref[hh, w, 0:HEAD_DIM, :] / acc_ref[hh, w, HEAD_DIM:HEAD_DIM + 1, :] for hh in range(2)],
                axis=0)
            z = z_ref[0]
            o_ref[0] = out_t.T * (z * _sigmoid(z))

    @pl.when(step % 2 == 0)
    def _even_step():
        both_passes(scores0, mx0, scores1, mx1)

    @pl.when(step % 2 == 1)
    def _odd_step():
        both_passes(scores1, mx1, scores0, mx0)


def _moba(qkv, za, rel_bias):
    bsz, seq, _ = qkv.shape
    n_blk = seq // MOBA_BLOCK
    assert n_blk % 2 == 0
    half = n_blk // 2
    own_b, prev_b = _bucket_tables()
    blk = MOBA_BLOCK
    n_items = N_PAIRS * bsz * half
    kernel = functools.partial(_moba_kernel, n_blk=n_blk, bsz=bsz)

    def cur(s):
        return _moba_decode(jnp.minimum(s, n_items - 1), bsz, half)

    def prv(s):
        return _moba_decode(jnp.maximum(s - 1, 0), bsz, half)

    def seq_block(col0):
        def index_map(s):
            hp, b, _ = cur(s)
            return b, 0, col0 + hp
        return pl.BlockSpec((1, seq, LANES), index_map)

    def block_a(s):
        hp, b, i = prv(s)
        return b, i, hp

    def block_b_in(s):
        hp, b, i = prv(s)
        return b, n_blk - 1 - i, hp

    def block_b_out(s):
        hp, b, i = prv(s)
        return b, half - 1 - i, hp

    return pl.pallas_call(
        kernel,
        grid=(n_items + 1,),
        in_specs=[
            pl.BlockSpec((blk, blk), lambda s: (0, 0)),
            pl.BlockSpec((blk, blk), lambda s: (0, 0)),
            pl.BlockSpec(memory_space=pltpu.SMEM),
            seq_block(0), seq_block(N_PAIRS), seq_block(2 * N_PAIRS),
            pl.BlockSpec((1, blk, LANES), block_a),
            pl.BlockSpec((1, blk, LANES), block_b_in),
        ],
        out_specs=[
            pl.BlockSpec((1, blk, LANES), block_a),
            pl.BlockSpec((1, blk, LANES), block_b_out),
        ],
        out_shape=[
            jax.ShapeDtypeStruct((bsz, seq // 2, WIDTH), F32),
            jax.ShapeDtypeStruct((bsz, seq // 2, WIDTH), F32),
        ],
        scratch_shapes=[
            pltpu.VMEM((2, 3, blk, blk), F32),
            pltpu.VMEM((n_blk, LANES), F32),
            pltpu.VMEM((2, n_blk, blk, LANES), BF16),
            pltpu.VMEM((2, 2, n_blk, VT_ROWS, blk), BF16),
            pltpu.VMEM((2, n_blk, LANES, blk), BF16),
            pltpu.VMEM((2, n_blk + 1, blk, blk), F32),
            pltpu.VMEM((2, n_blk + 1, blk, blk), F32),
            pltpu.VMEM((2, 2, 8, blk), F32),
            pltpu.VMEM((2, 2, 8, blk), F32),
            pltpu.VMEM((2, 2, VT_ROWS, blk), F32),
        ],
        compiler_params=pltpu.CompilerParams(
            dimension_semantics=("arbitrary",), vmem_limit_bytes=VMEM_LIMIT),
        name="moba",
    )(jnp.asarray(own_b), jnp.asarray(prev_b), rel_bias, qkv, qkv, qkv, za, za)


RW_CHUNKS = 8
RW_GROUP_CHUNKS = 2


def _rwkv_kernel(rw_ref, mu_ref, w0_ref, lora_ref, a0_ref, kk_ref, ka_ref, rk_ref,
                 lnw_ref, lnb_ref, ltri_ref, ones_ref, o_ref,
                 prev_row, state, t_s, p_s, arb_s, ayk_s, at_s, vst_s, bkh_s, x1_s, loc_s, y_s,
                 lc_s, nc_s, dm_s, h0_s, akv_s):
    c = pl.program_id(1)
    C = CHUNK
    NC = RW_CHUNKS
    R = NC * C

    @pl.when(c == 0)
    def _reset():
        prev_row[...] = jnp.zeros_like(prev_row)
        state[...] = jnp.zeros_like(state)

    lane = lax.broadcasted_iota(jnp.int32, (1, LANES), 1)
    first_half = lane < HEAD_DIM
    second_half = jnp.logical_not(first_half)
    ones_bd = ones_ref[...]
    GC = RW_GROUP_CHUNKS
    RG = GC * C

    def seg_sum(x):
        outs = []
        for p in range(N_PAIRS):
            outs.append(_dot(x[:, p * LANES:(p + 1) * LANES].astype(BF16), ones_bd))
        return jnp.concatenate(outs, axis=1)

    def prepare(g, pre):
        start = g * RG
        cols = rw_ref[0, start:start + RG, :]
        prev = prev_row[0:1, :] if g == 0 else rw_ref[0, start - 1:start, :]
        row = lax.broadcasted_iota(jnp.int32, (RG, 1), 0)
        shifted = jnp.where(row == 0, prev, pltpu.roll(cols, 1, axis=0))
        xs = cols + (shifted - cols) * mu_ref[...]
        r = xs[:, 0:WIDTH]
        k = xs[:, WIDTH:2 * WIDTH]
        v = xs[:, 2 * WIDTH:3 * WIDTH]
        z = xs[:, 3 * WIDTH:4 * WIDTH]
        pre.update(r=r, v=v, z=z)
        yield

        lo_in = xs[:, 4 * WIDTH:4 * WIDTH + LANES]
        lo_in = jnp.where(first_half, jnp.tanh(lo_in), lo_in)
        lora = _dot(lo_in.astype(BF16), lora_ref[...])
        dw = lora[:, 0:WIDTH]
        da = lora[:, WIDTH:2 * WIDTH]
        lw = (-math.exp(-0.5)) * _sigmoid(w0_ref[...] + dw)
        a_ic = _sigmoid(a0_ref[...] + da)
        yield

        kk = k * kk_ref[...]
        kk = kk * lax.rsqrt(jnp.maximum(seg_sum(kk * kk), 1e-24))
        k2 = k * (1.0 + (a_ic - 1.0) * ka_ref[...])
        b_vec = kk * a_ic
        pre.update(k2=k2)
        yield

        l1, l2 = _split2(lw)
        ltri = ltri_ref[0:RG, 0:RG]
        g_inc = _dot(ltri, l1) + _dot(ltri, l2)
        g_end_rows = [g_inc[(n + 1) * C - 1:(n + 1) * C, :] for n in range(GC)]
        decay_end = [jnp.exp(g) for g in g_end_rows]
        d_end = jnp.concatenate([jnp.broadcast_to(d, (C, WIDTH)) for d in decay_end], axis=0)
        e_inc = jnp.exp(g_inc)
        e_neg = 1.0 / e_inc
        yield

        e_end = d_end * e_neg
        pre.update(
            decay_end=decay_end,
            rt=r * e_inc,
            at=kk * (-e_inc * jnp.exp(-lw)),
            bt=b_vec * e_neg, kt=k2 * e_neg, bh=b_vec * e_end, kh=k2 * e_end)

    def stack(x):
        return jnp.concatenate([jnp.where(first_half, x, 0.0), jnp.where(second_half, x, 0.0)], axis=0)

    def twice(x):
        return jnp.concatenate([x, x], axis=0)

    ri = lax.broadcasted_iota(jnp.int32, (2 * C, 2 * C), 0)
    ci = lax.broadcasted_iota(jnp.int32, (2 * C, 2 * C), 1)
    same_head = (ri < C) == (ci < C)
    incl = same_head & (ci <= ri)
    strict = same_head & (ci < ri)
    eye = (ri == ci).astype(F32)
    items = [(n, p) for n in range(NC) for p in range(N_PAIRS)]
    H = 2 * C

    def state_independent(group, pre):
        rt, at, bt, kt, bh, kh, v, decay_end = (pre[name] for name in
                                                ("rt", "at", "bt", "kt", "bh", "kh", "v", "decay_end"))
        for i, n, p in group:
            rows = slice(n * C, (n + 1) * C)
            sl = slice(p * LANES, (p + 1) * LANES)
            r_st = stack(rt[rows, sl]).astype(BF16)
            a_st = stack(at[rows, sl]).astype(BF16)
            ra = jnp.concatenate([r_st, a_st], axis=0)
            bk = jnp.concatenate([twice(bt[rows, sl].astype(BF16)), twice(kt[rows, sl].astype(BF16))], axis=0)
            m_all = _dot_nt(ra, bk)
            n_ab = jnp.where(strict, m_all[H:2 * H, 0:H], 0.0)
            t_s[i] = eye + n_ab
            p_s[i] = n_ab.astype(BF16)
            arb_s[i] = jnp.where(incl, m_all[0:H, 0:H], 0.0).astype(BF16)
            ayk_s[i, 0:H, :] = jnp.where(incl, m_all[0:H, H:2 * H], 0.0).astype(BF16)
            ayk_s[i, H:2 * H, :] = jnp.where(strict, m_all[H:2 * H, H:2 * H], 0.0).astype(BF16)
            x1_s[i, 0:H, :] = r_st
            at_s[i] = a_st
            vst_s[i] = twice(v[rows, sl].astype(BF16))
            bkh_s[i] = jnp.concatenate([stack(bh[rows, sl]), stack(kh[rows, sl])], axis=0).astype(BF16)

        yield

        for i, _, _ in group:
            pb = p_s[i]
            p_s[i] = _dot(pb, pb).astype(BF16)
        yield
        for _ in range(int(math.log2(C)) - 2):
            for i, _, _ in group:
                pb = p_s[i]
                tb = t_s[i]
                both = _dot(pb, jnp.concatenate([tb.astype(BF16), pb], axis=1))
                t_s[i] = tb + both[:, 0:H]
                p_s[i] = both[:, H:2 * H].astype(BF16)
            yield
        for i, _, _ in group:
            tb = t_s[i]
            t_s[i] = tb + _dot(p_s[i], tb.astype(BF16))
        yield

        for i, n, p in group:
            sl = slice(p * LANES, (p + 1) * LANES)
            yk = _dot(ayk_s[i], vst_s[i])
            loc_s[i, 0:H, :] = yk[0:H, :]
            akv_s[i] = yk[H:2 * H, :].astype(BF16)
            dm_s[i] = jnp.broadcast_to(decay_end[n][:, sl], (LANES, LANES)).T
        yield
        for i, _, _ in group:
            wz = _dot(t_s[i].astype(BF16), jnp.concatenate([at_s[i], akv_s[i]], axis=1))
            x1_s[i, H:2 * H, :] = wz[:, 0:LANES].astype(BF16)
            loc_s[i, H:2 * H, :] = wz[:, LANES:2 * LANES]
        yield
        for i, _, _ in group:
            w_bf = x1_s[i, H:2 * H, :]
            u_loc = loc_s[i, H:2 * H, :]
            lc_s[i] = _dot_tn(bkh_s[i, 0:H, :], w_bf).astype(BF16)
            nc_s[i] = _dot_tn(bkh_s[i], jnp.concatenate([u_loc.astype(BF16), vst_s[i]], axis=0))

    inv_n = 1.0 / HEAD_DIM

    def finish(g, pre):
        mine = [(i, n, p) for i, (n, p) in enumerate(items) if n // GC == g]
        for i, n, p in mine:
            h0 = state[p]
            h0_bf = h0.astype(BF16)
            h0_s[i] = h0_bf
            state[p] = dm_s[i] * h0 + _dot(lc_s[i], h0_bf) + nc_s[i]
            if p == N_PAIRS - 1:
                yield
        for i, n, p in mine:
            rs = _dot(x1_s[i], h0_s[i])
            akv_s[i] = (rs[H:2 * H, :] + loc_s[i, H:2 * H, :]).astype(BF16)
            loc_s[i, 0:H, :] = rs[0:H, :] + loc_s[i, 0:H, :]
        yield
        for i, n, p in mine:
            sl = slice(p * LANES, (p + 1) * LANES)
            y_st = loc_s[i, 0:H, :] + _dot(arb_s[i], akv_s[i])
            y_s[n * C:(n + 1) * C, sl] = jnp.where(first_half, y_st[0:C, :], y_st[C:H, :])
        yield
        rows = slice(g * RG, (g + 1) * RG)
        y = y_s[rows, :]
        mean = seg_sum(y) * inv_n
        d = y - mean
        var = seg_sum(d * d) * inv_n
        yn = d * lax.rsqrt(var + GN_EPS) * lnw_ref[...] + lnb_ref[...]
        bonus = seg_sum(pre["r"] * pre["k2"] * rk_ref[...]) * pre["v"]
        z = pre["z"]
        o_ref[0, rows, :] = (yn + bonus) * (z * _sigmoid(z))

    n_groups = NC // GC
    pres = [{} for _ in range(n_groups)]
    for _ in prepare(0, pres[0]):
        pass
    fin = iter(())
    for g in range(n_groups):
        group = [(i, n - g * GC, p) for i, (n, p) in enumerate(items) if n // GC == g]
        nxt = prepare(g + 1, pres[g + 1]) if g + 1 < n_groups else iter(())
        for _ in state_independent(group, pres[g]):
            next(nxt, None)
            next(fin, None)
        for _ in nxt:
            pass
        for _ in fin:
            pass
        fin = finish(g, pres[g])
    for _ in fin:
        pass
    prev_row[0:1, :] = rw_ref[0, R - 1:R, :]


def _rwkv(rw, mu, w0, w_up, a0, a_up, k_k, k_a, r_k, ln_w, ln_b):
    bsz, seq, _ = rw.shape
    C = CHUNK
    row = lambda t: t.reshape(1, -1).astype(F32)
    lora = jnp.zeros((LANES, 2 * WIDTH), F32)
    lora = lora.at[:LORA_RANK, :WIDTH].set(w_up).at[LORA_RANK:, WIDTH:].set(a_up)
    lora = lora.astype(BF16)
    R = RW_CHUNKS * C
    G = RW_CHUNKS * N_PAIRS
    H = 2 * C
    assert seq % R == 0 and H == LANES and RW_CHUNKS % RW_GROUP_CHUNKS == 0
    ltri = jnp.asarray(np.kron(np.eye(RW_CHUNKS), np.tril(np.ones((C, C)))).astype(np.float32)).astype(BF16)
    hid = np.arange(LANES) // HEAD_DIM
    ones_bd = jnp.asarray((hid[:, None] == hid[None, :]).astype(np.float32)).astype(BF16)
    vec = lambda n: pl.BlockSpec((1, n), lambda b, c: (0, 0))
    return pl.pallas_call(
        _rwkv_kernel,
        grid=(bsz, seq // R),
        in_specs=[
            pl.BlockSpec((1, R, RW_COLS), lambda b, c: (b, c, 0)),
            vec(RW_COLS), vec(WIDTH),
            pl.BlockSpec((LANES, 2 * WIDTH), lambda b, c: (0, 0)),
            vec(WIDTH), vec(WIDTH), vec(WIDTH), vec(WIDTH), vec(WIDTH), vec(WIDTH),
            pl.BlockSpec((R, R), lambda b, c: (0, 0)),
            pl.BlockSpec((LANES, LANES), lambda b, c: (0, 0)),
        ],
        out_specs=pl.BlockSpec((1, R, WIDTH), lambda b, c: (b, c, 0)),
        out_shape=jax.ShapeDtypeStruct((bsz, seq, WIDTH), F32),
        scratch_shapes=[
            pltpu.VMEM((8, RW_COLS), F32),
            pltpu.VMEM((N_PAIRS, LANES, LANES), F32),
            pltpu.VMEM((G, H, H), F32),
            pltpu.VMEM((G, H, H), BF16),
            pltpu.VMEM((G, H, H), BF16),
            pltpu.VMEM((G, 2 * H, H), BF16),
            pltpu.VMEM((G, H, LANES), BF16),
            pltpu.VMEM((G, H, LANES), BF16),
            pltpu.VMEM((G, 2 * H, LANES), BF16),
            pltpu.VMEM((G, 2 * H, LANES), BF16),
            pltpu.VMEM((G, 2 * H, LANES), F32),
            pltpu.VMEM((R, WIDTH), F32),
            pltpu.VMEM((G, LANES, LANES), BF16),
            pltpu.VMEM((G, LANES, LANES), F32),
            pltpu.VMEM((G, LANES, LANES), F32),
            pltpu.VMEM((G, LANES, LANES), BF16),
            pltpu.VMEM((G, H, LANES), BF16),
        ],
        compiler_params=pltpu.CompilerParams(
            dimension_semantics=("arbitrary", "arbitrary"), vmem_limit_bytes=VMEM_LIMIT),
        name="rwkv",
    )(rw, row(mu), row(w0), lora, row(a0), row(k_k), row(k_a), row(r_k), row(ln_w), row(ln_b),
      ltri, ones_bd)


def _merge_kernel(ya_lo_ref, ya_hi_ref, yb_ref, ga_ref, gb_ref, x_ref, p_ref, pa_ref, pb_ref, wo_ref, gpost_ref,
                  wpu_ref, wpg_ref, o_ref, *, tiles_per_half):
    first_half = pl.program_id(1) < tiles_per_half
    ya = jnp.where(first_half, ya_lo_ref[0], ya_hi_ref[0])
    ma = _dot(ya.astype(BF16), pa_ref[...])
    mb = _dot(yb_ref[0].astype(BF16), pb_ref[...])
    merged = _sigmoid(ga_ref[0]) * ma + _sigmoid(gb_ref[0]) * mb
    y = _dot(merged.astype(BF16), wo_ref[...])
    ms = jnp.mean(y * y, axis=-1, keepdims=True)
    h = x_ref[0] + y * lax.rsqrt(ms + RMS_EPS) * gpost_ref[...]
    e = _dot(p_ref[0].astype(BF16), wpu_ref[...])
    gate = _dot(h.astype(BF16), wpg_ref[...])
    o_ref[0] = h + _sigmoid(gate) * e


def _merge(ya_lo, ya_hi, yb, gates, x, p, p_a, p_b, w_out, g_post, w_pu, w_pg):
    bsz, seq, _ = x.shape
    tm = math.gcd(MERGE_TOKENS, seq // 2)
    th = seq // 2 // tm
    full = lambda a: pl.BlockSpec(a.shape, lambda b, t: (0, 0))
    tile = lambda w, col=0: pl.BlockSpec((1, tm, w), lambda b, t: (b, t, col))
    return pl.pallas_call(
        functools.partial(_merge_kernel, tiles_per_half=th),
        grid=(bsz, seq // tm),
        in_specs=[
            pl.BlockSpec((1, tm, WIDTH), lambda b, t: (b, jnp.minimum(t, th - 1), 0)),
            pl.BlockSpec((1, tm, WIDTH), lambda b, t: (b, jnp.maximum(t - th, 0), 0)),
            tile(WIDTH), tile(D_MODEL, 0), tile(D_MODEL, 1), tile(D_MODEL), tile(PLE_DIM),
            full(p_a), full(p_b), full(w_out), full(g_post), full(w_pu), full(w_pg),
        ],
        out_specs=tile(D_MODEL),
        out_shape=jax.ShapeDtypeStruct((bsz, seq, D_MODEL), F32),
        compiler_params=pltpu.CompilerParams(
            dimension_semantics=("arbitrary", "arbitrary"), vmem_limit_bytes=VMEM_LIMIT),
        name="merge",
    )(ya_lo, ya_hi, yb, gates, gates, x, p, p_a, p_b, w_out, g_post, w_pu, w_pg)


def kernel(x, p, g_pre, w_in, rel_bias, mu_shift, w0, w_up, a0, a_up, k_k, k_a, r_k, ln_x_w, ln_x_b,
           p_a, p_b, w_out, g_post, w_ple_up, w_ple_gate):
    bsz, seq, d = x.shape
    assert d == D_MODEL and seq % MOBA_BLOCK == 0 and seq // MOBA_BLOCK >= MOBA_TOPK
    assert g_pre.shape[0] == 1, "one layer"
    n = bsz * seq
    x2 = x.reshape(n, d).astype(F32)
    qkv, za, rw, gates = _project(x2, g_pre.astype(F32), w_in[0].astype(BF16))
    ya_lo, ya_hi = _moba(qkv.reshape(bsz, seq, 3 * WIDTH), za.reshape(bsz, seq, WIDTH), rel_bias.astype(F32))
    yb = _rwkv(rw.reshape(bsz, seq, RW_COLS), mu_shift[0], w0[0], w_up[0], a0[0], a_up[0], k_k[0], k_a[0],
               r_k[0], ln_x_w[0], ln_x_b[0])
    out = _merge(ya_lo, ya_hi, yb, gates.reshape(bsz, seq, G_COLS), x.astype(F32), p[0],
                 p_a[0].astype(BF16), p_b[0].astype(BF16), w_out[0].astype(BF16), g_post.astype(F32),
                 w_ple_up[0].astype(BF16), w_ple_gate[0].astype(BF16))
    return out.astype(x.dtype)
```

```python
import functools
import math

import jax
import jax.numpy as jnp
import numpy as np
from jax import lax
from jax.experimental import pallas as pl
from jax.experimental.pallas import tpu as pltpu

F32 = jnp.float32
BF16 = jnp.bfloat16

D_MODEL = 1024
PLE_DIM = 256
RMS_EPS = 1e-6
HEAD_DIM = 64
N_HEADS = 8
WIDTH = N_HEADS * HEAD_DIM
MOBA_BLOCK = 256
MOBA_TOPK = 3
REL_BUCKETS = 32
REL_MAX_EXACT = REL_BUCKETS // 2
REL_MAX_DIST = 128
LORA_RANK = 64
GN_EPS = 64e-5
A_COLS = 4 * WIDTH
RW_COLS = 4 * WIDTH + 2 * LORA_RANK
G_COLS = 2 * D_MODEL
IN_COLS = A_COLS + RW_COLS + G_COLS

LANES = 128
BF16_SUBLANES = 16
N_PAIRS = N_HEADS // 2
CHUNK = 64
PROJ_TOKENS = 512
PROJ_COLS = WIDTH
MERGE_TOKENS = 512
VMEM_LIMIT = 48 * 1024 * 1024
LOG2E = math.log2(math.e)
Q_SCALE = LOG2E * HEAD_DIM ** -0.5


def _dot(a, b):
    return jnp.dot(a, b, preferred_element_type=F32)


def _dot_nt(a, b):
    return lax.dot_general(a, b, (((1,), (1,)), ((), ())), preferred_element_type=F32)


def _dot_tn(a, b):
    return lax.dot_general(a, b, (((0,), (0,)), ((), ())), preferred_element_type=F32)


def _split2(x):
    hi = x.astype(BF16)
    lo = (x - hi.astype(F32)).astype(BF16)
    return hi, lo


def _sigmoid(x):
    return 1.0 / (1.0 + jnp.exp(-x))


def _proj_kernel(x_ref, g_ref, w_ref, qkv_ref, za_ref, rw_ref, gt_ref):
    x = x_ref[...]
    ms = jnp.mean(x * x, axis=-1, keepdims=True)
    u = (x * lax.rsqrt(ms + RMS_EPS) * g_ref[...]).astype(BF16)

    def emit(out_ref, col0, width, scale_first=None):
        for c in range(0, width, PROJ_COLS):
            w = min(PROJ_COLS, width - c)
            y = _dot(u, w_ref[:, col0 + c:col0 + c + w])
            if scale_first is not None and c == 0:
                y = y * scale_first
            out_ref[:, c:c + w] = y.astype(out_ref.dtype)

    emit(qkv_ref, 0, 3 * WIDTH, scale_first=Q_SCALE)
    emit(za_ref, 3 * WIDTH, WIDTH)
    emit(rw_ref, A_COLS, RW_COLS)
    emit(gt_ref, A_COLS + RW_COLS, G_COLS)


def _project(x2, g_pre, w_in_bf):
    n = x2.shape[0]
    tm = math.gcd(PROJ_TOKENS, n)
    return pl.pallas_call(
        _proj_kernel,
        grid=(n // tm,),
        in_specs=[
            pl.BlockSpec((tm, D_MODEL), lambda i: (i, 0)),
            pl.BlockSpec((1, D_MODEL), lambda i: (0, 0)),
            pl.BlockSpec((D_MODEL, IN_COLS), lambda i: (0, 0), pipeline_mode=pl.Buffered(1)),
        ],
        out_specs=[
            pl.BlockSpec((tm, 3 * WIDTH), lambda i: (i, 0)),
            pl.BlockSpec((tm, WIDTH), lambda i: (i, 0)),
            pl.BlockSpec((tm, RW_COLS), lambda i: (i, 0)),
            pl.BlockSpec((tm, G_COLS), lambda i: (i, 0)),
        ],
        out_shape=[
            jax.ShapeDtypeStruct((n, 3 * WIDTH), BF16),
            jax.ShapeDtypeStruct((n, WIDTH), F32),
            jax.ShapeDtypeStruct((n, RW_COLS), F32),
            jax.ShapeDtypeStruct((n, G_COLS), F32),
        ],
        compiler_params=pltpu.CompilerParams(
            dimension_semantics=("arbitrary",), vmem_limit_bytes=VMEM_LIMIT),
        name="proj",
    )(x2, g_pre, w_in_bf)


def _t5_bucket_np(dist):
    n = np.maximum(dist, 0)
    nf = np.maximum(n, 1).astype(np.float32)
    large = REL_MAX_EXACT + (np.log(nf / np.float32(REL_MAX_EXACT)) / np.float32(math.log(REL_MAX_DIST / REL_MAX_EXACT))
                             * np.float32(REL_BUCKETS - REL_MAX_EXACT)).astype(np.int32)
    large = np.minimum(large, REL_BUCKETS - 1)
    return np.where(n < REL_MAX_EXACT, n, large).astype(np.int32)


def _bucket_tables():
    s = np.arange(MOBA_BLOCK)[:, None]
    t = np.arange(MOBA_BLOCK)[None, :]
    own = np.where(t >= s, _t5_bucket_np(t - s), -1).astype(np.int32)
    prev = _t5_bucket_np(MOBA_BLOCK + t - s)
    assert _t5_bucket_np(np.array(MOBA_BLOCK + 1)) == REL_BUCKETS - 1
    return own, prev


MASKED = -1e30
VT_ROWS = HEAD_DIM + BF16_SUBLANES
KIND_FAR, KIND_PREV, KIND_OWN = 0, 1, 2


def _moba_decode(s, bsz, half):
    per_pair = bsz * half
    return s // per_pair, (s % per_pair) // half, s % half


def _moba_kernel(own_b_ref, prev_b_ref, relb_ref, q_ref, k_ref, v_ref, za_ref, zb_ref, oa_ref, ob_ref,
                 tabs, kmean, kaug, vt, qaug, scores0, scores1, mx0, mx1, acc_ref, *, n_blk, bsz):
    half = n_blk // 2
    n_items = N_PAIRS * bsz * half
    step = pl.program_id(0)
    cur = jnp.minimum(step, n_items - 1)
    prv = jnp.maximum(step - 1, 0)
    hp, b, i = _moba_decode(cur, bsz, half)
    hp_p, b_p, i_p = _moba_decode(prv, bsz, half)
    seq_slot = (hp * bsz + b) % 2
    seq_slot_p = (hp_p * bsz + b_p) % 2
    blk = MOBA_BLOCK
    seq = n_blk * blk
    lane = lax.broadcasted_iota(jnp.int32, (1, LANES), 1)
    head_mask = [lane < HEAD_DIM, lane >= HEAD_DIM]
    flag_base = [HEAD_DIM, 0]
    neg_inf = jnp.float32(-jnp.inf)

    @pl.when((b == 0) & (i == 0))
    def _build_bias_tables():
        ob = own_b_ref[...]
        pb = prev_b_ref[...]
        for hh in range(2):
            far = relb_ref[REL_BUCKETS - 1, 2 * hp + hh]
            to = jnp.full((blk, blk), neg_inf, F32)
            tp = jnp.zeros((blk, blk), F32)
            for bkt in range(REL_BUCKETS):
                val = (relb_ref[bkt, 2 * hp + hh] - far) * LOG2E
                to = jnp.where(ob == bkt, val, to)
                tp = jnp.where(pb == bkt, val, tp)
            tabs[hh, KIND_FAR] = jnp.zeros((blk, blk), F32)
            tabs[hh, KIND_PREV] = tp
            tabs[hh, KIND_OWN] = to

    @pl.when(step == 0)
    def _placeholders_for_first_pass2():
        scores1[...] = jnp.zeros_like(scores1)
        mx1[...] = jnp.zeros_like(mx1)

    @pl.when((i == 0) & (step < n_items))
    def _per_sequence_setup():
        ones_row = (lax.broadcasted_iota(jnp.int32, (VT_ROWS - HEAD_DIM, blk), 0) == 0).astype(BF16)
        eye = (lax.broadcasted_iota(jnp.int32, (LANES, LANES), 0)
               == lax.broadcasted_iota(jnp.int32, (LANES, LANES), 1)).astype(BF16)
        for j in range(n_blk):
            kj = k_ref[0, j * blk:(j + 1) * blk, :].astype(F32)
            kmean[j:j + 1, :] = jnp.mean(kj, axis=0, keepdims=True)
            vjt = v_ref[0, j * blk:(j + 1) * blk, :].astype(F32).T
            for hh in range(2):
                flag = (lane == flag_base[hh] + j).astype(F32)
                kaug[hh, j] = jnp.where(head_mask[hh], kj, flag).astype(BF16)
                vt[seq_slot, hh, j, 0:HEAD_DIM, :] = vjt[hh * HEAD_DIM:(hh + 1) * HEAD_DIM, :].astype(BF16)
                vt[seq_slot, hh, j, HEAD_DIM:VT_ROWS, :] = ones_row

        q = q_ref[0]
        blk_id = lax.broadcasted_iota(jnp.int32, (n_blk, seq), 0)
        q_blk = lax.broadcasted_iota(jnp.int32, (n_blk, seq), 1) // blk
        flag_row = lax.broadcasted_iota(jnp.int32, (n_blk, LANES), 0)
        flag_lane = lax.broadcasted_iota(jnp.int32, (n_blk, LANES), 1)
        km_hi, km_lo = _split2(kmean[...])
        channel = lax.broadcasted_iota(jnp.int32, (LANES, 1), 0)
        head_rows = [channel < HEAD_DIM, channel >= HEAD_DIM]
        q_t = _dot_nt(eye, q)
        for hh in range(2):
            q_h = jnp.where(head_rows[hh], q_t, 0.0).astype(BF16)
            g = _dot(km_hi, q_h) + _dot(km_lo, q_h)
            g = jnp.where(blk_id < q_blk, g, neg_inf)
            allowed = blk_id == q_blk
            for _ in range(MOBA_TOPK):
                mx = jnp.max(g, axis=0, keepdims=True)
                first = jnp.min(jnp.where(g == mx, blk_id, n_blk), axis=0, keepdims=True)
                hit = (blk_id == first) & (mx > neg_inf)
                allowed = allowed | hit
                g = jnp.where(hit, neg_inf, g)
            pen_t = jnp.where(allowed, 0.0, MASKED).astype(BF16)
            place = (flag_lane == flag_row + flag_base[hh]).astype(BF16)
            pen = _dot_tn(place, pen_t)
            qa = jnp.where(head_rows[hh], q_t, pen).astype(BF16)
            for j in range(n_blk):
                qaug[hh, j] = qa[:, j * blk:(j + 1) * blk]

    def tile_max(s):
        return jnp.max(s.reshape(blk // 8, 8, blk), axis=0)

    n_tiles = n_blk + 1

    def tile_ids(t, qb_a):
        qb_b = n_blk - 1 - qb_a
        is_a = t <= qb_a
        qb = jnp.where(is_a, qb_a, qb_b)
        kb = jnp.where(is_a, t, t - qb_a - 1)
        kind = jnp.where(kb == qb, KIND_OWN, jnp.where(kb == qb - 1, KIND_PREV, KIND_FAR))
        return is_a, qb, kb, kind

    def both_passes(sc_w, mx_w, sc_r, mx_r):
        m_rows = [[jnp.max(mx_r[hh, w], axis=0, keepdims=True) for w in range(2)] for hh in range(2)]
        mx_w[...] = jnp.full(mx_w.shape, neg_inf, F32)
        acc_ref[...] = jnp.zeros_like(acc_ref)
        tail_b = [None, None]
        q_b = [qaug[hh, n_blk - 1 - i] for hh in range(2)]
        for t in range(n_tiles):
            is_a, qb, kb, kind = tile_ids(t, i)
            which = jnp.where(is_a, 0, 1)
            if half <= t <= n_blk - 2:
                kind = None
            elif t >= n_blk - 1:
                kind = KIND_PREV if t == n_blk - 1 else KIND_OWN
            for hh in range(2):
                s = _dot(kaug[hh, kb], q_b[hh] if t >= half else qaug[hh, qb])
                if kind is not None:
                    s = s + tabs[hh, kind]
                sc_w[hh, t] = s
                mx_w[hh, which] = jnp.maximum(mx_w[hh, which], tile_max(s))
            is_a, _, kb, _ = tile_ids(t, i_p)
            which = jnp.where(is_a, 0, 1)
            for hh in range(2):
                if t >= half:
                    p = jnp.exp2(sc_r[hh, t] - m_rows[hh][1]).astype(BF16)
                    d = _dot(vt[seq_slot_p, hh, kb], p)
                    tail_b[hh] = d if tail_b[hh] is None else tail_b[hh] + d
                else:
                    m_row = jnp.where(is_a, m_rows[hh][0], m_rows[hh][1])
                    p = jnp.exp2(sc_r[hh, t] - m_row).astype(BF16)
                    acc_ref[hh, which] += _dot(vt[seq_slot_p, hh, kb], p)
        for hh in range(2):
            acc_ref[hh, 1] += tail_b[hh]

        for w, (z_ref, o_ref) in enumerate(((za_ref, oa_ref), (zb_ref, ob_ref))):
            out_t = jnp.concatenate(
                [acc_ref[hh, w, 0:HEAD_DIM, :] / acc_ref[hh, w, HEAD_DIM:HEAD_DIM + 1, :] for hh in range(2)],
                axis=0)
            z = z_ref[0]
            o_ref[0] = out_t.T * (z * _sigmoid(z))

    @pl.when(step % 2 == 0)
    def _even_step():
        both_passes(scores0, mx0, scores1, mx1)

    @pl.when(step % 2 == 1)
    def _odd_step():
        both_passes(scores1, mx1, scores0, mx0)


def _moba(qkv, za, rel_bias):
    bsz, seq, _ = qkv.shape
    n_blk = seq // MOBA_BLOCK
    assert n_blk % 2 == 0
    half = n_blk // 2
    own_b, prev_b = _bucket_tables()
    blk = MOBA_BLOCK
    n_items = N_PAIRS * bsz * half
    kernel = functools.partial(_moba_kernel, n_blk=n_blk, bsz=bsz)

    def cur(s):
        return _moba_decode(jnp.minimum(s, n_items - 1), bsz, half)

    def prv(s):
        return _moba_decode(jnp.maximum(s - 1, 0), bsz, half)

    def seq_block(col0):
        def index_map(s):
            hp, b, _ = cur(s)
            return b, 0, col0 + hp
        return pl.BlockSpec((1, seq, LANES), index_map)

    def block_a(s):
        hp, b, i = prv(s)
        return b, i, hp

    def block_b_in(s):
        hp, b, i = prv(s)
        return b, n_blk - 1 - i, hp

    def block_b_out(s):
        hp, b, i = prv(s)
        return b, half - 1 - i, hp

    return pl.pallas_call(
        kernel,
        grid=(n_items + 1,),
        in_specs=[
            pl.BlockSpec((blk, blk), lambda s: (0, 0)),
            pl.BlockSpec((blk, blk), lambda s: (0, 0)),
            pl.BlockSpec(memory_space=pltpu.SMEM),
            seq_block(0), seq_block(N_PAIRS), seq_block(2 * N_PAIRS),
            pl.BlockSpec((1, blk, LANES), block_a),
            pl.BlockSpec((1, blk, LANES), block_b_in),
        ],
        out_specs=[
            pl.BlockSpec((1, blk, LANES), block_a),
            pl.BlockSpec((1, blk, LANES), block_b_out),
        ],
        out_shape=[
            jax.ShapeDtypeStruct((bsz, seq // 2, WIDTH), F32),
            jax.ShapeDtypeStruct((bsz, seq // 2, WIDTH), F32),
        ],
        scratch_shapes=[
            pltpu.VMEM((2, 3, blk, blk), F32),
            pltpu.VMEM((n_blk, LANES), F32),
            pltpu.VMEM((2, n_blk, blk, LANES), BF16),
            pltpu.VMEM((2, 2, n_blk, VT_ROWS, blk), BF16),
            pltpu.VMEM((2, n_blk, LANES, blk), BF16),
            pltpu.VMEM((2, n_blk + 1, blk, blk), F32),
            pltpu.VMEM((2, n_blk + 1, blk, blk), F32),
            pltpu.VMEM((2, 2, 8, blk), F32),
            pltpu.VMEM((2, 2, 8, blk), F32),
            pltpu.VMEM((2, 2, VT_ROWS, blk), F32),
        ],
        compiler_params=pltpu.CompilerParams(
            dimension_semantics=("arbitrary",), vmem_limit_bytes=VMEM_LIMIT),
        name="moba",
    )(jnp.asarray(own_b), jnp.asarray(prev_b), rel_bias, qkv, qkv, qkv, za, za)


RW_CHUNKS = 8
RW_GROUP_CHUNKS = 2


def _rwkv_kernel(rw_ref, mu_ref, w0_ref, lora_ref, a0_ref, kk_ref, ka_ref, rk_ref,
                 lnw_ref, lnb_ref, ltri_ref, ones_ref, o_ref,
                 prev_row, state, t_s, p_s, arb_s, ayk_s, at_s, vst_s, bkh_s, x1_s, loc_s, y_s,
                 lc_s, nc_s, dm_s, h0_s, akv_s):
    c = pl.program_id(1)
    C = CHUNK
    NC = RW_CHUNKS
    R = NC * C

    @pl.when(c == 0)
    def _reset():
        prev_row[...] = jnp.zeros_like(prev_row)
        state[...] = jnp.zeros_like(state)

    lane = lax.broadcasted_iota(jnp.int32, (1, LANES), 1)
    first_half = lane < HEAD_DIM
    second_half = jnp.logical_not(first_half)
    ones_bd = ones_ref[...]
    GC = RW_GROUP_CHUNKS
    RG = GC * C

    def seg_sum(x):
        outs = []
        for p in range(N_PAIRS):
            outs.append(_dot(x[:, p * LANES:(p + 1) * LANES].astype(BF16), ones_bd))
        return jnp.concatenate(outs, axis=1)

    def prepare(g, pre):
        start = g * RG
        cols = rw_ref[0, start:start + RG, :]
        prev = prev_row[0:1, :] if g == 0 else rw_ref[0, start - 1:start, :]
        row = lax.broadcasted_iota(jnp.int32, (RG, 1), 0)
        shifted = jnp.where(row == 0, prev, pltpu.roll(cols, 1, axis=0))
        xs = cols + (shifted - cols) * mu_ref[...]
        r = xs[:, 0:WIDTH]
        k = xs[:, WIDTH:2 * WIDTH]
        v = xs[:, 2 * WIDTH:3 * WIDTH]
        z = xs[:, 3 * WIDTH:4 * WIDTH]
        pre.update(r=r, v=v, z=z)
        yield

        lo_in = xs[:, 4 * WIDTH:4 * WIDTH + LANES]
        lo_in = jnp.where(first_half, jnp.tanh(lo_in), lo_in)
        lora = _dot(lo_in.astype(BF16), lora_ref[...])
        dw = lora[:, 0:WIDTH]
        da = lora[:, WIDTH:2 * WIDTH]
        lw = (-math.exp(-0.5)) * _sigmoid(w0_ref[...] + dw)
        a_ic = _sigmoid(a0_ref[...] + da)
        yield

        kk = k * kk_ref[...]
        kk = kk * lax.rsqrt(jnp.maximum(seg_sum(kk * kk), 1e-24))
        k2 = k * (1.0 + (a_ic - 1.0) * ka_ref[...])
        b_vec = kk * a_ic
        pre.update(k2=k2)
        yield

        l1, l2 = _split2(lw)
        ltri = ltri_ref[0:RG, 0:RG]
        g_inc = _dot(ltri, l1) + _dot(ltri, l2)
        g_end_rows = [g_inc[(n + 1) * C - 1:(n + 1) * C, :] for n in range(GC)]
        decay_end = [jnp.exp(g) for g in g_end_rows]
        d_end = jnp.concatenate([jnp.broadcast_to(d, (C, WIDTH)) for d in decay_end], axis=0)
        e_inc = jnp.exp(g_inc)
        e_neg = 1.0 / e_inc
        yield

        e_end = d_end * e_neg
        pre.update(
            decay_end=decay_end,
            rt=r * e_inc,
            at=kk * (-e_inc * jnp.exp(-lw)),
            bt=b_vec * e_neg, kt=k2 * e_neg, bh=b_vec * e_end, kh=k2 * e_end)

    def stack(x):
        return jnp.concatenate([jnp.where(first_half, x, 0.0), jnp.where(second_half, x, 0.0)], axis=0)

    def twice(x):
        return jnp.concatenate([x, x], axis=0)

    ri = lax.broadcasted_iota(jnp.int32, (2 * C, 2 * C), 0)
    ci = lax.broadcasted_iota(jnp.int32, (2 * C, 2 * C), 1)
    same_head = (ri < C) == (ci < C)
    incl = same_head & (ci <= ri)
    strict = same_head & (ci < ri)
    eye = (ri == ci).astype(F32)
    items = [(n, p) for n in range(NC) for p in range(N_PAIRS)]
    H = 2 * C

    def state_independent(group, pre):
        rt, at, bt, kt, bh, kh, v, decay_end = (pre[name] for name in
                                                ("rt", "at", "bt", "kt", "bh", "kh", "v", "decay_end"))
        for i, n, p in group:
            rows = slice(n * C, (n + 1) * C)
            sl = slice(p * LANES, (p + 1) * LANES)
            r_st = stack(rt[rows, sl]).astype(BF16)
            a_st = stack(at[rows, sl]).astype(BF16)
            ra = jnp.concatenate([r_st, a_st], axis=0)
            bk = jnp.concatenate([twice(bt[rows, sl].astype(BF16)), twice(kt[rows, sl].astype(BF16))], axis=0)
            m_all = _dot_nt(ra, bk)
            n_ab = jnp.where(strict, m_all[H:2 * H, 0:H], 0.0)
            t_s[i] = eye + n_ab
            p_s[i] = n_ab.astype(BF16)
            arb_s[i] = jnp.where(incl, m_all[0:H, 0:H], 0.0).astype(BF16)
            ayk_s[i, 0:H, :] = jnp.where(incl, m_all[0:H, H:2 * H], 0.0).astype(BF16)
            ayk_s[i, H:2 * H, :] = jnp.where(strict, m_all[H:2 * H, H:2 * H], 0.0).astype(BF16)
            x1_s[i, 0:H, :] = r_st
            at_s[i] = a_st
            vst_s[i] = twice(v[rows, sl].astype(BF16))
            bkh_s[i] = jnp.concatenate([stack(bh[rows, sl]), stack(kh[rows, sl])], axis=0).astype(BF16)

        yield

        for i, _, _ in group:
            pb = p_s[i]
            p_s[i] = _dot(pb, pb).astype(BF16)
        yield
        for _ in range(int(math.log2(C)) - 2):
            for i, _, _ in group:
                pb = p_s[i]
                tb = t_s[i]
                both = _dot(pb, jnp.concatenate([tb.astype(BF16), pb], axis=1))
                t_s[i] = tb + both[:, 0:H]
                p_s[i] = both[:, H:2 * H].astype(BF16)
            yield
        for i, _, _ in group:
            tb = t_s[i]
            t_s[i] = tb + _dot(p_s[i], tb.astype(BF16))
        yield

        for i, n, p in group:
            sl = slice(p * LANES, (p + 1) * LANES)
            yk = _dot(ayk_s[i], vst_s[i])
            loc_s[i, 0:H, :] = yk[0:H, :]
            akv_s[i] = yk[H:2 * H, :].astype(BF16)
            dm_s[i] = jnp.broadcast_to(decay_end[n][:, sl], (LANES, LANES)).T
        yield
        for i, _, _ in group:
            wz = _dot(t_s[i].astype(BF16), jnp.concatenate([at_s[i], akv_s[i]], axis=1))
            x1_s[i, H:2 * H, :] = wz[:, 0:LANES].astype(BF16)
            loc_s[i, H:2 * H, :] = wz[:, LANES:2 * LANES]
        yield
        for i, _, _ in group:
            w_bf = x1_s[i, H:2 * H, :]
            u_loc = loc_s[i, H:2 * H, :]
            lc_s[i] = _dot_tn(bkh_s[i, 0:H, :], w_bf).astype(BF16)
            nc_s[i] = _dot_tn(bkh_s[i], jnp.concatenate([u_loc.astype(BF16), vst_s[i]], axis=0))

    inv_n = 1.0 / HEAD_DIM

    def finish(g, pre):
        mine = [(i, n, p) for i, (n, p) in enumerate(items) if n // GC == g]
        for i, n, p in mine:
            h0 = state[p]
            h0_bf = h0.astype(BF16)
            h0_s[i] = h0_bf
            state[p] = dm_s[i] * h0 + _dot(lc_s[i], h0_bf) + nc_s[i]
            if p == N_PAIRS - 1:
                yield
        for i, n, p in mine:
            rs = _dot(x1_s[i], h0_s[i])
            akv_s[i] = (rs[H:2 * H, :] + loc_s[i, H:2 * H, :]).astype(BF16)
            loc_s[i, 0:H, :] = rs[0:H, :] + loc_s[i, 0:H, :]
        yield
        for i, n, p in mine:
            sl = slice(p * LANES, (p + 1) * LANES)
            y_st = loc_s[i, 0:H, :] + _dot(arb_s[i], akv_s[i])
            y_s[n * C:(n + 1) * C, sl] = jnp.where(first_half, y_st[0:C, :], y_st[C:H, :])
        yield
        rows = slice(g * RG, (g + 1) * RG)
        y = y_s[rows, :]
        mean = seg_sum(y) * inv_n
        d = y - mean
        var = seg_sum(d * d) * inv_n
        yn = d * lax.rsqrt(var + GN_EPS) * lnw_ref[...] + lnb_ref[...]
        bonus = seg_sum(pre["r"] * pre["k2"] * rk_ref[...]) * pre["v"]
        z = pre["z"]
        o_ref[0, rows, :] = (yn + bonus) * (z * _sigmoid(z))

    n_groups = NC // GC
    pres = [{} for _ in range(n_groups)]
    for _ in prepare(0, pres[0]):
        pass
    fin = iter(())
    for g in range(n_groups):
        group = [(i, n - g * GC, p) for i, (n, p) in enumerate(items) if n // GC == g]
        nxt = prepare(g + 1, pres[g + 1]) if g + 1 < n_groups else iter(())
        for _ in state_independent(group, pres[g]):
            next(nxt, None)
            next(fin, None)
        for _ in nxt:
            pass
        for _ in fin:
            pass
        fin = finish(g, pres[g])
    for _ in fin:
        pass
    prev_row[0:1, :] = rw_ref[0, R - 1:R, :]


def _rwkv(rw, mu, w0, w_up, a0, a_up, k_k, k_a, r_k, ln_w, ln_b):
    bsz, seq, _ = rw.shape
    C = CHUNK
    row = lambda t: t.reshape(1, -1).astype(F32)
    lora = jnp.zeros((LANES, 2 * WIDTH), F32)
    lora = lora.at[:LORA_RANK, :WIDTH].set(w_up).at[LORA_RANK:, WIDTH:].set(a_up)
    lora = lora.astype(BF16)
    R = RW_CHUNKS * C
    G = RW_CHUNKS * N_PAIRS
    H = 2 * C
    assert seq % R == 0 and H == LANES and RW_CHUNKS % RW_GROUP_CHUNKS == 0
    ltri = jnp.asarray(np.kron(np.eye(RW_CHUNKS), np.tril(np.ones((C, C)))).astype(np.float32)).astype(BF16)
    hid = np.arange(LANES) // HEAD_DIM
    ones_bd = jnp.asarray((hid[:, None] == hid[None, :]).astype(np.float32)).astype(BF16)
    vec = lambda n: pl.BlockSpec((1, n), lambda b, c: (0, 0))
    return pl.pallas_call(
        _rwkv_kernel,
        grid=(bsz, seq // R),
        in_specs=[
            pl.BlockSpec((1, R, RW_COLS), lambda b, c: (b, c, 0)),
            vec(RW_COLS), vec(WIDTH),
            pl.BlockSpec((LANES, 2 * WIDTH), lambda b, c: (0, 0)),
            vec(WIDTH), vec(WIDTH), vec(WIDTH), vec(WIDTH), vec(WIDTH), vec(WIDTH),
            pl.BlockSpec((R, R), lambda b, c: (0, 0)),
            pl.BlockSpec((LANES, LANES), lambda b, c: (0, 0)),
        ],
        out_specs=pl.BlockSpec((1, R, WIDTH), lambda b, c: (b, c, 0)),
        out_shape=jax.ShapeDtypeStruct((bsz, seq, WIDTH), F32),
        scratch_shapes=[
            pltpu.VMEM((8, RW_COLS), F32),
            pltpu.VMEM((N_PAIRS, LANES, LANES), F32),
            pltpu.VMEM((G, H, H), F32),
            pltpu.VMEM((G, H, H), BF16),
            pltpu.VMEM((G, H, H), BF16),
            pltpu.VMEM((G, 2 * H, H), BF16),
            pltpu.VMEM((G, H, LANES), BF16),
            pltpu.VMEM((G, H, LANES), BF16),
            pltpu.VMEM((G, 2 * H, LANES), BF16),
            pltpu.VMEM((G, 2 * H, LANES), BF16),
            pltpu.VMEM((G, 2 * H, LANES), F32),
            pltpu.VMEM((R, WIDTH), F32),
            pltpu.VMEM((G, LANES, LANES), BF16),
            pltpu.VMEM((G, LANES, LANES), F32),
            pltpu.VMEM((G, LANES, LANES), F32),
            pltpu.VMEM((G, LANES, LANES), BF16),
            pltpu.VMEM((G, H, LANES), BF16),
        ],
        compiler_params=pltpu.CompilerParams(
            dimension_semantics=("arbitrary", "arbitrary"), vmem_limit_bytes=VMEM_LIMIT),
        name="rwkv",
    )(rw, row(mu), row(w0), lora, row(a0), row(k_k), row(k_a), row(r_k), row(ln_w), row(ln_b),
      ltri, ones_bd)


def _merge_kernel(ya_lo_ref, ya_hi_ref, yb_ref, ga_ref, gb_ref, x_ref, p_ref, pa_ref, pb_ref, wo_ref, gpost_ref,
                  wpu_ref, wpg_ref, o_ref, *, tiles_per_half):
    first_half = pl.program_id(1) < tiles_per_half
    ya = jnp.where(first_half, ya_lo_ref[0], ya_hi_ref[0])
    ma = _dot(ya.astype(BF16), pa_ref[...])
    mb = _dot(yb_ref[0].astype(BF16), pb_ref[...])
    merged = _sigmoid(ga_ref[0]) * ma + _sigmoid(gb_ref[0]) * mb
    y = _dot(merged.astype(BF16), wo_ref[...])
    ms = jnp.mean(y * y, axis=-1, keepdims=True)
    h = x_ref[0] + y * lax.rsqrt(ms + RMS_EPS) * gpost_ref[...]
    e = _dot(p_ref[0].astype(BF16), wpu_ref[...])
    gate = _dot(h.astype(BF16), wpg_ref[...])
    o_ref[0] = h + _sigmoid(gate) * e


def _merge(ya_lo, ya_hi, yb, gates, x, p, p_a, p_b, w_out, g_post, w_pu, w_pg):
    bsz, seq, _ = x.shape
    tm = math.gcd(MERGE_TOKENS, seq // 2)
    th = seq // 2 // tm
    full = lambda a: pl.BlockSpec(a.shape, lambda b, t: (0, 0))
    tile = lambda w, col=0: pl.BlockSpec((1, tm, w), lambda b, t: (b, t, col))
    return pl.pallas_call(
        functools.partial(_merge_kernel, tiles_per_half=th),
        grid=(bsz, seq // tm),
        in_specs=[
            pl.BlockSpec((1, tm, WIDTH), lambda b, t: (b, jnp.minimum(t, th - 1), 0)),
            pl.BlockSpec((1, tm, WIDTH), lambda b, t: (b, jnp.maximum(t - th, 0), 0)),
            tile(WIDTH), tile(D_MODEL, 0), tile(D_MODEL, 1), tile(D_MODEL), tile(PLE_DIM),
            full(p_a), full(p_b), full(w_out), full(g_post), full(w_pu), full(w_pg),
        ],
        out_specs=tile(D_MODEL),
        out_shape=jax.ShapeDtypeStruct((bsz, seq, D_MODEL), F32),
        compiler_params=pltpu.CompilerParams(
            dimension_semantics=("arbitrary", "arbitrary"), vmem_limit_bytes=VMEM_LIMIT),
        name="merge",
    )(ya_lo, ya_hi, yb, gates, gates, x, p, p_a, p_b, w_out, g_post, w_pu, w_pg)


def kernel(x, p, g_pre, w_in, rel_bias, mu_shift, w0, w_up, a0, a_up, k_k, k_a, r_k, ln_x_w, ln_x_b,
           p_a, p_b, w_out, g_post, w_ple_up, w_ple_gate):
    bsz, seq, d = x.shape
    assert d == D_MODEL and seq % MOBA_BLOCK == 0 and seq // MOBA_BLOCK >= MOBA_TOPK
    assert g_pre.shape[0] == 1, "one layer"
    n = bsz * seq
    x2 = x.reshape(n, d).astype(F32)
    qkv, za, rw, gates = _project(x2, g_pre.astype(F32), w_in[0].astype(BF16))
    ya_lo, ya_hi = _moba(qkv.reshape(bsz, seq, 3 * WIDTH), za.reshape(bsz, seq, WIDTH), rel_bias.astype(F32))
    yb = _rwkv(rw.reshape(bsz, seq, RW_COLS), mu_shift[0], w0[0], w_up[0], a0[0], a_up[0], k_k[0], k_a[0],
               r_k[0], ln_x_w[0], ln_x_b[0])
    out = _merge(ya_lo, ya_hi, yb, gates.reshape(bsz, seq, G_COLS), x.astype(F32), p[0],
                 p_a[0].astype(BF16), p_b[0].astype(BF16), w_out[0].astype(BF16), g_post.astype(F32),
                 w_ple_up[0].astype(BF16), w_ple_gate[0].astype(BF16))
    return out.astype(x.dtype)
```

```python
import functools
import math

import jax
import jax.numpy as jnp
import numpy as np
from jax import lax
from jax.experimental import pallas as pl
from jax.experimental.pallas import tpu as pltpu

F32 = jnp.float32
BF16 = jnp.bfloat16

D_MODEL = 1024
PLE_DIM = 256
RMS_EPS = 1e-6
HEAD_DIM = 64
N_HEADS = 8
WIDTH = N_HEADS * HEAD_DIM
MOBA_BLOCK = 256
MOBA_TOPK = 3
REL_BUCKETS = 32
REL_MAX_EXACT = REL_BUCKETS // 2
REL_MAX_DIST = 128
LORA_RANK = 64
GN_EPS = 64e-5
A_COLS = 4 * WIDTH
RW_COLS = 4 * WIDTH + 2 * LORA_RANK
G_COLS = 2 * D_MODEL
IN_COLS = A_COLS + RW_COLS + G_COLS

LANES = 128
BF16_SUBLANES = 16
N_PAIRS = N_HEADS // 2
CHUNK = 64
PROJ_TOKENS = 512
PROJ_COLS = WIDTH
MERGE_TOKENS = 512
VMEM_LIMIT = 48 * 1024 * 1024
LOG2E = math.log2(math.e)
Q_SCALE = LOG2E * HEAD_DIM ** -0.5


def _dot(a, b):
    return jnp.dot(a, b, preferred_element_type=F32)


def _dot_nt(a, b):
    return lax.dot_general(a, b, (((1,), (1,)), ((), ())), preferred_element_type=F32)


def _dot_tn(a, b):
    return lax.dot_general(a, b, (((0,), (0,)), ((), ())), preferred_element_type=F32)


def _split2(x):
    hi = x.astype(BF16)
    lo = (x - hi.astype(F32)).astype(BF16)
    return hi, lo


def _sigmoid(x):
    return 1.0 / (1.0 + jnp.exp(-x))


def _proj_kernel(x_ref, g_ref, w_ref, qkv_ref, za_ref, rw_ref, gt_ref):
    x = x_ref[...]
    ms = jnp.mean(x * x, axis=-1, keepdims=True)
    u = (x * lax.rsqrt(ms + RMS_EPS) * g_ref[...]).astype(BF16)

    def emit(out_ref, col0, width, scale_first=None):
        for c in range(0, width, PROJ_COLS):
            w = min(PROJ_COLS, width - c)
            y = _dot(u, w_ref[:, col0 + c:col0 + c + w])
            if scale_first is not None and c == 0:
                y = y * scale_first
            out_ref[:, c:c + w] = y.astype(out_ref.dtype)

    emit(qkv_ref, 0, 3 * WIDTH, scale_first=Q_SCALE)
    emit(za_ref, 3 * WIDTH, WIDTH)
    emit(rw_ref, A_COLS, RW_COLS)
    emit(gt_ref, A_COLS + RW_COLS, G_COLS)


def _project(x2, g_pre, w_in_bf):
    n = x2.shape[0]
    tm = math.gcd(PROJ_TOKENS, n)
    return pl.pallas_call(
        _proj_kernel,
        grid=(n // tm,),
        in_specs=[
            pl.BlockSpec((tm, D_MODEL), lambda i: (i, 0)),
            pl.BlockSpec((1, D_MODEL), lambda i: (0, 0)),
            pl.BlockSpec((D_MODEL, IN_COLS), lambda i: (0, 0), pipeline_mode=pl.Buffered(1)),
        ],
        out_specs=[
            pl.BlockSpec((tm, 3 * WIDTH), lambda i: (i, 0)),
            pl.BlockSpec((tm, WIDTH), lambda i: (i, 0)),
            pl.BlockSpec((tm, RW_COLS), lambda i: (i, 0)),
            pl.BlockSpec((tm, G_COLS), lambda i: (i, 0)),
        ],
        out_shape=[
            jax.ShapeDtypeStruct((n, 3 * WIDTH), BF16),
            jax.ShapeDtypeStruct((n, WIDTH), F32),
            jax.ShapeDtypeStruct((n, RW_COLS), F32),
            jax.ShapeDtypeStruct((n, G_COLS), F32),
        ],
        compiler_params=pltpu.CompilerParams(
            dimension_semantics=("arbitrary",), vmem_limit_bytes=VMEM_LIMIT),
        name="proj",
    )(x2, g_pre, w_in_bf)


def _t5_bucket_np(dist):
    n = np.maximum(dist, 0)
    nf = np.maximum(n, 1).astype(np.float32)
    large = REL_MAX_EXACT + (np.log(nf / np.float32(REL_MAX_EXACT)) / np.float32(math.log(REL_MAX_DIST / REL_MAX_EXACT))
                             * np.float32(REL_BUCKETS - REL_MAX_EXACT)).astype(np.int32)
    large = np.minimum(large, REL_BUCKETS - 1)
    return np.where(n < REL_MAX_EXACT, n, large).astype(np.int32)


def _bucket_tables():
    s = np.arange(MOBA_BLOCK)[:, None]
    t = np.arange(MOBA_BLOCK)[None, :]
    own = np.where(t >= s, _t5_bucket_np(t - s), -1).astype(np.int32)
    prev = _t5_bucket_np(MOBA_BLOCK + t - s)
    assert _t5_bucket_np(np.array(MOBA_BLOCK + 1)) == REL_BUCKETS - 1
    return own, prev


MASKED = -1e30
VT_ROWS = HEAD_DIM + BF16_SUBLANES
KIND_FAR, KIND_PREV, KIND_OWN = 0, 1, 2


def _moba_decode(s, bsz, half):
    per_pair = bsz * half
    return s // per_pair, (s % per_pair) // half, s % half


def _moba_kernel(own_b_ref, prev_b_ref, relb_ref, q_ref, k_ref, v_ref, za_ref, zb_ref, oa_ref, ob_ref,
                 tabs, kmean, kaug, vt, qaug, scores0, scores1, mx0, mx1, acc_ref, *, n_blk, bsz):
    half = n_blk // 2
    n_items = N_PAIRS * bsz * half
    step = pl.program_id(0)
    cur = jnp.minimum(step, n_items - 1)
    prv = jnp.maximum(step - 1, 0)
    hp, b, i = _moba_decode(cur, bsz, half)
    hp_p, b_p, i_p = _moba_decode(prv, bsz, half)
    seq_slot = (hp * bsz + b) % 2
    seq_slot_p = (hp_p * bsz + b_p) % 2
    blk = MOBA_BLOCK
    seq = n_blk * blk
    lane = lax.broadcasted_iota(jnp.int32, (1, LANES), 1)
    head_mask = [lane < HEAD_DIM, lane >= HEAD_DIM]
    flag_base = [HEAD_DIM, 0]
    neg_inf = jnp.float32(-jnp.inf)

    @pl.when((b == 0) & (i == 0))
    def _build_bias_tables():
        ob = own_b_ref[...]
        pb = prev_b_ref[...]
        for hh in range(2):
            far = relb_ref[REL_BUCKETS - 1, 2 * hp + hh]
            to = jnp.full((blk, blk), neg_inf, F32)
            tp = jnp.zeros((blk, blk), F32)
            for bkt in range(REL_BUCKETS):
                val = (relb_ref[bkt, 2 * hp + hh] - far) * LOG2E
                to = jnp.where(ob == bkt, val, to)
                tp = jnp.where(pb == bkt, val, tp)
            tabs[hh, KIND_FAR] = jnp.zeros((blk, blk), F32)
            tabs[hh, KIND_PREV] = tp
            tabs[hh, KIND_OWN] = to

    @pl.when(step == 0)
    def _placeholders_for_first_pass2():
        scores1[...] = jnp.zeros_like(scores1)
        mx1[...] = jnp.zeros_like(mx1)

    @pl.when((i == 0) & (step < n_items))
    def _per_sequence_setup():
        ones_row = (lax.broadcasted_iota(jnp.int32, (VT_ROWS - HEAD_DIM, blk), 0) == 0).astype(BF16)
        eye = (lax.broadcasted_iota(jnp.int32, (LANES, LANES), 0)
               == lax.broadcasted_iota(jnp.int32, (LANES, LANES), 1)).astype(BF16)
        ones_rows = jnp.ones((BF16_SUBLANES, blk), BF16)
        for j in range(n_blk):
            kj = k_ref[0, j * blk:(j + 1) * blk, :]
            kmean[j:j + 1, :] = _dot(ones_rows, kj)[0:1, :] * (1.0 / blk)
            vjt = v_ref[0, j * blk:(j + 1) * blk, :].astype(F32).T
            for hh in range(2):
                flag = (lane == flag_base[hh] + j).astype(BF16)
                kaug[hh, j] = jnp.where(head_mask[hh], kj, flag)
                vt[seq_slot, hh, j, 0:HEAD_DIM, :] = vjt[hh * HEAD_DIM:(hh + 1) * HEAD_DIM, :].astype(BF16)
                vt[seq_slot, hh, j, HEAD_DIM:VT_ROWS, :] = ones_row

        q = q_ref[0]
        blk_id = lax.broadcasted_iota(jnp.int32, (n_blk, seq), 0)
        q_blk = lax.broadcasted_iota(jnp.int32, (n_blk, seq), 1) // blk
        flag_row = lax.broadcasted_iota(jnp.int32, (n_blk, LANES), 0)
        flag_lane = lax.broadcasted_iota(jnp.int32, (n_blk, LANES), 1)
        km_hi, km_lo = _split2(kmean[...])
        channel = lax.broadcasted_iota(jnp.int32, (LANES, 1), 0)
        head_rows = [channel < HEAD_DIM, channel >= HEAD_DIM]
        q_t = _dot_nt(eye, q)
        for hh in range(2):
            q_h = jnp.where(head_rows[hh], q_t, 0.0).astype(BF16)
            g = _dot(km_hi, q_h) + _dot(km_lo, q_h)
            g = jnp.where(blk_id < q_blk, g, neg_inf)
            allowed = blk_id == q_blk
            for _ in range(MOBA_TOPK):
                mx = jnp.max(g, axis=0, keepdims=True)
                first = jnp.min(jnp.where(g == mx, blk_id, n_blk), axis=0, keepdims=True)
                hit = (blk_id == first) & (mx > neg_inf)
                allowed = allowed | hit
                g = jnp.where(hit, neg_inf, g)
            pen_t = jnp.where(allowed, 0.0, MASKED).astype(BF16)
            place = (flag_lane == flag_row + flag_base[hh]).astype(BF16)
            pen = _dot_tn(place, pen_t)
            qa = jnp.where(head_rows[hh], q_t, pen).astype(BF16)
            for j in range(n_blk):
                qaug[hh, j] = qa[:, j * blk:(j + 1) * blk]

    def tile_max(s):
        return jnp.max(s.reshape(blk // 8, 8, blk), axis=0)

    n_tiles = n_blk + 1

    def tile_ids(t, qb_a):
        qb_b = n_blk - 1 - qb_a
        is_a = t <= qb_a
        qb = jnp.where(is_a, qb_a, qb_b)
        kb = jnp.where(is_a, t, t - qb_a - 1)
        kind = jnp.where(kb == qb, KIND_OWN, jnp.where(kb == qb - 1, KIND_PREV, KIND_FAR))
        return is_a, qb, kb, kind

    def both_passes(sc_w, mx_w, sc_r, mx_r):
        m_rows = [[jnp.max(mx_r[hh, w], axis=0, keepdims=True) for w in range(2)] for hh in range(2)]
        mx_w[...] = jnp.full(mx_w.shape, neg_inf, F32)
        acc_ref[...] = jnp.zeros_like(acc_ref)
        tail_b = [None, None]
        q_b = [qaug[hh, n_blk - 1 - i] for hh in range(2)]
        for t in range(n_tiles):
            is_a, qb, kb, kind = tile_ids(t, i)
            which = jnp.where(is_a, 0, 1)
            if half <= t <= n_blk - 2:
                kind = None
            elif t >= n_blk - 1:
                kind = KIND_PREV if t == n_blk - 1 else KIND_OWN
            for hh in range(2):
                s = _dot(kaug[hh, kb], q_b[hh] if t >= half else qaug[hh, qb])
                if kind is not None:
                    s = s + tabs[hh, kind]
                sc_w[hh, t] = s
                mx_w[hh, which] = jnp.maximum(mx_w[hh, which], tile_max(s))
            is_a, _, kb, _ = tile_ids(t, i_p)
            which = jnp.where(is_a, 0, 1)
            for hh in range(2):
                if t >= half:
                    p = jnp.exp2(sc_r[hh, t] - m_rows[hh][1]).astype(BF16)
                    d = _dot(vt[seq_slot_p, hh, kb], p)
                    tail_b[hh] = d if tail_b[hh] is None else tail_b[hh] + d
                else:
                    m_row = jnp.where(is_a, m_rows[hh][0], m_rows[hh][1])
                    p = jnp.exp2(sc_r[hh, t] - m_row).astype(BF16)
                    acc_ref[hh, which] += _dot(vt[seq_slot_p, hh, kb], p)
        for hh in range(2):
            acc_ref[hh, 1] += tail_b[hh]

        for w, (z_ref, o_ref) in enumerate(((za_ref, oa_ref), (zb_ref, ob_ref))):
            out_t = jnp.concatenate(
                [acc_ref[hh, w, 0:HEAD_DIM, :] / acc_ref[hh, w, HEAD_DIM:HEAD_DIM + 1, :] for hh in range(2)],
                axis=0)
            z = z_ref[0]
            o_ref[0] = out_t.T * (z * _sigmoid(z))

    @pl.when(step % 2 == 0)
    def _even_step():
        both_passes(scores0, mx0, scores1, mx1)

    @pl.when(step % 2 == 1)
    def _odd_step():
        both_passes(scores1, mx1, scores0, mx0)


def _moba(qkv, za, rel_bias):
    bsz, seq, _ = qkv.shape
    n_blk = seq // MOBA_BLOCK
    assert n_blk % 2 == 0
    half = n_blk // 2
    own_b, prev_b = _bucket_tables()
    blk = MOBA_BLOCK
    n_items = N_PAIRS * bsz * half
    kernel = functools.partial(_moba_kernel, n_blk=n_blk, bsz=bsz)

    def cur(s):
        return _moba_decode(jnp.minimum(s, n_items - 1), bsz, half)

    def prv(s):
        return _moba_decode(jnp.maximum(s - 1, 0), bsz, half)

    def seq_block(col0):
        def index_map(s):
            hp, b, _ = cur(s)
            return b, 0, col0 + hp
        return pl.BlockSpec((1, seq, LANES), index_map)

    def block_a(s):
        hp, b, i = prv(s)
        return b, i, hp

    def block_b_in(s):
        hp, b, i = prv(s)
        return b, n_blk - 1 - i, hp

    def block_b_out(s):
        hp, b, i = prv(s)
        return b, half - 1 - i, hp

    return pl.pallas_call(
        kernel,
        grid=(n_items + 1,),
        in_specs=[
            pl.BlockSpec((blk, blk), lambda s: (0, 0)),
            pl.BlockSpec((blk, blk), lambda s: (0, 0)),
            pl.BlockSpec(memory_space=pltpu.SMEM),
            seq_block(0), seq_block(N_PAIRS), seq_block(2 * N_PAIRS),
            pl.BlockSpec((1, blk, LANES), block_a),
            pl.BlockSpec((1, blk, LANES), block_b_in),
        ],
        out_specs=[
            pl.BlockSpec((1, blk, LANES), block_a),
            pl.BlockSpec((1, blk, LANES), block_b_out),
        ],
        out_shape=[
            jax.ShapeDtypeStruct((bsz, seq // 2, WIDTH), F32),
            jax.ShapeDtypeStruct((bsz, seq // 2, WIDTH), F32),
        ],
        scratch_shapes=[
            pltpu.VMEM((2, 3, blk, blk), F32),
            pltpu.VMEM((n_blk, LANES), F32),
            pltpu.VMEM((2, n_blk, blk, LANES), BF16),
            pltpu.VMEM((2, 2, n_blk, VT_ROWS, blk), BF16),
            pltpu.VMEM((2, n_blk, LANES, blk), BF16),
            pltpu.VMEM((2, n_blk + 1, blk, blk), F32),
            pltpu.VMEM((2, n_blk + 1, blk, blk), F32),
            pltpu.VMEM((2, 2, 8, blk), F32),
            pltpu.VMEM((2, 2, 8, blk), F32),
            pltpu.VMEM((2, 2, VT_ROWS, blk), F32),
        ],
        compiler_params=pltpu.CompilerParams(
            dimension_semantics=("arbitrary",), vmem_limit_bytes=VMEM_LIMIT),
        name="moba",
    )(jnp.asarray(own_b), jnp.asarray(prev_b), rel_bias, qkv, qkv, qkv, za, za)


RW_CHUNKS = 8
RW_GROUP_CHUNKS = 2


def _rwkv_kernel(rw_ref, mu_ref, w0_ref, lora_ref, a0_ref, kk_ref, ka_ref, rk_ref,
                 lnw_ref, lnb_ref, ltri_ref, ones_ref, o_ref,
                 prev_row, state, t_s, p_s, arb_s, ayk_s, at_s, vst_s, bkh_s, x1_s, loc_s, y_s,
                 lc_s, nc_s, dm_s, h0_s, akv_s):
    c = pl.program_id(1)
    C = CHUNK
    NC = RW_CHUNKS
    R = NC * C

    @pl.when(c == 0)
    def _reset():
        prev_row[...] = jnp.zeros_like(prev_row)
        state[...] = jnp.zeros_like(state)

    lane = lax.broadcasted_iota(jnp.int32, (1, LANES), 1)
    first_half = lane < HEAD_DIM
    second_half = jnp.logical_not(first_half)
    ones_bd = ones_ref[...]
    GC = RW_GROUP_CHUNKS
    RG = GC * C

    def seg_sum(x):
        outs = []
        for p in range(N_PAIRS):
            outs.append(_dot(x[:, p * LANES:(p + 1) * LANES].astype(BF16), ones_bd))
        return jnp.concatenate(outs, axis=1)

    def prepare(g, pre):
        start = g * RG
        cols = rw_ref[0, start:start + RG, :]
        prev = prev_row[0:1, :] if g == 0 else rw_ref[0, start - 1:start, :]
        row = lax.broadcasted_iota(jnp.int32, (RG, 1), 0)
        shifted = jnp.where(row == 0, prev, pltpu.roll(cols, 1, axis=0))
        xs = cols + (shifted - cols) * mu_ref[...]
        r = xs[:, 0:WIDTH]
        k = xs[:, WIDTH:2 * WIDTH]
        v = xs[:, 2 * WIDTH:3 * WIDTH]
        z = xs[:, 3 * WIDTH:4 * WIDTH]
        pre.update(r=r, v=v, z=z)
        yield

        lo_in = xs[:, 4 * WIDTH:4 * WIDTH + LANES]
        lo_in = jnp.where(first_half, jnp.tanh(lo_in), lo_in)
        lora = _dot(lo_in.astype(BF16), lora_ref[...])
        dw = lora[:, 0:WIDTH]
        da = lora[:, WIDTH:2 * WIDTH]
        lw = (-math.exp(-0.5)) * _sigmoid(w0_ref[...] + dw)
        a_ic = _sigmoid(a0_ref[...] + da)
        yield

        kk = k * kk_ref[...]
        kk = kk * lax.rsqrt(jnp.maximum(seg_sum(kk * kk), 1e-24))
        k2 = k * (1.0 + (a_ic - 1.0) * ka_ref[...])
        b_vec = kk * a_ic
        pre.update(k2=k2)
        yield

        l1, l2 = _split2(lw)
        ltri = ltri_ref[0:RG, 0:RG]
        g_inc = _dot(ltri, l1) + _dot(ltri, l2)
        g_end_rows = [g_inc[(n + 1) * C - 1:(n + 1) * C, :] for n in range(GC)]
        decay_end = [jnp.exp(g) for g in g_end_rows]
        d_end = jnp.concatenate([jnp.broadcast_to(d, (C, WIDTH)) for d in decay_end], axis=0)
        e_inc = jnp.exp(g_inc)
        e_neg = 1.0 / e_inc
        yield

        e_end = d_end * e_neg
        pre.update(
            decay_end=decay_end,
            rt=r * e_inc,
            at=kk * (-e_inc * jnp.exp(-lw)),
            bt=b_vec * e_neg, kt=k2 * e_neg, bh=b_vec * e_end, kh=k2 * e_end)

    def stack(x):
        return jnp.concatenate([jnp.where(first_half, x, 0.0), jnp.where(second_half, x, 0.0)], axis=0)

    def twice(x):
        return jnp.concatenate([x, x], axis=0)

    ri = lax.broadcasted_iota(jnp.int32, (2 * C, 2 * C), 0)
    ci = lax.broadcasted_iota(jnp.int32, (2 * C, 2 * C), 1)
    same_head = (ri < C) == (ci < C)
    incl = same_head & (ci <= ri)
    strict = same_head & (ci < ri)
    eye = (ri == ci).astype(F32)
    items = [(n, p) for n in range(NC) for p in range(N_PAIRS)]
    H = 2 * C

    def state_independent(group, pre):
        rt, at, bt, kt, bh, kh, v, decay_end = (pre[name] for name in
                                                ("rt", "at", "bt", "kt", "bh", "kh", "v", "decay_end"))
        for i, n, p in group:
            rows = slice(n * C, (n + 1) * C)
            sl = slice(p * LANES, (p + 1) * LANES)
            r_st = stack(rt[rows, sl]).astype(BF16)
            a_st = stack(at[rows, sl]).astype(BF16)
            ra = jnp.concatenate([r_st, a_st], axis=0)
            bk = jnp.concatenate([twice(bt[rows, sl].astype(BF16)), twice(kt[rows, sl].astype(BF16))], axis=0)
            m_all = _dot_nt(ra, bk)
            n_ab = jnp.where(strict, m_all[H:2 * H, 0:H], 0.0)
            t_s[i] = eye + n_ab
            p_s[i] = n_ab.astype(BF16)
            arb_s[i] = jnp.where(incl, m_all[0:H, 0:H], 0.0).astype(BF16)
            ayk_s[i, 0:H, :] = jnp.where(incl, m_all[0:H, H:2 * H], 0.0).astype(BF16)
            ayk_s[i, H:2 * H, :] = jnp.where(strict, m_all[H:2 * H, H:2 * H], 0.0).astype(BF16)
            x1_s[i, 0:H, :] = r_st
            at_s[i] = a_st
            vst_s[i] = twice(v[rows, sl].astype(BF16))
            bkh_s[i] = jnp.concatenate([stack(bh[rows, sl]), stack(kh[rows, sl])], axis=0).astype(BF16)

        yield

        for i, _, _ in group:
            pb = p_s[i]
            p_s[i] = _dot(pb, pb).astype(BF16)
        yield
        for _ in range(int(math.log2(C)) - 2):
            for i, _, _ in group:
                pb = p_s[i]
                tb = t_s[i]
                both = _dot(pb, jnp.concatenate([tb.astype(BF16), pb], axis=1))
                t_s[i] = tb + both[:, 0:H]
                p_s[i] = both[:, H:2 * H].astype(BF16)
            yield
        for i, _, _ in group:
            tb = t_s[i]
            t_s[i] = tb + _dot(p_s[i], tb.astype(BF16))
        yield

        for i, n, p in group:
            sl = slice(p * LANES, (p + 1) * LANES)
            yk = _dot(ayk_s[i], vst_s[i])
            loc_s[i, 0:H, :] = yk[0:H, :]
            akv_s[i] = yk[H:2 * H, :].astype(BF16)
            dm_s[i] = jnp.broadcast_to(decay_end[n][:, sl], (LANES, LANES)).T
        yield
        for i, _, _ in group:
            wz = _dot(t_s[i].astype(BF16), jnp.concatenate([at_s[i], akv_s[i]], axis=1))
            x1_s[i, H:2 * H, :] = wz[:, 0:LANES].astype(BF16)
            loc_s[i, H:2 * H, :] = wz[:, LANES:2 * LANES]
        yield
        for i, _, _ in group:
            w_bf = x1_s[i, H:2 * H, :]
            u_loc = loc_s[i, H:2 * H, :]
            lc_s[i] = _dot_tn(bkh_s[i, 0:H, :], w_bf).astype(BF16)
            nc_s[i] = _dot_tn(bkh_s[i], jnp.concatenate([u_loc.astype(BF16), vst_s[i]], axis=0))

    inv_n = 1.0 / HEAD_DIM

    def finish(g, pre):
        mine = [(i, n, p) for i, (n, p) in enumerate(items) if n // GC == g]
        for i, n, p in mine:
            h0 = state[p]
            h0_bf = h0.astype(BF16)
            h0_s[i] = h0_bf
            state[p] = dm_s[i] * h0 + _dot(lc_s[i], h0_bf) + nc_s[i]
            if p == N_PAIRS - 1:
                yield
        for i, n, p in mine:
            rs = _dot(x1_s[i], h0_s[i])
            akv_s[i] = (rs[H:2 * H, :] + loc_s[i, H:2 * H, :]).astype(BF16)
            loc_s[i, 0:H, :] = rs[0:H, :] + loc_s[i, 0:H, :]
        yield
        for i, n, p in mine:
            sl = slice(p * LANES, (p + 1) * LANES)
            y_st = loc_s[i, 0:H, :] + _dot(arb_s[i], akv_s[i])
            y_s[n * C:(n + 1) * C, sl] = jnp.where(first_half, y_st[0:C, :], y_st[C:H, :])
        yield
        rows = slice(g * RG, (g + 1) * RG)
        y = y_s[rows, :]
        mean = seg_sum(y) * inv_n
        d = y - mean
        var = seg_sum(d * d) * inv_n
        yn = d * lax.rsqrt(var + GN_EPS) * lnw_ref[...] + lnb_ref[...]
        bonus = seg_sum(pre["r"] * pre["k2"] * rk_ref[...]) * pre["v"]
        z = pre["z"]
        o_ref[0, rows, :] = (yn + bonus) * (z * _sigmoid(z))

    n_groups = NC // GC
    pres = [{} for _ in range(n_groups)]
    for _ in prepare(0, pres[0]):
        pass
    fin = iter(())
    for g in range(n_groups):
        group = [(i, n - g * GC, p) for i, (n, p) in enumerate(items) if n // GC == g]
        nxt = prepare(g + 1, pres[g + 1]) if g + 1 < n_groups else iter(())
        for _ in state_independent(group, pres[g]):
            next(nxt, None)
            next(fin, None)
        for _ in nxt:
            pass
        for _ in fin:
            pass
        fin = finish(g, pres[g])
    for _ in fin:
        pass
    prev_row[0:1, :] = rw_ref[0, R - 1:R, :]


def _rwkv(rw, mu, w0, w_up, a0, a_up, k_k, k_a, r_k, ln_w, ln_b):
    bsz, seq, _ = rw.shape
    C = CHUNK
    row = lambda t: t.reshape(1, -1).astype(F32)
    lora = jnp.zeros((LANES, 2 * WIDTH), F32)
    lora = lora.at[:LORA_RANK, :WIDTH].set(w_up).at[LORA_RANK:, WIDTH:].set(a_up)
    lora = lora.astype(BF16)
    R = RW_CHUNKS * C
    G = RW_CHUNKS * N_PAIRS
    H = 2 * C
    assert seq % R == 0 and H == LANES and RW_CHUNKS % RW_GROUP_CHUNKS == 0
    ltri = jnp.asarray(np.kron(np.eye(RW_CHUNKS), np.tril(np.ones((C, C)))).astype(np.float32)).astype(BF16)
    hid = np.arange(LANES) // HEAD_DIM
    ones_bd = jnp.asarray((hid[:, None] == hid[None, :]).astype(np.float32)).astype(BF16)
    vec = lambda n: pl.BlockSpec((1, n), lambda b, c: (0, 0))
    return pl.pallas_call(
        _rwkv_kernel,
        grid=(bsz, seq // R),
        in_specs=[
            pl.BlockSpec((1, R, RW_COLS), lambda b, c: (b, c, 0)),
            vec(RW_COLS), vec(WIDTH),
            pl.BlockSpec((LANES, 2 * WIDTH), lambda b, c: (0, 0)),
            vec(WIDTH), vec(WIDTH), vec(WIDTH), vec(WIDTH), vec(WIDTH), vec(WIDTH),
            pl.BlockSpec((R, R), lambda b, c: (0, 0)),
            pl.BlockSpec((LANES, LANES), lambda b, c: (0, 0)),
        ],
        out_specs=pl.BlockSpec((1, R, WIDTH), lambda b, c: (b, c, 0)),
        out_shape=jax.ShapeDtypeStruct((bsz, seq, WIDTH), F32),
        scratch_shapes=[
            pltpu.VMEM((8, RW_COLS), F32),
            pltpu.VMEM((N_PAIRS, LANES, LANES), F32),
            pltpu.VMEM((G, H, H), F32),
            pltpu.VMEM((G, H, H), BF16),
            pltpu.VMEM((G, H, H), BF16),
            pltpu.VMEM((G, 2 * H, H), BF16),
            pltpu.VMEM((G, H, LANES), BF16),
            pltpu.VMEM((G, H, LANES), BF16),
            pltpu.VMEM((G, 2 * H, LANES), BF16),
            pltpu.VMEM((G, 2 * H, LANES), BF16),
            pltpu.VMEM((G, 2 * H, LANES), F32),
            pltpu.VMEM((R, WIDTH), F32),
            pltpu.VMEM((G, LANES, LANES), BF16),
            pltpu.VMEM((G, LANES, LANES), F32),
            pltpu.VMEM((G, LANES, LANES), F32),
            pltpu.VMEM((G, LANES, LANES), BF16),
            pltpu.VMEM((G, H, LANES), BF16),
        ],
        compiler_params=pltpu.CompilerParams(
            dimension_semantics=("arbitrary", "arbitrary"), vmem_limit_bytes=VMEM_LIMIT),
        name="rwkv",
    )(rw, row(mu), row(w0), lora, row(a0), row(k_k), row(k_a), row(r_k), row(ln_w), row(ln_b),
      ltri, ones_bd)


def _merge_kernel(ya_lo_ref, ya_hi_ref, yb_ref, ga_ref, gb_ref, x_ref, p_ref, pa_ref, pb_ref, wo_ref, gpost_ref,
                  wpu_ref, wpg_ref, o_ref, *, tiles_per_half):
    first_half = pl.program_id(1) < tiles_per_half
    ya = jnp.where(first_half, ya_lo_ref[0], ya_hi_ref[0])
    ma = _dot(ya.astype(BF16), pa_ref[...])
    mb = _dot(yb_ref[0].astype(BF16), pb_ref[...])
    merged = _sigmoid(ga_ref[0]) * ma + _sigmoid(gb_ref[0]) * mb
    y = _dot(merged.astype(BF16), wo_ref[...])
    ms = jnp.mean(y * y, axis=-1, keepdims=True)
    h = x_ref[0] + y * lax.rsqrt(ms + RMS_EPS) * gpost_ref[...]
    e = _dot(p_ref[0].astype(BF16), wpu_ref[...])
    gate = _dot(h.astype(BF16), wpg_ref[...])
    o_ref[0] = h + _sigmoid(gate) * e


def _merge(ya_lo, ya_hi, yb, gates, x, p, p_a, p_b, w_out, g_post, w_pu, w_pg):
    bsz, seq, _ = x.shape
    tm = math.gcd(MERGE_TOKENS, seq // 2)
    th = seq // 2 // tm
    full = lambda a: pl.BlockSpec(a.shape, lambda b, t: (0, 0))
    tile = lambda w, col=0: pl.BlockSpec((1, tm, w), lambda b, t: (b, t, col))
    return pl.pallas_call(
        functools.partial(_merge_kernel, tiles_per_half=th),
        grid=(bsz, seq // tm),
        in_specs=[
            pl.BlockSpec((1, tm, WIDTH), lambda b, t: (b, jnp.minimum(t, th - 1), 0)),
            pl.BlockSpec((1, tm, WIDTH), lambda b, t: (b, jnp.maximum(t - th, 0), 0)),
            tile(WIDTH), tile(D_MODEL, 0), tile(D_MODEL, 1), tile(D_MODEL), tile(PLE_DIM),
            full(p_a), full(p_b), full(w_out), full(g_post), full(w_pu), full(w_pg),
        ],
        out_specs=tile(D_MODEL),
        out_shape=jax.ShapeDtypeStruct((bsz, seq, D_MODEL), F32),
        compiler_params=pltpu.CompilerParams(
            dimension_semantics=("arbitrary", "arbitrary"), vmem_limit_bytes=VMEM_LIMIT),
        name="merge",
    )(ya_lo, ya_hi, yb, gates, gates, x, p, p_a, p_b, w_out, g_post, w_pu, w_pg)


def kernel(x, p, g_pre, w_in, rel_bias, mu_shift, w0, w_up, a0, a_up, k_k, k_a, r_k, ln_x_w, ln_x_b,
           p_a, p_b, w_out, g_post, w_ple_up, w_ple_gate):
    bsz, seq, d = x.shape
    assert d == D_MODEL and seq % MOBA_BLOCK == 0 and seq // MOBA_BLOCK >= MOBA_TOPK
    assert g_pre.shape[0] == 1, "one layer"
    n = bsz * seq
    x2 = x.reshape(n, d).astype(F32)
    qkv, za, rw, gates = _project(x2, g_pre.astype(F32), w_in[0].astype(BF16))
    ya_lo, ya_hi = _moba(qkv.reshape(bsz, seq, 3 * WIDTH), za.reshape(bsz, seq, WIDTH), rel_bias.astype(F32))
    yb = _rwkv(rw.reshape(bsz, seq, RW_COLS), mu_shift[0], w0[0], w_up[0], a0[0], a_up[0], k_k[0], k_a[0],
               r_k[0], ln_x_w[0], ln_x_b[0])
    out = _merge(ya_lo, ya_hi, yb, gates.reshape(bsz, seq, G_COLS), x.astype(F32), p[0],
                 p_a[0].astype(BF16), p_b[0].astype(BF16), w_out[0].astype(BF16), g_post.astype(F32),
                 w_ple_up[0].astype(BF16), w_ple_gate[0].astype(BF16))
    return out.astype(x.dtype)
```

```python
import functools
import math

import jax
import jax.numpy as jnp
import numpy as np
from jax import lax
from jax.experimental import pallas as pl
from jax.experimental.pallas import tpu as pltpu

F32 = jnp.float32
BF16 = jnp.bfloat16

D_MODEL = 1024
PLE_DIM = 256
RMS_EPS = 1e-6
HEAD_DIM = 64
N_HEADS = 8
WIDTH = N_HEADS * HEAD_DIM
MOBA_BLOCK = 256
MOBA_TOPK = 3
REL_BUCKETS = 32
REL_MAX_EXACT = REL_BUCKETS // 2
REL_MAX_DIST = 128
LORA_RANK = 64
GN_EPS = 64e-5
A_COLS = 4 * WIDTH
RW_COLS = 4 * WIDTH + 2 * LORA_RANK
G_COLS = 2 * D_MODEL
IN_COLS = A_COLS + RW_COLS + G_COLS

LANES = 128
BF16_SUBLANES = 16
N_PAIRS = N_HEADS // 2
CHUNK = 64
PROJ_TOKENS = 512
PROJ_COLS = WIDTH
MERGE_TOKENS = 512
VMEM_LIMIT = 48 * 1024 * 1024
LOG2E = math.log2(math.e)
Q_SCALE = LOG2E * HEAD_DIM ** -0.5


def _dot(a, b):
    return jnp.dot(a, b, preferred_element_type=F32)


def _dot_nt(a, b):
    return lax.dot_general(a, b, (((1,), (1,)), ((), ())), preferred_element_type=F32)


def _dot_tn(a, b):
    return lax.dot_general(a, b, (((0,), (0,)), ((), ())), preferred_element_type=F32)


def _split2(x):
    hi = x.astype(BF16)
    lo = (x - hi.astype(F32)).astype(BF16)
    return hi, lo


def _sigmoid(x):
    return 1.0 / (1.0 + jnp.exp(-x))


def _proj_kernel(x_ref, g_ref, w_ref, qkv_ref, za_ref, rw_ref, gt_ref):
    x = x_ref[...]
    ms = jnp.mean(x * x, axis=-1, keepdims=True)
    u = (x * lax.rsqrt(ms + RMS_EPS) * g_ref[...]).astype(BF16)

    def emit(out_ref, col0, width, scale_first=None):
        for c in range(0, width, PROJ_COLS):
            w = min(PROJ_COLS, width - c)
            y = _dot(u, w_ref[:, col0 + c:col0 + c + w])
            if scale_first is not None and c == 0:
                y = y * scale_first
            out_ref[:, c:c + w] = y.astype(out_ref.dtype)

    emit(qkv_ref, 0, 3 * WIDTH, scale_first=Q_SCALE)
    emit(za_ref, 3 * WIDTH, WIDTH)
    emit(rw_ref, A_COLS, RW_COLS)
    emit(gt_ref, A_COLS + RW_COLS, G_COLS)


def _project(x2, g_pre, w_in_bf):
    n = x2.shape[0]
    tm = math.gcd(PROJ_TOKENS, n)
    return pl.pallas_call(
        _proj_kernel,
        grid=(n // tm,),
        in_specs=[
            pl.BlockSpec((tm, D_MODEL), lambda i: (i, 0)),
            pl.BlockSpec((1, D_MODEL), lambda i: (0, 0)),
            pl.BlockSpec((D_MODEL, IN_COLS), lambda i: (0, 0), pipeline_mode=pl.Buffered(1)),
        ],
        out_specs=[
            pl.BlockSpec((tm, 3 * WIDTH), lambda i: (i, 0)),
            pl.BlockSpec((tm, WIDTH), lambda i: (i, 0)),
            pl.BlockSpec((tm, RW_COLS), lambda i: (i, 0)),
            pl.BlockSpec((tm, G_COLS), lambda i: (i, 0)),
        ],
        out_shape=[
            jax.ShapeDtypeStruct((n, 3 * WIDTH), BF16),
            jax.ShapeDtypeStruct((n, WIDTH), F32),
            jax.ShapeDtypeStruct((n, RW_COLS), F32),
            jax.ShapeDtypeStruct((n, G_COLS), F32),
        ],
        compiler_params=pltpu.CompilerParams(
            dimension_semantics=("arbitrary",), vmem_limit_bytes=VMEM_LIMIT),
        name="proj",
    )(x2, g_pre, w_in_bf)


def _t5_bucket_np(dist):
    n = np.maximum(dist, 0)
    nf = np.maximum(n, 1).astype(np.float32)
    large = REL_MAX_EXACT + (np.log(nf / np.float32(REL_MAX_EXACT)) / np.float32(math.log(REL_MAX_DIST / REL_MAX_EXACT))
                             * np.float32(REL_BUCKETS - REL_MAX_EXACT)).astype(np.int32)
    large = np.minimum(large, REL_BUCKETS - 1)
    return np.where(n < REL_MAX_EXACT, n, large).astype(np.int32)


def _bucket_tables():
    s = np.arange(MOBA_BLOCK)[:, None]
    t = np.arange(MOBA_BLOCK)[None, :]
    own = np.where(t >= s, _t5_bucket_np(t - s), -1).astype(np.int32)
    prev = _t5_bucket_np(MOBA_BLOCK + t - s)
    assert _t5_bucket_np(np.array(MOBA_BLOCK + 1)) == REL_BUCKETS - 1
    return own, prev


MASKED = -1e30
VT_ROWS = HEAD_DIM + BF16_SUBLANES
KIND_FAR, KIND_PREV, KIND_OWN = 0, 1, 2


def _moba_decode(s, bsz, half):
    per_pair = bsz * half
    return s // per_pair, (s % per_pair) // half, s % half


def _moba_kernel(own_b_ref, prev_b_ref, relb_ref, q_ref, k_ref, v_ref, za_ref, zb_ref, oa_ref, ob_ref,
                 tabs, kmean, kaug, vt, qaug, scores0, scores1, mx0, mx1, acc_ref, *, n_blk, bsz):
    half = n_blk // 2
    n_items = N_PAIRS * bsz * half
    step = pl.program_id(0)
    cur = jnp.minimum(step, n_items - 1)
    prv = jnp.maximum(step - 1, 0)
    hp, b, i = _moba_decode(cur, bsz, half)
    hp_p, b_p, i_p = _moba_decode(prv, bsz, half)
    seq_slot = (hp * bsz + b) % 2
    seq_slot_p = (hp_p * bsz + b_p) % 2
    blk = MOBA_BLOCK
    seq = n_blk * blk
    lane = lax.broadcasted_iota(jnp.int32, (1, LANES), 1)
    head_mask = [lane < HEAD_DIM, lane >= HEAD_DIM]
    flag_base = [HEAD_DIM, 0]
    neg_inf = jnp.float32(-jnp.inf)

    @pl.when((b == 0) & (i == 0))
    def _build_bias_tables():
        ob = own_b_ref[...]
        pb = prev_b_ref[...]
        for hh in range(2):
            far = relb_ref[REL_BUCKETS - 1, 2 * hp + hh]
            to = jnp.full((blk, blk), neg_inf, F32)
            tp = jnp.zeros((blk, blk), F32)
            for bkt in range(REL_BUCKETS):
                val = (relb_ref[bkt, 2 * hp + hh] - far) * LOG2E
                to = jnp.where(ob == bkt, val, to)
                tp = jnp.where(pb == bkt, val, tp)
            tabs[hh, KIND_FAR] = jnp.zeros((blk, blk), F32)
            tabs[hh, KIND_PREV] = tp
            tabs[hh, KIND_OWN] = to

    @pl.when(step == 0)
    def _placeholders_for_first_pass2():
        scores1[...] = jnp.zeros_like(scores1)
        mx1[...] = jnp.zeros_like(mx1)

    @pl.when((i == 0) & (step < n_items))
    def _per_sequence_setup():
        ones_row = (lax.broadcasted_iota(jnp.int32, (VT_ROWS - HEAD_DIM, blk), 0) == 0).astype(BF16)
        eye = (lax.broadcasted_iota(jnp.int32, (LANES, LANES), 0)
               == lax.broadcasted_iota(jnp.int32, (LANES, LANES), 1)).astype(BF16)
        ones_rows = jnp.ones((BF16_SUBLANES, blk), BF16)
        for j in range(n_blk):
            kj = k_ref[0, j * blk:(j + 1) * blk, :]
            kmean[j:j + 1, :] = _dot(ones_rows, kj)[0:1, :] * (1.0 / blk)
            vjt = v_ref[0, j * blk:(j + 1) * blk, :].astype(F32).T
            for hh in range(2):
                flag = (lane == flag_base[hh] + j).astype(BF16)
                kaug[hh, j] = jnp.where(head_mask[hh], kj, flag)
                vt[seq_slot, hh, j, 0:HEAD_DIM, :] = vjt[hh * HEAD_DIM:(hh + 1) * HEAD_DIM, :].astype(BF16)
                vt[seq_slot, hh, j, HEAD_DIM:VT_ROWS, :] = ones_row

        q = q_ref[0]
        blk_id = lax.broadcasted_iota(jnp.int32, (n_blk, seq), 0)
        q_blk = lax.broadcasted_iota(jnp.int32, (n_blk, seq), 1) // blk
        km_hi, km_lo = _split2(kmean[...])
        channel = lax.broadcasted_iota(jnp.int32, (LANES, 1), 0)
        head_rows = [channel < HEAD_DIM, channel >= HEAD_DIM]
        q_t = _dot_nt(eye, q)
        for hh in range(2):
            q_h = jnp.where(head_rows[hh], q_t, 0.0).astype(BF16)
            g = _dot(km_hi, q_h) + _dot(km_lo, q_h)
            g = jnp.where(blk_id < q_blk, g, neg_inf)
            allowed = blk_id == q_blk
            for _ in range(MOBA_TOPK):
                mx = jnp.max(g, axis=0, keepdims=True)
                first = jnp.min(jnp.where(g == mx, blk_id, n_blk), axis=0, keepdims=True)
                hit = (blk_id == first) & (mx > neg_inf)
                allowed = allowed | hit
                g = jnp.where(hit, neg_inf, g)
            pen = jnp.where(allowed, 0.0, MASKED)
            spare = jnp.zeros((HEAD_DIM - n_blk, seq), F32)
            if hh == 0:
                qa = jnp.concatenate([q_t[0:HEAD_DIM, :], pen, spare], axis=0)
            else:
                qa = jnp.concatenate([pen, spare, q_t[HEAD_DIM:LANES, :]], axis=0)
            qa = qa.astype(BF16)
            for j in range(n_blk):
                qaug[hh, j] = qa[:, j * blk:(j + 1) * blk]

    def tile_max(s):
        return jnp.max(s.reshape(blk // 8, 8, blk), axis=0)

    n_tiles = n_blk + 1

    def tile_ids(t, qb_a):
        qb_b = n_blk - 1 - qb_a
        is_a = t <= qb_a
        qb = jnp.where(is_a, qb_a, qb_b)
        kb = jnp.where(is_a, t, t - qb_a - 1)
        kind = jnp.where(kb == qb, KIND_OWN, jnp.where(kb == qb - 1, KIND_PREV, KIND_FAR))
        return is_a, qb, kb, kind

    def both_passes(sc_w, mx_w, sc_r, mx_r):
        m_rows = [[jnp.max(mx_r[hh, w], axis=0, keepdims=True) for w in range(2)] for hh in range(2)]
        mx_w[...] = jnp.full(mx_w.shape, neg_inf, F32)
        acc_ref[...] = jnp.zeros_like(acc_ref)
        tail_b = [None, None]
        q_b = [qaug[hh, n_blk - 1 - i] for hh in range(2)]
        for t in range(n_tiles):
            is_a, qb, kb, kind = tile_ids(t, i)
            which = jnp.where(is_a, 0, 1)
            if half <= t <= n_blk - 2:
                kind = None
            elif t >= n_blk - 1:
                kind = KIND_PREV if t == n_blk - 1 else KIND_OWN
            for hh in range(2):
                s = _dot(kaug[hh, kb], q_b[hh] if t >= half else qaug[hh, qb])
                if kind is not None:
                    s = s + tabs[hh, kind]
                sc_w[hh, t] = s
                mx_w[hh, which] = jnp.maximum(mx_w[hh, which], tile_max(s))
            is_a, _, kb, _ = tile_ids(t, i_p)
            which = jnp.where(is_a, 0, 1)
            for hh in range(2):
                if t >= half:
                    p = jnp.exp2(sc_r[hh, t] - m_rows[hh][1]).astype(BF16)
                    d = _dot(vt[seq_slot_p, hh, kb], p)
                    tail_b[hh] = d if tail_b[hh] is None else tail_b[hh] + d
                else:
                    m_row = jnp.where(is_a, m_rows[hh][0], m_rows[hh][1])
                    p = jnp.exp2(sc_r[hh, t] - m_row).astype(BF16)
                    acc_ref[hh, which] += _dot(vt[seq_slot_p, hh, kb], p)
        for hh in range(2):
            acc_ref[hh, 1] += tail_b[hh]

        for w, (z_ref, o_ref) in enumerate(((za_ref, oa_ref), (zb_ref, ob_ref))):
            out_t = jnp.concatenate(
                [acc_ref[hh, w, 0:HEAD_DIM, :] / acc_ref[hh, w, HEAD_DIM:HEAD_DIM + 1, :] for hh in range(2)],
                axis=0)
            z = z_ref[0]
            o_ref[0] = out_t.T * (z * _sigmoid(z))

    @pl.when(step % 2 == 0)
    def _even_step():
        both_passes(scores0, mx0, scores1, mx1)

    @pl.when(step % 2 == 1)
    def _odd_step():
        both_passes(scores1, mx1, scores0, mx0)


def _moba(qkv, za, rel_bias):
    bsz, seq, _ = qkv.shape
    n_blk = seq // MOBA_BLOCK
    assert n_blk % 2 == 0
    half = n_blk // 2
    own_b, prev_b = _bucket_tables()
    blk = MOBA_BLOCK
    n_items = N_PAIRS * bsz * half
    kernel = functools.partial(_moba_kernel, n_blk=n_blk, bsz=bsz)

    def cur(s):
        return _moba_decode(jnp.minimum(s, n_items - 1), bsz, half)

    def prv(s):
        return _moba_decode(jnp.maximum(s - 1, 0), bsz, half)

    def seq_block(col0):
        def index_map(s):
            hp, b, _ = cur(s)
            return b, 0, col0 + hp
        return pl.BlockSpec((1, seq, LANES), index_map)

    def block_a(s):
        hp, b, i = prv(s)
        return b, i, hp

    def block_b_in(s):
        hp, b, i = prv(s)
        return b, n_blk - 1 - i, hp

    def block_b_out(s):
        hp, b, i = prv(s)
        return b, half - 1 - i, hp

    return pl.pallas_call(
        kernel,
        grid=(n_items + 1,),
        in_specs=[
            pl.BlockSpec((blk, blk), lambda s: (0, 0)),
            pl.BlockSpec((blk, blk), lambda s: (0, 0)),
            pl.BlockSpec(memory_space=pltpu.SMEM),
            seq_block(0), seq_block(N_PAIRS), seq_block(2 * N_PAIRS),
            pl.BlockSpec((1, blk, LANES), block_a),
            pl.BlockSpec((1, blk, LANES), block_b_in),
        ],
        out_specs=[
            pl.BlockSpec((1, blk, LANES), block_a),
            pl.BlockSpec((1, blk, LANES), block_b_out),
        ],
        out_shape=[
            jax.ShapeDtypeStruct((bsz, seq // 2, WIDTH), F32),
            jax.ShapeDtypeStruct((bsz, seq // 2, WIDTH), F32),
        ],
        scratch_shapes=[
            pltpu.VMEM((2, 3, blk, blk), F32),
            pltpu.VMEM((n_blk, LANES), F32),
            pltpu.VMEM((2, n_blk, blk, LANES), BF16),
            pltpu.VMEM((2, 2, n_blk, VT_ROWS, blk), BF16),
            pltpu.VMEM((2, n_blk, LANES, blk), BF16),
            pltpu.VMEM((2, n_blk + 1, blk, blk), F32),
            pltpu.VMEM((2, n_blk + 1, blk, blk), F32),
            pltpu.VMEM((2, 2, 8, blk), F32),
            pltpu.VMEM((2, 2, 8, blk), F32),
            pltpu.VMEM((2, 2, VT_ROWS, blk), F32),
        ],
        compiler_params=pltpu.CompilerParams(
            dimension_semantics=("arbitrary",), vmem_limit_bytes=VMEM_LIMIT),
        name="moba",
    )(jnp.asarray(own_b), jnp.asarray(prev_b), rel_bias, qkv, qkv, qkv, za, za)


RW_CHUNKS = 8
RW_GROUP_CHUNKS = 2


def _rwkv_kernel(rw_ref, mu_ref, w0_ref, lora_ref, a0_ref, kk_ref, ka_ref, rk_ref,
                 lnw_ref, lnb_ref, ltri_ref, ones_ref, o_ref,
                 prev_row, state, t_s, p_s, arb_s, ayk_s, at_s, vst_s, bkh_s, x1_s, loc_s, y_s,
                 lc_s, nc_s, dm_s, h0_s, akv_s):
    c = pl.program_id(1)
    C = CHUNK
    NC = RW_CHUNKS
    R = NC * C

    @pl.when(c == 0)
    def _reset():
        prev_row[...] = jnp.zeros_like(prev_row)
        state[...] = jnp.zeros_like(state)

    lane = lax.broadcasted_iota(jnp.int32, (1, LANES), 1)
    first_half = lane < HEAD_DIM
    second_half = jnp.logical_not(first_half)
    ones_bd = ones_ref[...]
    GC = RW_GROUP_CHUNKS
    RG = GC * C

    def seg_sum(x):
        outs = []
        for p in range(N_PAIRS):
            outs.append(_dot(x[:, p * LANES:(p + 1) * LANES].astype(BF16), ones_bd))
        return jnp.concatenate(outs, axis=1)

    def prepare(g, pre):
        start = g * RG
        cols = rw_ref[0, start:start + RG, :]
        prev = prev_row[0:1, :] if g == 0 else rw_ref[0, start - 1:start, :]
        row = lax.broadcasted_iota(jnp.int32, (RG, 1), 0)
        shifted = jnp.where(row == 0, prev, pltpu.roll(cols, 1, axis=0))
        xs = cols + (shifted - cols) * mu_ref[...]
        r = xs[:, 0:WIDTH]
        k = xs[:, WIDTH:2 * WIDTH]
        v = xs[:, 2 * WIDTH:3 * WIDTH]
        z = xs[:, 3 * WIDTH:4 * WIDTH]
        pre.update(r=r, v=v, z=z)
        yield

        lo_in = xs[:, 4 * WIDTH:4 * WIDTH + LANES]
        lo_in = jnp.where(first_half, jnp.tanh(lo_in), lo_in)
        lora = _dot(lo_in.astype(BF16), lora_ref[...])
        dw = lora[:, 0:WIDTH]
        da = lora[:, WIDTH:2 * WIDTH]
        lw = (-math.exp(-0.5)) * _sigmoid(w0_ref[...] + dw)
        a_ic = _sigmoid(a0_ref[...] + da)
        yield

        kk = k * kk_ref[...]
        kk = kk * lax.rsqrt(jnp.maximum(seg_sum(kk * kk), 1e-24))
        k2 = k * (1.0 + (a_ic - 1.0) * ka_ref[...])
        b_vec = kk * a_ic
        pre.update(k2=k2)
        yield

        l1, l2 = _split2(lw)
        ltri = ltri_ref[0:RG, 0:RG]
        g_inc = _dot(ltri, l1) + _dot(ltri, l2)
        g_end_rows = [g_inc[(n + 1) * C - 1:(n + 1) * C, :] for n in range(GC)]
        decay_end = [jnp.exp(g) for g in g_end_rows]
        d_end = jnp.concatenate([jnp.broadcast_to(d, (C, WIDTH)) for d in decay_end], axis=0)
        e_inc = jnp.exp(g_inc)
        e_neg = 1.0 / e_inc
        yield

        e_end = d_end * e_neg
        pre.update(
            decay_end=decay_end,
            rt=r * e_inc,
            at=kk * (-e_inc * jnp.exp(-lw)),
            bt=b_vec * e_neg, kt=k2 * e_neg, bh=b_vec * e_end, kh=k2 * e_end)

    def stack(x):
        return jnp.concatenate([jnp.where(first_half, x, 0.0), jnp.where(second_half, x, 0.0)], axis=0)

    def twice(x):
        return jnp.concatenate([x, x], axis=0)

    ri = lax.broadcasted_iota(jnp.int32, (2 * C, 2 * C), 0)
    ci = lax.broadcasted_iota(jnp.int32, (2 * C, 2 * C), 1)
    same_head = (ri < C) == (ci < C)
    incl = same_head & (ci <= ri)
    strict = same_head & (ci < ri)
    eye = (ri == ci).astype(F32)
    items = [(n, p) for n in range(NC) for p in range(N_PAIRS)]
    H = 2 * C

    def state_independent(group, pre):
        rt, at, bt, kt, bh, kh, v, decay_end = (pre[name] for name in
                                                ("rt", "at", "bt", "kt", "bh", "kh", "v", "decay_end"))
        for i, n, p in group:
            rows = slice(n * C, (n + 1) * C)
            sl = slice(p * LANES, (p + 1) * LANES)
            r_st = stack(rt[rows, sl]).astype(BF16)
            a_st = stack(at[rows, sl]).astype(BF16)
            ra = jnp.concatenate([r_st, a_st], axis=0)
            bk = jnp.concatenate([twice(bt[rows, sl].astype(BF16)), twice(kt[rows, sl].astype(BF16))], axis=0)
            m_all = _dot_nt(ra, bk)
            n_ab = jnp.where(strict, m_all[H:2 * H, 0:H], 0.0)
            t_s[i] = eye + n_ab
            p_s[i] = n_ab.astype(BF16)
            arb_s[i] = jnp.where(incl, m_all[0:H, 0:H], 0.0).astype(BF16)
            ayk_s[i, 0:H, :] = jnp.where(incl, m_all[0:H, H:2 * H], 0.0).astype(BF16)
            ayk_s[i, H:2 * H, :] = jnp.where(strict, m_all[H:2 * H, H:2 * H], 0.0).astype(BF16)
            x1_s[i, 0:H, :] = r_st
            at_s[i] = a_st
            vst_s[i] = twice(v[rows, sl].astype(BF16))
            bkh_s[i] = jnp.concatenate([stack(bh[rows, sl]), stack(kh[rows, sl])], axis=0).astype(BF16)

        yield

        for i, _, _ in group:
            pb = p_s[i]
            p_s[i] = _dot(pb, pb).astype(BF16)
        yield
        for _ in range(int(math.log2(C)) - 2):
            for i, _, _ in group:
                pb = p_s[i]
                tb = t_s[i]
                both = _dot(pb, jnp.concatenate([tb.astype(BF16), pb], axis=1))
                t_s[i] = tb + both[:, 0:H]
                p_s[i] = both[:, H:2 * H].astype(BF16)
            yield
        for i, _, _ in group:
            tb = t_s[i]
            t_s[i] = tb + _dot(p_s[i], tb.astype(BF16))
        yield

        for i, n, p in group:
            sl = slice(p * LANES, (p + 1) * LANES)
            yk = _dot(ayk_s[i], vst_s[i])
            loc_s[i, 0:H, :] = yk[0:H, :]
            akv_s[i] = yk[H:2 * H, :].astype(BF16)
            dm_s[i] = jnp.broadcast_to(decay_end[n][:, sl], (LANES, LANES)).T
        yield
        for i, _, _ in group:
            wz = _dot(t_s[i].astype(BF16), jnp.concatenate([at_s[i], akv_s[i]], axis=1))
            x1_s[i, H:2 * H, :] = wz[:, 0:LANES].astype(BF16)
            loc_s[i, H:2 * H, :] = wz[:, LANES:2 * LANES]
        yield
        for i, _, _ in group:
            w_bf = x1_s[i, H:2 * H, :]
            u_loc = loc_s[i, H:2 * H, :]
            lc_s[i] = _dot_tn(bkh_s[i, 0:H, :], w_bf).astype(BF16)
            nc_s[i] = _dot_tn(bkh_s[i], jnp.concatenate([u_loc.astype(BF16), vst_s[i]], axis=0))

    inv_n = 1.0 / HEAD_DIM

    def finish(g, pre):
        mine = [(i, n, p) for i, (n, p) in enumerate(items) if n // GC == g]
        for i, n, p in mine:
            h0 = state[p]
            h0_bf = h0.astype(BF16)
            h0_s[i] = h0_bf
            state[p] = dm_s[i] * h0 + _dot(lc_s[i], h0_bf) + nc_s[i]
            if p == N_PAIRS - 1:
                yield
        for i, n, p in mine:
            rs = _dot(x1_s[i], h0_s[i])
            akv_s[i] = (rs[H:2 * H, :] + loc_s[i, H:2 * H, :]).astype(BF16)
            loc_s[i, 0:H, :] = rs[0:H, :] + loc_s[i, 0:H, :]
        yield
        for i, n, p in mine:
            sl = slice(p * LANES, (p + 1) * LANES)
            y_st = loc_s[i, 0:H, :] + _dot(arb_s[i], akv_s[i])
            y_s[n * C:(n + 1) * C, sl] = jnp.where(first_half, y_st[0:C, :], y_st[C:H, :])
        yield
        rows = slice(g * RG, (g + 1) * RG)
        y = y_s[rows, :]
        mean = seg_sum(y) * inv_n
        d = y - mean
        var = seg_sum(d * d) * inv_n
        yn = d * lax.rsqrt(var + GN_EPS) * lnw_ref[...] + lnb_ref[...]
        bonus = seg_sum(pre["r"] * pre["k2"] * rk_ref[...]) * pre["v"]
        z = pre["z"]
        o_ref[0, rows, :] = (yn + bonus) * (z * _sigmoid(z))

    n_groups = NC // GC
    pres = [{} for _ in range(n_groups)]
    for _ in prepare(0, pres[0]):
        pass
    fin = iter(())
    for g in range(n_groups):
        group = [(i, n - g * GC, p) for i, (n, p) in enumerate(items) if n // GC == g]
        nxt = prepare(g + 1, pres[g + 1]) if g + 1 < n_groups else iter(())
        for _ in state_independent(group, pres[g]):
            next(nxt, None)
            next(fin, None)
        for _ in nxt:
            pass
        for _ in fin:
            pass
        fin = finish(g, pres[g])
    for _ in fin:
        pass
    prev_row[0:1, :] = rw_ref[0, R - 1:R, :]


def _rwkv(rw, mu, w0, w_up, a0, a_up, k_k, k_a, r_k, ln_w, ln_b):
    bsz, seq, _ = rw.shape
    C = CHUNK
    row = lambda t: t.reshape(1, -1).astype(F32)
    lora = jnp.zeros((LANES, 2 * WIDTH), F32)
    lora = lora.at[:LORA_RANK, :WIDTH].set(w_up).at[LORA_RANK:, WIDTH:].set(a_up)
    lora = lora.astype(BF16)
    R = RW_CHUNKS * C
    G = RW_CHUNKS * N_PAIRS
    H = 2 * C
    assert seq % R == 0 and H == LANES and RW_CHUNKS % RW_GROUP_CHUNKS == 0
    ltri = jnp.asarray(np.kron(np.eye(RW_CHUNKS), np.tril(np.ones((C, C)))).astype(np.float32)).astype(BF16)
    hid = np.arange(LANES) // HEAD_DIM
    ones_bd = jnp.asarray((hid[:, None] == hid[None, :]).astype(np.float32)).astype(BF16)
    vec = lambda n: pl.BlockSpec((1, n), lambda b, c: (0, 0))
    return pl.pallas_call(
        _rwkv_kernel,
        grid=(bsz, seq // R),
        in_specs=[
            pl.BlockSpec((1, R, RW_COLS), lambda b, c: (b, c, 0)),
            vec(RW_COLS), vec(WIDTH),
            pl.BlockSpec((LANES, 2 * WIDTH), lambda b, c: (0, 0)),
            vec(WIDTH), vec(WIDTH), vec(WIDTH), vec(WIDTH), vec(WIDTH), vec(WIDTH),
            pl.BlockSpec((R, R), lambda b, c: (0, 0)),
            pl.BlockSpec((LANES, LANES), lambda b, c: (0, 0)),
        ],
        out_specs=pl.BlockSpec((1, R, WIDTH), lambda b, c: (b, c, 0)),
        out_shape=jax.ShapeDtypeStruct((bsz, seq, WIDTH), F32),
        scratch_shapes=[
            pltpu.VMEM((8, RW_COLS), F32),
            pltpu.VMEM((N_PAIRS, LANES, LANES), F32),
            pltpu.VMEM((G, H, H), F32),
            pltpu.VMEM((G, H, H), BF16),
            pltpu.VMEM((G, H, H), BF16),
            pltpu.VMEM((G, 2 * H, H), BF16),
            pltpu.VMEM((G, H, LANES), BF16),
            pltpu.VMEM((G, H, LANES), BF16),
            pltpu.VMEM((G, 2 * H, LANES), BF16),
            pltpu.VMEM((G, 2 * H, LANES), BF16),
            pltpu.VMEM((G, 2 * H, LANES), F32),
            pltpu.VMEM((R, WIDTH), F32),
            pltpu.VMEM((G, LANES, LANES), BF16),
            pltpu.VMEM((G, LANES, LANES), F32),
            pltpu.VMEM((G, LANES, LANES), F32),
            pltpu.VMEM((G, LANES, LANES), BF16),
            pltpu.VMEM((G, H, LANES), BF16),
        ],
        compiler_params=pltpu.CompilerParams(
            dimension_semantics=("arbitrary", "arbitrary"), vmem_limit_bytes=VMEM_LIMIT),
        name="rwkv",
    )(rw, row(mu), row(w0), lora, row(a0), row(k_k), row(k_a), row(r_k), row(ln_w), row(ln_b),
      ltri, ones_bd)


def _merge_kernel(ya_lo_ref, ya_hi_ref, yb_ref, ga_ref, gb_ref, x_ref, p_ref, pa_ref, pb_ref, wo_ref, gpost_ref,
                  wpu_ref, wpg_ref, o_ref, *, tiles_per_half):
    first_half = pl.program_id(1) < tiles_per_half
    ya = jnp.where(first_half, ya_lo_ref[0], ya_hi_ref[0])
    ma = _dot(ya.astype(BF16), pa_ref[...])
    mb = _dot(yb_ref[0].astype(BF16), pb_ref[...])
    merged = _sigmoid(ga_ref[0]) * ma + _sigmoid(gb_ref[0]) * mb
    y = _dot(merged.astype(BF16), wo_ref[...])
    ms = jnp.mean(y * y, axis=-1, keepdims=True)
    h = x_ref[0] + y * lax.rsqrt(ms + RMS_EPS) * gpost_ref[...]
    e = _dot(p_ref[0].astype(BF16), wpu_ref[...])
    gate = _dot(h.astype(BF16), wpg_ref[...])
    o_ref[0] = h + _sigmoid(gate) * e


def _merge(ya_lo, ya_hi, yb, gates, x, p, p_a, p_b, w_out, g_post, w_pu, w_pg):
    bsz, seq, _ = x.shape
    tm = math.gcd(MERGE_TOKENS, seq // 2)
    th = seq // 2 // tm
    full = lambda a: pl.BlockSpec(a.shape, lambda b, t: (0, 0))
    tile = lambda w, col=0: pl.BlockSpec((1, tm, w), lambda b, t: (b, t, col))
    return pl.pallas_call(
        functools.partial(_merge_kernel, tiles_per_half=th),
        grid=(bsz, seq // tm),
        in_specs=[
            pl.BlockSpec((1, tm, WIDTH), lambda b, t: (b, jnp.minimum(t, th - 1), 0)),
            pl.BlockSpec((1, tm, WIDTH), lambda b, t: (b, jnp.maximum(t - th, 0), 0)),
            tile(WIDTH), tile(D_MODEL, 0), tile(D_MODEL, 1), tile(D_MODEL), tile(PLE_DIM),
            full(p_a), full(p_b), full(w_out), full(g_post), full(w_pu), full(w_pg),
        ],
        out_specs=tile(D_MODEL),
        out_shape=jax.ShapeDtypeStruct((bsz, seq, D_MODEL), F32),
        compiler_params=pltpu.CompilerParams(
            dimension_semantics=("arbitrary", "arbitrary"), vmem_limit_bytes=VMEM_LIMIT),
        name="merge",
    )(ya_lo, ya_hi, yb, gates, gates, x, p, p_a, p_b, w_out, g_post, w_pu, w_pg)


def kernel(x, p, g_pre, w_in, rel_bias, mu_shift, w0, w_up, a0, a_up, k_k, k_a, r_k, ln_x_w, ln_x_b,
           p_a, p_b, w_out, g_post, w_ple_up, w_ple_gate):
    bsz, seq, d = x.shape
    assert d == D_MODEL and seq % MOBA_BLOCK == 0 and seq // MOBA_BLOCK >= MOBA_TOPK
    assert g_pre.shape[0] == 1, "one layer"
    n = bsz * seq
    x2 = x.reshape(n, d).astype(F32)
    qkv, za, rw, gates = _project(x2, g_pre.astype(F32), w_in[0].astype(BF16))
    ya_lo, ya_hi = _moba(qkv.reshape(bsz, seq, 3 * WIDTH), za.reshape(bsz, seq, WIDTH), rel_bias.astype(F32))
    yb = _rwkv(rw.reshape(bsz, seq, RW_COLS), mu_shift[0], w0[0], w_up[0], a0[0], a_up[0], k_k[0], k_a[0],
               r_k[0], ln_x_w[0], ln_x_b[0])
    out = _merge(ya_lo, ya_hi, yb, gates.reshape(bsz, seq, G_COLS), x.astype(F32), p[0],
                 p_a[0].astype(BF16), p_b[0].astype(BF16), w_out[0].astype(BF16), g_post.astype(F32),
                 w_ple_up[0].astype(BF16), w_ple_gate[0].astype(BF16))
    return out.astype(x.dtype)
```

```python
import functools
import math

import jax
import jax.numpy as jnp
import numpy as np
from jax import lax
from jax.experimental import pallas as pl
from jax.experimental.pallas import tpu as pltpu

F32 = jnp.float32
BF16 = jnp.bfloat16

D_MODEL = 1024
PLE_DIM = 256
RMS_EPS = 1e-6
HEAD_DIM = 64
N_HEADS = 8
WIDTH = N_HEADS * HEAD_DIM
MOBA_BLOCK = 256
MOBA_TOPK = 3
REL_BUCKETS = 32
REL_MAX_EXACT = REL_BUCKETS // 2
REL_MAX_DIST = 128
LORA_RANK = 64
GN_EPS = 64e-5
A_COLS = 4 * WIDTH
RW_COLS = 4 * WIDTH + 2 * LORA_RANK
G_COLS = 2 * D_MODEL
IN_COLS = A_COLS + RW_COLS + G_COLS

LANES = 128
BF16_SUBLANES = 16
N_PAIRS = N_HEADS // 2
CHUNK = 64
PROJ_TOKENS = 512
PROJ_COLS = WIDTH
MERGE_TOKENS = 512
VMEM_LIMIT = 48 * 1024 * 1024
LOG2E = math.log2(math.e)
Q_SCALE = LOG2E * HEAD_DIM ** -0.5


def _dot(a, b):
    return jnp.dot(a, b, preferred_element_type=F32)


def _dot_nt(a, b):
    return lax.dot_general(a, b, (((1,), (1,)), ((), ())), preferred_element_type=F32)


def _dot_tn(a, b):
    return lax.dot_general(a, b, (((0,), (0,)), ((), ())), preferred_element_type=F32)


def _split2(x):
    hi = x.astype(BF16)
    lo = (x - hi.astype(F32)).astype(BF16)
    return hi, lo


def _sigmoid(x):
    return 1.0 / (1.0 + jnp.exp(-x))


def _proj_kernel(x_ref, g_ref, w_ref, qkv_ref, za_ref, rw_ref, gt_ref):
    x = x_ref[...]
    ms = jnp.mean(x * x, axis=-1, keepdims=True)
    u = (x * lax.rsqrt(ms + RMS_EPS) * g_ref[...]).astype(BF16)

    def emit(out_ref, col0, width, scale_first=None):
        for c in range(0, width, PROJ_COLS):
            w = min(PROJ_COLS, width - c)
            y = _dot(u, w_ref[:, col0 + c:col0 + c + w])
            if scale_first is not None and c == 0:
                y = y * scale_first
            out_ref[:, c:c + w] = y.astype(out_ref.dtype)

    emit(qkv_ref, 0, 3 * WIDTH, scale_first=Q_SCALE)
    emit(za_ref, 3 * WIDTH, WIDTH)
    emit(rw_ref, A_COLS, RW_COLS)
    emit(gt_ref, A_COLS + RW_COLS, G_COLS)


def _project(x2, g_pre, w_in_bf):
    n = x2.shape[0]
    tm = math.gcd(PROJ_TOKENS, n)
    return pl.pallas_call(
        _proj_kernel,
        grid=(n // tm,),
        in_specs=[
            pl.BlockSpec((tm, D_MODEL), lambda i: (i, 0)),
            pl.BlockSpec((1, D_MODEL), lambda i: (0, 0)),
            pl.BlockSpec((D_MODEL, IN_COLS), lambda i: (0, 0), pipeline_mode=pl.Buffered(1)),
        ],
        out_specs=[
            pl.BlockSpec((tm, 3 * WIDTH), lambda i: (i, 0)),
            pl.BlockSpec((tm, WIDTH), lambda i: (i, 0)),
            pl.BlockSpec((tm, RW_COLS), lambda i: (i, 0)),
            pl.BlockSpec((tm, G_COLS), lambda i: (i, 0)),
        ],
        out_shape=[
            jax.ShapeDtypeStruct((n, 3 * WIDTH), BF16),
            jax.ShapeDtypeStruct((n, WIDTH), F32),
            jax.ShapeDtypeStruct((n, RW_COLS), F32),
            jax.ShapeDtypeStruct((n, G_COLS), F32),
        ],
        compiler_params=pltpu.CompilerParams(
            dimension_semantics=("arbitrary",), vmem_limit_bytes=VMEM_LIMIT),
        name="proj",
    )(x2, g_pre, w_in_bf)


def _t5_bucket_np(dist):
    n = np.maximum(dist, 0)
    nf = np.maximum(n, 1).astype(np.float32)
    large = REL_MAX_EXACT + (np.log(nf / np.float32(REL_MAX_EXACT)) / np.float32(math.log(REL_MAX_DIST / REL_MAX_EXACT))
                             * np.float32(REL_BUCKETS - REL_MAX_EXACT)).astype(np.int32)
    large = np.minimum(large, REL_BUCKETS - 1)
    return np.where(n < REL_MAX_EXACT, n, large).astype(np.int32)


def _bucket_tables():
    s = np.arange(MOBA_BLOCK)[:, None]
    t = np.arange(MOBA_BLOCK)[None, :]
    own = np.where(t >= s, _t5_bucket_np(t - s), -1).astype(np.int32)
    prev = _t5_bucket_np(MOBA_BLOCK + t - s)
    assert _t5_bucket_np(np.array(MOBA_BLOCK + 1)) == REL_BUCKETS - 1
    return own, prev


MASKED = -1e30
VT_ROWS = HEAD_DIM + BF16_SUBLANES
KIND_FAR, KIND_PREV, KIND_OWN = 0, 1, 2


def _moba_decode(s, bsz, half):
    per_pair = bsz * half
    return s // per_pair, (s % per_pair) // half, s % half


def _moba_kernel(own_b_ref, prev_b_ref, relb_ref, q_ref, k_ref, v_ref, za_ref, zb_ref, oa_ref, ob_ref,
                 tabs, kmean, kaug, vt, qaug, scores0, scores1, mx0, mx1, acc_ref, *, n_blk, bsz):
    half = n_blk // 2
    n_items = N_PAIRS * bsz * half
    step = pl.program_id(0)
    cur = jnp.minimum(step, n_items - 1)
    prv = jnp.maximum(step - 1, 0)
    hp, b, i = _moba_decode(cur, bsz, half)
    hp_p, b_p, i_p = _moba_decode(prv, bsz, half)
    seq_slot = (hp * bsz + b) % 2
    seq_slot_p = (hp_p * bsz + b_p) % 2
    blk = MOBA_BLOCK
    seq = n_blk * blk
    lane = lax.broadcasted_iota(jnp.int32, (1, LANES), 1)
    head_mask = [lane < HEAD_DIM, lane >= HEAD_DIM]
    flag_base = [HEAD_DIM, 0]
    neg_inf = jnp.float32(-jnp.inf)

    @pl.when((b == 0) & (i == 0))
    def _build_bias_tables():
        ob = own_b_ref[...]
        pb = prev_b_ref[...]
        for hh in range(2):
            far = relb_ref[REL_BUCKETS - 1, 2 * hp + hh]
            to = jnp.full((blk, blk), neg_inf, F32)
            tp = jnp.zeros((blk, blk), F32)
            for bkt in range(REL_BUCKETS):
                val = (relb_ref[bkt, 2 * hp + hh] - far) * LOG2E
                to = jnp.where(ob == bkt, val, to)
                tp = jnp.where(pb == bkt, val, tp)
            tabs[hh, KIND_FAR] = jnp.zeros((blk, blk), F32)
            tabs[hh, KIND_PREV] = tp
            tabs[hh, KIND_OWN] = to

    @pl.when(step == 0)
    def _placeholders_for_first_pass2():
        scores1[...] = jnp.zeros_like(scores1)
        mx1[...] = jnp.zeros_like(mx1)

    @pl.when((i == 0) & (step < n_items))
    def _per_sequence_setup():
        ones_row = (lax.broadcasted_iota(jnp.int32, (VT_ROWS - HEAD_DIM, blk), 0) == 0).astype(BF16)
        eye = (lax.broadcasted_iota(jnp.int32, (LANES, LANES), 0)
               == lax.broadcasted_iota(jnp.int32, (LANES, LANES), 1)).astype(BF16)
        ones_rows = jnp.ones((BF16_SUBLANES, blk), BF16)
        for j in range(n_blk):
            kj = k_ref[0, j * blk:(j + 1) * blk, :]
            kmean[j:j + 1, :] = _dot(ones_rows, kj)[0:1, :] * (1.0 / blk)
            vjt = v_ref[0, j * blk:(j + 1) * blk, :].astype(F32).T
            for hh in range(2):
                flag = (lane == flag_base[hh] + j).astype(BF16)
                kaug[hh, j] = jnp.where(head_mask[hh], kj, flag)
                vt[seq_slot, hh, j, 0:HEAD_DIM, :] = vjt[hh * HEAD_DIM:(hh + 1) * HEAD_DIM, :].astype(BF16)
                vt[seq_slot, hh, j, HEAD_DIM:VT_ROWS, :] = ones_row

        q = q_ref[0]
        blk_id = lax.broadcasted_iota(jnp.int32, (n_blk, seq), 0)
        q_blk = lax.broadcasted_iota(jnp.int32, (n_blk, seq), 1) // blk
        km_hi, km_lo = _split2(kmean[...])
        channel = lax.broadcasted_iota(jnp.int32, (LANES, 1), 0)
        head_rows = [channel < HEAD_DIM, channel >= HEAD_DIM]
        q_t = _dot_nt(eye, q)
        for hh in range(2):
            q_h = jnp.where(head_rows[hh], q_t, 0.0).astype(BF16)
            g = _dot(km_hi, q_h) + _dot(km_lo, q_h)
            g = jnp.where(blk_id < q_blk, g, neg_inf)
            allowed = blk_id == q_blk
            for _ in range(MOBA_TOPK):
                mx = jnp.max(g, axis=0, keepdims=True)
                first = jnp.min(jnp.where(g == mx, blk_id, n_blk), axis=0, keepdims=True)
                hit = blk_id == first
                allowed = allowed | hit
                g = jnp.where(hit, neg_inf, g)
            pen = jnp.where(allowed, 0.0, MASKED)
            spare = jnp.zeros((HEAD_DIM - n_blk, seq), F32)
            if hh == 0:
                qa = jnp.concatenate([q_t[0:HEAD_DIM, :], pen, spare], axis=0)
            else:
                qa = jnp.concatenate([pen, spare, q_t[HEAD_DIM:LANES, :]], axis=0)
            qa = qa.astype(BF16)
            for j in range(n_blk):
                qaug[hh, j] = qa[:, j * blk:(j + 1) * blk]

    def tile_max(s):
        return jnp.max(s.reshape(blk // 8, 8, blk), axis=0)

    n_tiles = n_blk + 1

    def tile_ids(t, qb_a):
        qb_b = n_blk - 1 - qb_a
        is_a = t <= qb_a
        qb = jnp.where(is_a, qb_a, qb_b)
        kb = jnp.where(is_a, t, t - qb_a - 1)
        kind = jnp.where(kb == qb, KIND_OWN, jnp.where(kb == qb - 1, KIND_PREV, KIND_FAR))
        return is_a, qb, kb, kind

    def both_passes(sc_w, mx_w, sc_r, mx_r):
        m_rows = [[jnp.max(mx_r[hh, w], axis=0, keepdims=True) for w in range(2)] for hh in range(2)]
        mx_w[...] = jnp.full(mx_w.shape, neg_inf, F32)
        acc_ref[...] = jnp.zeros_like(acc_ref)
        tail_b = [None, None]
        q_b = [qaug[hh, n_blk - 1 - i] for hh in range(2)]
        for t in range(n_tiles):
            is_a, qb, kb, kind = tile_ids(t, i)
            which = jnp.where(is_a, 0, 1)
            if half <= t <= n_blk - 2:
                kind = None
            elif t >= n_blk - 1:
                kind = KIND_PREV if t == n_blk - 1 else KIND_OWN
            for hh in range(2):
                s = _dot(kaug[hh, kb], q_b[hh] if t >= half else qaug[hh, qb])
                if kind is not None:
                    s = s + tabs[hh, kind]
                sc_w[hh, t] = s
                mx_w[hh, which] = jnp.maximum(mx_w[hh, which], tile_max(s))
            is_a, _, kb, _ = tile_ids(t, i_p)
            which = jnp.where(is_a, 0, 1)
            for hh in range(2):
                if t >= half:
                    p = jnp.exp2(sc_r[hh, t] - m_rows[hh][1]).astype(BF16)
                    d = _dot(vt[seq_slot_p, hh, kb], p)
                    tail_b[hh] = d if tail_b[hh] is None else tail_b[hh] + d
                else:
                    m_row = jnp.where(is_a, m_rows[hh][0], m_rows[hh][1])
                    p = jnp.exp2(sc_r[hh, t] - m_row).astype(BF16)
                    acc_ref[hh, which] += _dot(vt[seq_slot_p, hh, kb], p)
        for hh in range(2):
            acc_ref[hh, 1] += tail_b[hh]

        for w, (z_ref, o_ref) in enumerate(((za_ref, oa_ref), (zb_ref, ob_ref))):
            out_t = jnp.concatenate(
                [acc_ref[hh, w, 0:HEAD_DIM, :] / acc_ref[hh, w, HEAD_DIM:HEAD_DIM + 1, :] for hh in range(2)],
                axis=0)
            z = z_ref[0]
            o_ref[0] = out_t.T * (z * _sigmoid(z))

    @pl.when(step % 2 == 0)
    def _even_step():
        both_passes(scores0, mx0, scores1, mx1)

    @pl.when(step % 2 == 1)
    def _odd_step():
        both_passes(scores1, mx1, scores0, mx0)


def _moba(qkv, za, rel_bias):
    bsz, seq, _ = qkv.shape
    n_blk = seq // MOBA_BLOCK
    assert n_blk % 2 == 0
    half = n_blk // 2
    own_b, prev_b = _bucket_tables()
    blk = MOBA_BLOCK
    n_items = N_PAIRS * bsz * half
    kernel = functools.partial(_moba_kernel, n_blk=n_blk, bsz=bsz)

    def cur(s):
        return _moba_decode(jnp.minimum(s, n_items - 1), bsz, half)

    def prv(s):
        return _moba_decode(jnp.maximum(s - 1, 0), bsz, half)

    def seq_block(col0):
        def index_map(s):
            hp, b, _ = cur(s)
            return b, 0, col0 + hp
        return pl.BlockSpec((1, seq, LANES), index_map)

    def block_a(s):
        hp, b, i = prv(s)
        return b, i, hp

    def block_b_in(s):
        hp, b, i = prv(s)
        return b, n_blk - 1 - i, hp

    def block_b_out(s):
        hp, b, i = prv(s)
        return b, half - 1 - i, hp

    return pl.pallas_call(
        kernel,
        grid=(n_items + 1,),
        in_specs=[
            pl.BlockSpec((blk, blk), lambda s: (0, 0)),
            pl.BlockSpec((blk, blk), lambda s: (0, 0)),
            pl.BlockSpec(memory_space=pltpu.SMEM),
            seq_block(0), seq_block(N_PAIRS), seq_block(2 * N_PAIRS),
            pl.BlockSpec((1, blk, LANES), block_a),
            pl.BlockSpec((1, blk, LANES), block_b_in),
        ],
        out_specs=[
            pl.BlockSpec((1, blk, LANES), block_a),
            pl.BlockSpec((1, blk, LANES), block_b_out),
        ],
        out_shape=[
            jax.ShapeDtypeStruct((bsz, seq // 2, WIDTH), F32),
            jax.ShapeDtypeStruct((bsz, seq // 2, WIDTH), F32),
        ],
        scratch_shapes=[
            pltpu.VMEM((2, 3, blk, blk), F32),
            pltpu.VMEM((n_blk, LANES), F32),
            pltpu.VMEM((2, n_blk, blk, LANES), BF16),
            pltpu.VMEM((2, 2, n_blk, VT_ROWS, blk), BF16),
            pltpu.VMEM((2, n_blk, LANES, blk), BF16),
            pltpu.VMEM((2, n_blk + 1, blk, blk), F32),
            pltpu.VMEM((2, n_blk + 1, blk, blk), F32),
            pltpu.VMEM((2, 2, 8, blk), F32),
            pltpu.VMEM((2, 2, 8, blk), F32),
            pltpu.VMEM((2, 2, VT_ROWS, blk), F32),
        ],
        compiler_params=pltpu.CompilerParams(
            dimension_semantics=("arbitrary",), vmem_limit_bytes=VMEM_LIMIT),
        name="moba",
    )(jnp.asarray(own_b), jnp.asarray(prev_b), rel_bias, qkv, qkv, qkv, za, za)


RW_CHUNKS = 8
RW_GROUP_CHUNKS = 2


def _rwkv_kernel(rw_ref, mu_ref, w0_ref, lora_ref, a0_ref, kk_ref, ka_ref, rk_ref,
                 lnw_ref, lnb_ref, ltri_ref, ones_ref, o_ref,
                 prev_row, state, t_s, p_s, arb_s, ayk_s, at_s, vst_s, bkh_s, x1_s, loc_s, y_s,
                 lc_s, nc_s, dm_s, h0_s, akv_s):
    c = pl.program_id(1)
    C = CHUNK
    NC = RW_CHUNKS
    R = NC * C

    @pl.when(c == 0)
    def _reset():
        prev_row[...] = jnp.zeros_like(prev_row)
        state[...] = jnp.zeros_like(state)

    lane = lax.broadcasted_iota(jnp.int32, (1, LANES), 1)
    first_half = lane < HEAD_DIM
    second_half = jnp.logical_not(first_half)
    ones_bd = ones_ref[...]
    GC = RW_GROUP_CHUNKS
    RG = GC * C

    def seg_sum(x):
        outs = []
        for p in range(N_PAIRS):
            outs.append(_dot(x[:, p * LANES:(p + 1) * LANES].astype(BF16), ones_bd))
        return jnp.concatenate(outs, axis=1)

    def prepare(g, pre):
        start = g * RG
        cols = rw_ref[0, start:start + RG, :]
        prev = prev_row[0:1, :] if g == 0 else rw_ref[0, start - 1:start, :]
        row = lax.broadcasted_iota(jnp.int32, (RG, 1), 0)
        shifted = jnp.where(row == 0, prev, pltpu.roll(cols, 1, axis=0))
        xs = cols + (shifted - cols) * mu_ref[...]
        r = xs[:, 0:WIDTH]
        k = xs[:, WIDTH:2 * WIDTH]
        v = xs[:, 2 * WIDTH:3 * WIDTH]
        z = xs[:, 3 * WIDTH:4 * WIDTH]
        pre.update(r=r, v=v, z=z)
        yield

        lo_in = xs[:, 4 * WIDTH:4 * WIDTH + LANES]
        lo_in = jnp.where(first_half, jnp.tanh(lo_in), lo_in)
        lora = _dot(lo_in.astype(BF16), lora_ref[...])
        dw = lora[:, 0:WIDTH]
        da = lora[:, WIDTH:2 * WIDTH]
        lw = (-math.exp(-0.5)) * _sigmoid(w0_ref[...] + dw)
        a_ic = _sigmoid(a0_ref[...] + da)
        yield

        kk = k * kk_ref[...]
        kk = kk * lax.rsqrt(jnp.maximum(seg_sum(kk * kk), 1e-24))
        k2 = k * (1.0 + (a_ic - 1.0) * ka_ref[...])
        b_vec = kk * a_ic
        pre.update(k2=k2)
        yield

        l1, l2 = _split2(lw)
        ltri = ltri_ref[0:RG, 0:RG]
        g_inc = _dot(ltri, l1) + _dot(ltri, l2)
        g_end_rows = [g_inc[(n + 1) * C - 1:(n + 1) * C, :] for n in range(GC)]
        decay_end = [jnp.exp(g) for g in g_end_rows]
        d_end = jnp.concatenate([jnp.broadcast_to(d, (C, WIDTH)) for d in decay_end], axis=0)
        e_inc = jnp.exp(g_inc)
        e_neg = 1.0 / e_inc
        yield

        e_end = d_end * e_neg
        pre.update(
            decay_end=decay_end,
            rt=r * e_inc,
            at=kk * (-e_inc * jnp.exp(-lw)),
            bt=b_vec * e_neg, kt=k2 * e_neg, bh=b_vec * e_end, kh=k2 * e_end)

    def stack(x):
        return jnp.concatenate([jnp.where(first_half, x, 0.0), jnp.where(second_half, x, 0.0)], axis=0)

    def twice(x):
        return jnp.concatenate([x, x], axis=0)

    ri = lax.broadcasted_iota(jnp.int32, (2 * C, 2 * C), 0)
    ci = lax.broadcasted_iota(jnp.int32, (2 * C, 2 * C), 1)
    same_head = (ri < C) == (ci < C)
    incl = same_head & (ci <= ri)
    strict = same_head & (ci < ri)
    eye = (ri == ci).astype(F32)
    items = [(n, p) for n in range(NC) for p in range(N_PAIRS)]
    H = 2 * C

    def state_independent(group, pre):
        rt, at, bt, kt, bh, kh, v, decay_end = (pre[name] for name in
                                                ("rt", "at", "bt", "kt", "bh", "kh", "v", "decay_end"))
        for i, n, p in group:
            rows = slice(n * C, (n + 1) * C)
            sl = slice(p * LANES, (p + 1) * LANES)
            r_st = stack(rt[rows, sl]).astype(BF16)
            a_st = stack(at[rows, sl]).astype(BF16)
            ra = jnp.concatenate([r_st, a_st], axis=0)
            bk = jnp.concatenate([twice(bt[rows, sl].astype(BF16)), twice(kt[rows, sl].astype(BF16))], axis=0)
            m_all = _dot_nt(ra, bk)
            n_ab = jnp.where(strict, m_all[H:2 * H, 0:H], 0.0)
            t_s[i] = eye + n_ab
            p_s[i] = n_ab.astype(BF16)
            arb_s[i] = jnp.where(incl, m_all[0:H, 0:H], 0.0).astype(BF16)
            ayk_s[i, 0:H, :] = jnp.where(incl, m_all[0:H, H:2 * H], 0.0).astype(BF16)
            ayk_s[i, H:2 * H, :] = jnp.where(strict, m_all[H:2 * H, H:2 * H], 0.0).astype(BF16)
            x1_s[i, 0:H, :] = r_st
            at_s[i] = a_st
            vst_s[i] = twice(v[rows, sl].astype(BF16))
            bkh_s[i] = jnp.concatenate([stack(bh[rows, sl]), stack(kh[rows, sl])], axis=0).astype(BF16)

        yield

        for i, _, _ in group:
            pb = p_s[i]
            p_s[i] = _dot(pb, pb).astype(BF16)
        yield
        for _ in range(int(math.log2(C)) - 2):
            for i, _, _ in group:
                pb = p_s[i]
                tb = t_s[i]
                both = _dot(pb, jnp.concatenate([tb.astype(BF16), pb], axis=1))
                t_s[i] = tb + both[:, 0:H]
                p_s[i] = both[:, H:2 * H].astype(BF16)
            yield
        for i, _, _ in group:
            tb = t_s[i]
            t_s[i] = tb + _dot(p_s[i], tb.astype(BF16))
        yield

        for i, n, p in group:
            sl = slice(p * LANES, (p + 1) * LANES)
            yk = _dot(ayk_s[i], vst_s[i])
            loc_s[i, 0:H, :] = yk[0:H, :]
            akv_s[i] = yk[H:2 * H, :].astype(BF16)
            dm_s[i] = jnp.broadcast_to(decay_end[n][:, sl], (LANES, LANES)).T
        yield
        for i, _, _ in group:
            wz = _dot(t_s[i].astype(BF16), jnp.concatenate([at_s[i], akv_s[i]], axis=1))
            x1_s[i, H:2 * H, :] = wz[:, 0:LANES].astype(BF16)
            loc_s[i, H:2 * H, :] = wz[:, LANES:2 * LANES]
        yield
        for i, _, _ in group:
            w_bf = x1_s[i, H:2 * H, :]
            u_loc = loc_s[i, H:2 * H, :]
            lc_s[i] = _dot_tn(bkh_s[i, 0:H, :], w_bf).astype(BF16)
            nc_s[i] = _dot_tn(bkh_s[i], jnp.concatenate([u_loc.astype(BF16), vst_s[i]], axis=0))

    inv_n = 1.0 / HEAD_DIM

    def finish(g, pre):
        mine = [(i, n, p) for i, (n, p) in enumerate(items) if n // GC == g]
        for i, n, p in mine:
            h0 = state[p]
            h0_bf = h0.astype(BF16)
            h0_s[i] = h0_bf
            state[p] = dm_s[i] * h0 + _dot(lc_s[i], h0_bf) + nc_s[i]
            if p == N_PAIRS - 1:
                yield
        for i, n, p in mine:
            rs = _dot(x1_s[i], h0_s[i])
            akv_s[i] = (rs[H:2 * H, :] + loc_s[i, H:2 * H, :]).astype(BF16)
            loc_s[i, 0:H, :] = rs[0:H, :] + loc_s[i, 0:H, :]
        yield
        for i, n, p in mine:
            sl = slice(p * LANES, (p + 1) * LANES)
            y_st = loc_s[i, 0:H, :] + _dot(arb_s[i], akv_s[i])
            y_s[n * C:(n + 1) * C, sl] = jnp.where(first_half, y_st[0:C, :], y_st[C:H, :])
        yield
        rows = slice(g * RG, (g + 1) * RG)
        y = y_s[rows, :]
        mean = seg_sum(y) * inv_n
        d = y - mean
        var = seg_sum(d * d) * inv_n
        yn = d * lax.rsqrt(var + GN_EPS) * lnw_ref[...] + lnb_ref[...]
        bonus = seg_sum(pre["r"] * pre["k2"] * rk_ref[...]) * pre["v"]
        z = pre["z"]
        o_ref[0, rows, :] = (yn + bonus) * (z * _sigmoid(z))

    n_groups = NC // GC
    pres = [{} for _ in range(n_groups)]
    for _ in prepare(0, pres[0]):
        pass
    fin = iter(())
    for g in range(n_groups):
        group = [(i, n - g * GC, p) for i, (n, p) in enumerate(items) if n // GC == g]
        nxt = prepare(g + 1, pres[g + 1]) if g + 1 < n_groups else iter(())
        for _ in state_independent(group, pres[g]):
            next(nxt, None)
            next(fin, None)
        for _ in nxt:
            pass
        for _ in fin:
            pass
        fin = finish(g, pres[g])
    for _ in fin:
        pass
    prev_row[0:1, :] = rw_ref[0, R - 1:R, :]


def _rwkv(rw, mu, w0, w_up, a0, a_up, k_k, k_a, r_k, ln_w, ln_b):
    bsz, seq, _ = rw.shape
    C = CHUNK
    row = lambda t: t.reshape(1, -1).astype(F32)
    lora = jnp.zeros((LANES, 2 * WIDTH), F32)
    lora = lora.at[:LORA_RANK, :WIDTH].set(w_up).at[LORA_RANK:, WIDTH:].set(a_up)
    lora = lora.astype(BF16)
    R = RW_CHUNKS * C
    G = RW_CHUNKS * N_PAIRS
    H = 2 * C
    assert seq % R == 0 and H == LANES and RW_CHUNKS % RW_GROUP_CHUNKS == 0
    ltri = jnp.asarray(np.kron(np.eye(RW_CHUNKS), np.tril(np.ones((C, C)))).astype(np.float32)).astype(BF16)
    hid = np.arange(LANES) // HEAD_DIM
    ones_bd = jnp.asarray((hid[:, None] == hid[None, :]).astype(np.float32)).astype(BF16)
    vec = lambda n: pl.BlockSpec((1, n), lambda b, c: (0, 0))
    return pl.pallas_call(
        _rwkv_kernel,
        grid=(bsz, seq // R),
        in_specs=[
            pl.BlockSpec((1, R, RW_COLS), lambda b, c: (b, c, 0)),
            vec(RW_COLS), vec(WIDTH),
            pl.BlockSpec((LANES, 2 * WIDTH), lambda b, c: (0, 0)),
            vec(WIDTH), vec(WIDTH), vec(WIDTH), vec(WIDTH), vec(WIDTH), vec(WIDTH),
            pl.BlockSpec((R, R), lambda b, c: (0, 0)),
            pl.BlockSpec((LANES, LANES), lambda b, c: (0, 0)),
        ],
        out_specs=pl.BlockSpec((1, R, WIDTH), lambda b, c: (b, c, 0)),
        out_shape=jax.ShapeDtypeStruct((bsz, seq, WIDTH), F32),
        scratch_shapes=[
            pltpu.VMEM((8, RW_COLS), F32),
            pltpu.VMEM((N_PAIRS, LANES, LANES), F32),
            pltpu.VMEM((G, H, H), F32),
            pltpu.VMEM((G, H, H), BF16),
            pltpu.VMEM((G, H, H), BF16),
            pltpu.VMEM((G, 2 * H, H), BF16),
            pltpu.VMEM((G, H, LANES), BF16),
            pltpu.VMEM((G, H, LANES), BF16),
            pltpu.VMEM((G, 2 * H, LANES), BF16),
            pltpu.VMEM((G, 2 * H, LANES), BF16),
            pltpu.VMEM((G, 2 * H, LANES), F32),
            pltpu.VMEM((R, WIDTH), F32),
            pltpu.VMEM((G, LANES, LANES), BF16),
            pltpu.VMEM((G, LANES, LANES), F32),
            pltpu.VMEM((G, LANES, LANES), F32),
            pltpu.VMEM((G, LANES, LANES), BF16),
            pltpu.VMEM((G, H, LANES), BF16),
        ],
        compiler_params=pltpu.CompilerParams(
            dimension_semantics=("arbitrary", "arbitrary"), vmem_limit_bytes=VMEM_LIMIT),
        name="rwkv",
    )(rw, row(mu), row(w0), lora, row(a0), row(k_k), row(k_a), row(r_k), row(ln_w), row(ln_b),
      ltri, ones_bd)


def _merge_kernel(ya_lo_ref, ya_hi_ref, yb_ref, ga_ref, gb_ref, x_ref, p_ref, pa_ref, pb_ref, wo_ref, gpost_ref,
                  wpu_ref, wpg_ref, o_ref, *, tiles_per_half):
    first_half = pl.program_id(1) < tiles_per_half
    ya = jnp.where(first_half, ya_lo_ref[0], ya_hi_ref[0])
    ma = _dot(ya.astype(BF16), pa_ref[...])
    mb = _dot(yb_ref[0].astype(BF16), pb_ref[...])
    merged = _sigmoid(ga_ref[0]) * ma + _sigmoid(gb_ref[0]) * mb
    y = _dot(merged.astype(BF16), wo_ref[...])
    ms = jnp.mean(y * y, axis=-1, keepdims=True)
    h = x_ref[0] + y * lax.rsqrt(ms + RMS_EPS) * gpost_ref[...]
    e = _dot(p_ref[0].astype(BF16), wpu_ref[...])
    gate = _dot(h.astype(BF16), wpg_ref[...])
    o_ref[0] = h + _sigmoid(gate) * e


def _merge(ya_lo, ya_hi, yb, gates, x, p, p_a, p_b, w_out, g_post, w_pu, w_pg):
    bsz, seq, _ = x.shape
    tm = math.gcd(MERGE_TOKENS, seq // 2)
    th = seq // 2 // tm
    full = lambda a: pl.BlockSpec(a.shape, lambda b, t: (0, 0))
    tile = lambda w, col=0: pl.BlockSpec((1, tm, w), lambda b, t: (b, t, col))
    return pl.pallas_call(
        functools.partial(_merge_kernel, tiles_per_half=th),
        grid=(bsz, seq // tm),
        in_specs=[
            pl.BlockSpec((1, tm, WIDTH), lambda b, t: (b, jnp.minimum(t, th - 1), 0)),
            pl.BlockSpec((1, tm, WIDTH), lambda b, t: (b, jnp.maximum(t - th, 0), 0)),
            tile(WIDTH), tile(D_MODEL, 0), tile(D_MODEL, 1), tile(D_MODEL), tile(PLE_DIM),
            full(p_a), full(p_b), full(w_out), full(g_post), full(w_pu), full(w_pg),
        ],
        out_specs=tile(D_MODEL),
        out_shape=jax.ShapeDtypeStruct((bsz, seq, D_MODEL), F32),
        compiler_params=pltpu.CompilerParams(
            dimension_semantics=("arbitrary", "arbitrary"), vmem_limit_bytes=VMEM_LIMIT),
        name="merge",
    )(ya_lo, ya_hi, yb, gates, gates, x, p, p_a, p_b, w_out, g_post, w_pu, w_pg)


def kernel(x, p, g_pre, w_in, rel_bias, mu_shift, w0, w_up, a0, a_up, k_k, k_a, r_k, ln_x_w, ln_x_b,
           p_a, p_b, w_out, g_post, w_ple_up, w_ple_gate):
    bsz, seq, d = x.shape
    assert d == D_MODEL and seq % MOBA_BLOCK == 0 and seq // MOBA_BLOCK >= MOBA_TOPK
    assert g_pre.shape[0] == 1, "one layer"
    n = bsz * seq
    x2 = x.reshape(n, d).astype(F32)
    qkv, za, rw, gates = _project(x2, g_pre.astype(F32), w_in[0].astype(BF16))
    ya_lo, ya_hi = _moba(qkv.reshape(bsz, seq, 3 * WIDTH), za.reshape(bsz, seq, WIDTH), rel_bias.astype(F32))
    yb = _rwkv(rw.reshape(bsz, seq, RW_COLS), mu_shift[0], w0[0], w_up[0], a0[0], a_up[0], k_k[0], k_a[0],
               r_k[0], ln_x_w[0], ln_x_b[0])
    out = _merge(ya_lo, ya_hi, yb, gates.reshape(bsz, seq, G_COLS), x.astype(F32), p[0],
                 p_a[0].astype(BF16), p_b[0].astype(BF16), w_out[0].astype(BF16), g_post.astype(F32),
                 w_ple_up[0].astype(BF16), w_ple_gate[0].astype(BF16))
    return out.astype(x.dtype)
```
